```python
import math
import jax, jax.numpy as jnp
from jax import lax
import numpy as np

D_MODEL = 1024
BATCH = 1
SEQ = 16384
DEPTH = 2
DEC_BATCH = 128
DEC_SEQ = 4
PAST_LEN = 16384
PAGE_SIZE = 128

N_MIXERS = 2
N_ATTN_LAYERS = (DEPTH + 1) // 2
N_GLA_LAYERS = DEPTH // 2
HEAD_DIM = 64
N_Q_HEADS = D_MODEL // HEAD_DIM
N_KV_HEADS = 4
GROUP = N_Q_HEADS // N_KV_HEADS
WINDOW = 128
ATTN_BLOCK = 128
GLA_HEADS = 4
GLA_DK = (D_MODEL // 2) // GLA_HEADS
GLA_DV = D_MODEL // GLA_HEADS
GLA_GATE_RANK = 16
GLA_TAU = 16.0
GLA_CHUNK = 64
D_FF = 4 * D_MODEL
ALPHA = (2.0 * DEPTH) ** 0.25
BETA = (8.0 * DEPTH) ** -0.25
LN_EPS = 1e-5

kernel_name = "hybrid_swa_sink_gla_deepnorm_adaln_step"

F32 = jnp.float32


def _layernorm(x, g, b):
    xf = x.astype(F32)
    mu = jnp.mean(xf, axis=-1, keepdims=True)
    var = jnp.mean(jnp.square(xf - mu), axis=-1, keepdims=True)
    return ((xf - mu) * lax.rsqrt(var + LN_EPS) * g.astype(F32) + b.astype(F32)).astype(x.dtype)


def _adaln(c, w_mod, b_mod):
    mod = jax.nn.silu(c) @ w_mod + b_mod
    shift, scale, gate = jnp.split(mod, 3, axis=-1)
    return shift[:, None, :], scale[:, None, :], gate[:, None, :]


def _alibi_slopes():
    h = jnp.arange(1, N_Q_HEADS + 1, dtype=F32)
    return jnp.exp2(-8.0 * h / N_Q_HEADS).reshape(N_KV_HEADS, GROUP)


def _attend(q, k, v, qpos, kpos, sinks):
    s = jnp.einsum('bntkgd,bnskd->bnkgts', q, k).astype(F32) * (HEAD_DIM ** -0.5)
    dist = qpos[:, :, None] - kpos[:, None, :]
    valid = (dist >= 0) & (dist <= WINDOW) & (kpos[:, None, :] >= 0)
    slopes = _alibi_slopes()
    s = s - slopes[:, :, None, None] * dist[None, :, None, None].astype(F32)
    s = jnp.where(valid[None, :, None, None], s, -jnp.inf)
    sink = jnp.broadcast_to(sinks.astype(F32).reshape(N_KV_HEADS, GROUP, 1, 1), s.shape[:-1] + (1,))
    p = jax.nn.softmax(jnp.concatenate([s, sink], axis=-1), axis=-1)[..., :-1]
    return jnp.einsum('bnkgts,bnskd->bntkgd', p.astype(v.dtype), v)


def _attn_mixer(h, w_in, w_out, sinks, cache_k, cache_v, pos0):
    B, L, _ = h.shape
    qkv = h @ w_in
    nq, nkv = N_Q_HEADS * HEAD_DIM, N_KV_HEADS * HEAD_DIM
    q, k, v = jnp.split(qkv, [nq, nq + nkv], axis=-1)
    q = q.reshape(B, L, N_KV_HEADS, GROUP, HEAD_DIM)
    k = k.reshape(B, L, N_KV_HEADS, HEAD_DIM)
    v = v.reshape(B, L, N_KV_HEADS, HEAD_DIM)
    if cache_k is None:
        nb = L // ATTN_BLOCK
        qb = q.reshape(B, nb, ATTN_BLOCK, N_KV_HEADS, GROUP, HEAD_DIM)
        kb = k.reshape(B, nb, ATTN_BLOCK, N_KV_HEADS, HEAD_DIM)
        vb = v.reshape(B, nb, ATTN_BLOCK, N_KV_HEADS, HEAD_DIM)
        padw = ((0, 0), (1, 0), (0, 0), (0, 0), (0, 0))
        kk = jnp.concatenate([jnp.pad(kb, padw)[:, :-1], kb], axis=2)
        vv = jnp.concatenate([jnp.pad(vb, padw)[:, :-1], vb], axis=2)
        qpos = pos0 + jnp.arange(L, dtype=jnp.int32).reshape(nb, ATTN_BLOCK)
        kpos = jnp.concatenate([qpos - ATTN_BLOCK, qpos], axis=1)
        o = _attend(qb, kk, vv, qpos, kpos, sinks)
        w = min(WINDOW, L)
        new_k, new_v = k[:, L - w:], v[:, L - w:]
    else:
        W = cache_k.shape[1]
        kk = jnp.concatenate([cache_k.astype(k.dtype), k], axis=1)
        vv = jnp.concatenate([cache_v.astype(v.dtype), v], axis=1)
        qpos = (pos0 + jnp.arange(L, dtype=jnp.int32))[None]
        kpos = (pos0 - W + jnp.arange(W + L, dtype=jnp.int32))[None]
        o = _attend(q[:, None], kk[:, None], vv[:, None], qpos, kpos, sinks)
        new_k, new_v = kk[:, L:], vv[:, L:]
    o = o.reshape(B, L, nq) @ w_out
    return o, new_k, new_v


def _gla_chunked(q, k, v, lg, S0):
    B, L, H, dk = q.shape
    C = min(GLA_CHUNK, L)
    pad = (-L) % C
    n = (L + pad) // C

    def blk(a):
        a = jnp.pad(a.astype(F32), ((0, 0), (0, pad), (0, 0), (0, 0)))
        return a.reshape(B, n, C, H, a.shape[-1])

    q, k, v, lg = blk(q), blk(k), blk(v), blk(lg)
    b = jnp.cumsum(lg, axis=2)
    qd = q * jnp.exp(b)
    kd = k * jnp.exp(-b)
    causal = jnp.tril(jnp.ones((C, C), dtype=bool))
    A = jnp.where(causal, jnp.einsum('bnthk,bnshk->bnhts', qd, kd), 0.0)
    o_intra = jnp.einsum('bnhts,bnshv->bnthv', A, v)
    bC = b[:, :, -1]
    kdec = k * jnp.exp(bC[:, :, None] - b)

    def step(S, inp):
        qd_c, kdec_c, v_c, bC_c = inp
        o_c = jnp.einsum('bthk,bhkv->bthv', qd_c, S)
        S = jnp.exp(bC_c)[..., None] * S + jnp.einsum('bthk,bthv->bhkv', kdec_c, v_c)
        return S, o_c

    S, o_inter = lax.scan(step, S0.astype(F32),
                          (jnp.moveaxis(qd, 1, 0), jnp.moveaxis(kdec, 1, 0),
                           jnp.moveaxis(v, 1, 0), jnp.moveaxis(bC, 1, 0)))
    o = o_intra + jnp.moveaxis(o_inter, 0, 1)
    return o.reshape(B, n * C, H, -1)[:, :L], S


def _gla_mixer(h, w_in, w_gate_up, b_gate, norm_g, w_out, S0):
    B, L, _ = h.shape
    nk, nv = GLA_HEADS * GLA_DK, GLA_HEADS * GLA_DV
    proj = h @ w_in
    q, k, v, r, gdown = jnp.split(proj, [nk, 2 * nk, 2 * nk + nv, 2 * nk + 2 * nv], axis=-1)
    lg = jax.nn.log_sigmoid((gdown @ w_gate_up + b_gate).astype(F32)) / GLA_TAU
    q = q.reshape(B, L, GLA_HEADS, GLA_DK) * (GLA_DK ** -0.5)
    k = k.reshape(B, L, GLA_HEADS, GLA_DK)
    v = v.reshape(B, L, GLA_HEADS, GLA_DV)
    lg = lg.reshape(B, L, GLA_HEADS, GLA_DK)
    if S0 is None:
        S0 = jnp.zeros((B, GLA_HEADS, GLA_DK, GLA_DV), F32)
    o, S = _gla_chunked(q, k, v, lg, S0)
    o = o * lax.rsqrt(jnp.mean(jnp.square(o), axis=-1, keepdims=True) + LN_EPS) * norm_g.astype(F32)
    o = o.reshape(B, L, nv) * jax.nn.silu(r.astype(F32))
    return o.astype(h.dtype) @ w_out, S


def _mlp(h, w1, w2):
    return jnp.square(jax.nn.relu(h @ w1)) @ w2


def _trunk(x, c, caches_k, caches_v, states, pos0, w_mod, b_mod, ln_g, ln_b,
           attn_w_in, attn_w_out, attn_sinks, gla_w_in, gla_w_gate_up, gla_b_gate,
           gla_norm_g, gla_w_out, mlp_w1, mlp_w2):
    new_k, new_v, new_s = [], [], []
    for i in range(DEPTH):
        j = i // N_MIXERS
        shift, scale, gate = _adaln(c, w_mod[i, 0], b_mod[i, 0])
        h = x * (1.0 + scale) + shift
        if i % N_MIXERS == 0:
            ck = None if caches_k is None else caches_k[j]
            cv = None if caches_v is None else caches_v[j]
            out, kn, vn = _attn_mixer(h, attn_w_in[j], attn_w_out[j], attn_sinks[j], ck, cv, pos0)
            new_k.append(kn)
            new_v.append(vn)
        else:
            s0 = None if states is None else states[j]
            out, sn = _gla_mixer(h, gla_w_in[j], gla_w_gate_up[j], gla_b_gate[j],
                                 gla_norm_g[j], gla_w_out[j], s0)
            new_s.append(sn)
        x = _layernorm(ALPHA * x + gate * out, ln_g[i, 0], ln_b[i, 0])
        shift, scale, gate = _adaln(c, w_mod[i, 1], b_mod[i, 1])
        out = _mlp(x * (1.0 + scale) + shift, mlp_w1[i], mlp_w2[i])
        x = _layernorm(ALPHA * x + gate * out, ln_g[i, 1], ln_b[i, 1])
    return x, jnp.stack(new_k), jnp.stack(new_v), jnp.stack(new_s)


def setup_inputs(seed: int = 0) -> dict:
    key = jax.random.key(seed)
    ks = jax.random.split(key, 24)
    nrm = jax.random.normal
    win = min(WINDOW, PAST_LEN)
    nq, nkv = N_Q_HEADS * HEAD_DIM, N_KV_HEADS * HEAD_DIM
    nk, nv = GLA_HEADS * GLA_DK, GLA_HEADS * GLA_DV
    attn_cols = nq + 2 * nkv
    gla_cols = 2 * nk + 2 * nv + GLA_GATE_RANK
    attn_col_scale = jnp.concatenate([jnp.ones((nq + nkv,), F32), jnp.full((nkv,), BETA, F32)])
    gla_col_scale = jnp.concatenate([jnp.ones((2 * nk,), F32), jnp.full((nv,), BETA, F32),
                                     jnp.ones((nv + GLA_GATE_RANK,), F32)])
    return {
        "x_prompt": nrm(ks[0], (BATCH, SEQ, D_MODEL), F32),
        "x_sample": nrm(ks[1], (DEC_BATCH, DEC_SEQ, D_MODEL), F32),
        "cache_k": nrm(ks[2], (N_ATTN_LAYERS, DEC_BATCH, win, N_KV_HEADS, HEAD_DIM), F32),
        "cache_v": nrm(ks[3], (N_ATTN_LAYERS, DEC_BATCH, win, N_KV_HEADS, HEAD_DIM), F32),
        "state_gla": nrm(ks[4], (N_GLA_LAYERS, DEC_BATCH, GLA_HEADS, GLA_DK, GLA_DV), F32),
        "c_prompt": nrm(ks[5], (BATCH, D_MODEL), F32),
        "c_sample": nrm(ks[6], (DEC_BATCH, D_MODEL), F32),
        "w_mod": nrm(ks[7], (DEPTH, 2, D_MODEL, 3 * D_MODEL), F32) * D_MODEL ** -0.5,
        "b_mod": nrm(ks[8], (DEPTH, 2, 3 * D_MODEL), F32) * 0.02,
        "ln_g": 1.0 + 0.02 * nrm(ks[9], (DEPTH, 2, D_MODEL), F32),
        "ln_b": 0.02 * nrm(ks[10], (DEPTH, 2, D_MODEL), F32),
        "attn_w_in": nrm(ks[11], (N_ATTN_LAYERS, D_MODEL, attn_cols), F32) * D_MODEL ** -0.5 * attn_col_scale,
        "attn_w_out": nrm(ks[12], (N_ATTN_LAYERS, nq, D_MODEL), F32) * nq ** -0.5 * BETA,
        "attn_sinks": nrm(ks[13], (N_ATTN_LAYERS, N_Q_HEADS), F32),
        "gla_w_in": nrm(ks[14], (N_GLA_LAYERS, D_MODEL, gla_cols), F32) * D_MODEL ** -0.5 * gla_col_scale,
        "gla_w_gate_up": nrm(ks[15], (N_GLA_LAYERS, GLA_GATE_RANK, nk), F32) * GLA_GATE_RANK ** -0.5,
        "gla_b_gate": 0.1 * nrm(ks[16], (N_GLA_LAYERS, nk), F32),
        "gla_norm_g": 1.0 + 0.02 * nrm(ks[17], (N_GLA_LAYERS, GLA_DV), F32),
        "gla_w_out": nrm(ks[18], (N_GLA_LAYERS, nv, D_MODEL), F32) * nv ** -0.5 * BETA,
        "mlp_w1": nrm(ks[19], (DEPTH, D_MODEL, D_FF), F32) * D_MODEL ** -0.5,
        "mlp_w2": nrm(ks[20], (DEPTH, D_FF, D_MODEL), F32) * D_FF ** -0.5 * BETA,
    }


def reference(x_prompt, x_sample, cache_k, cache_v, state_gla, c_prompt, c_sample,
              w_mod, b_mod, ln_g, ln_b, attn_w_in, attn_w_out, attn_sinks,
              gla_w_in, gla_w_gate_up, gla_b_gate, gla_norm_g, gla_w_out, mlp_w1, mlp_w2):
    y_prompt, k_p, v_p, s_p = _trunk(
        x_prompt, c_prompt, None, None, None, 0, w_mod, b_mod, ln_g, ln_b,
        attn_w_in, attn_w_out, attn_sinks, gla_w_in, gla_w_gate_up, gla_b_gate,
        gla_norm_g, gla_w_out, mlp_w1, mlp_w2)
    y_sample, k_s, v_s, s_s = _trunk(
        x_sample, c_sample, cache_k, cache_v, state_gla, PAST_LEN, w_mod, b_mod, ln_g, ln_b,
        attn_w_in, attn_w_out, attn_sinks, gla_w_in, gla_w_gate_up, gla_b_gate,
        gla_norm_g, gla_w_out, mlp_w1, mlp_w2)
    return (y_prompt, y_sample, k_p, v_p, s_p, k_s, v_s, s_s)
```

```python
import functools

import jax
import jax.numpy as jnp
from jax import lax
from jax.experimental import pallas as pl
from jax.experimental.pallas import tpu as pltpu

F32 = jnp.float32
BF16 = jnp.bfloat16

D_MODEL = 1024
DEPTH = 2
HEAD_DIM = 64
N_Q_HEADS = 16
N_KV_HEADS = 4
GROUP = 4
WINDOW = 128
ATTN_BLOCK = 128
GLA_HEADS = 4
GLA_DK = 128
GLA_DV = 256
GLA_GATE_RANK = 16
GLA_TAU = 16.0
GLA_CHUNK = 64
D_FF = 4 * D_MODEL
ALPHA = (2.0 * DEPTH) ** 0.25
LN_EPS = 1e-5

NQ = N_Q_HEADS * HEAD_DIM
NKV = N_KV_HEADS * HEAD_DIM
NK = GLA_HEADS * GLA_DK
NV = GLA_HEADS * GLA_DV
LANES = 128
NEG_BIG = -1e30

ROW_TILE = 512
FF_CHUNK = 1024
GLA_TILE = 256
SEQ_BLOCK = 8
VMEM_LIMIT = 56 * 1024 * 1024


def _cparams(n_axes):
    return pltpu.CompilerParams(
        dimension_semantics=("arbitrary",) * n_axes,
        vmem_limit_bytes=VMEM_LIMIT,
    )


def _full(shape):
    zeros = (0,) * len(shape)
    return pl.BlockSpec(shape, lambda *_: zeros)


def _mod_specs(mod, tm):
    rows = mod.shape[0]
    if rows == 1:
        return lambda c: pl.BlockSpec((1, D_MODEL), lambda i, c=c: (0, c))
    return lambda c: pl.BlockSpec((tm, D_MODEL), lambda i, c=c: (i, c))


def _modulate(x, shift, scale):
    return x * (1.0 + scale) + shift


def _res_ln(x, gate, o, g, b):
    y = ALPHA * x + gate * o
    mu = jnp.mean(y, axis=-1, keepdims=True)
    yc = y - mu
    var = jnp.mean(yc * yc, axis=-1, keepdims=True)
    return yc * lax.rsqrt(var + LN_EPS) * g + b


def _mod_kernel(c_ref, w_ref, b_ref, o_ref):
    c = c_ref[...]
    a = (c * jax.nn.sigmoid(c)).astype(BF16)
    o_ref[...] = jnp.dot(a, w_ref[...].astype(BF16), preferred_element_type=F32) + b_ref[...]


def _adaln_all(c_all, w_mod, b_mod):
    rows = c_all.shape[0]
    tn = 1024
    return pl.pallas_call(
        _mod_kernel,
        grid=(4, 3 * D_MODEL // tn),
        in_specs=[
            pl.BlockSpec((rows, D_MODEL), lambda p, n: (0, 0)),
            pl.BlockSpec((None, D_MODEL, tn), lambda p, n: (p, 0, n)),
            pl.BlockSpec((None, 1, tn), lambda p, n: (p, 0, n)),
        ],
        out_specs=pl.BlockSpec((None, rows, tn), lambda p, n: (p, 0, n)),
        out_shape=jax.ShapeDtypeStruct((4, rows, 3 * D_MODEL), F32),
        compiler_params=_cparams(2),
        name="adaln_mod",
    )(c_all, w_mod, b_mod)


def _attn_proj_kernel(x_ref, sh_ref, sc_ref, w_ref, q_ref, k_ref, v_ref):
    h = _modulate(x_ref[...], sh_ref[...], sc_ref[...]).astype(BF16)
    q_ref[...] = jnp.dot(h, w_ref[:, 0:NQ], preferred_element_type=F32).astype(BF16)
    k_ref[...] = jnp.dot(h, w_ref[:, NQ:NQ + NKV], preferred_element_type=F32)
    v_ref[...] = jnp.dot(h, w_ref[:, NQ + NKV:NQ + 2 * NKV], preferred_element_type=F32)


def _attn_proj(x, mod, w_in):
    m = x.shape[0]
    tm = min(ROW_TILE, m)
    ms = _mod_specs(mod, tm)
    row = lambda n: pl.BlockSpec((tm, n), lambda i: (i, 0))
    return pl.pallas_call(
        _attn_proj_kernel,
        grid=(m // tm,),
        in_specs=[row(D_MODEL), ms(0), ms(1), _full(w_in.shape)],
        out_specs=[row(NQ), row(NKV), row(NKV)],
        out_shape=[
            jax.ShapeDtypeStruct((m, NQ), BF16),
            jax.ShapeDtypeStruct((m, NKV), F32),
            jax.ShapeDtypeStruct((m, NKV), F32),
        ],
        compiler_params=_cparams(1),
        name="attn_proj",
    )(x, mod, mod, w_in)


def _gla_proj_kernel(x_ref, sh_ref, sc_ref, w_ref, wgd_ref, wgu_ref, bg_ref,
                     q_ref, k_ref, lg_ref, v_ref, r_ref):
    h = _modulate(x_ref[...], sh_ref[...], sc_ref[...]).astype(BF16)
    dot = functools.partial(jnp.dot, preferred_element_type=F32)
    q_ref[...] = dot(h, w_ref[:, 0:NK]) * (GLA_DK ** -0.5)
    k_ref[...] = dot(h, w_ref[:, NK:2 * NK])
    v_ref[...] = dot(h, w_ref[:, 2 * NK:2 * NK + NV]).astype(v_ref.dtype)
    r_ref[...] = dot(h, w_ref[:, 2 * NK + NV:2 * NK + 2 * NV]).astype(r_ref.dtype)
    gdown = dot(h, wgd_ref[...])
    pre = dot(gdown.astype(BF16), wgu_ref[...]) + bg_ref[...]
    log_sig = jnp.minimum(pre, 0.0) - jnp.log1p(jnp.exp(-jnp.abs(pre)))
    lg_ref[...] = log_sig / GLA_TAU


def _gla_proj(x, mod, w_main, w_gd, w_gu, b_gate, vr_dtype):
    m = x.shape[0]
    tm = min(ROW_TILE, m)
    ms = _mod_specs(mod, tm)
    row = lambda n: pl.BlockSpec((tm, n), lambda i: (i, 0))
    return pl.pallas_call(
        _gla_proj_kernel,
        grid=(m // tm,),
        in_specs=[row(D_MODEL), ms(0), ms(1), _full(w_main.shape), _full(w_gd.shape),
                  _full(w_gu.shape), _full(b_gate.shape)],
        out_specs=[row(NK), row(NK), row(NK), row(NV), row(NV)],
        out_shape=[
            jax.ShapeDtypeStruct((m, NK), F32),
            jax.ShapeDtypeStruct((m, NK), F32),
            jax.ShapeDtypeStruct((m, NK), F32),
            jax.ShapeDtypeStruct((m, NV), vr_dtype),
            jax.ShapeDtypeStruct((m, NV), vr_dtype),
        ],
        compiler_params=_cparams(1),
        name="gla_proj",
    )(x, mod, mod, w_main, w_gd, w_gu, b_gate)


def _out_ln_kernel(a_ref, x_ref, gt_ref, w_ref, g_ref, b_ref, o_ref):
    o = jnp.dot(a_ref[...].astype(BF16), w_ref[...], preferred_element_type=F32)
    o_ref[...] = _res_ln(x_ref[...], gt_ref[...], o, g_ref[...], b_ref[...])


def _out_ln(a, x, mod, w_out, g, b):
    m = x.shape[0]
    tm = min(ROW_TILE, m)
    ms = _mod_specs(mod, tm)
    row = lambda n: pl.BlockSpec((tm, n), lambda i: (i, 0))
    return pl.pallas_call(
        _out_ln_kernel,
        grid=(m // tm,),
        in_specs=[row(a.shape[1]), row(D_MODEL), ms(2), _full(w_out.shape),
                  _full(g.shape), _full(b.shape)],
        out_specs=row(D_MODEL),
        out_shape=jax.ShapeDtypeStruct((m, D_MODEL), F32),
        compiler_params=_cparams(1),
        name="out_ln",
    )(a, x, mod, w_out, g, b)


def _mlp_kernel(x_ref, sh_ref, sc_ref, gt_ref, w1_ref, w2_ref, g_ref, b_ref, o_ref, acc_ref):
    x = x_ref[...]
    h = _modulate(x, sh_ref[...], sc_ref[...]).astype(BF16)
    for c in range(D_FF // FF_CHUNK):
        cols = slice(c * FF_CHUNK, (c + 1) * FF_CHUNK)
        a = jnp.dot(h, w1_ref[:, cols], preferred_element_type=F32)
        a = jnp.square(jnp.maximum(a, 0.0)).astype(BF16)
        d = jnp.dot(a, w2_ref[cols, :], preferred_element_type=F32)
        if c == 0:
            acc_ref[...] = d
        else:
            acc_ref[...] += d
    o_ref[...] = _res_ln(x, gt_ref[...], acc_ref[...], g_ref[...], b_ref[...])


def _mlp(x, mod, w1, w2, g, b):
    m = x.shape[0]
    tm = min(ROW_TILE, m)
    ms = _mod_specs(mod, tm)
    row = lambda n: pl.BlockSpec((tm, n), lambda i: (i, 0))
    return pl.pallas_call(
        _mlp_kernel,
        grid=(m // tm,),
        in_specs=[row(D_MODEL), ms(0), ms(1), ms(2), _full(w1.shape), _full(w2.shape),
                  _full(g.shape), _full(b.shape)],
        out_specs=row(D_MODEL),
        out_shape=jax.ShapeDtypeStruct((m, D_MODEL), F32),
        scratch_shapes=[pltpu.VMEM((tm, D_MODEL), F32)],
        compiler_params=_cparams(1),
        name="mlp",
    )(x, mod, mod, mod, w1, w2, g, b)


def _alibi_slope(head):
    return 2.0 ** (-8.0 * (head + 1) / N_Q_HEADS)


def _softmax_sink(s, sink):
    m = jnp.maximum(jnp.max(s, axis=-1, keepdims=True), sink)
    e = jnp.exp(s - m)
    den = jnp.sum(e, axis=-1, keepdims=True) + jnp.exp(sink - m)
    return e / den


def _band_attn_kernel(sinks_ref, q_ref, kp_ref, kc_ref, vp_ref, vc_ref, o_ref):
    i = pl.program_id(0)
    blk = ATTN_BLOCK
    kk = jnp.concatenate([kp_ref[...], kc_ref[...]], axis=0).astype(BF16)
    vv = jnp.concatenate([vp_ref[...], vc_ref[...]], axis=0).astype(BF16)
    r = lax.broadcasted_iota(jnp.int32, (blk, 2 * blk), 0)
    c = lax.broadcasted_iota(jnp.int32, (blk, 2 * blk), 1)
    dist = blk + r - c
    valid = (dist >= 0) & (dist <= WINDOW) & ((c >= blk) | (i > 0))
    distf = dist.astype(F32)
    head_of_lane = lax.broadcasted_iota(jnp.int32, (blk, NKV), 1) // HEAD_DIM
    for g in range(GROUP):
        qg = q_ref[:, g * NKV:(g + 1) * NKV]
        qm = jnp.concatenate(
            [jnp.where(head_of_lane == j, qg, jnp.zeros_like(qg)) for j in range(N_KV_HEADS)],
            axis=0)
        s_all = lax.dot_general(qm, kk, (((1,), (1,)), ((), ())),
                                preferred_element_type=F32)
        ps = []
        for j in range(N_KV_HEADS):
            head = j * GROUP + g
            s = s_all[j * blk:(j + 1) * blk] * (HEAD_DIM ** -0.5) - _alibi_slope(head) * distf
            s = jnp.where(valid, s, NEG_BIG)
            ps.append(_softmax_sink(s, sinks_ref[head]).astype(BF16))
        pv = jnp.dot(jnp.concatenate(ps, axis=0), vv, preferred_element_type=F32)
        og = jnp.zeros((blk, NKV), F32)
        for j in range(N_KV_HEADS):
            og = og + jnp.where(head_of_lane == j, pv[j * blk:(j + 1) * blk], 0.0)
        o_ref[:, g * NKV:(g + 1) * NKV] = og.astype(BF16)


def _band_attn(q, k, v, sinks):
    m = q.shape[0]
    blk = ATTN_BLOCK
    cur = lambda n: pl.BlockSpec((blk, n), lambda i: (i, 0))
    prev = lambda n: pl.BlockSpec((blk, n), lambda i: (jnp.maximum(i - 1, 0), 0))
    return pl.pallas_call(
        _band_attn_kernel,
        grid=(m // blk,),
        in_specs=[pl.BlockSpec(memory_space=pltpu.SMEM), cur(NQ), prev(NKV), cur(NKV),
                  prev(NKV), cur(NKV)],
        out_specs=cur(NQ),
        out_shape=jax.ShapeDtypeStruct((m, NQ), BF16),
        compiler_params=_cparams(1),
        name="band_attn",
    )(sinks, q, k, k, v, v)


def _dec_attn_kernel(sinks_ref, q_ref, kn_ref, vn_ref, ck_ref, cv_ref,
                     o_ref, nk_ref, nv_ref, kpad_ref, vpad_ref):
    n_new = kn_ref.shape[1]
    win = ck_ref.shape[1]
    n_rows = N_KV_HEADS * n_new * GROUP
    n_keys = kpad_ref.shape[0]
    row = lax.broadcasted_iota(jnp.int32, (n_rows, n_keys), 0)
    col = lax.broadcasted_iota(jnp.int32, (n_rows, n_keys), 1)
    j_r = row // (n_new * GROUP)
    t_r = (row // GROUP) % n_new
    g_r = row % GROUP
    h_r = j_r * GROUP + g_r
    slope = jnp.exp2(-8.0 * (h_r + 1).astype(F32) / N_Q_HEADS)
    sink = jnp.zeros((n_rows, 1), F32)
    h_col = h_r[:, 0:1]
    for h in range(N_Q_HEADS):
        sink = jnp.where(h_col == h, sinks_ref[h], sink)
    dist = t_r + win - col
    valid = (dist >= 0) & (dist <= WINDOW)
    bias = slope * dist.astype(F32)
    head_of_lane = lax.broadcasted_iota(jnp.int32, (n_new * GROUP, NKV), 1) // HEAD_DIM

    kpad_ref[...] = jnp.zeros_like(kpad_ref)
    vpad_ref[...] = jnp.zeros_like(vpad_ref)

    def per_seq(s, carry):
        kn = kn_ref[s]
        vn = vn_ref[s]
        kpad_ref[0:win, :] = ck_ref[s]
        kpad_ref[win:win + n_new, :] = kn
        vpad_ref[0:win, :] = cv_ref[s]
        vpad_ref[win:win + n_new, :] = vn
        qs = q_ref[s]
        qbd = jnp.concatenate(
            [jnp.where(head_of_lane == j, qs, jnp.zeros_like(qs)) for j in range(N_KV_HEADS)],
            axis=0)
        sc = lax.dot_general(qbd, kpad_ref[...].astype(BF16), (((1,), (1,)), ((), ())),
                             preferred_element_type=F32)
        sc = jnp.where(valid, sc * (HEAD_DIM ** -0.5) - bias, NEG_BIG)
        p = _softmax_sink(sc, sink).astype(BF16)
        pv = jnp.dot(p, vpad_ref[...].astype(BF16), preferred_element_type=F32)
        rows = n_new * GROUP
        o = jnp.zeros((rows, NKV), F32)
        for j in range(N_KV_HEADS):
            o = o + jnp.where(head_of_lane == j, pv[j * rows:(j + 1) * rows], 0.0)
        o_ref[s] = o.astype(BF16)
        nk_ref[s] = kpad_ref[n_new:win + n_new, :]
        nv_ref[s] = vpad_ref[n_new:win + n_new, :]
        return carry

    lax.fori_loop(0, q_ref.shape[0], per_seq, 0)


def _dec_attn(q, k_new, v_new, cache_k, cache_v, sinks):
    nseq, win = cache_k.shape[0], cache_k.shape[1]
    n_new = k_new.shape[1]
    n_keys = 2 * LANES
    assert win + n_new <= n_keys
    sb = SEQ_BLOCK
    blk = lambda a: pl.BlockSpec((sb,) + a.shape[1:], lambda i: (i, 0, 0))
    return pl.pallas_call(
        _dec_attn_kernel,
        grid=(nseq // sb,),
        in_specs=[pl.BlockSpec(memory_space=pltpu.SMEM), blk(q), blk(k_new), blk(v_new),
                  blk(cache_k), blk(cache_v)],
        out_specs=[blk(q), blk(cache_k), blk(cache_v)],
        out_shape=[
            jax.ShapeDtypeStruct(q.shape, BF16),
            jax.ShapeDtypeStruct(cache_k.shape, F32),
            jax.ShapeDtypeStruct(cache_v.shape, F32),
        ],
        scratch_shapes=[pltpu.VMEM((n_keys, NKV), F32), pltpu.VMEM((n_keys, NKV), F32)],
        compiler_params=_cparams(1),
        name="dec_attn",
    )(sinks, q, k_new, v_new, cache_k, cache_v)


def _split3(x):
    hi = x.astype(BF16)
    r1 = x - hi.astype(F32)
    mid = r1.astype(BF16)
    lo = (r1 - mid.astype(F32)).astype(BF16)
    return hi, mid, lo


def _gla_out(o, r, norm_g):
    ms = jnp.mean(o * o, axis=-1, keepdims=True)
    o = o * lax.rsqrt(ms + LN_EPS) * norm_g
    return o * (r * jax.nn.sigmoid(r))


def _gla_prompt_kernel(q_ref, k_ref, lg_ref, v_ref, r_ref, ng_ref, o_ref, s_out_ref, s_ref):
    i = pl.program_id(0)
    t_rows = q_ref.shape[0]
    ch = GLA_CHUNK
    n_ch = t_rows // ch
    dot = functools.partial(jnp.dot, preferred_element_type=F32)

    @pl.when(i == 0)
    def _():
        s_ref[...] = jnp.zeros_like(s_ref)

    row = lax.broadcasted_iota(jnp.int32, (t_rows, t_rows), 0)
    col = lax.broadcasted_iota(jnp.int32, (t_rows, t_rows), 1)
    causal = ((row // ch) == (col // ch)) & (col <= row)
    tril = jnp.where(causal, 1.0, 0.0).astype(BF16)

    hi, mid, lo = _split3(lg_ref[...])
    b = dot(tril, hi) + dot(tril, mid) + dot(tril, lo)
    b_end = jnp.concatenate(
        [jnp.broadcast_to(b[(c + 1) * ch - 1:(c + 1) * ch, :], (ch, NK)) for c in range(n_ch)],
        axis=0)
    q = q_ref[...]
    k = k_ref[...]
    qd = (q * jnp.exp(b)).astype(BF16)
    kd = (k * jnp.exp(-b)).astype(BF16)
    kdec = k * jnp.exp(b_end - b)
    dec_rows = jnp.concatenate(
        [jnp.exp(b[(c + 1) * ch - 1:(c + 1) * ch, :]) for c in range(n_ch)]
        + [jnp.zeros((LANES - n_ch, NK), F32)], axis=0)
    chunk_of_col = lax.broadcasted_iota(jnp.int32, (GLA_DK, t_rows), 1) // ch

    for h in range(GLA_HEADS):
        ks = slice(h * GLA_DK, (h + 1) * GLA_DK)
        vs = slice(h * GLA_DV, (h + 1) * GLA_DV)
        v_h = v_ref[:, vs]
        a = lax.dot_general(qd[:, ks], kd[:, ks], (((1,), (1,)), ((), ())),
                            preferred_element_type=F32)
        a = jnp.where(causal, a, 0.0).astype(BF16)
        o = dot(a, v_h)
        kdec_t = kdec[:, ks].T.astype(BF16)
        u_all = dot(
            jnp.concatenate(
                [jnp.where(chunk_of_col == c, kdec_t, jnp.zeros_like(kdec_t))
                 for c in range(n_ch)], axis=0),
            v_h)
        dec_t = dec_rows[:, ks].T
        s = s_ref[h]
        o_inter = []
        for c in range(n_ch):
            o_inter.append(dot(qd[c * ch:(c + 1) * ch, ks], s.astype(BF16)))
            s = dec_t[:, c:c + 1] * s + u_all[c * GLA_DK:(c + 1) * GLA_DK]
        s_ref[h] = s
        o = o + jnp.concatenate(o_inter, axis=0)
        o_ref[:, vs] = _gla_out(o, r_ref[:, vs].astype(F32), ng_ref[...]).astype(o_ref.dtype)

    s_out_ref[...] = s_ref[...]


def _gla_prompt(q, k, lg, v, r, norm_g):
    m = q.shape[0]
    tt = GLA_TILE
    row = lambda n: pl.BlockSpec((tt, n), lambda i: (i, 0))
    state = (GLA_HEADS, GLA_DK, GLA_DV)
    return pl.pallas_call(
        _gla_prompt_kernel,
        grid=(m // tt,),
        in_specs=[row(NK), row(NK), row(NK), row(NV), row(NV), _full(norm_g.shape)],
        out_specs=[row(NV), _full(state)],
        out_shape=[jax.ShapeDtypeStruct((m, NV), BF16), jax.ShapeDtypeStruct(state, F32)],
        scratch_shapes=[pltpu.VMEM(state, F32)],
        compiler_params=_cparams(1),
        name="gla_prompt",
    )(q, k, lg, v, r, norm_g)


def _gla_step_kernel(q_ref, k_ref, lg_ref, v_ref, r_ref, ng_ref, s0_ref,
                     o_ref, s1_ref, z_ref, v_pad_ref, q_pad_ref):
    n_new = q_ref.shape[1]
    dot = functools.partial(jnp.dot, preferred_element_type=F32)
    rowi = lax.broadcasted_iota(jnp.int32, (n_new, NK), 0)
    rowv = lax.broadcasted_iota(jnp.int32, (n_new, GLA_DV), 0)

    z_ref[...] = jnp.zeros_like(z_ref)
    v_pad_ref[...] = jnp.zeros_like(v_pad_ref)
    q_pad_ref[...] = jnp.zeros_like(q_pad_ref)

    def per_seq(s, carry):
        lg = lg_ref[s]
        b = jnp.zeros_like(lg)
        for u in range(n_new):
            b = b + jnp.where(rowi >= u, jnp.broadcast_to(lg[u:u + 1, :], lg.shape), 0.0)
        b_end = b[n_new - 1:n_new, :]
        q = q_ref[s]
        k = k_ref[s]
        qd = q * jnp.exp(b)
        kd = k * jnp.exp(-b)
        kdec = k * jnp.exp(b_end - b)
        dec = jnp.exp(b_end)
        v = v_ref[s]
        r = r_ref[s]
        for h in range(GLA_HEADS):
            ks = slice(h * GLA_DK, (h + 1) * GLA_DK)
            vs = slice(h * GLA_DV, (h + 1) * GLA_DV)
            v_h = v[:, vs]
            qd_h = qd[:, ks]
            o = jnp.zeros((n_new, GLA_DV), F32)
            for u in range(n_new):
                a_u = jnp.sum(qd_h * kd[u:u + 1, ks], axis=-1, keepdims=True)
                o = o + jnp.where(rowv >= u, a_u * v_h[u:u + 1, :], 0.0)
            s0 = s0_ref[s, h]
            q_pad_ref[0:n_new, :] = qd_h
            o = o + dot(q_pad_ref[...], s0)[0:n_new]
            z_ref[0:n_new, :] = kdec[:, ks]
            z_ref[n_new:n_new + 1, :] = dec[:, ks]
            z_t = z_ref[...].T
            v_pad_ref[0:n_new, :] = v_h
            s1_ref[s, h] = z_t[:, n_new:n_new + 1] * s0 + dot(z_t, v_pad_ref[...])
            o_ref[s, :, vs] = _gla_out(o, r[:, vs], ng_ref[...])
        return carry

    lax.fori_loop(0, q_ref.shape[0], per_seq, 0)


def _gla_step(q, k, lg, v, r, norm_g, s0):
    nseq, n_new = q.shape[0], q.shape[1]
    assert n_new <= GLA_CHUNK and n_new < 8
    sb = SEQ_BLOCK
    blk3 = lambda a: pl.BlockSpec((sb,) + a.shape[1:], lambda i: (i, 0, 0))
    blk4 = lambda a: pl.BlockSpec((sb,) + a.shape[1:], lambda i: (i, 0, 0, 0))
    return pl.pallas_call(
        _gla_step_kernel,
        grid=(nseq // sb,),
        in_specs=[blk3(q), blk3(k), blk3(lg), blk3(v), blk3(r), _full(norm_g.shape), blk4(s0)],
        out_specs=[blk3(v), blk4(s0)],
        out_shape=[jax.ShapeDtypeStruct(v.shape, F32), jax.ShapeDtypeStruct(s0.shape, F32)],
        scratch_shapes=[pltpu.VMEM((LANES, GLA_DK), F32), pltpu.VMEM((LANES, GLA_DV), F32),
                        pltpu.VMEM((8, GLA_DK), F32)],
        compiler_params=_cparams(1),
        name="gla_step",
    )(q, k, lg, v, r, norm_g, s0)


def _trunk(x, mods, weights, attn_fn, gla_fn):
    (w_attn_in, w_attn_out, w_gla_main, w_gla_gd, w_gla_gu, b_gate, norm_g, w_gla_out,
     w1, w2, ln_g, ln_b) = weights
    q, k, v = _attn_proj(x, mods[0], w_attn_in)
    o, attn_state = attn_fn(q, k, v)
    x = _out_ln(o, x, mods[0], w_attn_out, ln_g[0, 0], ln_b[0, 0])
    x = _mlp(x, mods[1], w1[0], w2[0], ln_g[0, 1], ln_b[0, 1])
    q, k, lg, v, r = gla_fn[0](x, mods[2], w_gla_main, w_gla_gd, w_gla_gu, b_gate)
    o, gla_state = gla_fn[1](q, k, lg, v, r, norm_g)
    x = _out_ln(o, x, mods[2], w_gla_out, ln_g[1, 0], ln_b[1, 0])
    x = _mlp(x, mods[3], w1[1], w2[1], ln_g[1, 1], ln_b[1, 1])
    return x, attn_state, gla_state


def kernel(x_prompt, x_sample, cache_k, cache_v, state_gla, c_prompt, c_sample, w_mod, b_mod,
           ln_g, ln_b, attn_w_in, attn_w_out, attn_sinks, gla_w_in, gla_w_gate_up, gla_b_gate,
           gla_norm_g, gla_w_out, mlp_w1, mlp_w2):
    assert x_prompt.shape[0] == 1 and w_mod.shape[0] == DEPTH == 2
    seq = x_prompt.shape[1]
    nseq, n_new = x_sample.shape[0], x_sample.shape[1]
    win = cache_k.shape[2]

    wq = attn_w_in[0][:, :NQ].reshape(D_MODEL, N_KV_HEADS, GROUP, HEAD_DIM)
    wq = wq.transpose(0, 2, 1, 3).reshape(D_MODEL, NQ)
    w_attn_in = jnp.concatenate([wq, attn_w_in[0][:, NQ:]], axis=1).astype(BF16)
    w_attn_out = attn_w_out[0].reshape(N_KV_HEADS, GROUP, HEAD_DIM, D_MODEL)
    w_attn_out = w_attn_out.transpose(1, 0, 2, 3).reshape(NQ, D_MODEL).astype(BF16)
    n_main = 2 * NK + 2 * NV
    w_gla_main = gla_w_in[0][:, :n_main].astype(BF16)
    w_gla_gd = jnp.pad(gla_w_in[0][:, n_main:], ((0, 0), (0, LANES - GLA_GATE_RANK))).astype(BF16)
    w_gla_gu = jnp.pad(gla_w_gate_up[0], ((0, LANES - GLA_GATE_RANK), (0, 0))).astype(BF16)
    weights = (w_attn_in, w_attn_out, w_gla_main, w_gla_gd, w_gla_gu,
               gla_b_gate[0].reshape(1, NK), gla_norm_g[0].reshape(1, GLA_DV),
               gla_w_out[0].astype(BF16), mlp_w1.astype(BF16), mlp_w2.astype(BF16),
               ln_g.reshape(DEPTH, 2, 1, D_MODEL), ln_b.reshape(DEPTH, 2, 1, D_MODEL))
    sinks = attn_sinks[0]

    pad_rows = (-(nseq + 1)) % 8
    c_all = jnp.concatenate([c_sample, c_prompt, jnp.zeros((pad_rows, D_MODEL), F32)], axis=0)
    mod_all = _adaln_all(c_all, w_mod.reshape(2 * DEPTH, D_MODEL, 3 * D_MODEL),
                         b_mod.reshape(2 * DEPTH, 1, 3 * D_MODEL))
    mods_p = [mod_all[p, nseq:nseq + 1] for p in range(2 * DEPTH)]
    mods_s = [jnp.repeat(mod_all[p, :nseq], n_new, axis=0) for p in range(2 * DEPTH)]

    win_p = min(WINDOW, seq)

    def attn_prompt(q, k, v):
        return _band_attn(q, k, v, sinks), (k[seq - win_p:], v[seq - win_p:])

    def gla_prompt(q, k, lg, v, r, norm_g):
        return _gla_prompt(q, k, lg, v, r, norm_g)

    gla_proj_p = functools.partial(_gla_proj, vr_dtype=BF16)
    y_p, (k_p, v_p), s_p = _trunk(x_prompt[0], mods_p, weights, attn_prompt,
                                  (gla_proj_p, gla_prompt))

    def attn_sample(q, k, v):
        o, nk, nv = _dec_attn(
            q.reshape(nseq, n_new * GROUP, NKV), k.reshape(nseq, n_new, NKV),
            v.reshape(nseq, n_new, NKV), cache_k[0].reshape(nseq, win, NKV),
            cache_v[0].reshape(nseq, win, NKV), sinks)
        return o.reshape(nseq * n_new, NQ), (nk, nv)

    def gla_sample(q, k, lg, v, r, norm_g):
        three = lambda a: a.reshape(nseq, n_new, a.shape[-1])
        o, s1 = _gla_step(three(q), three(k), three(lg), three(v), three(r), norm_g,
                          state_gla[0])
        return o.reshape(nseq * n_new, NV), s1

    gla_proj_s = functools.partial(_gla_proj, vr_dtype=F32)
    y_s, (k_s, v_s), s_s = _trunk(x_sample.reshape(nseq * n_new, D_MODEL), mods_s, weights,
                                  attn_sample, (gla_proj_s, gla_sample))

    kv_shape_p = (1, 1, win_p, N_KV_HEADS, HEAD_DIM)
    kv_shape_s = (1, nseq, win, N_KV_HEADS, HEAD_DIM)
    return (y_p[None], y_s.reshape(nseq, n_new, D_MODEL),
            k_p.reshape(kv_shape_p), v_p.reshape(kv_shape_p), s_p[None, None],
            k_s.reshape(kv_shape_s), v_s.reshape(kv_shape_s), s_s[None])
```

```python
import functools

import jax
import jax.numpy as jnp
from jax import lax
from jax.experimental import pallas as pl
from jax.experimental.pallas import tpu as pltpu

F32 = jnp.float32
BF16 = jnp.bfloat16

D_MODEL = 1024
DEPTH = 2
HEAD_DIM = 64
N_Q_HEADS = 16
N_KV_HEADS = 4
GROUP = 4
WINDOW = 128
ATTN_BLOCK = 128
GLA_HEADS = 4
GLA_DK = 128
GLA_DV = 256
GLA_GATE_RANK = 16
GLA_TAU = 16.0
GLA_CHUNK = 64
D_FF = 4 * D_MODEL
ALPHA = (2.0 * DEPTH) ** 0.25
LN_EPS = 1e-5

NQ = N_Q_HEADS * HEAD_DIM
NKV = N_KV_HEADS * HEAD_DIM
NK = GLA_HEADS * GLA_DK
NV = GLA_HEADS * GLA_DV
LANES = 128
SUBLANES = 8
NEG_BIG = -1e30

ROW_TILE = 512
FF_CHUNK = 1024
GLA_TILE = 256
SEQ_BLOCK = 8
SEQ_UNROLL = 2
VMEM_LIMIT = 56 * 1024 * 1024


def _cparams(n_axes):
    return pltpu.CompilerParams(
        dimension_semantics=("arbitrary",) * n_axes,
        vmem_limit_bytes=VMEM_LIMIT,
    )


def _full(shape):
    zeros = (0,) * len(shape)
    return pl.BlockSpec(shape, lambda *_: zeros)


def _layer(arr, idx):
    tail = (0,) * (arr.ndim - 1)
    return pl.BlockSpec((None,) + arr.shape[1:], lambda *_: (idx,) + tail,
                        pipeline_mode=pl.Buffered(1))


def _row_spec(tm, n):
    return pl.BlockSpec((tm, n), lambda i: (i, 0))


class _Mod:
    def __init__(self, arr, p, per_row, row0=0):
        self.arr, self.p, self.per_row, self.row0 = arr, p, per_row, row0

    def spec(self, tm, col):
        p = self.p
        if self.per_row:
            return pl.BlockSpec((None, tm, D_MODEL), lambda i: (p, i, col))
        blk = self.row0 // SUBLANES
        return pl.BlockSpec((None, SUBLANES, D_MODEL), lambda i: (p, blk, col))


def _mod_rows(ref, tm):
    return ref[...] if ref.shape[0] == tm else ref[0:1, :]


def _modulate(x, shift, scale):
    return x * (1.0 + scale) + shift


def _res_ln(x, gate, o, g, b):
    y = ALPHA * x + gate * o
    mu = jnp.mean(y, axis=-1, keepdims=True)
    yc = y - mu
    var = jnp.mean(yc * yc, axis=-1, keepdims=True)
    return yc * lax.rsqrt(var + LN_EPS) * g + b


def _mod_kernel(c_ref, w_ref, b_ref, o_ref):
    c = c_ref[...]
    a = (c * jax.nn.sigmoid(c)).astype(BF16)
    o_ref[...] = jnp.dot(a, w_ref[...].astype(BF16), preferred_element_type=F32) + b_ref[...]


def _adaln_all(c_all, w_mod, b_mod):
    rows = c_all.shape[0]
    tn = 1024
    return pl.pallas_call(
        _mod_kernel,
        grid=(4, 3 * D_MODEL // tn),
        in_specs=[
            pl.BlockSpec((rows, D_MODEL), lambda p, n: (0, 0)),
            pl.BlockSpec((None, D_MODEL, tn), lambda p, n: (p, 0, n)),
            pl.BlockSpec((None, 1, tn), lambda p, n: (p, 0, n)),
        ],
        out_specs=pl.BlockSpec((None, rows, tn), lambda p, n: (p, 0, n)),
        out_shape=jax.ShapeDtypeStruct((4, rows, 3 * D_MODEL), F32),
        compiler_params=_cparams(2),
        name="adaln_mod",
    )(c_all, w_mod, b_mod)


def _attn_proj_kernel(x_ref, sh_ref, sc_ref, w_ref, q_ref, k_ref, v_ref):
    tm = x_ref.shape[0]
    h = _modulate(x_ref[...], _mod_rows(sh_ref, tm), _mod_rows(sc_ref, tm)).astype(BF16)
    q = jnp.dot(h, w_ref[:, 0:NQ], preferred_element_type=F32)
    q_ref[...] = (q * (HEAD_DIM ** -0.5)).astype(BF16)
    k_ref[...] = jnp.dot(h, w_ref[:, NQ:NQ + NKV], preferred_element_type=F32)
    v_ref[...] = jnp.dot(h, w_ref[:, NQ + NKV:NQ + 2 * NKV], preferred_element_type=F32)


def _attn_proj(x, mod, w_in):
    m = x.shape[0]
    tm = min(ROW_TILE, m)
    row = functools.partial(_row_spec, tm)
    return pl.pallas_call(
        _attn_proj_kernel,
        grid=(m // tm,),
        in_specs=[row(D_MODEL), mod.spec(tm, 0), mod.spec(tm, 1), _full(w_in.shape)],
        out_specs=[row(NQ), row(NKV), row(NKV)],
        out_shape=[
            jax.ShapeDtypeStruct((m, NQ), BF16),
            jax.ShapeDtypeStruct((m, NKV), F32),
            jax.ShapeDtypeStruct((m, NKV), F32),
        ],
        compiler_params=_cparams(1),
        name="attn_proj",
    )(x, mod.arr, mod.arr, w_in)


def _gla_proj_kernel(x_ref, sh_ref, sc_ref, w_ref, wgd_ref, wgu_ref, bg_ref,
                     q_ref, k_ref, lg_ref, v_ref, r_ref):
    tm = x_ref.shape[0]
    h = _modulate(x_ref[...], _mod_rows(sh_ref, tm), _mod_rows(sc_ref, tm)).astype(BF16)
    dot = functools.partial(jnp.dot, preferred_element_type=F32)
    q_ref[...] = dot(h, w_ref[:, 0:NK]) * (GLA_DK ** -0.5)
    k_ref[...] = dot(h, w_ref[:, NK:2 * NK])
    v_ref[...] = dot(h, w_ref[:, 2 * NK:2 * NK + NV]).astype(v_ref.dtype)
    r_ref[...] = dot(h, w_ref[:, 2 * NK + NV:2 * NK + 2 * NV]).astype(r_ref.dtype)
    gdown = dot(h, wgd_ref[...])
    pre = dot(gdown.astype(BF16), wgu_ref[...]) + bg_ref[...]
    log_sig = jnp.minimum(pre, 0.0) - jnp.log1p(jnp.exp(-jnp.abs(pre)))
    lg_ref[...] = log_sig / GLA_TAU


def _gla_proj(x, mod, w_main, w_gd, w_gu, b_gate, vr_dtype):
    m = x.shape[0]
    tm = min(ROW_TILE, m)
    row = functools.partial(_row_spec, tm)
    return pl.pallas_call(
        _gla_proj_kernel,
        grid=(m // tm,),
        in_specs=[row(D_MODEL), mod.spec(tm, 0), mod.spec(tm, 1), _full(w_main.shape),
                  _full(w_gd.shape), _full(w_gu.shape), _full(b_gate.shape)],
        out_specs=[row(NK), row(NK), row(NK), row(NV), row(NV)],
        out_shape=[
            jax.ShapeDtypeStruct((m, NK), F32),
            jax.ShapeDtypeStruct((m, NK), F32),
            jax.ShapeDtypeStruct((m, NK), F32),
            jax.ShapeDtypeStruct((m, NV), vr_dtype),
            jax.ShapeDtypeStruct((m, NV), vr_dtype),
        ],
        compiler_params=_cparams(1),
        name="gla_proj",
    )(x, mod.arr, mod.arr, w_main, w_gd, w_gu, b_gate)


def _mix_mlp_kernel(a_ref, x_ref, gt0_ref, sh_ref, sc_ref, gt1_ref, wo_ref, w1_ref, w2_ref,
                    g0_ref, b0_ref, g1_ref, b1_ref, o_ref, acc_ref):
    tm = x_ref.shape[0]
    o = jnp.dot(a_ref[...].astype(BF16), wo_ref[...], preferred_element_type=F32)
    x1 = _res_ln(x_ref[...], _mod_rows(gt0_ref, tm), o, g0_ref[...], b0_ref[...])
    h = _modulate(x1, _mod_rows(sh_ref, tm), _mod_rows(sc_ref, tm)).astype(BF16)
    for c in range(D_FF // FF_CHUNK):
        cols = slice(c * FF_CHUNK, (c + 1) * FF_CHUNK)
        a = jnp.dot(h, w1_ref[:, cols], preferred_element_type=F32)
        a = jnp.square(jnp.maximum(a, 0.0)).astype(BF16)
        d = jnp.dot(a, w2_ref[cols, :], preferred_element_type=F32)
        if c == 0:
            acc_ref[...] = d
        else:
            acc_ref[...] += d
    o_ref[...] = _res_ln(x1, _mod_rows(gt1_ref, tm), acc_ref[...], g1_ref[...], b1_ref[...])


def _mix_mlp(a, x, mod_mix, mod_mlp, w_out, w1, w2, ln_g, ln_b, layer):
    m = x.shape[0]
    tm = min(ROW_TILE, m)
    row = functools.partial(_row_spec, tm)
    return pl.pallas_call(
        _mix_mlp_kernel,
        grid=(m // tm,),
        in_specs=[row(a.shape[1]), row(D_MODEL), mod_mix.spec(tm, 2), mod_mlp.spec(tm, 0),
                  mod_mlp.spec(tm, 1), mod_mlp.spec(tm, 2), _full(w_out.shape),
                  _layer(w1, layer), _layer(w2, layer),
                  _layer(ln_g, 2 * layer), _layer(ln_b, 2 * layer),
                  _layer(ln_g, 2 * layer + 1), _layer(ln_b, 2 * layer + 1)],
        out_specs=row(D_MODEL),
        out_shape=jax.ShapeDtypeStruct((m, D_MODEL), F32),
        scratch_shapes=[pltpu.VMEM((tm, D_MODEL), F32)],
        compiler_params=_cparams(1),
        name="mix_mlp",
    )(a, x, mod_mix.arr, mod_mlp.arr, mod_mlp.arr, mod_mlp.arr, w_out, w1, w2,
      ln_g, ln_b, ln_g, ln_b)


def _alibi_slope(head):
    return 2.0 ** (-8.0 * (head + 1) / N_Q_HEADS)


def _softmax_sink(s, sink):
    m = jnp.maximum(jnp.max(s, axis=-1, keepdims=True), sink)
    e = jnp.exp(s - m)
    den = jnp.sum(e, axis=-1, keepdims=True) + jnp.exp(sink - m)
    return e / den


def _band_attn_kernel(sinks_ref, q_ref, kp_ref, kc_ref, vp_ref, vc_ref, o_ref, bias_ref):
    i = pl.program_id(0)
    blk = ATTN_BLOCK

    @pl.when(i <= 1)
    def _():
        r = lax.broadcasted_iota(jnp.int32, (blk, 2 * blk), 0)
        c = lax.broadcasted_iota(jnp.int32, (blk, 2 * blk), 1)
        dist = blk + r - c
        valid = (dist >= 0) & (dist <= WINDOW) & ((c >= blk) | (i > 0))
        distf = dist.astype(F32)
        for head in range(N_Q_HEADS):
            bias_ref[head] = jnp.where(valid, -_alibi_slope(head) * distf, NEG_BIG)

    kk = jnp.concatenate([kp_ref[...], kc_ref[...]], axis=0).astype(BF16)
    vv = jnp.concatenate([vp_ref[...], vc_ref[...]], axis=0).astype(BF16)
    head_of_lane_q = lax.broadcasted_iota(jnp.int32, (blk, NKV), 1) // HEAD_DIM
    head_of_lane_v = lax.broadcasted_iota(jnp.int32, (2 * blk, NKV), 1) // HEAD_DIM
    v_bd = jnp.concatenate(
        [jnp.where(head_of_lane_v == j, vv, jnp.zeros_like(vv)) for j in range(N_KV_HEADS)],
        axis=0)
    p_rows = []
    for g in range(GROUP):
        qg = q_ref[:, g * NKV:(g + 1) * NKV]
        qm = jnp.concatenate(
            [jnp.where(head_of_lane_q == j, qg, jnp.zeros_like(qg))
             for j in range(N_KV_HEADS)], axis=0)
        s_all = lax.dot_general(qm, kk, (((1,), (1,)), ((), ())),
                                preferred_element_type=F32)
        ps = []
        for j in range(N_KV_HEADS):
            head = j * GROUP + g
            s = s_all[j * blk:(j + 1) * blk] + bias_ref[head]
            ps.append(_softmax_sink(s, sinks_ref[head]).astype(BF16))
        p_rows.append(jnp.concatenate(ps, axis=1))
    o_all = jnp.dot(jnp.concatenate(p_rows, axis=0), v_bd, preferred_element_type=F32)
    for g in range(GROUP):
        o_ref[:, g * NKV:(g + 1) * NKV] = o_all[g * blk:(g + 1) * blk].astype(BF16)


def _band_attn(q, k, v, sinks):
    m = q.shape[0]
    blk = ATTN_BLOCK
    cur = lambda n: pl.BlockSpec((blk, n), lambda i: (i, 0))
    prev = lambda n: pl.BlockSpec((blk, n), lambda i: (jnp.maximum(i - 1, 0), 0))
    return pl.pallas_call(
        _band_attn_kernel,
        grid=(m // blk,),
        in_specs=[pl.BlockSpec(memory_space=pltpu.SMEM), cur(NQ), prev(NKV), cur(NKV),
                  prev(NKV), cur(NKV)],
        out_specs=cur(NQ),
        out_shape=jax.ShapeDtypeStruct((m, NQ), BF16),
        scratch_shapes=[pltpu.VMEM((N_Q_HEADS, blk, 2 * blk), F32)],
        compiler_params=_cparams(1),
        name="band_attn",
    )(sinks, q, k, k, v, v)


def _dec_attn_kernel(sinks_ref, q_ref, kn_ref, vn_ref, ck_ref, cv_ref,
                     o_ref, nk_ref, nv_ref, zk_ref, zv_ref):
    n_new = kn_ref.shape[1]
    win = ck_ref.shape[2]
    rows = n_new * GROUP
    n_rows = N_KV_HEADS * rows
    keep = win - n_new
    row = lax.broadcasted_iota(jnp.int32, (n_rows, 2 * win), 0)
    col = lax.broadcasted_iota(jnp.int32, (n_rows, 2 * win), 1)
    j_r = row // rows
    t_r = (row // GROUP) % n_new
    g_r = row % GROUP
    h_r = j_r * GROUP + g_r
    slope = jnp.exp2(-8.0 * (h_r + 1).astype(F32) / N_Q_HEADS)
    sink = jnp.zeros((n_rows, 1), F32)
    h_col = h_r[:, 0:1]
    for h in range(N_Q_HEADS):
        sink = jnp.where(h_col == h, sinks_ref[h], sink)
    is_key = (col < win) | (col >= win + keep)
    frame = jnp.where(col < win, col, col - keep)
    dist = t_r + win - frame
    valid = is_key & (dist >= 0) & (dist <= WINDOW)
    bias = jnp.where(valid, -slope * dist.astype(F32), NEG_BIG)
    head_of_lane = lax.broadcasted_iota(jnp.int32, (rows, NKV), 1) // HEAD_DIM
    lane_w = lax.broadcasted_iota(jnp.int32, (NKV, win), 1)

    @pl.when(pl.program_id(0) == 0)
    def _():
        zk_ref[...] = jnp.zeros_like(zk_ref)
        zv_ref[...] = jnp.zeros_like(zv_ref)

    def per_seq(s, u):
        k_t = ck_ref[s]
        v_t = cv_ref[s]
        zk_ref[u, keep:win, :] = kn_ref[s]
        zv_ref[u, keep:win, :] = vn_ref[s]
        zk_t = zk_ref[u].T
        zv_t = zv_ref[u].T
        nk_ref[s] = jnp.where(lane_w < keep, pltpu.roll(k_t, keep, 1), zk_t)
        nv_ref[s] = jnp.where(lane_w < keep, pltpu.roll(v_t, keep, 1), zv_t)
        keys = jnp.concatenate([k_t, zk_t], axis=1).astype(BF16)
        vals = jnp.concatenate([v_t, zv_t], axis=1).astype(BF16)
        qs = q_ref[s]
        qbd = jnp.concatenate(
            [jnp.where(head_of_lane == j, qs, jnp.zeros_like(qs)) for j in range(N_KV_HEADS)],
            axis=0)
        sc = jnp.dot(qbd, keys, preferred_element_type=F32) + bias
        p = _softmax_sink(sc, sink).astype(BF16)
        pv = lax.dot_general(p, vals, (((1,), (1,)), ((), ())), preferred_element_type=F32)
        o = jnp.zeros((rows, NKV), F32)
        for j in range(N_KV_HEADS):
            o = o + jnp.where(head_of_lane == j, pv[j * rows:(j + 1) * rows], 0.0)
        o_ref[s] = o.astype(BF16)

    def body(it, carry):
        for u in range(SEQ_UNROLL):
            per_seq(it * SEQ_UNROLL + u, u)
        return carry

    lax.fori_loop(0, q_ref.shape[0] // SEQ_UNROLL, body, 0)


def _dec_attn(q, k_new, v_new, cache_kt, cache_vt, sinks):
    nseq, win = cache_kt.shape[0], cache_kt.shape[2]
    n_new = k_new.shape[1]
    assert win == LANES and n_new < SUBLANES
    sb = SEQ_BLOCK
    blk = lambda a: pl.BlockSpec((sb,) + a.shape[1:], lambda i: (i, 0, 0))
    zshape = (SEQ_UNROLL, win, NKV)
    return pl.pallas_call(
        _dec_attn_kernel,
        grid=(nseq // sb,),
        in_specs=[pl.BlockSpec(memory_space=pltpu.SMEM), blk(q), blk(k_new), blk(v_new),
                  blk(cache_kt), blk(cache_vt)],
        out_specs=[blk(q), blk(cache_kt), blk(cache_vt)],
        out_shape=[
            jax.ShapeDtypeStruct(q.shape, BF16),
            jax.ShapeDtypeStruct(cache_kt.shape, F32),
            jax.ShapeDtypeStruct(cache_vt.shape, F32),
        ],
        scratch_shapes=[pltpu.VMEM(zshape, F32), pltpu.VMEM(zshape, F32)],
        compiler_params=_cparams(1),
        name="dec_attn",
    )(sinks, q, k_new, v_new, cache_kt, cache_vt)


def _split3(x):
    hi = x.astype(BF16)
    r1 = x - hi.astype(F32)
    mid = r1.astype(BF16)
    lo = (r1 - mid.astype(F32)).astype(BF16)
    return hi, mid, lo


def _gla_out(o, r, norm_g):
    ms = jnp.mean(o * o, axis=-1, keepdims=True)
    o = o * lax.rsqrt(ms + LN_EPS) * norm_g
    return o * (r * jax.nn.sigmoid(r))


def _gla_prompt_kernel(q_ref, k_ref, lg_ref, v_ref, r_ref, ng_ref, o_ref, s_out_ref, s_ref):
    i = pl.program_id(0)
    t_rows = q_ref.shape[0]
    ch = GLA_CHUNK
    n_ch = t_rows // ch
    dot = functools.partial(jnp.dot, preferred_element_type=F32)

    @pl.when(i == 0)
    def _():
        s_ref[...] = jnp.zeros_like(s_ref)

    row = lax.broadcasted_iota(jnp.int32, (t_rows, t_rows), 0)
    col = lax.broadcasted_iota(jnp.int32, (t_rows, t_rows), 1)
    causal = ((row // ch) == (col // ch)) & (col <= row)
    tril = jnp.where(causal, 1.0, 0.0).astype(BF16)

    hi, mid, lo = _split3(lg_ref[...])
    b = dot(tril, hi) + dot(tril, mid) + dot(tril, lo)
    b_end = jnp.concatenate(
        [jnp.broadcast_to(b[(c + 1) * ch - 1:(c + 1) * ch, :], (ch, NK)) for c in range(n_ch)],
        axis=0)
    q = q_ref[...]
    k = k_ref[...]
    qd = (q * jnp.exp(b)).astype(BF16)
    kd = (k * jnp.exp(-b)).astype(BF16)
    kdec = k * jnp.exp(b_end - b)
    dec_rows = jnp.concatenate(
        [jnp.exp(b[(c + 1) * ch - 1:(c + 1) * ch, :]) for c in range(n_ch)]
        + [jnp.zeros((LANES - n_ch, NK), F32)], axis=0)
    chunk_of_col = lax.broadcasted_iota(jnp.int32, (GLA_DK, t_rows), 1) // ch

    for h in range(GLA_HEADS):
        ks = slice(h * GLA_DK, (h + 1) * GLA_DK)
        vs = slice(h * GLA_DV, (h + 1) * GLA_DV)
        v_h = v_ref[:, vs]
        a = lax.dot_general(qd[:, ks], kd[:, ks], (((1,), (1,)), ((), ())),
                            preferred_element_type=F32)
        a = jnp.where(causal, a, 0.0).astype(BF16)
        o = dot(a, v_h)
        kdec_t = kdec[:, ks].T.astype(BF16)
        u_all = dot(
            jnp.concatenate(
                [jnp.where(chunk_of_col == c, kdec_t, jnp.zeros_like(kdec_t))
                 for c in range(n_ch)], axis=0),
            v_h)
        dec_t = dec_rows[:, ks].T
        s = s_ref[h]
        o_inter = []
        for c in range(n_ch):
            o_inter.append(dot(qd[c * ch:(c + 1) * ch, ks], s.astype(BF16)))
            s = dec_t[:, c:c + 1] * s + u_all[c * GLA_DK:(c + 1) * GLA_DK]
        s_ref[h] = s
        o = o + jnp.concatenate(o_inter, axis=0)
        o_ref[:, vs] = _gla_out(o, r_ref[:, vs].astype(F32), ng_ref[...]).astype(o_ref.dtype)

    s_out_ref[...] = s_ref[...]


def _gla_prompt(q, k, lg, v, r, norm_g):
    m = q.shape[0]
    tt = GLA_TILE
    row = functools.partial(_row_spec, tt)
    state = (GLA_HEADS, GLA_DK, GLA_DV)
    return pl.pallas_call(
        _gla_prompt_kernel,
        grid=(m // tt,),
        in_specs=[row(NK), row(NK), row(NK), row(NV), row(NV), _full(norm_g.shape)],
        out_specs=[row(NV), _full(state)],
        out_shape=[jax.ShapeDtypeStruct((m, NV), BF16), jax.ShapeDtypeStruct(state, F32)],
        scratch_shapes=[pltpu.VMEM(state, F32)],
        compiler_params=_cparams(1),
        name="gla_prompt",
    )(q, k, lg, v, r, norm_g)


def _gla_step_kernel(q_ref, k_ref, lg_ref, v_ref, r_ref, ng_ref, s0_ref,
                     o_ref, s1_ref, z_ref, v_pad_ref, q_pad_ref):
    n_new = q_ref.shape[1]
    dot = functools.partial(jnp.dot, preferred_element_type=F32)
    rowi = lax.broadcasted_iota(jnp.int32, (n_new, NK), 0)
    rowv = lax.broadcasted_iota(jnp.int32, (n_new, GLA_DV), 0)

    @pl.when(pl.program_id(0) == 0)
    def _():
        z_ref[...] = jnp.zeros_like(z_ref)
        v_pad_ref[...] = jnp.zeros_like(v_pad_ref)
        q_pad_ref[...] = jnp.zeros_like(q_pad_ref)

    def per_seq(s, u):
        lg = lg_ref[s]
        b = jnp.zeros_like(lg)
        for t in range(n_new):
            b = b + jnp.where(rowi >= t, jnp.broadcast_to(lg[t:t + 1, :], lg.shape), 0.0)
        b_end = b[n_new - 1:n_new, :]
        q = q_ref[s]
        k = k_ref[s]
        qd = q * jnp.exp(b)
        kd = k * jnp.exp(-b)
        kdec = k * jnp.exp(b_end - b)
        dec = jnp.exp(b_end)
        v = v_ref[s]
        r = r_ref[s]
        for h in range(GLA_HEADS):
            ks = slice(h * GLA_DK, (h + 1) * GLA_DK)
            vs = slice(h * GLA_DV, (h + 1) * GLA_DV)
            v_h = v[:, vs]
            qd_h = qd[:, ks]
            o = jnp.zeros((n_new, GLA_DV), F32)
            for t in range(n_new):
                a_t = jnp.sum(qd_h * kd[t:t + 1, ks], axis=-1, keepdims=True)
                o = o + jnp.where(rowv >= t, a_t * v_h[t:t + 1, :], 0.0)
            s0 = s0_ref[s, h]
            q_pad_ref[u, h, 0:n_new, :] = qd_h
            o = o + dot(q_pad_ref[u, h], s0)[0:n_new]
            z_ref[u, h, 0:n_new, :] = kdec[:, ks]
            z_ref[u, h, n_new:n_new + 1, :] = dec[:, ks]
            z_t = z_ref[u, h].T
            v_pad_ref[u, h, 0:n_new, :] = v_h
            s1_ref[s, h] = z_t[:, n_new:n_new + 1] * s0 + dot(z_t, v_pad_ref[u, h])
            o_ref[s, :, vs] = _gla_out(o, r[:, vs], ng_ref[...])

    def body(it, carry):
        for u in range(SEQ_UNROLL):
            per_seq(it * SEQ_UNROLL + u, u)
        return carry

    lax.fori_loop(0, q_ref.shape[0] // SEQ_UNROLL, body, 0)


def _gla_step(q, k, lg, v, r, norm_g, s0):
    nseq, n_new = q.shape[0], q.shape[1]
    assert n_new <= GLA_CHUNK and n_new < SUBLANES
    sb = SEQ_BLOCK
    blk3 = lambda a: pl.BlockSpec((sb,) + a.shape[1:], lambda i: (i, 0, 0))
    blk4 = lambda a: pl.BlockSpec((sb,) + a.shape[1:], lambda i: (i, 0, 0, 0))
    per = (SEQ_UNROLL, GLA_HEADS)
    return pl.pallas_call(
        _gla_step_kernel,
        grid=(nseq // sb,),
        in_specs=[blk3(q), blk3(k), blk3(lg), blk3(v), blk3(r), _full(norm_g.shape), blk4(s0)],
        out_specs=[blk3(v), blk4(s0)],
        out_shape=[jax.ShapeDtypeStruct(v.shape, F32), jax.ShapeDtypeStruct(s0.shape, F32)],
        scratch_shapes=[pltpu.VMEM(per + (LANES, GLA_DK), F32),
                        pltpu.VMEM(per + (LANES, GLA_DV), F32),
                        pltpu.VMEM(per + (SUBLANES, GLA_DK), F32)],
        compiler_params=_cparams(1),
        name="gla_step",
    )(q, k, lg, v, r, norm_g, s0)


def _trunk(x, mods, weights, attn_fn, gla_proj_fn, gla_fn):
    (w_attn_in, w_attn_out, w_gla_main, w_gla_gd, w_gla_gu, b_gate, norm_g, w_gla_out,
     w1, w2, ln_g, ln_b) = weights
    q, k, v = _attn_proj(x, mods[0], w_attn_in)
    o, attn_state = attn_fn(q, k, v)
    x = _mix_mlp(o, x, mods[0], mods[1], w_attn_out, w1, w2, ln_g, ln_b, 0)
    q, k, lg, v, r = gla_proj_fn(x, mods[2], w_gla_main, w_gla_gd, w_gla_gu, b_gate)
    o, gla_state = gla_fn(q, k, lg, v, r, norm_g)
    x = _mix_mlp(o, x, mods[2], mods[3], w_gla_out, w1, w2, ln_g, ln_b, 1)
    return x, attn_state, gla_state


def kernel(x_prompt, x_sample, cache_k, cache_v, state_gla, c_prompt, c_sample, w_mod, b_mod,
           ln_g, ln_b, attn_w_in, attn_w_out, attn_sinks, gla_w_in, gla_w_gate_up, gla_b_gate,
           gla_norm_g, gla_w_out, mlp_w1, mlp_w2):
    assert x_prompt.shape[0] == 1 and w_mod.shape[0] == DEPTH == 2
    seq = x_prompt.shape[1]
    nseq, n_new = x_sample.shape[0], x_sample.shape[1]
    win = cache_k.shape[2]
    m_s = nseq * n_new

    wq = attn_w_in[0][:, :NQ].reshape(D_MODEL, N_KV_HEADS, GROUP, HEAD_DIM)
    wq = wq.transpose(0, 2, 1, 3).reshape(D_MODEL, NQ)
    w_attn_in = jnp.concatenate([wq, attn_w_in[0][:, NQ:]], axis=1).astype(BF16)
    w_attn_out = attn_w_out[0].reshape(N_KV_HEADS, GROUP, HEAD_DIM, D_MODEL)
    w_attn_out = w_attn_out.transpose(1, 0, 2, 3).reshape(NQ, D_MODEL).astype(BF16)
    n_main = 2 * NK + 2 * NV
    w_gla_main = gla_w_in[0][:, :n_main].astype(BF16)
    w_gla_gd = jnp.pad(gla_w_in[0][:, n_main:], ((0, 0), (0, LANES - GLA_GATE_RANK))).astype(BF16)
    w_gla_gu = jnp.pad(gla_w_gate_up[0], ((0, LANES - GLA_GATE_RANK), (0, 0))).astype(BF16)
    weights = (w_attn_in, w_attn_out, w_gla_main, w_gla_gd, w_gla_gu,
               gla_b_gate[0].reshape(1, NK), gla_norm_g[0].reshape(1, GLA_DV),
               gla_w_out[0].astype(BF16), mlp_w1.astype(BF16), mlp_w2.astype(BF16),
               ln_g.reshape(2 * DEPTH, 1, D_MODEL), ln_b.reshape(2 * DEPTH, 1, D_MODEL))
    sinks = attn_sinks[0]

    pad_rows = (-(m_s + 1)) % SUBLANES
    c_all = jnp.concatenate([jnp.repeat(c_sample, n_new, axis=0), c_prompt,
                             jnp.zeros((pad_rows, D_MODEL), F32)], axis=0)
    mod_all = _adaln_all(c_all, w_mod.reshape(2 * DEPTH, D_MODEL, 3 * D_MODEL),
                         b_mod.reshape(2 * DEPTH, 1, 3 * D_MODEL))
    mods_p = [_Mod(mod_all, p, per_row=False, row0=m_s) for p in range(2 * DEPTH)]
    mods_s = [_Mod(mod_all, p, per_row=True) for p in range(2 * DEPTH)]

    win_p = min(WINDOW, seq)

    def attn_prompt(q, k, v):
        return _band_attn(q, k, v, sinks), (k[seq - win_p:], v[seq - win_p:])

    y_p, (k_p, v_p), s_p = _trunk(x_prompt[0], mods_p, weights, attn_prompt,
                                  functools.partial(_gla_proj, vr_dtype=BF16), _gla_prompt)

    to_slab = lambda c: c[0].transpose(0, 2, 3, 1).reshape(nseq, NKV, win)
    from_slab = lambda c: c.reshape(nseq, N_KV_HEADS, HEAD_DIM, win).transpose(0, 3, 1, 2)[None]

    def attn_sample(q, k, v):
        o, nk, nv = _dec_attn(
            q.reshape(nseq, n_new * GROUP, NKV), k.reshape(nseq, n_new, NKV),
            v.reshape(nseq, n_new, NKV), to_slab(cache_k), to_slab(cache_v), sinks)
        return o.reshape(m_s, NQ), (from_slab(nk), from_slab(nv))

    def gla_sample(q, k, lg, v, r, norm_g):
        three = lambda a: a.reshape(nseq, n_new, a.shape[-1])
        o, s1 = _gla_step(three(q), three(k), three(lg), three(v), three(r), norm_g,
                          state_gla[0])
        return o.reshape(m_s, NV), s1

    y_s, (k_s, v_s), s_s = _trunk(x_sample.reshape(m_s, D_MODEL), mods_s, weights, attn_sample,
                                  functools.partial(_gla_proj, vr_dtype=F32), gla_sample)

    kv_shape_p = (1, 1, win_p, N_KV_HEADS, HEAD_DIM)
    return (y_p[None], y_s.reshape(nseq, n_new, D_MODEL),
            k_p.reshape(kv_shape_p), v_p.reshape(kv_shape_p), s_p[None, None],
            k_s, v_s, s_s[None])
```

```python
import functools

import jax
import jax.numpy as jnp
from jax import lax
from jax.experimental import pallas as pl
from jax.experimental.pallas import tpu as pltpu

F32 = jnp.float32
BF16 = jnp.bfloat16

D_MODEL = 1024
DEPTH = 2
HEAD_DIM = 64
N_Q_HEADS = 16
N_KV_HEADS = 4
GROUP = 4
WINDOW = 128
ATTN_BLOCK = 128
GLA_HEADS = 4
GLA_DK = 128
GLA_DV = 256
GLA_GATE_RANK = 16
GLA_TAU = 16.0
GLA_CHUNK = 64
D_FF = 4 * D_MODEL
ALPHA = (2.0 * DEPTH) ** 0.25
LN_EPS = 1e-5

NQ = N_Q_HEADS * HEAD_DIM
NKV = N_KV_HEADS * HEAD_DIM
NK = GLA_HEADS * GLA_DK
NV = GLA_HEADS * GLA_DV
LANES = 128
SUBLANES = 8
NEG_BIG = -1e30

ROW_TILE = 512
FF_CHUNK = 1024
GLA_TILE = 256
SEQ_BLOCK = 8
SEQ_UNROLL = 2
VMEM_LIMIT = 56 * 1024 * 1024


def _cparams(n_axes):
    return pltpu.CompilerParams(
        dimension_semantics=("arbitrary",) * n_axes,
        vmem_limit_bytes=VMEM_LIMIT,
    )


def _full(shape):
    zeros = (0,) * len(shape)
    return pl.BlockSpec(shape, lambda *_: zeros)


def _layer(arr, idx):
    tail = (0,) * (arr.ndim - 1)
    return pl.BlockSpec((None,) + arr.shape[1:], lambda *_: (idx,) + tail,
                        pipeline_mode=pl.Buffered(1))


def _row_spec(tm, n):
    return pl.BlockSpec((tm, n), lambda i: (i, 0))


class _Mod:
    def __init__(self, arr, p, per_row, row0=0):
        self.arr, self.p, self.per_row, self.row0 = arr, p, per_row, row0

    def spec(self, tm, col):
        p = self.p
        if self.per_row:
            return pl.BlockSpec((None, tm, D_MODEL), lambda i: (p, i, col))
        blk = self.row0 // SUBLANES
        return pl.BlockSpec((None, SUBLANES, D_MODEL), lambda i: (p, blk, col))


def _mod_rows(ref, tm):
    return ref[...] if ref.shape[0] == tm else ref[0:1, :]


def _modulate(x, shift, scale):
    return x * (1.0 + scale) + shift


def _res_ln(x, gate, o, g, b):
    y = ALPHA * x + gate * o
    mu = jnp.mean(y, axis=-1, keepdims=True)
    yc = y - mu
    var = jnp.mean(yc * yc, axis=-1, keepdims=True)
    return yc * lax.rsqrt(var + LN_EPS) * g + b


def _mod_kernel(c_ref, w_ref, b_ref, o_ref):
    c = c_ref[...]
    a = (c * jax.nn.sigmoid(c)).astype(BF16)
    o_ref[...] = jnp.dot(a, w_ref[...].astype(BF16), preferred_element_type=F32) + b_ref[...]


def _adaln_all(c_all, w_mod, b_mod):
    rows = c_all.shape[0]
    tn = 1024
    return pl.pallas_call(
        _mod_kernel,
        grid=(4, 3 * D_MODEL // tn),
        in_specs=[
            pl.BlockSpec((rows, D_MODEL), lambda p, n: (0, 0)),
            pl.BlockSpec((None, D_MODEL, tn), lambda p, n: (p, 0, n)),
            pl.BlockSpec((None, 1, tn), lambda p, n: (p, 0, n)),
        ],
        out_specs=pl.BlockSpec((None, rows, tn), lambda p, n: (p, 0, n)),
        out_shape=jax.ShapeDtypeStruct((4, rows, 3 * D_MODEL), F32),
        compiler_params=_cparams(2),
        name="adaln_mod",
    )(c_all, w_mod, b_mod)


def _attn_proj_kernel(x_ref, sh_ref, sc_ref, w_ref, wvt_ref, q_ref, k_ref, v_ref, vt_ref):
    tm = x_ref.shape[0]
    h = _modulate(x_ref[...], _mod_rows(sh_ref, tm), _mod_rows(sc_ref, tm)).astype(BF16)
    q = jnp.dot(h, w_ref[:, 0:NQ], preferred_element_type=F32)
    q_ref[...] = (q * (HEAD_DIM ** -0.5)).astype(BF16)
    k_ref[...] = jnp.dot(h, w_ref[:, NQ:NQ + NKV], preferred_element_type=F32)
    v_ref[...] = jnp.dot(h, w_ref[:, NQ + NKV:NQ + 2 * NKV], preferred_element_type=F32)
    vt = lax.dot_general(wvt_ref[...], h, (((1,), (1,)), ((), ())), preferred_element_type=F32)
    vt_ref[...] = vt.astype(BF16)


def _attn_proj(x, mod, w_in, w_vt):
    m = x.shape[0]
    tm = min(ROW_TILE, m)
    row = functools.partial(_row_spec, tm)
    return pl.pallas_call(
        _attn_proj_kernel,
        grid=(m // tm,),
        in_specs=[row(D_MODEL), mod.spec(tm, 0), mod.spec(tm, 1), _full(w_in.shape),
                  _full(w_vt.shape)],
        out_specs=[row(NQ), row(NKV), row(NKV), pl.BlockSpec((NKV, tm), lambda i: (0, i))],
        out_shape=[
            jax.ShapeDtypeStruct((m, NQ), BF16),
            jax.ShapeDtypeStruct((m, NKV), F32),
            jax.ShapeDtypeStruct((m, NKV), F32),
            jax.ShapeDtypeStruct((NKV, m), BF16),
        ],
        compiler_params=_cparams(1),
        name="attn_proj",
    )(x, mod.arr, mod.arr, w_in, w_vt)


def _gla_proj_kernel(x_ref, sh_ref, sc_ref, w_ref, wgd_ref, wgu_ref, bg_ref,
                     q_ref, k_ref, lg_ref, v_ref, r_ref):
    tm = x_ref.shape[0]
    h = _modulate(x_ref[...], _mod_rows(sh_ref, tm), _mod_rows(sc_ref, tm)).astype(BF16)
    dot = functools.partial(jnp.dot, preferred_element_type=F32)
    q_ref[...] = dot(h, w_ref[:, 0:NK]) * (GLA_DK ** -0.5)
    k_ref[...] = dot(h, w_ref[:, NK:2 * NK])
    v_ref[...] = dot(h, w_ref[:, 2 * NK:2 * NK + NV]).astype(v_ref.dtype)
    r_ref[...] = dot(h, w_ref[:, 2 * NK + NV:2 * NK + 2 * NV]).astype(r_ref.dtype)
    gdown = dot(h, wgd_ref[...])
    pre = dot(gdown.astype(BF16), wgu_ref[...]) + bg_ref[...]
    log_sig = jnp.minimum(pre, 0.0) - jnp.log1p(jnp.exp(-jnp.abs(pre)))
    lg_ref[...] = log_sig / GLA_TAU


def _gla_proj(x, mod, w_main, w_gd, w_gu, b_gate, vr_dtype):
    m = x.shape[0]
    tm = min(ROW_TILE, m)
    row = functools.partial(_row_spec, tm)
    return pl.pallas_call(
        _gla_proj_kernel,
        grid=(m // tm,),
        in_specs=[row(D_MODEL), mod.spec(tm, 0), mod.spec(tm, 1), _full(w_main.shape),
                  _full(w_gd.shape), _full(w_gu.shape), _full(b_gate.shape)],
        out_specs=[row(NK), row(NK), row(NK), row(NV), row(NV)],
        out_shape=[
            jax.ShapeDtypeStruct((m, NK), F32),
            jax.ShapeDtypeStruct((m, NK), F32),
            jax.ShapeDtypeStruct((m, NK), F32),
            jax.ShapeDtypeStruct((m, NV), vr_dtype),
            jax.ShapeDtypeStruct((m, NV), vr_dtype),
        ],
        compiler_params=_cparams(1),
        name="gla_proj",
    )(x, mod.arr, mod.arr, w_main, w_gd, w_gu, b_gate)


def _mix_mlp_kernel(a_ref, x_ref, gt0_ref, sh_ref, sc_ref, gt1_ref, wo_ref, w1_ref, w2_ref,
                    g0_ref, b0_ref, g1_ref, b1_ref, o_ref, acc_ref):
    tm = x_ref.shape[0]
    o = jnp.dot(a_ref[...].astype(BF16), wo_ref[...], preferred_element_type=F32)
    x1 = _res_ln(x_ref[...], _mod_rows(gt0_ref, tm), o, g0_ref[...], b0_ref[...])
    h = _modulate(x1, _mod_rows(sh_ref, tm), _mod_rows(sc_ref, tm)).astype(BF16)
    for c in range(D_FF // FF_CHUNK):
        cols = slice(c * FF_CHUNK, (c + 1) * FF_CHUNK)
        a = jnp.dot(h, w1_ref[:, cols], preferred_element_type=F32)
        a = jnp.square(jnp.maximum(a, 0.0)).astype(BF16)
        d = jnp.dot(a, w2_ref[cols, :], preferred_element_type=F32)
        if c == 0:
            acc_ref[...] = d
        else:
            acc_ref[...] += d
    o_ref[...] = _res_ln(x1, _mod_rows(gt1_ref, tm), acc_ref[...], g1_ref[...], b1_ref[...])


def _mix_mlp(a, x, mod_mix, mod_mlp, w_out, w1, w2, ln_g, ln_b, layer):
    m = x.shape[0]
    tm = min(ROW_TILE, m)
    row = functools.partial(_row_spec, tm)
    return pl.pallas_call(
        _mix_mlp_kernel,
        grid=(m // tm,),
        in_specs=[row(a.shape[1]), row(D_MODEL), mod_mix.spec(tm, 2), mod_mlp.spec(tm, 0),
                  mod_mlp.spec(tm, 1), mod_mlp.spec(tm, 2), _full(w_out.shape),
                  _layer(w1, layer), _layer(w2, layer),
                  _layer(ln_g, 2 * layer), _layer(ln_b, 2 * layer),
                  _layer(ln_g, 2 * layer + 1), _layer(ln_b, 2 * layer + 1)],
        out_specs=row(D_MODEL),
        out_shape=jax.ShapeDtypeStruct((m, D_MODEL), F32),
        scratch_shapes=[pltpu.VMEM((tm, D_MODEL), F32)],
        compiler_params=_cparams(1),
        name="mix_mlp",
    )(a, x, mod_mix.arr, mod_mlp.arr, mod_mlp.arr, mod_mlp.arr, w_out, w1, w2,
      ln_g, ln_b, ln_g, ln_b)


def _alibi_slope(head):
    return 2.0 ** (-8.0 * (head + 1) / N_Q_HEADS)


def _softmax_sink(s, sink):
    m = jnp.maximum(jnp.max(s, axis=-1, keepdims=True), sink)
    e = jnp.exp(s - m)
    den = jnp.sum(e, axis=-1, keepdims=True) + jnp.exp(sink - m)
    return e / den


def _band_attn_kernel(sinks_ref, q_ref, kp_ref, kc_ref, vtp_ref, vtc_ref, o_ref, bias_ref):
    i = pl.program_id(0)
    blk = ATTN_BLOCK

    @pl.when(i <= 1)
    def _():
        c = lax.broadcasted_iota(jnp.int32, (2 * blk, blk), 0)
        r = lax.broadcasted_iota(jnp.int32, (2 * blk, blk), 1)
        dist = blk + r - c
        valid = (dist >= 0) & (dist <= WINDOW) & ((c >= blk) | (i > 0))
        distf = dist.astype(F32)
        for head in range(N_Q_HEADS):
            bias_ref[head] = jnp.where(valid, -_alibi_slope(head) * distf, NEG_BIG)

    kk = jnp.concatenate([kp_ref[...], kc_ref[...]], axis=0).astype(BF16)
    vvt = jnp.concatenate([vtp_ref[...], vtc_ref[...]], axis=1)
    head_of_lane = lax.broadcasted_iota(jnp.int32, (blk, NKV), 1) // HEAD_DIM
    scores = []
    for g in range(GROUP):
        qg = q_ref[:, g * NKV:(g + 1) * NKV]
        qm = jnp.concatenate(
            [jnp.where(head_of_lane == j, qg, jnp.zeros_like(qg))
             for j in range(N_KV_HEADS)], axis=0)
        scores.append(lax.dot_general(kk, qm, (((1,), (1,)), ((), ())),
                                      preferred_element_type=F32))
    for g in range(GROUP):
        st_all = scores[g]
        ps = []
        for j in range(N_KV_HEADS):
            head = j * GROUP + g
            sink = sinks_ref[head]
            st = st_all[:, j * blk:(j + 1) * blk] + bias_ref[head]
            m = jnp.maximum(jnp.max(st, axis=0, keepdims=True), sink)
            e = jnp.exp(st - m)
            den = jnp.sum(e, axis=0, keepdims=True) + jnp.exp(sink - m)
            ps.append((e * (1.0 / den)).astype(BF16))
        ot_all = jnp.dot(vvt, jnp.concatenate(ps, axis=1),
                         preferred_element_type=F32)
        ot = jnp.concatenate(
            [ot_all[j * HEAD_DIM:(j + 1) * HEAD_DIM, j * blk:(j + 1) * blk]
             for j in range(N_KV_HEADS)], axis=0)
        o_ref[:, g * NKV:(g + 1) * NKV] = ot.T.astype(BF16)


def _band_attn(q, k, vt, sinks):
    m = q.shape[0]
    blk = ATTN_BLOCK
    cur = lambda n: pl.BlockSpec((blk, n), lambda i: (i, 0))
    prev = lambda n: pl.BlockSpec((blk, n), lambda i: (jnp.maximum(i - 1, 0), 0))
    cur_t = pl.BlockSpec((NKV, blk), lambda i: (0, i))
    prev_t = pl.BlockSpec((NKV, blk), lambda i: (0, jnp.maximum(i - 1, 0)))
    return pl.pallas_call(
        _band_attn_kernel,
        grid=(m // blk,),
        in_specs=[pl.BlockSpec(memory_space=pltpu.SMEM), cur(NQ), prev(NKV), cur(NKV),
                  prev_t, cur_t],
        out_specs=cur(NQ),
        out_shape=jax.ShapeDtypeStruct((m, NQ), BF16),
        scratch_shapes=[pltpu.VMEM((N_Q_HEADS, 2 * blk, blk), F32)],
        compiler_params=_cparams(1),
        name="band_attn",
    )(sinks, q, k, k, vt, vt)


def _dec_attn_kernel(sinks_ref, q_ref, kn_ref, vn_ref, ck_ref, cv_ref,
                     o_ref, nk_ref, nv_ref, zk_ref, zv_ref):
    n_new = kn_ref.shape[1]
    win = ck_ref.shape[2]
    rows = n_new * GROUP
    n_rows = N_KV_HEADS * rows
    keep = win - n_new
    row = lax.broadcasted_iota(jnp.int32, (n_rows, 2 * win), 0)
    col = lax.broadcasted_iota(jnp.int32, (n_rows, 2 * win), 1)
    j_r = row // rows
    t_r = (row // GROUP) % n_new
    g_r = row % GROUP
    h_r = j_r * GROUP + g_r
    slope = jnp.exp2(-8.0 * (h_r + 1).astype(F32) / N_Q_HEADS)
    sink = jnp.zeros((n_rows, 1), F32)
    h_col = h_r[:, 0:1]
    for h in range(N_Q_HEADS):
        sink = jnp.where(h_col == h, sinks_ref[h], sink)
    is_key = (col < win) | (col >= win + keep)
    frame = jnp.where(col < win, col, col - keep)
    dist = t_r + win - frame
    valid = is_key & (dist >= 0) & (dist <= WINDOW)
    bias = jnp.where(valid, -slope * dist.astype(F32), NEG_BIG)
    head_of_lane = lax.broadcasted_iota(jnp.int32, (rows, NKV), 1) // HEAD_DIM
    lane_w = lax.broadcasted_iota(jnp.int32, (NKV, win), 1)

    @pl.when(pl.program_id(0) == 0)
    def _():
        zk_ref[...] = jnp.zeros_like(zk_ref)
        zv_ref[...] = jnp.zeros_like(zv_ref)

    def per_seq(s, u):
        k_t = ck_ref[s]
        v_t = cv_ref[s]
        zk_ref[u, keep:win, :] = kn_ref[s]
        zv_ref[u, keep:win, :] = vn_ref[s]
        zk_t = zk_ref[u].T
        zv_t = zv_ref[u].T
        nk_ref[s] = jnp.where(lane_w < keep, pltpu.roll(k_t, keep, 1), zk_t)
        nv_ref[s] = jnp.where(lane_w < keep, pltpu.roll(v_t, keep, 1), zv_t)
        keys = jnp.concatenate([k_t, zk_t], axis=1).astype(BF16)
        vals = jnp.concatenate([v_t, zv_t], axis=1).astype(BF16)
        qs = q_ref[s]
        qbd = jnp.concatenate(
            [jnp.where(head_of_lane == j, qs, jnp.zeros_like(qs)) for j in range(N_KV_HEADS)],
            axis=0)
        sc = jnp.dot(qbd, keys, preferred_element_type=F32) + bias
        p = _softmax_sink(sc, sink).astype(BF16)
        pv = lax.dot_general(p, vals, (((1,), (1,)), ((), ())), preferred_element_type=F32)
        o = jnp.zeros((rows, NKV), F32)
        for j in range(N_KV_HEADS):
            o = o + jnp.where(head_of_lane == j, pv[j * rows:(j + 1) * rows], 0.0)
        o_ref[s] = o.astype(BF16)

    def body(it, carry):
        for u in range(SEQ_UNROLL):
            per_seq(it * SEQ_UNROLL + u, u)
        return carry

    lax.fori_loop(0, q_ref.shape[0] // SEQ_UNROLL, body, 0)


def _dec_attn(q, k_new, v_new, cache_kt, cache_vt, sinks):
    nseq, win = cache_kt.shape[0], cache_kt.shape[2]
    n_new = k_new.shape[1]
    assert win == LANES and n_new < SUBLANES
    sb = SEQ_BLOCK
    blk = lambda a: pl.BlockSpec((sb,) + a.shape[1:], lambda i: (i, 0, 0))
    zshape = (SEQ_UNROLL, win, NKV)
    return pl.pallas_call(
        _dec_attn_kernel,
        grid=(nseq // sb,),
        in_specs=[pl.BlockSpec(memory_space=pltpu.SMEM), blk(q), blk(k_new), blk(v_new),
                  blk(cache_kt), blk(cache_vt)],
        out_specs=[blk(q), blk(cache_kt), blk(cache_vt)],
        out_shape=[
            jax.ShapeDtypeStruct(q.shape, BF16),
            jax.ShapeDtypeStruct(cache_kt.shape, F32),
            jax.ShapeDtypeStruct(cache_vt.shape, F32),
        ],
        scratch_shapes=[pltpu.VMEM(zshape, F32), pltpu.VMEM(zshape, F32)],
        compiler_params=_cparams(1),
        name="dec_attn",
    )(sinks, q, k_new, v_new, cache_kt, cache_vt)


def _split3(x):
    hi = x.astype(BF16)
    r1 = x - hi.astype(F32)
    mid = r1.astype(BF16)
    lo = (r1 - mid.astype(F32)).astype(BF16)
    return hi, mid, lo


def _gla_out(o, r, norm_g):
    ms = jnp.mean(o * o, axis=-1, keepdims=True)
    o = o * lax.rsqrt(ms + LN_EPS) * norm_g
    return o * (r * jax.nn.sigmoid(r))


def _gla_prompt_kernel(q_ref, k_ref, lg_ref, v_ref, r_ref, ng_ref, o_ref, s_out_ref, s_ref):
    i = pl.program_id(0)
    t_rows = q_ref.shape[0]
    ch = GLA_CHUNK
    n_ch = t_rows // ch
    dot = functools.partial(jnp.dot, preferred_element_type=F32)

    @pl.when(i == 0)
    def _():
        s_ref[...] = jnp.zeros_like(s_ref)

    row = lax.broadcasted_iota(jnp.int32, (t_rows, t_rows), 0)
    col = lax.broadcasted_iota(jnp.int32, (t_rows, t_rows), 1)
    causal = ((row // ch) == (col // ch)) & (col <= row)
    tril = jnp.where(causal, 1.0, 0.0).astype(BF16)

    hi, mid, lo = _split3(lg_ref[...])
    b = dot(tril, hi) + dot(tril, mid) + dot(tril, lo)
    b_end = jnp.concatenate(
        [jnp.broadcast_to(b[(c + 1) * ch - 1:(c + 1) * ch, :], (ch, NK)) for c in range(n_ch)],
        axis=0)
    q = q_ref[...]
    k = k_ref[...]
    qd = (q * jnp.exp(b)).astype(BF16)
    kd = (k * jnp.exp(-b)).astype(BF16)
    kdec = k * jnp.exp(b_end - b)
    dec_rows = jnp.concatenate(
        [jnp.exp(b[(c + 1) * ch - 1:(c + 1) * ch, :]) for c in range(n_ch)]
        + [jnp.zeros((LANES - n_ch, NK), F32)], axis=0)
    chunk_of_col = lax.broadcasted_iota(jnp.int32, (GLA_DK, t_rows), 1) // ch

    for h in range(GLA_HEADS):
        ks = slice(h * GLA_DK, (h + 1) * GLA_DK)
        vs = slice(h * GLA_DV, (h + 1) * GLA_DV)
        v_h = v_ref[:, vs]
        a = lax.dot_general(qd[:, ks], kd[:, ks], (((1,), (1,)), ((), ())),
                            preferred_element_type=F32)
        a = jnp.where(causal, a, 0.0).astype(BF16)
        o = dot(a, v_h)
        kdec_t = kdec[:, ks].T.astype(BF16)
        u_all = dot(
            jnp.concatenate(
                [jnp.where(chunk_of_col == c, kdec_t, jnp.zeros_like(kdec_t))
                 for c in range(n_ch)], axis=0),
            v_h)
        dec_t = dec_rows[:, ks].T
        s = s_ref[h]
        o_inter = []
        for c in range(n_ch):
            o_inter.append(dot(qd[c * ch:(c + 1) * ch, ks], s.astype(BF16)))
            s = dec_t[:, c:c + 1] * s + u_all[c * GLA_DK:(c + 1) * GLA_DK]
        s_ref[h] = s
        o = o + jnp.concatenate(o_inter, axis=0)
        o_ref[:, vs] = _gla_out(o, r_ref[:, vs].astype(F32), ng_ref[...]).astype(o_ref.dtype)

    s_out_ref[...] = s_ref[...]


def _gla_prompt(q, k, lg, v, r, norm_g):
    m = q.shape[0]
    tt = GLA_TILE
    row = functools.partial(_row_spec, tt)
    state = (GLA_HEADS, GLA_DK, GLA_DV)
    return pl.pallas_call(
        _gla_prompt_kernel,
        grid=(m // tt,),
        in_specs=[row(NK), row(NK), row(NK), row(NV), row(NV), _full(norm_g.shape)],
        out_specs=[row(NV), _full(state)],
        out_shape=[jax.ShapeDtypeStruct((m, NV), BF16), jax.ShapeDtypeStruct(state, F32)],
        scratch_shapes=[pltpu.VMEM(state, F32)],
        compiler_params=_cparams(1),
        name="gla_prompt",
    )(q, k, lg, v, r, norm_g)


def _gla_step_kernel(q_ref, k_ref, lg_ref, v_ref, r_ref, ng_ref, s0_ref,
                     o_ref, s1_ref, z_ref, v_pad_ref, q_pad_ref):
    n_new = q_ref.shape[1]
    dot = functools.partial(jnp.dot, preferred_element_type=F32)
    rowi = lax.broadcasted_iota(jnp.int32, (n_new, NK), 0)
    rowv = lax.broadcasted_iota(jnp.int32, (n_new, GLA_DV), 0)

    @pl.when(pl.program_id(0) == 0)
    def _():
        z_ref[...] = jnp.zeros_like(z_ref)
        v_pad_ref[...] = jnp.zeros_like(v_pad_ref)
        q_pad_ref[...] = jnp.zeros_like(q_pad_ref)

    def per_seq(s, u):
        lg = lg_ref[s]
        b = jnp.zeros_like(lg)
        for t in range(n_new):
            b = b + jnp.where(rowi >= t, jnp.broadcast_to(lg[t:t + 1, :], lg.shape), 0.0)
        b_end = b[n_new - 1:n_new, :]
        q = q_ref[s]
        k = k_ref[s]
        qd = q * jnp.exp(b)
        kd = k * jnp.exp(-b)
        kdec = k * jnp.exp(b_end - b)
        dec = jnp.exp(b_end)
        v = v_ref[s]
        r = r_ref[s]
        for h in range(GLA_HEADS):
            ks = slice(h * GLA_DK, (h + 1) * GLA_DK)
            vs = slice(h * GLA_DV, (h + 1) * GLA_DV)
            v_h = v[:, vs]
            qd_h = qd[:, ks]
            o = jnp.zeros((n_new, GLA_DV), F32)
            for t in range(n_new):
                a_t = jnp.sum(qd_h * kd[t:t + 1, ks], axis=-1, keepdims=True)
                o = o + jnp.where(rowv >= t, a_t * v_h[t:t + 1, :], 0.0)
            s0 = s0_ref[s, h]
            q_pad_ref[u, h, 0:n_new, :] = qd_h
            o = o + dot(q_pad_ref[u, h], s0)[0:n_new]
            z_ref[u, h, 0:n_new, :] = kdec[:, ks]
            z_ref[u, h, n_new:n_new + 1, :] = dec[:, ks]
            z_t = z_ref[u, h].T
            v_pad_ref[u, h, 0:n_new, :] = v_h
            s1_ref[s, h] = z_t[:, n_new:n_new + 1] * s0 + dot(z_t, v_pad_ref[u, h])
            o_ref[s, :, vs] = _gla_out(o, r[:, vs], ng_ref[...])

    def body(it, carry):
        for u in range(SEQ_UNROLL):
            per_seq(it * SEQ_UNROLL + u, u)
        return carry

    lax.fori_loop(0, q_ref.shape[0] // SEQ_UNROLL, body, 0)


def _gla_step(q, k, lg, v, r, norm_g, s0):
    nseq, n_new = q.shape[0], q.shape[1]
    assert n_new <= GLA_CHUNK and n_new < SUBLANES
    sb = SEQ_BLOCK
    blk3 = lambda a: pl.BlockSpec((sb,) + a.shape[1:], lambda i: (i, 0, 0))
    blk4 = lambda a: pl.BlockSpec((sb,) + a.shape[1:], lambda i: (i, 0, 0, 0))
    per = (SEQ_UNROLL, GLA_HEADS)
    return pl.pallas_call(
        _gla_step_kernel,
        grid=(nseq // sb,),
        in_specs=[blk3(q), blk3(k), blk3(lg), blk3(v), blk3(r), _full(norm_g.shape), blk4(s0)],
        out_specs=[blk3(v), blk4(s0)],
        out_shape=[jax.ShapeDtypeStruct(v.shape, F32), jax.ShapeDtypeStruct(s0.shape, F32)],
        scratch_shapes=[pltpu.VMEM(per + (LANES, GLA_DK), F32),
                        pltpu.VMEM(per + (LANES, GLA_DV), F32),
                        pltpu.VMEM(per + (SUBLANES, GLA_DK), F32)],
        compiler_params=_cparams(1),
        name="gla_step",
    )(q, k, lg, v, r, norm_g, s0)


def _trunk(x, mods, weights, attn_fn, gla_proj_fn, gla_fn):
    (w_attn_in, w_attn_vt, w_attn_out, w_gla_main, w_gla_gd, w_gla_gu, b_gate, norm_g, w_gla_out,
     w1, w2, ln_g, ln_b) = weights
    q, k, v, vt = _attn_proj(x, mods[0], w_attn_in, w_attn_vt)
    o, attn_state = attn_fn(q, k, v, vt)
    x = _mix_mlp(o, x, mods[0], mods[1], w_attn_out, w1, w2, ln_g, ln_b, 0)
    q, k, lg, v, r = gla_proj_fn(x, mods[2], w_gla_main, w_gla_gd, w_gla_gu, b_gate)
    o, gla_state = gla_fn(q, k, lg, v, r, norm_g)
    x = _mix_mlp(o, x, mods[2], mods[3], w_gla_out, w1, w2, ln_g, ln_b, 1)
    return x, attn_state, gla_state


def kernel(x_prompt, x_sample, cache_k, cache_v, state_gla, c_prompt, c_sample, w_mod, b_mod,
           ln_g, ln_b, attn_w_in, attn_w_out, attn_sinks, gla_w_in, gla_w_gate_up, gla_b_gate,
           gla_norm_g, gla_w_out, mlp_w1, mlp_w2):
    assert x_prompt.shape[0] == 1 and w_mod.shape[0] == DEPTH == 2
    seq = x_prompt.shape[1]
    nseq, n_new = x_sample.shape[0], x_sample.shape[1]
    win = cache_k.shape[2]
    m_s = nseq * n_new

    wq = attn_w_in[0][:, :NQ].reshape(D_MODEL, N_KV_HEADS, GROUP, HEAD_DIM)
    wq = wq.transpose(0, 2, 1, 3).reshape(D_MODEL, NQ)
    w_attn_in = jnp.concatenate([wq, attn_w_in[0][:, NQ:]], axis=1).astype(BF16)
    w_attn_out = attn_w_out[0].reshape(N_KV_HEADS, GROUP, HEAD_DIM, D_MODEL)
    w_attn_out = w_attn_out.transpose(1, 0, 2, 3).reshape(NQ, D_MODEL).astype(BF16)
    n_main = 2 * NK + 2 * NV
    w_gla_main = gla_w_in[0][:, :n_main].astype(BF16)
    w_gla_gd = jnp.pad(gla_w_in[0][:, n_main:], ((0, 0), (0, LANES - GLA_GATE_RANK))).astype(BF16)
    w_gla_gu = jnp.pad(gla_w_gate_up[0], ((0, LANES - GLA_GATE_RANK), (0, 0))).astype(BF16)
    w_attn_vt = attn_w_in[0][:, NQ + NKV:].T.astype(BF16)
    weights = (w_attn_in, w_attn_vt, w_attn_out, w_gla_main, w_gla_gd, w_gla_gu,
               gla_b_gate[0].reshape(1, NK), gla_norm_g[0].reshape(1, GLA_DV),
               gla_w_out[0].astype(BF16), mlp_w1.astype(BF16), mlp_w2.astype(BF16),
               ln_g.reshape(2 * DEPTH, 1, D_MODEL), ln_b.reshape(2 * DEPTH, 1, D_MODEL))
    sinks = attn_sinks[0]

    pad_rows = (-(m_s + 1)) % SUBLANES
    c_all = jnp.concatenate([jnp.repeat(c_sample, n_new, axis=0), c_prompt,
                             jnp.zeros((pad_rows, D_MODEL), F32)], axis=0)
    mod_all = _adaln_all(c_all, w_mod.reshape(2 * DEPTH, D_MODEL, 3 * D_MODEL),
                         b_mod.reshape(2 * DEPTH, 1, 3 * D_MODEL))
    mods_p = [_Mod(mod_all, p, per_row=False, row0=m_s) for p in range(2 * DEPTH)]
    mods_s = [_Mod(mod_all, p, per_row=True) for p in range(2 * DEPTH)]

    win_p = min(WINDOW, seq)

    def attn_prompt(q, k, v, vt):
        return _band_attn(q, k, vt, sinks), (k[seq - win_p:], v[seq - win_p:])

    y_p, (k_p, v_p), s_p = _trunk(x_prompt[0], mods_p, weights, attn_prompt,
                                  functools.partial(_gla_proj, vr_dtype=BF16), _gla_prompt)

    to_slab = lambda c: c[0].transpose(0, 2, 3, 1).reshape(nseq, NKV, win)
    from_slab = lambda c: c.reshape(nseq, N_KV_HEADS, HEAD_DIM, win).transpose(0, 3, 1, 2)[None]

    def attn_sample(q, k, v, vt):
        o, nk, nv = _dec_attn(
            q.reshape(nseq, n_new * GROUP, NKV), k.reshape(nseq, n_new, NKV),
            v.reshape(nseq, n_new, NKV), to_slab(cache_k), to_slab(cache_v), sinks)
        return o.reshape(m_s, NQ), (from_slab(nk), from_slab(nv))

    def gla_sample(q, k, lg, v, r, norm_g):
        three = lambda a: a.reshape(nseq, n_new, a.shape[-1])
        o, s1 = _gla_step(three(q), three(k), three(lg), three(v), three(r), norm_g,
                          state_gla[0])
        return o.reshape(m_s, NV), s1

    y_s, (k_s, v_s), s_s = _trunk(x_sample.reshape(m_s, D_MODEL), mods_s, weights, attn_sample,
                                  functools.partial(_gla_proj, vr_dtype=F32), gla_sample)

    kv_shape_p = (1, 1, win_p, N_KV_HEADS, HEAD_DIM)
    return (y_p[None], y_s.reshape(nseq, n_new, D_MODEL),
            k_p.reshape(kv_shape_p), v_p.reshape(kv_shape_p), s_p[None, None],
            k_s, v_s, s_s[None])
```

```python
import functools

import jax
import jax.numpy as jnp
from jax import lax
from jax.experimental import pallas as pl
from jax.experimental.pallas import tpu as pltpu

F32 = jnp.float32
BF16 = jnp.bfloat16

D_MODEL = 1024
DEPTH = 2
HEAD_DIM = 64
N_Q_HEADS = 16
N_KV_HEADS = 4
GROUP = 4
WINDOW = 128
ATTN_BLOCK = 128
GLA_HEADS = 4
GLA_DK = 128
GLA_DV = 256
GLA_GATE_RANK = 16
GLA_TAU = 16.0
GLA_CHUNK = 64
D_FF = 4 * D_MODEL
ALPHA = (2.0 * DEPTH) ** 0.25
LN_EPS = 1e-5

NQ = N_Q_HEADS * HEAD_DIM
NKV = N_KV_HEADS * HEAD_DIM
NK = GLA_HEADS * GLA_DK
NV = GLA_HEADS * GLA_DV
LANES = 128
SUBLANES = 8
NEG_BIG = -1e30

ROW_TILE = 512
FF_CHUNK = 1024
ATTN_TILE = 512
GLA_TILE = 512
GLA_SUB = 256
SEQ_BLOCK = 8
SEQ_UNROLL = 2
VMEM_LIMIT = 56 * 1024 * 1024


def _cparams(n_axes):
    return pltpu.CompilerParams(
        dimension_semantics=("arbitrary",) * n_axes,
        vmem_limit_bytes=VMEM_LIMIT,
    )


def _full(shape):
    zeros = (0,) * len(shape)
    return pl.BlockSpec(shape, lambda *_: zeros)


def _layer(arr, idx):
    tail = (0,) * (arr.ndim - 1)
    return pl.BlockSpec((None,) + arr.shape[1:], lambda *_: (idx,) + tail,
                        pipeline_mode=pl.Buffered(1))


def _row_spec(tm, n):
    return pl.BlockSpec((tm, n), lambda i: (i, 0))


class _Mod:
    def __init__(self, arr, p, per_row, row0=0):
        self.arr, self.p, self.per_row, self.row0 = arr, p, per_row, row0

    def spec(self, tm, col):
        p = self.p
        if self.per_row:
            return pl.BlockSpec((None, tm, D_MODEL), lambda i: (p, i, col))
        blk = self.row0 // SUBLANES
        return pl.BlockSpec((None, SUBLANES, D_MODEL), lambda i: (p, blk, col))


def _mod_rows(ref, tm):
    return ref[...] if ref.shape[0] == tm else ref[0:1, :]


def _modulate(x, shift, scale):
    return x * (1.0 + scale) + shift


def _res_ln(x, gate, o, g, b):
    y = ALPHA * x + gate * o
    mu = jnp.mean(y, axis=-1, keepdims=True)
    yc = y - mu
    var = jnp.mean(yc * yc, axis=-1, keepdims=True)
    return yc * lax.rsqrt(var + LN_EPS) * g + b


def _mod_kernel(c_ref, w_ref, b_ref, o_ref):
    c = c_ref[...]
    a = (c * jax.nn.sigmoid(c)).astype(BF16)
    o_ref[...] = jnp.dot(a, w_ref[...].astype(BF16), preferred_element_type=F32) + b_ref[...]


def _adaln_all(c_all, w_mod, b_mod):
    rows = c_all.shape[0]
    tn = 1024
    return pl.pallas_call(
        _mod_kernel,
        grid=(4, 3 * D_MODEL // tn),
        in_specs=[
            pl.BlockSpec((rows, D_MODEL), lambda p, n: (0, 0)),
            pl.BlockSpec((None, D_MODEL, tn), lambda p, n: (p, 0, n)),
            pl.BlockSpec((None, 1, tn), lambda p, n: (p, 0, n)),
        ],
        out_specs=pl.BlockSpec((None, rows, tn), lambda p, n: (p, 0, n)),
        out_shape=jax.ShapeDtypeStruct((4, rows, 3 * D_MODEL), F32),
        compiler_params=_cparams(2),
        name="adaln_mod",
    )(c_all, w_mod, b_mod)


def _attn_proj_kernel(x_ref, sh_ref, sc_ref, w_ref, q_ref, k_ref, v_ref, vt_ref):
    tm = x_ref.shape[0]
    h = _modulate(x_ref[...], _mod_rows(sh_ref, tm), _mod_rows(sc_ref, tm)).astype(BF16)
    q = jnp.dot(h, w_ref[:, 0:NQ], preferred_element_type=F32)
    q_ref[...] = (q * (HEAD_DIM ** -0.5)).astype(BF16)
    k_ref[...] = jnp.dot(h, w_ref[:, NQ:NQ + NKV], preferred_element_type=F32)
    v = jnp.dot(h, w_ref[:, NQ + NKV:NQ + 2 * NKV], preferred_element_type=F32)
    v_ref[...] = v
    vt_ref[...] = v.T.astype(BF16)


def _attn_proj(x, mod, w_in):
    m = x.shape[0]
    tm = min(ROW_TILE, m)
    row = functools.partial(_row_spec, tm)
    return pl.pallas_call(
        _attn_proj_kernel,
        grid=(m // tm,),
        in_specs=[row(D_MODEL), mod.spec(tm, 0), mod.spec(tm, 1), _full(w_in.shape)],
        out_specs=[row(NQ), row(NKV), row(NKV), pl.BlockSpec((NKV, tm), lambda i: (0, i))],
        out_shape=[
            jax.ShapeDtypeStruct((m, NQ), BF16),
            jax.ShapeDtypeStruct((m, NKV), F32),
            jax.ShapeDtypeStruct((m, NKV), F32),
            jax.ShapeDtypeStruct((NKV, m), BF16),
        ],
        compiler_params=_cparams(1),
        name="attn_proj",
    )(x, mod.arr, mod.arr, w_in)


def _gla_proj_kernel(x_ref, sh_ref, sc_ref, w_ref, wgd_ref, wgu_ref, bg_ref,
                     q_ref, k_ref, lg_ref, v_ref, r_ref):
    tm = x_ref.shape[0]
    h = _modulate(x_ref[...], _mod_rows(sh_ref, tm), _mod_rows(sc_ref, tm)).astype(BF16)
    dot = functools.partial(jnp.dot, preferred_element_type=F32)
    q_ref[...] = dot(h, w_ref[:, 0:NK]) * (GLA_DK ** -0.5)
    k_ref[...] = dot(h, w_ref[:, NK:2 * NK])
    v_ref[...] = dot(h, w_ref[:, 2 * NK:2 * NK + NV]).astype(v_ref.dtype)
    r_ref[...] = dot(h, w_ref[:, 2 * NK + NV:2 * NK + 2 * NV]).astype(r_ref.dtype)
    gdown = dot(h, wgd_ref[...])
    pre = dot(gdown.astype(BF16), wgu_ref[...]) + bg_ref[...]
    log_sig = jnp.minimum(pre, 0.0) - jnp.log1p(jnp.exp(-jnp.abs(pre)))
    lg_ref[...] = log_sig / GLA_TAU


def _gla_proj(x, mod, w_main, w_gd, w_gu, b_gate, vr_dtype):
    m = x.shape[0]
    tm = min(ROW_TILE, m)
    row = functools.partial(_row_spec, tm)
    return pl.pallas_call(
        _gla_proj_kernel,
        grid=(m // tm,),
        in_specs=[row(D_MODEL), mod.spec(tm, 0), mod.spec(tm, 1), _full(w_main.shape),
                  _full(w_gd.shape), _full(w_gu.shape), _full(b_gate.shape)],
        out_specs=[row(NK), row(NK), row(NK), row(NV), row(NV)],
        out_shape=[
            jax.ShapeDtypeStruct((m, NK), F32),
            jax.ShapeDtypeStruct((m, NK), F32),
            jax.ShapeDtypeStruct((m, NK), F32),
            jax.ShapeDtypeStruct((m, NV), vr_dtype),
            jax.ShapeDtypeStruct((m, NV), vr_dtype),
        ],
        compiler_params=_cparams(1),
        name="gla_proj",
    )(x, mod.arr, mod.arr, w_main, w_gd, w_gu, b_gate)


def _mix_mlp_kernel(a_ref, x_ref, gt0_ref, sh_ref, sc_ref, gt1_ref, wo_ref, w1_ref, w2_ref,
                    g0_ref, b0_ref, g1_ref, b1_ref, o_ref, acc_ref):
    tm = x_ref.shape[0]
    o = jnp.dot(a_ref[...].astype(BF16), wo_ref[...], preferred_element_type=F32)
    x1 = _res_ln(x_ref[...], _mod_rows(gt0_ref, tm), o, g0_ref[...], b0_ref[...])
    h = _modulate(x1, _mod_rows(sh_ref, tm), _mod_rows(sc_ref, tm)).astype(BF16)
    for c in range(D_FF // FF_CHUNK):
        cols = slice(c * FF_CHUNK, (c + 1) * FF_CHUNK)
        a = jnp.dot(h, w1_ref[:, cols], preferred_element_type=F32)
        a = jnp.square(jnp.maximum(a, 0.0)).astype(BF16)
        d = jnp.dot(a, w2_ref[cols, :], preferred_element_type=F32)
        if c == 0:
            acc_ref[...] = d
        else:
            acc_ref[...] += d
    o_ref[...] = _res_ln(x1, _mod_rows(gt1_ref, tm), acc_ref[...], g1_ref[...], b1_ref[...])


def _mix_mlp(a, x, mod_mix, mod_mlp, w_out, w1, w2, ln_g, ln_b, layer):
    m = x.shape[0]
    tm = min(ROW_TILE, m)
    row = functools.partial(_row_spec, tm)
    return pl.pallas_call(
        _mix_mlp_kernel,
        grid=(m // tm,),
        in_specs=[row(a.shape[1]), row(D_MODEL), mod_mix.spec(tm, 2), mod_mlp.spec(tm, 0),
                  mod_mlp.spec(tm, 1), mod_mlp.spec(tm, 2), _full(w_out.shape),
                  _layer(w1, layer), _layer(w2, layer),
                  _layer(ln_g, 2 * layer), _layer(ln_b, 2 * layer),
                  _layer(ln_g, 2 * layer + 1), _layer(ln_b, 2 * layer + 1)],
        out_specs=row(D_MODEL),
        out_shape=jax.ShapeDtypeStruct((m, D_MODEL), F32),
        scratch_shapes=[pltpu.VMEM((tm, D_MODEL), F32)],
        compiler_params=_cparams(1),
        name="mix_mlp",
    )(a, x, mod_mix.arr, mod_mlp.arr, mod_mlp.arr, mod_mlp.arr, w_out, w1, w2,
      ln_g, ln_b, ln_g, ln_b)


def _alibi_slope(head):
    return 2.0 ** (-8.0 * (head + 1) / N_Q_HEADS)


def _softmax_sink(s, sink):
    m = jnp.maximum(jnp.max(s, axis=-1, keepdims=True), sink)
    e = jnp.exp(s - m)
    den = jnp.sum(e, axis=-1, keepdims=True) + jnp.exp(sink - m)
    return e / den


def _band_attn_kernel(sinks_ref, q_ref, kp_ref, kc_ref, vtp_ref, vtc_ref, o_ref, bias_ref):
    i = pl.program_id(0)
    blk = ATTN_BLOCK
    n_blk = q_ref.shape[0] // blk

    @pl.when(i == 0)
    def _():
        c = lax.broadcasted_iota(jnp.int32, (2 * blk, blk), 0)
        r = lax.broadcasted_iota(jnp.int32, (2 * blk, blk), 1)
        dist = blk + r - c
        valid = (dist >= 0) & (dist <= WINDOW)
        distf = dist.astype(F32)
        for head in range(N_Q_HEADS):
            pen = -_alibi_slope(head) * distf
            bias_ref[0, head] = jnp.where(valid, pen, NEG_BIG)
            bias_ref[1, head] = jnp.where(valid & (c >= blk), pen, NEG_BIG)

    kk = jnp.concatenate([kp_ref[...], kc_ref[...]], axis=0).astype(BF16)
    vvt = jnp.concatenate([vtp_ref[...], vtc_ref[...]], axis=1)
    head_of_lane = lax.broadcasted_iota(jnp.int32, (blk, NKV), 1) // HEAD_DIM
    scores = []
    for b in range(n_blk):
        keys = kk[b * blk:(b + 2) * blk]
        for g in range(GROUP):
            qg = q_ref[b * blk:(b + 1) * blk, g * NKV:(g + 1) * NKV]
            qm = jnp.concatenate(
                [jnp.where(head_of_lane == j, qg, jnp.zeros_like(qg))
                 for j in range(N_KV_HEADS)], axis=0)
            scores.append(lax.dot_general(keys, qm, (((1,), (1,)), ((), ())),
                                          preferred_element_type=F32))
    for b in range(n_blk):
        table = jnp.where(i == 0, 1, 0) if b == 0 else 0
        vals_t = vvt[:, b * blk:(b + 2) * blk]
        for g in range(GROUP):
            st_all = scores[b * GROUP + g]
            ps = []
            for j in range(N_KV_HEADS):
                head = j * GROUP + g
                sink = sinks_ref[head]
                st = st_all[:, j * blk:(j + 1) * blk] + bias_ref[table, head]
                m = jnp.maximum(jnp.max(st, axis=0, keepdims=True), sink)
                e = jnp.exp(st - m)
                den = jnp.sum(e, axis=0, keepdims=True) + jnp.exp(sink - m)
                ps.append((e * (1.0 / den)).astype(BF16))
            ot_all = jnp.dot(vals_t, jnp.concatenate(ps, axis=1),
                             preferred_element_type=F32)
            ot = jnp.concatenate(
                [ot_all[j * HEAD_DIM:(j + 1) * HEAD_DIM, j * blk:(j + 1) * blk]
                 for j in range(N_KV_HEADS)], axis=0)
            o_ref[b * blk:(b + 1) * blk, g * NKV:(g + 1) * NKV] = ot.T.astype(BF16)


def _band_attn(q, k, vt, sinks):
    m = q.shape[0]
    blk = ATTN_BLOCK
    tm = ATTN_TILE
    per = tm // blk
    cur = lambda n: pl.BlockSpec((tm, n), lambda i: (i, 0))
    prev = pl.BlockSpec((blk, NKV), lambda i: (jnp.maximum(i * per - 1, 0), 0))
    cur_t = pl.BlockSpec((NKV, tm), lambda i: (0, i))
    prev_t = pl.BlockSpec((NKV, blk), lambda i: (0, jnp.maximum(i * per - 1, 0)))
    return pl.pallas_call(
        _band_attn_kernel,
        grid=(m // tm,),
        in_specs=[pl.BlockSpec(memory_space=pltpu.SMEM), cur(NQ), prev, cur(NKV),
                  prev_t, cur_t],
        out_specs=cur(NQ),
        out_shape=jax.ShapeDtypeStruct((m, NQ), BF16),
        scratch_shapes=[pltpu.VMEM((2, N_Q_HEADS, 2 * blk, blk), F32)],
        compiler_params=_cparams(1),
        name="band_attn",
    )(sinks, q, k, k, vt, vt)


def _dec_attn_kernel(sinks_ref, q_ref, kn_ref, vn_ref, ck_ref, cv_ref,
                     o_ref, nk_ref, nv_ref, zk_ref, zv_ref):
    n_new = kn_ref.shape[1]
    win = ck_ref.shape[2]
    rows = n_new * GROUP
    n_rows = N_KV_HEADS * rows
    keep = win - n_new
    row = lax.broadcasted_iota(jnp.int32, (n_rows, 2 * win), 0)
    col = lax.broadcasted_iota(jnp.int32, (n_rows, 2 * win), 1)
    j_r = row // rows
    t_r = (row // GROUP) % n_new
    g_r = row % GROUP
    h_r = j_r * GROUP + g_r
    slope = jnp.exp2(-8.0 * (h_r + 1).astype(F32) / N_Q_HEADS)
    sink = jnp.zeros((n_rows, 1), F32)
    h_col = h_r[:, 0:1]
    for h in range(N_Q_HEADS):
        sink = jnp.where(h_col == h, sinks_ref[h], sink)
    is_key = (col < win) | (col >= win + keep)
    frame = jnp.where(col < win, col, col - keep)
    dist = t_r + win - frame
    valid = is_key & (dist >= 0) & (dist <= WINDOW)
    bias = jnp.where(valid, -slope * dist.astype(F32), NEG_BIG)
    head_of_lane = lax.broadcasted_iota(jnp.int32, (rows, NKV), 1) // HEAD_DIM
    lane_w = lax.broadcasted_iota(jnp.int32, (NKV, win), 1)

    @pl.when(pl.program_id(0) == 0)
    def _():
        zk_ref[...] = jnp.zeros_like(zk_ref)
        zv_ref[...] = jnp.zeros_like(zv_ref)

    def per_seq(s, u):
        k_t = ck_ref[s]
        v_t = cv_ref[s]
        zk_ref[u, keep:win, :] = kn_ref[s]
        zv_ref[u, keep:win, :] = vn_ref[s]
        zk_t = zk_ref[u].T
        zv_t = zv_ref[u].T
        nk_ref[s] = jnp.where(lane_w < keep, pltpu.roll(k_t, keep, 1), zk_t)
        nv_ref[s] = jnp.where(lane_w < keep, pltpu.roll(v_t, keep, 1), zv_t)
        keys = jnp.concatenate([k_t, zk_t], axis=1).astype(BF16)
        vals = jnp.concatenate([v_t, zv_t], axis=1).astype(BF16)
        qs = q_ref[s]
        qbd = jnp.concatenate(
            [jnp.where(head_of_lane == j, qs, jnp.zeros_like(qs)) for j in range(N_KV_HEADS)],
            axis=0)
        sc = jnp.dot(qbd, keys, preferred_element_type=F32) + bias
        p = _softmax_sink(sc, sink).astype(BF16)
        pv = lax.dot_general(p, vals, (((1,), (1,)), ((), ())), preferred_element_type=F32)
        o = jnp.zeros((rows, NKV), F32)
        for j in range(N_KV_HEADS):
            o = o + jnp.where(head_of_lane == j, pv[j * rows:(j + 1) * rows], 0.0)
        o_ref[s] = o.astype(BF16)

    def body(it, carry):
        for u in range(SEQ_UNROLL):
            per_seq(it * SEQ_UNROLL + u, u)
        return carry

    lax.fori_loop(0, q_ref.shape[0] // SEQ_UNROLL, body, 0)


def _dec_attn(q, k_new, v_new, cache_kt, cache_vt, sinks):
    nseq, win = cache_kt.shape[0], cache_kt.shape[2]
    n_new = k_new.shape[1]
    assert win == LANES and n_new < SUBLANES
    sb = SEQ_BLOCK
    blk = lambda a: pl.BlockSpec((sb,) + a.shape[1:], lambda i: (i, 0, 0))
    zshape = (SEQ_UNROLL, win, NKV)
    return pl.pallas_call(
        _dec_attn_kernel,
        grid=(nseq // sb,),
        in_specs=[pl.BlockSpec(memory_space=pltpu.SMEM), blk(q), blk(k_new), blk(v_new),
                  blk(cache_kt), blk(cache_vt)],
        out_specs=[blk(q), blk(cache_kt), blk(cache_vt)],
        out_shape=[
            jax.ShapeDtypeStruct(q.shape, BF16),
            jax.ShapeDtypeStruct(cache_kt.shape, F32),
            jax.ShapeDtypeStruct(cache_vt.shape, F32),
        ],
        scratch_shapes=[pltpu.VMEM(zshape, F32), pltpu.VMEM(zshape, F32)],
        compiler_params=_cparams(1),
        name="dec_attn",
    )(sinks, q, k_new, v_new, cache_kt, cache_vt)


def _split3(x):
    hi = x.astype(BF16)
    r1 = x - hi.astype(F32)
    mid = r1.astype(BF16)
    lo = (r1 - mid.astype(F32)).astype(BF16)
    return hi, mid, lo


def _gla_out(o, r, norm_g):
    ms = jnp.mean(o * o, axis=-1, keepdims=True)
    o = o * lax.rsqrt(ms + LN_EPS) * norm_g
    return o * (r * jax.nn.sigmoid(r))


def _gla_prompt_kernel(q_ref, k_ref, lg_ref, v_ref, r_ref, ng_ref, o_ref, s_out_ref, s_ref):
    i = pl.program_id(0)
    sub = GLA_SUB
    n_sub = q_ref.shape[0] // sub
    ch = GLA_CHUNK
    n_ch = sub // ch
    dot = functools.partial(jnp.dot, preferred_element_type=F32)
    heads = range(GLA_HEADS)
    ks = [slice(h * GLA_DK, (h + 1) * GLA_DK) for h in heads]
    vs = [slice(h * GLA_DV, (h + 1) * GLA_DV) for h in heads]

    @pl.when(i == 0)
    def _():
        s_ref[...] = jnp.zeros_like(s_ref)

    row = lax.broadcasted_iota(jnp.int32, (sub, sub), 0)
    col = lax.broadcasted_iota(jnp.int32, (sub, sub), 1)
    causal = ((row // ch) == (col // ch)) & (col <= row)
    tril = jnp.where(causal, 1.0, 0.0).astype(BF16)
    chunk_of_col = lax.broadcasted_iota(jnp.int32, (GLA_DK, sub), 1) // ch

    qd, kd, kdec, dec_rows = [], [], [], []
    for t in range(n_sub):
        rows = slice(t * sub, (t + 1) * sub)
        hi, mid, lo = _split3(lg_ref[rows, :])
        b = dot(tril, hi) + dot(tril, mid) + dot(tril, lo)
        ends = [b[(c + 1) * ch - 1:(c + 1) * ch, :] for c in range(n_ch)]
        b_end = jnp.concatenate([jnp.broadcast_to(e, (ch, NK)) for e in ends], axis=0)
        q = q_ref[rows, :]
        k = k_ref[rows, :]
        qd.append((q * jnp.exp(b)).astype(BF16))
        kd.append((k * jnp.exp(-b)).astype(BF16))
        kdec.append(k * jnp.exp(b_end - b))
        dec_rows.append(jnp.concatenate(
            [jnp.exp(e) for e in ends] + [jnp.zeros((LANES - n_ch, NK), F32)], axis=0))

    a = [[lax.dot_general(qd[t][:, ks[h]], kd[t][:, ks[h]], (((1,), (1,)), ((), ())),
                          preferred_element_type=F32) for h in heads] for t in range(n_sub)]

    u = []
    for t in range(n_sub):
        rows = slice(t * sub, (t + 1) * sub)
        u_t = []
        for h in heads:
            kdec_t = kdec[t][:, ks[h]].T.astype(BF16)
            stacked = jnp.concatenate(
                [jnp.where(chunk_of_col == c, kdec_t, jnp.zeros_like(kdec_t))
                 for c in range(n_ch)], axis=0)
            u_t.append(dot(stacked, v_ref[rows, vs[h]]))
        u.append(u_t)

    o_intra = []
    for t in range(n_sub):
        rows = slice(t * sub, (t + 1) * sub)
        o_intra.append([dot(jnp.where(causal, a[t][h], 0.0).astype(BF16), v_ref[rows, vs[h]])
                        for h in heads])

    for h in heads:
        s = s_ref[h]
        for t in range(n_sub):
            rows = slice(t * sub, (t + 1) * sub)
            dec_t = dec_rows[t][:, ks[h]].T
            o_inter = []
            for c in range(n_ch):
                o_inter.append(dot(qd[t][c * ch:(c + 1) * ch, ks[h]], s.astype(BF16)))
                s = dec_t[:, c:c + 1] * s + u[t][h][c * GLA_DK:(c + 1) * GLA_DK]
            o = o_intra[t][h] + jnp.concatenate(o_inter, axis=0)
            o_ref[rows, vs[h]] = _gla_out(
                o, r_ref[rows, vs[h]].astype(F32), ng_ref[...]).astype(o_ref.dtype)
        s_ref[h] = s

    s_out_ref[...] = s_ref[...]


def _gla_prompt(q, k, lg, v, r, norm_g):
    m = q.shape[0]
    tt = GLA_TILE
    row = functools.partial(_row_spec, tt)
    state = (GLA_HEADS, GLA_DK, GLA_DV)
    return pl.pallas_call(
        _gla_prompt_kernel,
        grid=(m // tt,),
        in_specs=[row(NK), row(NK), row(NK), row(NV), row(NV), _full(norm_g.shape)],
        out_specs=[row(NV), _full(state)],
        out_shape=[jax.ShapeDtypeStruct((m, NV), BF16), jax.ShapeDtypeStruct(state, F32)],
        scratch_shapes=[pltpu.VMEM(state, F32)],
        compiler_params=_cparams(1),
        name="gla_prompt",
    )(q, k, lg, v, r, norm_g)


def _gla_step_kernel(q_ref, k_ref, lg_ref, v_ref, r_ref, ng_ref, s0_ref,
                     o_ref, s1_ref, z_ref, v_pad_ref, q_pad_ref):
    n_new = q_ref.shape[1]
    dot = functools.partial(jnp.dot, preferred_element_type=F32)
    rowi = lax.broadcasted_iota(jnp.int32, (n_new, NK), 0)
    rowv = lax.broadcasted_iota(jnp.int32, (n_new, GLA_DV), 0)

    @pl.when(pl.program_id(0) == 0)
    def _():
        z_ref[...] = jnp.zeros_like(z_ref)
        v_pad_ref[...] = jnp.zeros_like(v_pad_ref)
        q_pad_ref[...] = jnp.zeros_like(q_pad_ref)

    def per_seq(s, u):
        lg = lg_ref[s]
        b = jnp.zeros_like(lg)
        for t in range(n_new):
            b = b + jnp.where(rowi >= t, jnp.broadcast_to(lg[t:t + 1, :], lg.shape), 0.0)
        b_end = b[n_new - 1:n_new, :]
        q = q_ref[s]
        k = k_ref[s]
        qd = q * jnp.exp(b)
        kd = k * jnp.exp(-b)
        kdec = k * jnp.exp(b_end - b)
        dec = jnp.exp(b_end)
        v = v_ref[s]
        r = r_ref[s]
        for h in range(GLA_HEADS):
            ks = slice(h * GLA_DK, (h + 1) * GLA_DK)
            vs = slice(h * GLA_DV, (h + 1) * GLA_DV)
            v_h = v[:, vs]
            qd_h = qd[:, ks]
            o = jnp.zeros((n_new, GLA_DV), F32)
            for t in range(n_new):
                a_t = jnp.sum(qd_h * kd[t:t + 1, ks], axis=-1, keepdims=True)
                o = o + jnp.where(rowv >= t, a_t * v_h[t:t + 1, :], 0.0)
            s0 = s0_ref[s, h]
            q_pad_ref[u, h, 0:n_new, :] = qd_h
            o = o + dot(q_pad_ref[u, h], s0)[0:n_new]
            z_ref[u, h, 0:n_new, :] = kdec[:, ks]
            z_ref[u, h, n_new:n_new + 1, :] = dec[:, ks]
            z_t = z_ref[u, h].T
            v_pad_ref[u, h, 0:n_new, :] = v_h
            s1_ref[s, h] = z_t[:, n_new:n_new + 1] * s0 + dot(z_t, v_pad_ref[u, h])
            o_ref[s, :, vs] = _gla_out(o, r[:, vs], ng_ref[...])

    def body(it, carry):
        for u in range(SEQ_UNROLL):
            per_seq(it * SEQ_UNROLL + u, u)
        return carry

    lax.fori_loop(0, q_ref.shape[0] // SEQ_UNROLL, body, 0)


def _gla_step(q, k, lg, v, r, norm_g, s0):
    nseq, n_new = q.shape[0], q.shape[1]
    assert n_new <= GLA_CHUNK and n_new < SUBLANES
    sb = SEQ_BLOCK
    blk3 = lambda a: pl.BlockSpec((sb,) + a.shape[1:], lambda i: (i, 0, 0))
    blk4 = lambda a: pl.BlockSpec((sb,) + a.shape[1:], lambda i: (i, 0, 0, 0))
    per = (SEQ_UNROLL, GLA_HEADS)
    return pl.pallas_call(
        _gla_step_kernel,
        grid=(nseq // sb,),
        in_specs=[blk3(q), blk3(k), blk3(lg), blk3(v), blk3(r), _full(norm_g.shape), blk4(s0)],
        out_specs=[blk3(v), blk4(s0)],
        out_shape=[jax.ShapeDtypeStruct(v.shape, F32), jax.ShapeDtypeStruct(s0.shape, F32)],
        scratch_shapes=[pltpu.VMEM(per + (LANES, GLA_DK), F32),
                        pltpu.VMEM(per + (LANES, GLA_DV), F32),
                        pltpu.VMEM(per + (SUBLANES, GLA_DK), F32)],
        compiler_params=_cparams(1),
        name="gla_step",
    )(q, k, lg, v, r, norm_g, s0)


def _trunk(x, mods, weights, attn_fn, gla_proj_fn, gla_fn):
    (w_attn_in, w_attn_out, w_gla_main, w_gla_gd, w_gla_gu, b_gate, norm_g, w_gla_out,
     w1, w2, ln_g, ln_b) = weights
    q, k, v, vt = _attn_proj(x, mods[0], w_attn_in)
    o, attn_state = attn_fn(q, k, v, vt)
    x = _mix_mlp(o, x, mods[0], mods[1], w_attn_out, w1, w2, ln_g, ln_b, 0)
    q, k, lg, v, r = gla_proj_fn(x, mods[2], w_gla_main, w_gla_gd, w_gla_gu, b_gate)
    o, gla_state = gla_fn(q, k, lg, v, r, norm_g)
    x = _mix_mlp(o, x, mods[2], mods[3], w_gla_out, w1, w2, ln_g, ln_b, 1)
    return x, attn_state, gla_state


def kernel(x_prompt, x_sample, cache_k, cache_v, state_gla, c_prompt, c_sample, w_mod, b_mod,
           ln_g, ln_b, attn_w_in, attn_w_out, attn_sinks, gla_w_in, gla_w_gate_up, gla_b_gate,
           gla_norm_g, gla_w_out, mlp_w1, mlp_w2):
    assert x_prompt.shape[0] == 1 and w_mod.shape[0] == DEPTH == 2
    seq = x_prompt.shape[1]
    nseq, n_new = x_sample.shape[0], x_sample.shape[1]
    win = cache_k.shape[2]
    m_s = nseq * n_new

    wq = attn_w_in[0][:, :NQ].reshape(D_MODEL, N_KV_HEADS, GROUP, HEAD_DIM)
    wq = wq.transpose(0, 2, 1, 3).reshape(D_MODEL, NQ)
    w_attn_in = jnp.concatenate([wq, attn_w_in[0][:, NQ:]], axis=1).astype(BF16)
    w_attn_out = attn_w_out[0].reshape(N_KV_HEADS, GROUP, HEAD_DIM, D_MODEL)
    w_attn_out = w_attn_out.transpose(1, 0, 2, 3).reshape(NQ, D_MODEL).astype(BF16)
    n_main = 2 * NK + 2 * NV
    w_gla_main = gla_w_in[0][:, :n_main].astype(BF16)
    w_gla_gd = jnp.pad(gla_w_in[0][:, n_main:], ((0, 0), (0, LANES - GLA_GATE_RANK))).astype(BF16)
    w_gla_gu = jnp.pad(gla_w_gate_up[0], ((0, LANES - GLA_GATE_RANK), (0, 0))).astype(BF16)
    weights = (w_attn_in, w_attn_out, w_gla_main, w_gla_gd, w_gla_gu,
               gla_b_gate[0].reshape(1, NK), gla_norm_g[0].reshape(1, GLA_DV),
               gla_w_out[0].astype(BF16), mlp_w1.astype(BF16), mlp_w2.astype(BF16),
               ln_g.reshape(2 * DEPTH, 1, D_MODEL), ln_b.reshape(2 * DEPTH, 1, D_MODEL))
    sinks = attn_sinks[0]

    pad_rows = (-(m_s + 1)) % SUBLANES
    c_all = jnp.concatenate([jnp.repeat(c_sample, n_new, axis=0), c_prompt,
                             jnp.zeros((pad_rows, D_MODEL), F32)], axis=0)
    mod_all = _adaln_all(c_all, w_mod.reshape(2 * DEPTH, D_MODEL, 3 * D_MODEL),
                         b_mod.reshape(2 * DEPTH, 1, 3 * D_MODEL))
    mods_p = [_Mod(mod_all, p, per_row=False, row0=m_s) for p in range(2 * DEPTH)]
    mods_s = [_Mod(mod_all, p, per_row=True) for p in range(2 * DEPTH)]

    win_p = min(WINDOW, seq)

    def attn_prompt(q, k, v, vt):
        return _band_attn(q, k, vt, sinks), (k[seq - win_p:], v[seq - win_p:])

    y_p, (k_p, v_p), s_p = _trunk(x_prompt[0], mods_p, weights, attn_prompt,
                                  functools.partial(_gla_proj, vr_dtype=BF16), _gla_prompt)

    to_slab = lambda c: c[0].transpose(0, 2, 3, 1).reshape(nseq, NKV, win)
    from_slab = lambda c: c.reshape(nseq, N_KV_HEADS, HEAD_DIM, win).transpose(0, 3, 1, 2)[None]

    def attn_sample(q, k, v, vt):
        o, nk, nv = _dec_attn(
            q.reshape(nseq, n_new * GROUP, NKV), k.reshape(nseq, n_new, NKV),
            v.reshape(nseq, n_new, NKV), to_slab(cache_k), to_slab(cache_v), sinks)
        return o.reshape(m_s, NQ), (from_slab(nk), from_slab(nv))

    def gla_sample(q, k, lg, v, r, norm_g):
        three = lambda a: a.reshape(nseq, n_new, a.shape[-1])
        o, s1 = _gla_step(three(q), three(k), three(lg), three(v), three(r), norm_g,
                          state_gla[0])
        return o.reshape(m_s, NV), s1

    y_s, (k_s, v_s), s_s = _trunk(x_sample.reshape(m_s, D_MODEL), mods_s, weights, attn_sample,
                                  functools.partial(_gla_proj, vr_dtype=F32), gla_sample)

    kv_shape_p = (1, 1, win_p, N_KV_HEADS, HEAD_DIM)
    return (y_p[None], y_s.reshape(nseq, n_new, D_MODEL),
            k_p.reshape(kv_shape_p), v_p.reshape(kv_shape_p), s_p[None, None],
            k_s, v_s, s_s[None])
```

```python
import functools

import jax
import jax.numpy as jnp
from jax import lax
from jax.experimental import pallas as pl
from jax.experimental.pallas import tpu as pltpu

F32 = jnp.float32
BF16 = jnp.bfloat16

D_MODEL = 1024
DEPTH = 2
HEAD_DIM = 64
N_Q_HEADS = 16
N_KV_HEADS = 4
GROUP = 4
WINDOW = 128
ATTN_BLOCK = 128
GLA_HEADS = 4
GLA_DK = 128
GLA_DV = 256
GLA_GATE_RANK = 16
GLA_TAU = 16.0
GLA_CHUNK = 64
D_FF = 4 * D_MODEL
ALPHA = (2.0 * DEPTH) ** 0.25
LN_EPS = 1e-5

NQ = N_Q_HEADS * HEAD_DIM
NKV = N_KV_HEADS * HEAD_DIM
NK = GLA_HEADS * GLA_DK
NV = GLA_HEADS * GLA_DV
LANES = 128
SUBLANES = 8
NEG_BIG = -1e30

ROW_TILE = 512
FF_CHUNK = 1024
ATTN_TILE = 512
GLA_TILE = 512
GLA_SUB = 256
SEQ_BLOCK = 8
SEQ_UNROLL = 2
VMEM_LIMIT = 56 * 1024 * 1024


def _cparams(n_axes):
    return pltpu.CompilerParams(
        dimension_semantics=("arbitrary",) * n_axes,
        vmem_limit_bytes=VMEM_LIMIT,
    )


def _full(shape):
    zeros = (0,) * len(shape)
    return pl.BlockSpec(shape, lambda *_: zeros)


def _layer(arr, idx):
    tail = (0,) * (arr.ndim - 1)
    return pl.BlockSpec((None,) + arr.shape[1:], lambda *_: (idx,) + tail,
                        pipeline_mode=pl.Buffered(1))


def _row_spec(tm, n):
    return pl.BlockSpec((tm, n), lambda i: (i, 0))


class _Mod:
    def __init__(self, arr, p, per_row, row0=0):
        self.arr, self.p, self.per_row, self.row0 = arr, p, per_row, row0

    def spec(self, tm, col):
        p = self.p
        if self.per_row:
            return pl.BlockSpec((None, tm, D_MODEL), lambda i: (p, i, col))
        blk = self.row0 // SUBLANES
        return pl.BlockSpec((None, SUBLANES, D_MODEL), lambda i: (p, blk, col))


def _mod_rows(ref, tm):
    return ref[...] if ref.shape[0] == tm else ref[0:1, :]


def _modulate(x, shift, scale):
    return x * (1.0 + scale) + shift


def _res_ln(x, gate, o, g, b):
    y = ALPHA * x + gate * o
    mu = jnp.mean(y, axis=-1, keepdims=True)
    yc = y - mu
    var = jnp.mean(yc * yc, axis=-1, keepdims=True)
    return yc * lax.rsqrt(var + LN_EPS) * g + b


def _mod_kernel(c_ref, w_ref, b_ref, o_ref):
    c = c_ref[...]
    a = (c * jax.nn.sigmoid(c)).astype(BF16)
    o_ref[...] = jnp.dot(a, w_ref[...].astype(BF16), preferred_element_type=F32) + b_ref[...]


def _adaln_all(c_all, w_mod, b_mod):
    rows = c_all.shape[0]
    tn = 1024
    return pl.pallas_call(
        _mod_kernel,
        grid=(4, 3 * D_MODEL // tn),
        in_specs=[
            pl.BlockSpec((rows, D_MODEL), lambda p, n: (0, 0)),
            pl.BlockSpec((None, D_MODEL, tn), lambda p, n: (p, 0, n)),
            pl.BlockSpec((None, 1, tn), lambda p, n: (p, 0, n)),
        ],
        out_specs=pl.BlockSpec((None, rows, tn), lambda p, n: (p, 0, n)),
        out_shape=jax.ShapeDtypeStruct((4, rows, 3 * D_MODEL), F32),
        compiler_params=_cparams(2),
        name="adaln_mod",
    )(c_all, w_mod, b_mod)


def _attn_proj_kernel(x_ref, sh_ref, sc_ref, w_ref, q_ref, k_ref, v_ref):
    tm = x_ref.shape[0]
    h = _modulate(x_ref[...], _mod_rows(sh_ref, tm), _mod_rows(sc_ref, tm)).astype(BF16)
    q = jnp.dot(h, w_ref[:, 0:NQ], preferred_element_type=F32)
    q_ref[...] = (q * (HEAD_DIM ** -0.5)).astype(BF16)
    k_ref[...] = jnp.dot(h, w_ref[:, NQ:NQ + NKV], preferred_element_type=F32)
    v_ref[...] = jnp.dot(h, w_ref[:, NQ + NKV:NQ + 2 * NKV], preferred_element_type=F32)


def _attn_proj(x, mod, w_in):
    m = x.shape[0]
    tm = min(ROW_TILE, m)
    row = functools.partial(_row_spec, tm)
    return pl.pallas_call(
        _attn_proj_kernel,
        grid=(m // tm,),
        in_specs=[row(D_MODEL), mod.spec(tm, 0), mod.spec(tm, 1), _full(w_in.shape)],
        out_specs=[row(NQ), row(NKV), row(NKV)],
        out_shape=[
            jax.ShapeDtypeStruct((m, NQ), BF16),
            jax.ShapeDtypeStruct((m, NKV), F32),
            jax.ShapeDtypeStruct((m, NKV), F32),
        ],
        compiler_params=_cparams(1),
        name="attn_proj",
    )(x, mod.arr, mod.arr, w_in)


def _gla_proj_kernel(x_ref, sh_ref, sc_ref, w_ref, wgd_ref, wgu_ref, bg_ref,
                     q_ref, k_ref, lg_ref, v_ref, r_ref):
    tm = x_ref.shape[0]
    h = _modulate(x_ref[...], _mod_rows(sh_ref, tm), _mod_rows(sc_ref, tm)).astype(BF16)
    dot = functools.partial(jnp.dot, preferred_element_type=F32)
    q_ref[...] = dot(h, w_ref[:, 0:NK]) * (GLA_DK ** -0.5)
    k_ref[...] = dot(h, w_ref[:, NK:2 * NK])
    v_ref[...] = dot(h, w_ref[:, 2 * NK:2 * NK + NV]).astype(v_ref.dtype)
    r_ref[...] = dot(h, w_ref[:, 2 * NK + NV:2 * NK + 2 * NV]).astype(r_ref.dtype)
    gdown = dot(h, wgd_ref[...])
    pre = dot(gdown.astype(BF16), wgu_ref[...]) + bg_ref[...]
    log_sig = jnp.minimum(pre, 0.0) - jnp.log1p(jnp.exp(-jnp.abs(pre)))
    lg_ref[...] = log_sig / GLA_TAU


def _gla_proj(x, mod, w_main, w_gd, w_gu, b_gate, vr_dtype):
    m = x.shape[0]
    tm = min(ROW_TILE, m)
    row = functools.partial(_row_spec, tm)
    return pl.pallas_call(
        _gla_proj_kernel,
        grid=(m // tm,),
        in_specs=[row(D_MODEL), mod.spec(tm, 0), mod.spec(tm, 1), _full(w_main.shape),
                  _full(w_gd.shape), _full(w_gu.shape), _full(b_gate.shape)],
        out_specs=[row(NK), row(NK), row(NK), row(NV), row(NV)],
        out_shape=[
            jax.ShapeDtypeStruct((m, NK), F32),
            jax.ShapeDtypeStruct((m, NK), F32),
            jax.ShapeDtypeStruct((m, NK), F32),
            jax.ShapeDtypeStruct((m, NV), vr_dtype),
            jax.ShapeDtypeStruct((m, NV), vr_dtype),
        ],
        compiler_params=_cparams(1),
        name="gla_proj",
    )(x, mod.arr, mod.arr, w_main, w_gd, w_gu, b_gate)


def _mix_mlp_kernel(a_ref, x_ref, gt0_ref, sh_ref, sc_ref, gt1_ref, wo_ref, w1_ref, w2_ref,
                    g0_ref, b0_ref, g1_ref, b1_ref, o_ref, acc_ref):
    tm = x_ref.shape[0]
    o = jnp.dot(a_ref[...].astype(BF16), wo_ref[...], preferred_element_type=F32)
    x1 = _res_ln(x_ref[...], _mod_rows(gt0_ref, tm), o, g0_ref[...], b0_ref[...])
    h = _modulate(x1, _mod_rows(sh_ref, tm), _mod_rows(sc_ref, tm)).astype(BF16)
    _mlp_chunks(h, w1_ref, w2_ref, acc_ref, range(D_FF // FF_CHUNK))
    o_ref[...] = _res_ln(x1, _mod_rows(gt1_ref, tm), acc_ref[...], g1_ref[...], b1_ref[...])


def _mix_mlp(a, x, mod_mix, mod_mlp, w_out, w1, w2, ln_g, ln_b, layer):
    m = x.shape[0]
    tm = min(ROW_TILE, m)
    row = functools.partial(_row_spec, tm)
    return pl.pallas_call(
        _mix_mlp_kernel,
        grid=(m // tm,),
        in_specs=[row(a.shape[1]), row(D_MODEL), mod_mix.spec(tm, 2), mod_mlp.spec(tm, 0),
                  mod_mlp.spec(tm, 1), mod_mlp.spec(tm, 2), _full(w_out.shape),
                  _layer(w1, layer), _layer(w2, layer),
                  _layer(ln_g, 2 * layer), _layer(ln_b, 2 * layer),
                  _layer(ln_g, 2 * layer + 1), _layer(ln_b, 2 * layer + 1)],
        out_specs=row(D_MODEL),
        out_shape=jax.ShapeDtypeStruct((m, D_MODEL), F32),
        scratch_shapes=[pltpu.VMEM((tm, D_MODEL), F32)],
        compiler_params=_cparams(1),
        name="mix_mlp",
    )(a, x, mod_mix.arr, mod_mlp.arr, mod_mlp.arr, mod_mlp.arr, w_out, w1, w2,
      ln_g, ln_b, ln_g, ln_b)


def _alibi_slope(head):
    return 2.0 ** (-8.0 * (head + 1) / N_Q_HEADS)


def _softmax_sink(s, sink):
    m = jnp.maximum(jnp.max(s, axis=-1, keepdims=True), sink)
    e = jnp.exp(s - m)
    den = jnp.sum(e, axis=-1, keepdims=True) + jnp.exp(sink - m)
    return e / den


def _band_bias_init(bias_ref):
    blk = ATTN_BLOCK
    c = lax.broadcasted_iota(jnp.int32, (2 * blk, blk), 0)
    r = lax.broadcasted_iota(jnp.int32, (2 * blk, blk), 1)
    dist = blk + r - c
    valid = (dist >= 0) & (dist <= WINDOW)
    distf = dist.astype(F32)
    for head in range(N_Q_HEADS):
        pen = -_alibi_slope(head) * distf
        bias_ref[0, head] = jnp.where(valid, pen, NEG_BIG)
        bias_ref[1, head] = jnp.where(valid & (c >= blk), pen, NEG_BIG)


def _band_scores(q, kk):
    blk = ATTN_BLOCK
    head_of_lane = lax.broadcasted_iota(jnp.int32, (blk, NKV), 1) // HEAD_DIM
    scores = []
    for b in range(q.shape[0] // blk):
        keys = kk[b * blk:(b + 2) * blk]
        for g in range(GROUP):
            qg = q[b * blk:(b + 1) * blk, g * NKV:(g + 1) * NKV]
            qm = jnp.concatenate(
                [jnp.where(head_of_lane == j, qg, jnp.zeros_like(qg))
                 for j in range(N_KV_HEADS)], axis=0)
            scores.append(lax.dot_general(keys, qm, (((1,), (1,)), ((), ())),
                                          preferred_element_type=F32))
    return scores


def _band_outputs(scores, vvt, first_tile, sinks_ref, bias_ref, o_ref):
    blk = ATTN_BLOCK
    for b in range(len(scores) // GROUP):
        table = jnp.where(first_tile, 1, 0) if b == 0 else 0
        vals_t = vvt[:, b * blk:(b + 2) * blk]
        for g in range(GROUP):
            st_all = scores[b * GROUP + g]
            ps = []
            for j in range(N_KV_HEADS):
                head = j * GROUP + g
                sink = sinks_ref[head]
                st = st_all[:, j * blk:(j + 1) * blk] + bias_ref[table, head]
                m = jnp.maximum(jnp.max(st, axis=0, keepdims=True), sink)
                e = jnp.exp(st - m)
                den = jnp.sum(e, axis=0, keepdims=True) + jnp.exp(sink - m)
                ps.append((e * (1.0 / den)).astype(BF16))
            ot_all = jnp.dot(vals_t, jnp.concatenate(ps, axis=1),
                             preferred_element_type=F32)
            ot = jnp.concatenate(
                [ot_all[j * HEAD_DIM:(j + 1) * HEAD_DIM, j * blk:(j + 1) * blk]
                 for j in range(N_KV_HEADS)], axis=0)
            o_ref[b * blk:(b + 1) * blk, g * NKV:(g + 1) * NKV] = ot.T.astype(BF16)


def _mlp_chunks(h, w1_ref, w2_ref, acc_ref, chunks):
    for c in chunks:
        cols = slice(c * FF_CHUNK, (c + 1) * FF_CHUNK)
        a = jnp.dot(h, w1_ref[:, cols], preferred_element_type=F32)
        a = jnp.square(jnp.maximum(a, 0.0)).astype(BF16)
        d = jnp.dot(a, w2_ref[cols, :], preferred_element_type=F32)
        if c == 0:
            acc_ref[...] = d
        else:
            acc_ref[...] += d


def _attn_layer_kernel(sinks_ref, xc_ref, xp_ref, sh0_ref, sc0_ref, gt0_ref, sh1_ref, sc1_ref,
                       gt1_ref, win_ref, wo_ref, w1_ref, w2_ref, g0_ref, b0_ref, g1_ref, b1_ref,
                       y_ref, kl_ref, vl_ref, o_s, kprev_s, vtprev_s, acc_ref, bias_ref):
    i = pl.program_id(0)
    slot = i % 2
    blk = ATTN_BLOCK
    tm = xc_ref.shape[0]
    dot = functools.partial(jnp.dot, preferred_element_type=F32)
    n_chunks = D_FF // FF_CHUNK

    @pl.when(i == 0)
    def _():
        _band_bias_init(bias_ref)
        o_s[...] = jnp.zeros_like(o_s)
        kprev_s[...] = jnp.zeros_like(kprev_s)
        vtprev_s[...] = jnp.zeros_like(vtprev_s)

    x1 = _res_ln(xp_ref[...], gt0_ref[0:1, :], dot(o_s[1 - slot], wo_ref[...]),
                 g0_ref[...], b0_ref[...])
    h_mlp = _modulate(x1, sh1_ref[0:1, :], sc1_ref[0:1, :]).astype(BF16)

    h_in = _modulate(xc_ref[...], sh0_ref[0:1, :], sc0_ref[0:1, :]).astype(BF16)
    q = (dot(h_in, win_ref[:, 0:NQ]) * (HEAD_DIM ** -0.5)).astype(BF16)
    k = dot(h_in, win_ref[:, NQ:NQ + NKV])
    v = dot(h_in, win_ref[:, NQ + NKV:NQ + 2 * NKV])
    kl_ref[...] = k[tm - blk:, :]
    vl_ref[...] = v[tm - blk:, :]
    k_bf = k.astype(BF16)
    vt = v.T.astype(BF16)
    kk = jnp.concatenate([kprev_s[...], k_bf], axis=0)
    vvt = jnp.concatenate([vtprev_s[...], vt], axis=1)
    scores = _band_scores(q, kk)
    kprev_s[...] = k_bf[tm - blk:, :]
    vtprev_s[...] = vt[:, tm - blk:]

    _mlp_chunks(h_mlp, w1_ref, w2_ref, acc_ref, range(0, n_chunks - 1))
    _band_outputs(scores, vvt, i == 0, sinks_ref, bias_ref, o_s.at[slot])
    _mlp_chunks(h_mlp, w1_ref, w2_ref, acc_ref, range(n_chunks - 1, n_chunks))
    y_ref[...] = _res_ln(x1, gt1_ref[0:1, :], acc_ref[...], g1_ref[...], b1_ref[...])


def _attn_layer(x, mod_mix, mod_mlp, w_in, w_out, w1, w2, ln_g, ln_b, sinks, layer):
    m = x.shape[0]
    tm = ATTN_TILE
    blk = ATTN_BLOCK
    n = m // tm
    assert not mod_mix.per_row and not mod_mlp.per_row
    cur = pl.BlockSpec((tm, D_MODEL), lambda i: (jnp.minimum(i, n - 1), 0))
    prev = pl.BlockSpec((tm, D_MODEL), lambda i: (jnp.maximum(i - 1, 0), 0))
    last = pl.BlockSpec((blk, NKV), lambda i: (0, 0))
    return pl.pallas_call(
        _attn_layer_kernel,
        grid=(n + 1,),
        in_specs=[pl.BlockSpec(memory_space=pltpu.SMEM), cur, prev,
                  mod_mix.spec(tm, 0), mod_mix.spec(tm, 1), mod_mix.spec(tm, 2),
                  mod_mlp.spec(tm, 0), mod_mlp.spec(tm, 1), mod_mlp.spec(tm, 2),
                  _full(w_in.shape), _full(w_out.shape), _layer(w1, layer), _layer(w2, layer),
                  _layer(ln_g, 2 * layer), _layer(ln_b, 2 * layer),
                  _layer(ln_g, 2 * layer + 1), _layer(ln_b, 2 * layer + 1)],
        out_specs=[prev, last, last],
        out_shape=[jax.ShapeDtypeStruct((m, D_MODEL), F32),
                   jax.ShapeDtypeStruct((blk, NKV), F32),
                   jax.ShapeDtypeStruct((blk, NKV), F32)],
        scratch_shapes=[pltpu.VMEM((2, tm, NQ), BF16),
                        pltpu.VMEM((blk, NKV), BF16),
                        pltpu.VMEM((NKV, blk), BF16),
                        pltpu.VMEM((tm, D_MODEL), F32),
                        pltpu.VMEM((2, N_Q_HEADS, 2 * blk, blk), F32)],
        compiler_params=_cparams(1),
        name="attn_layer",
    )(sinks, x, x, mod_mix.arr, mod_mix.arr, mod_mix.arr, mod_mlp.arr, mod_mlp.arr,
      mod_mlp.arr, w_in, w_out, w1, w2, ln_g, ln_b, ln_g, ln_b)


def _dec_attn_kernel(sinks_ref, q_ref, kn_ref, vn_ref, ck_ref, cv_ref,
                     o_ref, nk_ref, nv_ref, zk_ref, zv_ref):
    n_new = kn_ref.shape[1]
    win = ck_ref.shape[2]
    rows = n_new * GROUP
    n_rows = N_KV_HEADS * rows
    keep = win - n_new
    row = lax.broadcasted_iota(jnp.int32, (n_rows, 2 * win), 0)
    col = lax.broadcasted_iota(jnp.int32, (n_rows, 2 * win), 1)
    j_r = row // rows
    t_r = (row // GROUP) % n_new
    g_r = row % GROUP
    h_r = j_r * GROUP + g_r
    slope = jnp.exp2(-8.0 * (h_r + 1).astype(F32) / N_Q_HEADS)
    sink = jnp.zeros((n_rows, 1), F32)
    h_col = h_r[:, 0:1]
    for h in range(N_Q_HEADS):
        sink = jnp.where(h_col == h, sinks_ref[h], sink)
    is_key = (col < win) | (col >= win + keep)
    frame = jnp.where(col < win, col, col - keep)
    dist = t_r + win - frame
    valid = is_key & (dist >= 0) & (dist <= WINDOW)
    bias = jnp.where(valid, -slope * dist.astype(F32), NEG_BIG)
    head_of_lane = lax.broadcasted_iota(jnp.int32, (rows, NKV), 1) // HEAD_DIM
    lane_w = lax.broadcasted_iota(jnp.int32, (NKV, win), 1)

    @pl.when(pl.program_id(0) == 0)
    def _():
        zk_ref[...] = jnp.zeros_like(zk_ref)
        zv_ref[...] = jnp.zeros_like(zv_ref)

    def per_seq(s, u):
        k_t = ck_ref[s]
        v_t = cv_ref[s]
        zk_ref[u, keep:win, :] = kn_ref[s]
        zv_ref[u, keep:win, :] = vn_ref[s]
        zk_t = zk_ref[u].T
        zv_t = zv_ref[u].T
        nk_ref[s] = jnp.where(lane_w < keep, pltpu.roll(k_t, keep, 1), zk_t)
        nv_ref[s] = jnp.where(lane_w < keep, pltpu.roll(v_t, keep, 1), zv_t)
        keys = jnp.concatenate([k_t, zk_t], axis=1).astype(BF16)
        vals = jnp.concatenate([v_t, zv_t], axis=1).astype(BF16)
        qs = q_ref[s]
        qbd = jnp.concatenate(
            [jnp.where(head_of_lane == j, qs, jnp.zeros_like(qs)) for j in range(N_KV_HEADS)],
            axis=0)
        sc = jnp.dot(qbd, keys, preferred_element_type=F32) + bias
        p = _softmax_sink(sc, sink).astype(BF16)
        pv = lax.dot_general(p, vals, (((1,), (1,)), ((), ())), preferred_element_type=F32)
        o = jnp.zeros((rows, NKV), F32)
        for j in range(N_KV_HEADS):
            o = o + jnp.where(head_of_lane == j, pv[j * rows:(j + 1) * rows], 0.0)
        o_ref[s] = o.astype(BF16)

    def body(it, carry):
        for u in range(SEQ_UNROLL):
            per_seq(it * SEQ_UNROLL + u, u)
        return carry

    lax.fori_loop(0, q_ref.shape[0] // SEQ_UNROLL, body, 0)


def _dec_attn(q, k_new, v_new, cache_kt, cache_vt, sinks):
    nseq, win = cache_kt.shape[0], cache_kt.shape[2]
    n_new = k_new.shape[1]
    assert win == LANES and n_new < SUBLANES
    sb = SEQ_BLOCK
    blk = lambda a: pl.BlockSpec((sb,) + a.shape[1:], lambda i: (i, 0, 0))
    zshape = (SEQ_UNROLL, win, NKV)
    return pl.pallas_call(
        _dec_attn_kernel,
        grid=(nseq // sb,),
        in_specs=[pl.BlockSpec(memory_space=pltpu.SMEM), blk(q), blk(k_new), blk(v_new),
                  blk(cache_kt), blk(cache_vt)],
        out_specs=[blk(q), blk(cache_kt), blk(cache_vt)],
        out_shape=[
            jax.ShapeDtypeStruct(q.shape, BF16),
            jax.ShapeDtypeStruct(cache_kt.shape, F32),
            jax.ShapeDtypeStruct(cache_vt.shape, F32),
        ],
        scratch_shapes=[pltpu.VMEM(zshape, F32), pltpu.VMEM(zshape, F32)],
        compiler_params=_cparams(1),
        name="dec_attn",
    )(sinks, q, k_new, v_new, cache_kt, cache_vt)


def _split3(x):
    hi = x.astype(BF16)
    r1 = x - hi.astype(F32)
    mid = r1.astype(BF16)
    lo = (r1 - mid.astype(F32)).astype(BF16)
    return hi, mid, lo


def _gla_out(o, r, norm_g):
    ms = jnp.mean(o * o, axis=-1, keepdims=True)
    o = o * lax.rsqrt(ms + LN_EPS) * norm_g
    return o * (r * jax.nn.sigmoid(r))


def _gla_prompt_kernel(q_ref, k_ref, lg_ref, v_ref, r_ref, ng_ref, o_ref, s_out_ref, s_ref):
    i = pl.program_id(0)
    sub = GLA_SUB
    n_sub = q_ref.shape[0] // sub
    ch = GLA_CHUNK
    n_ch = sub // ch
    dot = functools.partial(jnp.dot, preferred_element_type=F32)
    heads = range(GLA_HEADS)
    ks = [slice(h * GLA_DK, (h + 1) * GLA_DK) for h in heads]
    vs = [slice(h * GLA_DV, (h + 1) * GLA_DV) for h in heads]

    @pl.when(i == 0)
    def _():
        s_ref[...] = jnp.zeros_like(s_ref)

    row = lax.broadcasted_iota(jnp.int32, (sub, sub), 0)
    col = lax.broadcasted_iota(jnp.int32, (sub, sub), 1)
    causal = ((row // ch) == (col // ch)) & (col <= row)
    tril = jnp.where(causal, 1.0, 0.0).astype(BF16)
    chunk_of_col = lax.broadcasted_iota(jnp.int32, (GLA_DK, sub), 1) // ch

    qd, kd, kdec, dec_rows = [], [], [], []
    for t in range(n_sub):
        rows = slice(t * sub, (t + 1) * sub)
        hi, mid, lo = _split3(lg_ref[rows, :])
        b = dot(tril, hi) + dot(tril, mid) + dot(tril, lo)
        ends = [b[(c + 1) * ch - 1:(c + 1) * ch, :] for c in range(n_ch)]
        b_end = jnp.concatenate([jnp.broadcast_to(e, (ch, NK)) for e in ends], axis=0)
        q = q_ref[rows, :]
        k = k_ref[rows, :]
        qd.append((q * jnp.exp(b)).astype(BF16))
        kd.append((k * jnp.exp(-b)).astype(BF16))
        kdec.append(k * jnp.exp(b_end - b))
        dec_rows.append(jnp.concatenate(
            [jnp.exp(e) for e in ends] + [jnp.zeros((LANES - n_ch, NK), F32)], axis=0))

    a = [[lax.dot_general(qd[t][:, ks[h]], kd[t][:, ks[h]], (((1,), (1,)), ((), ())),
                          preferred_element_type=F32) for h in heads] for t in range(n_sub)]

    u = []
    for t in range(n_sub):
        rows = slice(t * sub, (t + 1) * sub)
        u_t = []
        for h in heads:
            kdec_t = kdec[t][:, ks[h]].T.astype(BF16)
            stacked = jnp.concatenate(
                [jnp.where(chunk_of_col == c, kdec_t, jnp.zeros_like(kdec_t))
                 for c in range(n_ch)], axis=0)
            u_t.append(dot(stacked, v_ref[rows, vs[h]]))
        u.append(u_t)

    o_intra = []
    for t in range(n_sub):
        rows = slice(t * sub, (t + 1) * sub)
        o_intra.append([dot(jnp.where(causal, a[t][h], 0.0).astype(BF16), v_ref[rows, vs[h]])
                        for h in heads])

    for h in heads:
        s = s_ref[h]
        for t in range(n_sub):
            rows = slice(t * sub, (t + 1) * sub)
            dec_t = dec_rows[t][:, ks[h]].T
            o_inter = []
            for c in range(n_ch):
                o_inter.append(dot(qd[t][c * ch:(c + 1) * ch, ks[h]], s.astype(BF16)))
                s = dec_t[:, c:c + 1] * s + u[t][h][c * GLA_DK:(c + 1) * GLA_DK]
            o = o_intra[t][h] + jnp.concatenate(o_inter, axis=0)
            o_ref[rows, vs[h]] = _gla_out(
                o, r_ref[rows, vs[h]].astype(F32), ng_ref[...]).astype(o_ref.dtype)
        s_ref[h] = s

    s_out_ref[...] = s_ref[...]


def _gla_prompt(q, k, lg, v, r, norm_g):
    m = q.shape[0]
    tt = GLA_TILE
    row = functools.partial(_row_spec, tt)
    state = (GLA_HEADS, GLA_DK, GLA_DV)
    return pl.pallas_call(
        _gla_prompt_kernel,
        grid=(m // tt,),
        in_specs=[row(NK), row(NK), row(NK), row(NV), row(NV), _full(norm_g.shape)],
        out_specs=[row(NV), _full(state)],
        out_shape=[jax.ShapeDtypeStruct((m, NV), BF16), jax.ShapeDtypeStruct(state, F32)],
        scratch_shapes=[pltpu.VMEM(state, F32)],
        compiler_params=_cparams(1),
        name="gla_prompt",
    )(q, k, lg, v, r, norm_g)


def _gla_step_kernel(q_ref, k_ref, lg_ref, v_ref, r_ref, ng_ref, s0_ref,
                     o_ref, s1_ref, z_ref, v_pad_ref, q_pad_ref):
    n_new = q_ref.shape[1]
    dot = functools.partial(jnp.dot, preferred_element_type=F32)
    rowi = lax.broadcasted_iota(jnp.int32, (n_new, NK), 0)
    rowv = lax.broadcasted_iota(jnp.int32, (n_new, GLA_DV), 0)

    @pl.when(pl.program_id(0) == 0)
    def _():
        z_ref[...] = jnp.zeros_like(z_ref)
        v_pad_ref[...] = jnp.zeros_like(v_pad_ref)
        q_pad_ref[...] = jnp.zeros_like(q_pad_ref)

    def per_seq(s, u):
        lg = lg_ref[s]
        b = jnp.zeros_like(lg)
        for t in range(n_new):
            b = b + jnp.where(rowi >= t, jnp.broadcast_to(lg[t:t + 1, :], lg.shape), 0.0)
        b_end = b[n_new - 1:n_new, :]
        q = q_ref[s]
        k = k_ref[s]
        qd = q * jnp.exp(b)
        kd = k * jnp.exp(-b)
        kdec = k * jnp.exp(b_end - b)
        dec = jnp.exp(b_end)
        v = v_ref[s]
        r = r_ref[s]
        for h in range(GLA_HEADS):
            ks = slice(h * GLA_DK, (h + 1) * GLA_DK)
            vs = slice(h * GLA_DV, (h + 1) * GLA_DV)
            v_h = v[:, vs]
            qd_h = qd[:, ks]
            o = jnp.zeros((n_new, GLA_DV), F32)
            for t in range(n_new):
                a_t = jnp.sum(qd_h * kd[t:t + 1, ks], axis=-1, keepdims=True)
                o = o + jnp.where(rowv >= t, a_t * v_h[t:t + 1, :], 0.0)
            s0 = s0_ref[s, h]
            q_pad_ref[u, h, 0:n_new, :] = qd_h
            o = o + dot(q_pad_ref[u, h], s0)[0:n_new]
            z_ref[u, h, 0:n_new, :] = kdec[:, ks]
            z_ref[u, h, n_new:n_new + 1, :] = dec[:, ks]
            z_t = z_ref[u, h].T
            v_pad_ref[u, h, 0:n_new, :] = v_h
            s1_ref[s, h] = z_t[:, n_new:n_new + 1] * s0 + dot(z_t, v_pad_ref[u, h])
            o_ref[s, :, vs] = _gla_out(o, r[:, vs], ng_ref[...])

    def body(it, carry):
        for u in range(SEQ_UNROLL):
            per_seq(it * SEQ_UNROLL + u, u)
        return carry

    lax.fori_loop(0, q_ref.shape[0] // SEQ_UNROLL, body, 0)


def _gla_step(q, k, lg, v, r, norm_g, s0):
    nseq, n_new = q.shape[0], q.shape[1]
    assert n_new <= GLA_CHUNK and n_new < SUBLANES
    sb = SEQ_BLOCK
    blk3 = lambda a: pl.BlockSpec((sb,) + a.shape[1:], lambda i: (i, 0, 0))
    blk4 = lambda a: pl.BlockSpec((sb,) + a.shape[1:], lambda i: (i, 0, 0, 0))
    per = (SEQ_UNROLL, GLA_HEADS)
    return pl.pallas_call(
        _gla_step_kernel,
        grid=(nseq // sb,),
        in_specs=[blk3(q), blk3(k), blk3(lg), blk3(v), blk3(r), _full(norm_g.shape), blk4(s0)],
        out_specs=[blk3(v), blk4(s0)],
        out_shape=[jax.ShapeDtypeStruct(v.shape, F32), jax.ShapeDtypeStruct(s0.shape, F32)],
        scratch_shapes=[pltpu.VMEM(per + (LANES, GLA_DK), F32),
                        pltpu.VMEM(per + (LANES, GLA_DV), F32),
                        pltpu.VMEM(per + (SUBLANES, GLA_DK), F32)],
        compiler_params=_cparams(1),
        name="gla_step",
    )(q, k, lg, v, r, norm_g, s0)


def _gla_layer(x, mods, weights, vr_dtype, gla_fn):
    (_, _, w_gla_main, w_gla_gd, w_gla_gu, b_gate, norm_g, w_gla_out, w1, w2, ln_g, ln_b) = weights
    q, k, lg, v, r = _gla_proj(x, mods[2], w_gla_main, w_gla_gd, w_gla_gu, b_gate, vr_dtype)
    o, gla_state = gla_fn(q, k, lg, v, r, norm_g)
    return _mix_mlp(o, x, mods[2], mods[3], w_gla_out, w1, w2, ln_g, ln_b, 1), gla_state


def kernel(x_prompt, x_sample, cache_k, cache_v, state_gla, c_prompt, c_sample, w_mod, b_mod,
           ln_g, ln_b, attn_w_in, attn_w_out, attn_sinks, gla_w_in, gla_w_gate_up, gla_b_gate,
           gla_norm_g, gla_w_out, mlp_w1, mlp_w2):
    assert x_prompt.shape[0] == 1 and w_mod.shape[0] == DEPTH == 2
    seq = x_prompt.shape[1]
    nseq, n_new = x_sample.shape[0], x_sample.shape[1]
    win = cache_k.shape[2]
    m_s = nseq * n_new

    wq = attn_w_in[0][:, :NQ].reshape(D_MODEL, N_KV_HEADS, GROUP, HEAD_DIM)
    wq = wq.transpose(0, 2, 1, 3).reshape(D_MODEL, NQ)
    w_attn_in = jnp.concatenate([wq, attn_w_in[0][:, NQ:]], axis=1).astype(BF16)
    w_attn_out = attn_w_out[0].reshape(N_KV_HEADS, GROUP, HEAD_DIM, D_MODEL)
    w_attn_out = w_attn_out.transpose(1, 0, 2, 3).reshape(NQ, D_MODEL).astype(BF16)
    n_main = 2 * NK + 2 * NV
    w_gla_main = gla_w_in[0][:, :n_main].astype(BF16)
    w_gla_gd = jnp.pad(gla_w_in[0][:, n_main:], ((0, 0), (0, LANES - GLA_GATE_RANK))).astype(BF16)
    w_gla_gu = jnp.pad(gla_w_gate_up[0], ((0, LANES - GLA_GATE_RANK), (0, 0))).astype(BF16)
    weights = (w_attn_in, w_attn_out, w_gla_main, w_gla_gd, w_gla_gu,
               gla_b_gate[0].reshape(1, NK), gla_norm_g[0].reshape(1, GLA_DV),
               gla_w_out[0].astype(BF16), mlp_w1.astype(BF16), mlp_w2.astype(BF16),
               ln_g.reshape(2 * DEPTH, 1, D_MODEL), ln_b.reshape(2 * DEPTH, 1, D_MODEL))
    sinks = attn_sinks[0]

    pad_rows = (-(m_s + 1)) % SUBLANES
    c_all = jnp.concatenate([jnp.repeat(c_sample, n_new, axis=0), c_prompt,
                             jnp.zeros((pad_rows, D_MODEL), F32)], axis=0)
    mod_all = _adaln_all(c_all, w_mod.reshape(2 * DEPTH, D_MODEL, 3 * D_MODEL),
                         b_mod.reshape(2 * DEPTH, 1, 3 * D_MODEL))
    mods_p = [_Mod(mod_all, p, per_row=False, row0=m_s) for p in range(2 * DEPTH)]
    mods_s = [_Mod(mod_all, p, per_row=True) for p in range(2 * DEPTH)]

    w_attn_in, w_attn_out = weights[0], weights[1]
    w1, w2, ln_g4, ln_b4 = weights[8:12]

    assert seq >= WINDOW == ATTN_BLOCK
    win_p = WINDOW
    x1_p, k_p, v_p = _attn_layer(x_prompt[0], mods_p[0], mods_p[1], w_attn_in, w_attn_out,
                                 w1, w2, ln_g4, ln_b4, sinks, 0)
    y_p, s_p = _gla_layer(x1_p, mods_p, weights, BF16, _gla_prompt)

    to_slab = lambda c: c[0].transpose(0, 2, 3, 1).reshape(nseq, NKV, win)
    from_slab = lambda c: c.reshape(nseq, N_KV_HEADS, HEAD_DIM, win).transpose(0, 3, 1, 2)[None]
    x_s = x_sample.reshape(m_s, D_MODEL)
    q, k, v = _attn_proj(x_s, mods_s[0], w_attn_in)
    o, k_s, v_s = _dec_attn(
        q.reshape(nseq, n_new * GROUP, NKV), k.reshape(nseq, n_new, NKV),
        v.reshape(nseq, n_new, NKV), to_slab(cache_k), to_slab(cache_v), sinks)
    x1_s = _mix_mlp(o.reshape(m_s, NQ), x_s, mods_s[0], mods_s[1], w_attn_out, w1, w2,
                    ln_g4, ln_b4, 0)

    def gla_sample(q, k, lg, v, r, norm_g):
        three = lambda a: a.reshape(nseq, n_new, a.shape[-1])
        o, s1 = _gla_step(three(q), three(k), three(lg), three(v), three(r), norm_g,
                          state_gla[0])
        return o.reshape(m_s, NV), s1

    y_s, s_s = _gla_layer(x1_s, mods_s, weights, F32, gla_sample)
    k_s, v_s = from_slab(k_s), from_slab(v_s)

    kv_shape_p = (1, 1, win_p, N_KV_HEADS, HEAD_DIM)
    return (y_p[None], y_s.reshape(nseq, n_new, D_MODEL),
            k_p.reshape(kv_shape_p), v_p.reshape(kv_shape_p), s_p[None, None],
            k_s, v_s, s_s[None])
```

```python
import functools

import jax
import jax.numpy as jnp
from jax import lax
from jax.experimental import pallas as pl
from jax.experimental.pallas import tpu as pltpu

F32 = jnp.float32
BF16 = jnp.bfloat16

D_MODEL = 1024
DEPTH = 2
HEAD_DIM = 64
N_Q_HEADS = 16
N_KV_HEADS = 4
GROUP = 4
WINDOW = 128
ATTN_BLOCK = 128
GLA_HEADS = 4
GLA_DK = 128
GLA_DV = 256
GLA_GATE_RANK = 16
GLA_TAU = 16.0
GLA_CHUNK = 64
D_FF = 4 * D_MODEL
ALPHA = (2.0 * DEPTH) ** 0.25
LN_EPS = 1e-5

NQ = N_Q_HEADS * HEAD_DIM
NKV = N_KV_HEADS * HEAD_DIM
NK = GLA_HEADS * GLA_DK
NV = GLA_HEADS * GLA_DV
LANES = 128
SUBLANES = 8
NEG_BIG = -1e30

ROW_TILE = 512
FF_CHUNK = 1024
ATTN_TILE = 512
GLA_TILE = 512
GLA_SUB = 256
SEQ_BLOCK = 8
SEQ_UNROLL = 2
VMEM_LIMIT = 56 * 1024 * 1024


def _cparams(n_axes):
    return pltpu.CompilerParams(
        dimension_semantics=("arbitrary",) * n_axes,
        vmem_limit_bytes=VMEM_LIMIT,
    )


def _full(shape):
    zeros = (0,) * len(shape)
    return pl.BlockSpec(shape, lambda *_: zeros)


def _layer(arr, idx):
    tail = (0,) * (arr.ndim - 1)
    return pl.BlockSpec((None,) + arr.shape[1:], lambda *_: (idx,) + tail,
                        pipeline_mode=pl.Buffered(1))


def _row_spec(tm, n):
    return pl.BlockSpec((tm, n), lambda i: (i, 0))


class _Mod:
    def __init__(self, arr, p, per_row, row0=0):
        self.arr, self.p, self.per_row, self.row0 = arr, p, per_row, row0

    def spec(self, tm, col):
        p = self.p
        if self.per_row:
            return pl.BlockSpec((None, tm, D_MODEL), lambda i: (p, i, col))
        blk = self.row0 // SUBLANES
        return pl.BlockSpec((None, SUBLANES, D_MODEL), lambda i: (p, blk, col))


def _mod_rows(ref, tm):
    return ref[...] if ref.shape[0] == tm else ref[0:1, :]


def _modulate(x, shift, scale):
    return x * (1.0 + scale) + shift


def _res_ln(x, gate, o, g, b):
    y = ALPHA * x + gate * o
    mu = jnp.mean(y, axis=-1, keepdims=True)
    yc = y - mu
    var = jnp.mean(yc * yc, axis=-1, keepdims=True)
    return yc * lax.rsqrt(var + LN_EPS) * g + b


def _mod_kernel(c_ref, w_ref, b_ref, o_ref):
    c = c_ref[...]
    a = (c * jax.nn.sigmoid(c)).astype(BF16)
    o_ref[...] = jnp.dot(a, w_ref[...].astype(BF16), preferred_element_type=F32) + b_ref[...]


def _adaln_all(c_all, w_mod, b_mod):
    rows = c_all.shape[0]
    tn = 1024
    return pl.pallas_call(
        _mod_kernel,
        grid=(4, 3 * D_MODEL // tn),
        in_specs=[
            pl.BlockSpec((rows, D_MODEL), lambda p, n: (0, 0)),
            pl.BlockSpec((None, D_MODEL, tn), lambda p, n: (p, 0, n)),
            pl.BlockSpec((None, 1, tn), lambda p, n: (p, 0, n)),
        ],
        out_specs=pl.BlockSpec((None, rows, tn), lambda p, n: (p, 0, n)),
        out_shape=jax.ShapeDtypeStruct((4, rows, 3 * D_MODEL), F32),
        compiler_params=_cparams(2),
        name="adaln_mod",
    )(c_all, w_mod, b_mod)


def _attn_proj_kernel(x_ref, sh_ref, sc_ref, w_ref, q_ref, k_ref, v_ref):
    tm = x_ref.shape[0]
    h = _modulate(x_ref[...], _mod_rows(sh_ref, tm), _mod_rows(sc_ref, tm)).astype(BF16)
    q = jnp.dot(h, w_ref[:, 0:NQ], preferred_element_type=F32)
    q_ref[...] = (q * (HEAD_DIM ** -0.5)).astype(BF16)
    k_ref[...] = jnp.dot(h, w_ref[:, NQ:NQ + NKV], preferred_element_type=F32)
    v_ref[...] = jnp.dot(h, w_ref[:, NQ + NKV:NQ + 2 * NKV], preferred_element_type=F32)


def _attn_proj(x, mod, w_in):
    m = x.shape[0]
    tm = min(ROW_TILE, m)
    row = functools.partial(_row_spec, tm)
    return pl.pallas_call(
        _attn_proj_kernel,
        grid=(m // tm,),
        in_specs=[row(D_MODEL), mod.spec(tm, 0), mod.spec(tm, 1), _full(w_in.shape)],
        out_specs=[row(NQ), row(NKV), row(NKV)],
        out_shape=[
            jax.ShapeDtypeStruct((m, NQ), BF16),
            jax.ShapeDtypeStruct((m, NKV), F32),
            jax.ShapeDtypeStruct((m, NKV), F32),
        ],
        compiler_params=_cparams(1),
        name="attn_proj",
    )(x, mod.arr, mod.arr, w_in)


def _gla_proj_kernel(x_ref, sh_ref, sc_ref, w_ref, wgd_ref, wgu_ref, bg_ref,
                     q_ref, k_ref, lg_ref, v_ref, r_ref):
    tm = x_ref.shape[0]
    h = _modulate(x_ref[...], _mod_rows(sh_ref, tm), _mod_rows(sc_ref, tm)).astype(BF16)
    dot = functools.partial(jnp.dot, preferred_element_type=F32)
    gdown = dot(h, wgd_ref[...])
    q_ref[...] = dot(h, w_ref[:, 0:NK]) * (GLA_DK ** -0.5)
    k_ref[...] = dot(h, w_ref[:, NK:2 * NK])
    pre = dot(gdown.astype(BF16), wgu_ref[...]) + bg_ref[...]
    v_ref[...] = dot(h, w_ref[:, 2 * NK:2 * NK + NV]).astype(v_ref.dtype)
    r_ref[...] = dot(h, w_ref[:, 2 * NK + NV:2 * NK + 2 * NV]).astype(r_ref.dtype)
    log_sig = jnp.minimum(pre, 0.0) - jnp.log1p(jnp.exp(-jnp.abs(pre)))
    lg_ref[...] = log_sig / GLA_TAU


def _gla_proj(x, mod, w_main, w_gd, w_gu, b_gate, vr_dtype):
    m = x.shape[0]
    tm = min(ROW_TILE, m)
    row = functools.partial(_row_spec, tm)
    return pl.pallas_call(
        _gla_proj_kernel,
        grid=(m // tm,),
        in_specs=[row(D_MODEL), mod.spec(tm, 0), mod.spec(tm, 1), _full(w_main.shape),
                  _full(w_gd.shape), _full(w_gu.shape), _full(b_gate.shape)],
        out_specs=[row(NK), row(NK), row(NK), row(NV), row(NV)],
        out_shape=[
            jax.ShapeDtypeStruct((m, NK), F32),
            jax.ShapeDtypeStruct((m, NK), F32),
            jax.ShapeDtypeStruct((m, NK), F32),
            jax.ShapeDtypeStruct((m, NV), vr_dtype),
            jax.ShapeDtypeStruct((m, NV), vr_dtype),
        ],
        compiler_params=_cparams(1),
        name="gla_proj",
    )(x, mod.arr, mod.arr, w_main, w_gd, w_gu, b_gate)


def _mix_mlp_kernel(a_ref, x_ref, gt0_ref, sh_ref, sc_ref, gt1_ref, wo_ref, w1_ref, w2_ref,
                    g0_ref, b0_ref, g1_ref, b1_ref, o_ref, acc_ref):
    tm = x_ref.shape[0]
    o = jnp.dot(a_ref[...].astype(BF16), wo_ref[...], preferred_element_type=F32)
    x1 = _res_ln(x_ref[...], _mod_rows(gt0_ref, tm), o, g0_ref[...], b0_ref[...])
    h = _modulate(x1, _mod_rows(sh_ref, tm), _mod_rows(sc_ref, tm)).astype(BF16)
    _mlp_chunks(h, w1_ref, w2_ref, acc_ref, range(D_FF // FF_CHUNK))
    o_ref[...] = _res_ln(x1, _mod_rows(gt1_ref, tm), acc_ref[...], g1_ref[...], b1_ref[...])


def _mix_mlp(a, x, mod_mix, mod_mlp, w_out, w1, w2, ln_g, ln_b, layer):
    m = x.shape[0]
    tm = min(ROW_TILE, m)
    row = functools.partial(_row_spec, tm)
    return pl.pallas_call(
        _mix_mlp_kernel,
        grid=(m // tm,),
        in_specs=[row(a.shape[1]), row(D_MODEL), mod_mix.spec(tm, 2), mod_mlp.spec(tm, 0),
                  mod_mlp.spec(tm, 1), mod_mlp.spec(tm, 2), _full(w_out.shape),
                  _layer(w1, layer), _layer(w2, layer),
                  _layer(ln_g, 2 * layer), _layer(ln_b, 2 * layer),
                  _layer(ln_g, 2 * layer + 1), _layer(ln_b, 2 * layer + 1)],
        out_specs=row(D_MODEL),
        out_shape=jax.ShapeDtypeStruct((m, D_MODEL), F32),
        scratch_shapes=[pltpu.VMEM((tm, D_MODEL), F32)],
        compiler_params=_cparams(1),
        name="mix_mlp",
    )(a, x, mod_mix.arr, mod_mlp.arr, mod_mlp.arr, mod_mlp.arr, w_out, w1, w2,
      ln_g, ln_b, ln_g, ln_b)


def _alibi_slope(head):
    return 2.0 ** (-8.0 * (head + 1) / N_Q_HEADS)


def _softmax_sink(s, sink):
    m = jnp.maximum(jnp.max(s, axis=-1, keepdims=True), sink)
    e = jnp.exp(s - m)
    den = jnp.sum(e, axis=-1, keepdims=True) + jnp.exp(sink - m)
    return e / den


def _band_bias_init(bias_ref):
    blk = ATTN_BLOCK
    c = lax.broadcasted_iota(jnp.int32, (2 * blk, blk), 0)
    r = lax.broadcasted_iota(jnp.int32, (2 * blk, blk), 1)
    dist = blk + r - c
    valid = (dist >= 0) & (dist <= WINDOW)
    distf = dist.astype(F32)
    for head in range(N_Q_HEADS):
        pen = -_alibi_slope(head) * distf
        bias_ref[0, head] = jnp.where(valid, pen, NEG_BIG)
        bias_ref[1, head] = jnp.where(valid & (c >= blk), pen, NEG_BIG)


def _band_scores(q, kk):
    blk = ATTN_BLOCK
    head_of_lane = lax.broadcasted_iota(jnp.int32, (blk, NKV), 1) // HEAD_DIM
    scores = []
    for b in range(q.shape[0] // blk):
        keys = kk[b * blk:(b + 2) * blk]
        for g in range(GROUP):
            qg = q[b * blk:(b + 1) * blk, g * NKV:(g + 1) * NKV]
            qm = jnp.concatenate(
                [jnp.where(head_of_lane == j, qg, jnp.zeros_like(qg))
                 for j in range(N_KV_HEADS)], axis=0)
            scores.append(lax.dot_general(keys, qm, (((1,), (1,)), ((), ())),
                                          preferred_element_type=F32))
    return scores


def _band_outputs(scores, vvt, first_tile, sinks_ref, bias_ref, o_ref):
    blk = ATTN_BLOCK
    for b in range(len(scores) // GROUP):
        table = jnp.where(first_tile, 1, 0) if b == 0 else 0
        vals_t = vvt[:, b * blk:(b + 2) * blk]
        for g in range(GROUP):
            st_all = scores[b * GROUP + g]
            ps = []
            for j in range(N_KV_HEADS):
                head = j * GROUP + g
                sink = sinks_ref[head]
                st = st_all[:, j * blk:(j + 1) * blk] + bias_ref[table, head]
                m = jnp.maximum(jnp.max(st, axis=0, keepdims=True), sink)
                e = jnp.exp(st - m)
                den = jnp.sum(e, axis=0, keepdims=True) + jnp.exp(sink - m)
                ps.append((e * (1.0 / den)).astype(BF16))
            ot_all = jnp.dot(vals_t, jnp.concatenate(ps, axis=1),
                             preferred_element_type=F32)
            ot = jnp.concatenate(
                [ot_all[j * HEAD_DIM:(j + 1) * HEAD_DIM, j * blk:(j + 1) * blk]
                 for j in range(N_KV_HEADS)], axis=0)
            o_ref[b * blk:(b + 1) * blk, g * NKV:(g + 1) * NKV] = ot.T.astype(BF16)


def _mlp_chunks(h, w1_ref, w2_ref, acc_ref, chunks):
    for c in chunks:
        cols = slice(c * FF_CHUNK, (c + 1) * FF_CHUNK)
        a = jnp.dot(h, w1_ref[:, cols], preferred_element_type=F32)
        a = jnp.square(jnp.maximum(a, 0.0)).astype(BF16)
        d = jnp.dot(a, w2_ref[cols, :], preferred_element_type=F32)
        if c == 0:
            acc_ref[...] = d
        else:
            acc_ref[...] += d


def _attn_layer_kernel(sinks_ref, xc_ref, xp_ref, sh0_ref, sc0_ref, gt0_ref, sh1_ref, sc1_ref,
                       gt1_ref, win_ref, wo_ref, w1_ref, w2_ref, g0_ref, b0_ref, g1_ref, b1_ref,
                       y_ref, kl_ref, vl_ref, o_s, kprev_s, vtprev_s, acc_ref, bias_ref):
    i = pl.program_id(0)
    slot = i % 2
    blk = ATTN_BLOCK
    tm = xc_ref.shape[0]
    dot = functools.partial(jnp.dot, preferred_element_type=F32)
    n_chunks = D_FF // FF_CHUNK

    @pl.when(i == 0)
    def _():
        _band_bias_init(bias_ref)
        o_s[...] = jnp.zeros_like(o_s)
        kprev_s[...] = jnp.zeros_like(kprev_s)
        vtprev_s[...] = jnp.zeros_like(vtprev_s)

    x1 = _res_ln(xp_ref[...], gt0_ref[0:1, :], dot(o_s[1 - slot], wo_ref[...]),
                 g0_ref[...], b0_ref[...])
    h_mlp = _modulate(x1, sh1_ref[0:1, :], sc1_ref[0:1, :]).astype(BF16)

    h_in = _modulate(xc_ref[...], sh0_ref[0:1, :], sc0_ref[0:1, :]).astype(BF16)
    q = (dot(h_in, win_ref[:, 0:NQ]) * (HEAD_DIM ** -0.5)).astype(BF16)
    k = dot(h_in, win_ref[:, NQ:NQ + NKV])
    v = dot(h_in, win_ref[:, NQ + NKV:NQ + 2 * NKV])
    kl_ref[...] = k[tm - blk:, :]
    vl_ref[...] = v[tm - blk:, :]
    k_bf = k.astype(BF16)
    vt = v.T.astype(BF16)
    kk = jnp.concatenate([kprev_s[...], k_bf], axis=0)
    vvt = jnp.concatenate([vtprev_s[...], vt], axis=1)
    scores = _band_scores(q, kk)
    kprev_s[...] = k_bf[tm - blk:, :]
    vtprev_s[...] = vt[:, tm - blk:]

    _mlp_chunks(h_mlp, w1_ref, w2_ref, acc_ref, range(n_chunks))
    _band_outputs(scores, vvt, i == 0, sinks_ref, bias_ref, o_s.at[slot])
    y_ref[...] = _res_ln(x1, gt1_ref[0:1, :], acc_ref[...], g1_ref[...], b1_ref[...])


def _attn_layer(x, mod_mix, mod_mlp, w_in, w_out, w1, w2, ln_g, ln_b, sinks, layer):
    m = x.shape[0]
    tm = ATTN_TILE
    blk = ATTN_BLOCK
    n = m // tm
    assert not mod_mix.per_row and not mod_mlp.per_row
    cur = pl.BlockSpec((tm, D_MODEL), lambda i: (jnp.minimum(i, n - 1), 0))
    prev = pl.BlockSpec((tm, D_MODEL), lambda i: (jnp.maximum(i - 1, 0), 0))
    last = pl.BlockSpec((blk, NKV), lambda i: (0, 0))
    return pl.pallas_call(
        _attn_layer_kernel,
        grid=(n + 1,),
        in_specs=[pl.BlockSpec(memory_space=pltpu.SMEM), cur, prev,
                  mod_mix.spec(tm, 0), mod_mix.spec(tm, 1), mod_mix.spec(tm, 2),
                  mod_mlp.spec(tm, 0), mod_mlp.spec(tm, 1), mod_mlp.spec(tm, 2),
                  _full(w_in.shape), _full(w_out.shape), _layer(w1, layer), _layer(w2, layer),
                  _layer(ln_g, 2 * layer), _layer(ln_b, 2 * layer),
                  _layer(ln_g, 2 * layer + 1), _layer(ln_b, 2 * layer + 1)],
        out_specs=[prev, last, last],
        out_shape=[jax.ShapeDtypeStruct((m, D_MODEL), F32),
                   jax.ShapeDtypeStruct((blk, NKV), F32),
                   jax.ShapeDtypeStruct((blk, NKV), F32)],
        scratch_shapes=[pltpu.VMEM((2, tm, NQ), BF16),
                        pltpu.VMEM((blk, NKV), BF16),
                        pltpu.VMEM((NKV, blk), BF16),
                        pltpu.VMEM((tm, D_MODEL), F32),
                        pltpu.VMEM((2, N_Q_HEADS, 2 * blk, blk), F32)],
        compiler_params=_cparams(1),
        name="attn_layer",
    )(sinks, x, x, mod_mix.arr, mod_mix.arr, mod_mix.arr, mod_mlp.arr, mod_mlp.arr,
      mod_mlp.arr, w_in, w_out, w1, w2, ln_g, ln_b, ln_g, ln_b)


def _dec_attn_kernel(sinks_ref, q_ref, kn_ref, vn_ref, ck_ref, cv_ref,
                     o_ref, nk_ref, nv_ref, zk_ref, zv_ref):
    n_new = kn_ref.shape[1]
    win = ck_ref.shape[2]
    rows = n_new * GROUP
    n_rows = N_KV_HEADS * rows
    keep = win - n_new
    row = lax.broadcasted_iota(jnp.int32, (n_rows, 2 * win), 0)
    col = lax.broadcasted_iota(jnp.int32, (n_rows, 2 * win), 1)
    j_r = row // rows
    t_r = (row // GROUP) % n_new
    g_r = row % GROUP
    h_r = j_r * GROUP + g_r
    slope = jnp.exp2(-8.0 * (h_r + 1).astype(F32) / N_Q_HEADS)
    sink = jnp.zeros((n_rows, 1), F32)
    h_col = h_r[:, 0:1]
    for h in range(N_Q_HEADS):
        sink = jnp.where(h_col == h, sinks_ref[h], sink)
    is_key = (col < win) | (col >= win + keep)
    frame = jnp.where(col < win, col, col - keep)
    dist = t_r + win - frame
    valid = is_key & (dist >= 0) & (dist <= WINDOW)
    bias = jnp.where(valid, -slope * dist.astype(F32), NEG_BIG)
    head_of_lane = lax.broadcasted_iota(jnp.int32, (rows, NKV), 1) // HEAD_DIM
    lane_w = lax.broadcasted_iota(jnp.int32, (NKV, win), 1)

    @pl.when(pl.program_id(0) == 0)
    def _():
        zk_ref[...] = jnp.zeros_like(zk_ref)
        zv_ref[...] = jnp.zeros_like(zv_ref)

    def per_seq(s, u):
        k_t = ck_ref[s]
        v_t = cv_ref[s]
        zk_ref[u, keep:win, :] = kn_ref[s]
        zv_ref[u, keep:win, :] = vn_ref[s]
        zk_t = zk_ref[u].T
        zv_t = zv_ref[u].T
        nk_ref[s] = jnp.where(lane_w < keep, pltpu.roll(k_t, keep, 1), zk_t)
        nv_ref[s] = jnp.where(lane_w < keep, pltpu.roll(v_t, keep, 1), zv_t)
        keys = jnp.concatenate([k_t, zk_t], axis=1).astype(BF16)
        vals = jnp.concatenate([v_t, zv_t], axis=1).astype(BF16)
        qs = q_ref[s]
        qbd = jnp.concatenate(
            [jnp.where(head_of_lane == j, qs, jnp.zeros_like(qs)) for j in range(N_KV_HEADS)],
            axis=0)
        sc = jnp.dot(qbd, keys, preferred_element_type=F32) + bias
        p = _softmax_sink(sc, sink).astype(BF16)
        pv = lax.dot_general(p, vals, (((1,), (1,)), ((), ())), preferred_element_type=F32)
        o = jnp.zeros((rows, NKV), F32)
        for j in range(N_KV_HEADS):
            o = o + jnp.where(head_of_lane == j, pv[j * rows:(j + 1) * rows], 0.0)
        o_ref[s] = o.astype(BF16)

    def body(it, carry):
        for u in range(SEQ_UNROLL):
            per_seq(it * SEQ_UNROLL + u, u)
        return carry

    lax.fori_loop(0, q_ref.shape[0] // SEQ_UNROLL, body, 0)


def _dec_attn(q, k_new, v_new, cache_kt, cache_vt, sinks):
    nseq, win = cache_kt.shape[0], cache_kt.shape[2]
    n_new = k_new.shape[1]
    assert win == LANES and n_new < SUBLANES
    sb = SEQ_BLOCK
    blk = lambda a: pl.BlockSpec((sb,) + a.shape[1:], lambda i: (i, 0, 0))
    zshape = (SEQ_UNROLL, win, NKV)
    return pl.pallas_call(
        _dec_attn_kernel,
        grid=(nseq // sb,),
        in_specs=[pl.BlockSpec(memory_space=pltpu.SMEM), blk(q), blk(k_new), blk(v_new),
                  blk(cache_kt), blk(cache_vt)],
        out_specs=[blk(q), blk(cache_kt), blk(cache_vt)],
        out_shape=[
            jax.ShapeDtypeStruct(q.shape, BF16),
            jax.ShapeDtypeStruct(cache_kt.shape, F32),
            jax.ShapeDtypeStruct(cache_vt.shape, F32),
        ],
        scratch_shapes=[pltpu.VMEM(zshape, F32), pltpu.VMEM(zshape, F32)],
        compiler_params=_cparams(1),
        name="dec_attn",
    )(sinks, q, k_new, v_new, cache_kt, cache_vt)


def _split2(x):
    hi = x.astype(BF16)
    lo = (x - hi.astype(F32)).astype(BF16)
    return hi, lo


def _gla_out(o, r, norm_g):
    ms = jnp.mean(o * o, axis=-1, keepdims=True)
    o = o * lax.rsqrt(ms + LN_EPS) * norm_g
    return o * (r * jax.nn.sigmoid(r))


def _gla_prompt_kernel(q_ref, k_ref, lg_ref, v_ref, r_ref, ng_ref, o_ref, s_out_ref, s_ref):
    i = pl.program_id(0)
    sub = GLA_SUB
    n_sub = q_ref.shape[0] // sub
    ch = GLA_CHUNK
    n_ch = sub // ch
    dot = functools.partial(jnp.dot, preferred_element_type=F32)
    heads = range(GLA_HEADS)
    ks = [slice(h * GLA_DK, (h + 1) * GLA_DK) for h in heads]
    vs = [slice(h * GLA_DV, (h + 1) * GLA_DV) for h in heads]

    @pl.when(i == 0)
    def _():
        s_ref[...] = jnp.zeros_like(s_ref)

    row = lax.broadcasted_iota(jnp.int32, (sub, sub), 0)
    col = lax.broadcasted_iota(jnp.int32, (sub, sub), 1)
    causal = ((row // ch) == (col // ch)) & (col <= row)
    tril = jnp.where(causal, 1.0, 0.0).astype(BF16)
    chunk_of_col = lax.broadcasted_iota(jnp.int32, (GLA_DK, sub), 1) // ch

    qd, kd, kdec, dec_rows = [], [], [], []
    for t in range(n_sub):
        rows = slice(t * sub, (t + 1) * sub)
        hi, lo = _split2(lg_ref[rows, :])
        b = dot(tril, hi) + dot(tril, lo)
        ends = [b[(c + 1) * ch - 1:(c + 1) * ch, :] for c in range(n_ch)]
        b_end = jnp.concatenate([jnp.broadcast_to(e, (ch, NK)) for e in ends], axis=0)
        q = q_ref[rows, :]
        k = k_ref[rows, :]
        qd.append((q * jnp.exp(b)).astype(BF16))
        kd.append((k * jnp.exp(-b)).astype(BF16))
        kdec.append(k * jnp.exp(b_end - b))
        dec_rows.append(jnp.concatenate(
            [jnp.exp(e) for e in ends] + [jnp.zeros((LANES - n_ch, NK), F32)], axis=0))

    a = [[lax.dot_general(qd[t][:, ks[h]], kd[t][:, ks[h]], (((1,), (1,)), ((), ())),
                          preferred_element_type=F32) for h in heads] for t in range(n_sub)]

    u = []
    for t in range(n_sub):
        rows = slice(t * sub, (t + 1) * sub)
        u_t = []
        for h in heads:
            kdec_t = kdec[t][:, ks[h]].T.astype(BF16)
            stacked = jnp.concatenate(
                [jnp.where(chunk_of_col == c, kdec_t, jnp.zeros_like(kdec_t))
                 for c in range(n_ch)], axis=0)
            u_t.append(dot(stacked, v_ref[rows, vs[h]]))
        u.append(u_t)

    o_intra = []
    for t in range(n_sub):
        rows = slice(t * sub, (t + 1) * sub)
        o_intra.append([dot(jnp.where(causal, a[t][h], 0.0).astype(BF16), v_ref[rows, vs[h]])
                        for h in heads])

    for h in heads:
        s = s_ref[h]
        for t in range(n_sub):
            rows = slice(t * sub, (t + 1) * sub)
            dec_t = dec_rows[t][:, ks[h]].T
            o_inter = []
            for c in range(n_ch):
                o_inter.append(dot(qd[t][c * ch:(c + 1) * ch, ks[h]], s.astype(BF16)))
                s = dec_t[:, c:c + 1] * s + u[t][h][c * GLA_DK:(c + 1) * GLA_DK]
            o = o_intra[t][h] + jnp.concatenate(o_inter, axis=0)
            o_ref[rows, vs[h]] = _gla_out(
                o, r_ref[rows, vs[h]].astype(F32), ng_ref[...]).astype(o_ref.dtype)
        s_ref[h] = s

    s_out_ref[...] = s_ref[...]


def _gla_prompt(q, k, lg, v, r, norm_g):
    m = q.shape[0]
    tt = GLA_TILE
    row = functools.partial(_row_spec, tt)
    state = (GLA_HEADS, GLA_DK, GLA_DV)
    return pl.pallas_call(
        _gla_prompt_kernel,
        grid=(m // tt,),
        in_specs=[row(NK), row(NK), row(NK), row(NV), row(NV), _full(norm_g.shape)],
        out_specs=[row(NV), _full(state)],
        out_shape=[jax.ShapeDtypeStruct((m, NV), BF16), jax.ShapeDtypeStruct(state, F32)],
        scratch_shapes=[pltpu.VMEM(state, F32)],
        compiler_params=_cparams(1),
        name="gla_prompt",
    )(q, k, lg, v, r, norm_g)


def _gla_step_kernel(q_ref, k_ref, lg_ref, v_ref, r_ref, ng_ref, s0_ref,
                     o_ref, s1_ref, z_ref, v_pad_ref, q_pad_ref):
    n_new = q_ref.shape[1]
    dot = functools.partial(jnp.dot, preferred_element_type=F32)
    rowi = lax.broadcasted_iota(jnp.int32, (n_new, NK), 0)
    rowv = lax.broadcasted_iota(jnp.int32, (n_new, GLA_DV), 0)

    @pl.when(pl.program_id(0) == 0)
    def _():
        z_ref[...] = jnp.zeros_like(z_ref)
        v_pad_ref[...] = jnp.zeros_like(v_pad_ref)
        q_pad_ref[...] = jnp.zeros_like(q_pad_ref)

    def per_seq(s, u):
        lg = lg_ref[s]
        b = jnp.zeros_like(lg)
        for t in range(n_new):
            b = b + jnp.where(rowi >= t, jnp.broadcast_to(lg[t:t + 1, :], lg.shape), 0.0)
        b_end = b[n_new - 1:n_new, :]
        q = q_ref[s]
        k = k_ref[s]
        qd = q * jnp.exp(b)
        kd = k * jnp.exp(-b)
        kdec = k * jnp.exp(b_end - b)
        dec = jnp.exp(b_end)
        v = v_ref[s]
        r = r_ref[s]
        for h in range(GLA_HEADS):
            ks = slice(h * GLA_DK, (h + 1) * GLA_DK)
            vs = slice(h * GLA_DV, (h + 1) * GLA_DV)
            v_h = v[:, vs]
            qd_h = qd[:, ks]
            o = jnp.zeros((n_new, GLA_DV), F32)
            for t in range(n_new):
                a_t = jnp.sum(qd_h * kd[t:t + 1, ks], axis=-1, keepdims=True)
                o = o + jnp.where(rowv >= t, a_t * v_h[t:t + 1, :], 0.0)
            s0 = s0_ref[s, h]
            q_pad_ref[u, h, 0:n_new, :] = qd_h
            o = o + dot(q_pad_ref[u, h], s0)[0:n_new]
            z_ref[u, h, 0:n_new, :] = kdec[:, ks]
            z_ref[u, h, n_new:n_new + 1, :] = dec[:, ks]
            z_t = z_ref[u, h].T
            v_pad_ref[u, h, 0:n_new, :] = v_h
            s1_ref[s, h] = z_t[:, n_new:n_new + 1] * s0 + dot(z_t, v_pad_ref[u, h])
            o_ref[s, :, vs] = _gla_out(o, r[:, vs], ng_ref[...])

    def body(it, carry):
        for u in range(SEQ_UNROLL):
            per_seq(it * SEQ_UNROLL + u, u)
        return carry

    lax.fori_loop(0, q_ref.shape[0] // SEQ_UNROLL, body, 0)


def _gla_step(q, k, lg, v, r, norm_g, s0):
    nseq, n_new = q.shape[0], q.shape[1]
    assert n_new <= GLA_CHUNK and n_new < SUBLANES
    sb = SEQ_BLOCK
    blk3 = lambda a: pl.BlockSpec((sb,) + a.shape[1:], lambda i: (i, 0, 0))
    blk4 = lambda a: pl.BlockSpec((sb,) + a.shape[1:], lambda i: (i, 0, 0, 0))
    per = (SEQ_UNROLL, GLA_HEADS)
    return pl.pallas_call(
        _gla_step_kernel,
        grid=(nseq // sb,),
        in_specs=[blk3(q), blk3(k), blk3(lg), blk3(v), blk3(r), _full(norm_g.shape), blk4(s0)],
        out_specs=[blk3(v), blk4(s0)],
        out_shape=[jax.ShapeDtypeStruct(v.shape, F32), jax.ShapeDtypeStruct(s0.shape, F32)],
        scratch_shapes=[pltpu.VMEM(per + (LANES, GLA_DK), F32),
                        pltpu.VMEM(per + (LANES, GLA_DV), F32),
                        pltpu.VMEM(per + (SUBLANES, GLA_DK), F32)],
        compiler_params=_cparams(1),
        name="gla_step",
    )(q, k, lg, v, r, norm_g, s0)


def _gla_layer(x, mods, weights, vr_dtype, gla_fn):
    (_, _, w_gla_main, w_gla_gd, w_gla_gu, b_gate, norm_g, w_gla_out, w1, w2, ln_g, ln_b) = weights
    q, k, lg, v, r = _gla_proj(x, mods[2], w_gla_main, w_gla_gd, w_gla_gu, b_gate, vr_dtype)
    o, gla_state = gla_fn(q, k, lg, v, r, norm_g)
    return _mix_mlp(o, x, mods[2], mods[3], w_gla_out, w1, w2, ln_g, ln_b, 1), gla_state


def kernel(x_prompt, x_sample, cache_k, cache_v, state_gla, c_prompt, c_sample, w_mod, b_mod,
           ln_g, ln_b, attn_w_in, attn_w_out, attn_sinks, gla_w_in, gla_w_gate_up, gla_b_gate,
           gla_norm_g, gla_w_out, mlp_w1, mlp_w2):
    assert x_prompt.shape[0] == 1 and w_mod.shape[0] == DEPTH == 2
    seq = x_prompt.shape[1]
    nseq, n_new = x_sample.shape[0], x_sample.shape[1]
    win = cache_k.shape[2]
    m_s = nseq * n_new

    wq = attn_w_in[0][:, :NQ].reshape(D_MODEL, N_KV_HEADS, GROUP, HEAD_DIM)
    wq = wq.transpose(0, 2, 1, 3).reshape(D_MODEL, NQ)
    w_attn_in = jnp.concatenate([wq, attn_w_in[0][:, NQ:]], axis=1).astype(BF16)
    w_attn_out = attn_w_out[0].reshape(N_KV_HEADS, GROUP, HEAD_DIM, D_MODEL)
    w_attn_out = w_attn_out.transpose(1, 0, 2, 3).reshape(NQ, D_MODEL).astype(BF16)
    n_main = 2 * NK + 2 * NV
    w_gla_main = gla_w_in[0][:, :n_main].astype(BF16)
    w_gla_gd = jnp.pad(gla_w_in[0][:, n_main:], ((0, 0), (0, LANES - GLA_GATE_RANK))).astype(BF16)
    w_gla_gu = jnp.pad(gla_w_gate_up[0], ((0, LANES - GLA_GATE_RANK), (0, 0))).astype(BF16)
    weights = (w_attn_in, w_attn_out, w_gla_main, w_gla_gd, w_gla_gu,
               gla_b_gate[0].reshape(1, NK), gla_norm_g[0].reshape(1, GLA_DV),
               gla_w_out[0].astype(BF16), mlp_w1.astype(BF16), mlp_w2.astype(BF16),
               ln_g.reshape(2 * DEPTH, 1, D_MODEL), ln_b.reshape(2 * DEPTH, 1, D_MODEL))
    sinks = attn_sinks[0]

    pad_rows = (-(m_s + 1)) % SUBLANES
    c_all = jnp.concatenate([jnp.repeat(c_sample, n_new, axis=0), c_prompt,
                             jnp.zeros((pad_rows, D_MODEL), F32)], axis=0)
    mod_all = _adaln_all(c_all, w_mod.reshape(2 * DEPTH, D_MODEL, 3 * D_MODEL),
                         b_mod.reshape(2 * DEPTH, 1, 3 * D_MODEL))
    mods_p = [_Mod(mod_all, p, per_row=False, row0=m_s) for p in range(2 * DEPTH)]
    mods_s = [_Mod(mod_all, p, per_row=True) for p in range(2 * DEPTH)]

    w_attn_in, w_attn_out = weights[0], weights[1]
    w1, w2, ln_g4, ln_b4 = weights[8:12]

    assert seq >= WINDOW == ATTN_BLOCK
    win_p = WINDOW
    x1_p, k_p, v_p = _attn_layer(x_prompt[0], mods_p[0], mods_p[1], w_attn_in, w_attn_out,
                                 w1, w2, ln_g4, ln_b4, sinks, 0)
    y_p, s_p = _gla_layer(x1_p, mods_p, weights, BF16, _gla_prompt)

    to_slab = lambda c: c[0].transpose(0, 2, 3, 1).reshape(nseq, NKV, win)
    from_slab = lambda c: c.reshape(nseq, N_KV_HEADS, HEAD_DIM, win).transpose(0, 3, 1, 2)[None]
    x_s = x_sample.reshape(m_s, D_MODEL)
    q, k, v = _attn_proj(x_s, mods_s[0], w_attn_in)
    o, k_s, v_s = _dec_attn(
        q.reshape(nseq, n_new * GROUP, NKV), k.reshape(nseq, n_new, NKV),
        v.reshape(nseq, n_new, NKV), to_slab(cache_k), to_slab(cache_v), sinks)
    x1_s = _mix_mlp(o.reshape(m_s, NQ), x_s, mods_s[0], mods_s[1], w_attn_out, w1, w2,
                    ln_g4, ln_b4, 0)

    def gla_sample(q, k, lg, v, r, norm_g):
        three = lambda a: a.reshape(nseq, n_new, a.shape[-1])
        o, s1 = _gla_step(three(q), three(k), three(lg), three(v), three(r), norm_g,
                          state_gla[0])
        return o.reshape(m_s, NV), s1

    y_s, s_s = _gla_layer(x1_s, mods_s, weights, F32, gla_sample)
    k_s, v_s = from_slab(k_s), from_slab(v_s)

    kv_shape_p = (1, 1, win_p, N_KV_HEADS, HEAD_DIM)
    return (y_p[None], y_s.reshape(nseq, n_new, D_MODEL),
            k_p.reshape(kv_shape_p), v_p.reshape(kv_shape_p), s_p[None, None],
            k_s, v_s, s_s[None])
```

```python
import functools

import jax
import jax.numpy as jnp
from jax import lax
from jax.experimental import pallas as pl
from jax.experimental.pallas import tpu as pltpu

F32 = jnp.float32
BF16 = jnp.bfloat16

D_MODEL = 1024
DEPTH = 2
HEAD_DIM = 64
N_Q_HEADS = 16
N_KV_HEADS = 4
GROUP = 4
WINDOW = 128
ATTN_BLOCK = 128
GLA_HEADS = 4
GLA_DK = 128
GLA_DV = 256
GLA_GATE_RANK = 16
GLA_TAU = 16.0
GLA_CHUNK = 64
D_FF = 4 * D_MODEL
ALPHA = (2.0 * DEPTH) ** 0.25
LN_EPS = 1e-5

NQ = N_Q_HEADS * HEAD_DIM
NKV = N_KV_HEADS * HEAD_DIM
NK = GLA_HEADS * GLA_DK
NV = GLA_HEADS * GLA_DV
LANES = 128
SUBLANES = 8
NEG_BIG = -1e30

ROW_TILE = 512
FF_CHUNK = 1024
ATTN_TILE = 512
GLA_TILE = 512
GLA_SUB = 256
SEQ_BLOCK = 8
SEQ_UNROLL = 2
VMEM_LIMIT = 56 * 1024 * 1024


def _cparams(n_axes):
    return pltpu.CompilerParams(
        dimension_semantics=("arbitrary",) * n_axes,
        vmem_limit_bytes=VMEM_LIMIT,
    )


def _full(shape):
    zeros = (0,) * len(shape)
    return pl.BlockSpec(shape, lambda *_: zeros)


def _layer(arr, idx):
    tail = (0,) * (arr.ndim - 1)
    return pl.BlockSpec((None,) + arr.shape[1:], lambda *_: (idx,) + tail,
                        pipeline_mode=pl.Buffered(1))


def _row_spec(tm, n):
    return pl.BlockSpec((tm, n), lambda i: (i, 0))


class _Mod:
    def __init__(self, arr, p, per_row, row0=0):
        self.arr, self.p, self.per_row, self.row0 = arr, p, per_row, row0

    def spec(self, tm, col):
        p = self.p
        if self.per_row:
            return pl.BlockSpec((None, tm, D_MODEL), lambda i: (p, i, col))
        blk = self.row0 // SUBLANES
        return pl.BlockSpec((None, SUBLANES, D_MODEL), lambda i: (p, blk, col))


def _mod_rows(ref, tm):
    return ref[...] if ref.shape[0] == tm else ref[0:1, :]


def _modulate(x, shift, scale):
    return x * (1.0 + scale) + shift


def _res_ln(x, gate, o, g, b):
    y = ALPHA * x + gate * o
    mu = jnp.mean(y, axis=-1, keepdims=True)
    yc = y - mu
    var = jnp.mean(yc * yc, axis=-1, keepdims=True)
    return yc * lax.rsqrt(var + LN_EPS) * g + b


def _mod_kernel(c_ref, w_ref, b_ref, o_ref):
    c = c_ref[...]
    a = (c * jax.nn.sigmoid(c)).astype(BF16)
    o_ref[...] = jnp.dot(a, w_ref[...].astype(BF16), preferred_element_type=F32) + b_ref[...]


def _adaln_all(c_all, w_mod, b_mod):
    rows = c_all.shape[0]
    tn = 1024
    return pl.pallas_call(
        _mod_kernel,
        grid=(4, 3 * D_MODEL // tn),
        in_specs=[
            pl.BlockSpec((rows, D_MODEL), lambda p, n: (0, 0)),
            pl.BlockSpec((None, D_MODEL, tn), lambda p, n: (p, 0, n)),
            pl.BlockSpec((None, 1, tn), lambda p, n: (p, 0, n)),
        ],
        out_specs=pl.BlockSpec((None, rows, tn), lambda p, n: (p, 0, n)),
        out_shape=jax.ShapeDtypeStruct((4, rows, 3 * D_MODEL), F32),
        compiler_params=_cparams(2),
        name="adaln_mod",
    )(c_all, w_mod, b_mod)


def _attn_proj_kernel(x_ref, sh_ref, sc_ref, w_ref, q_ref, k_ref, v_ref):
    tm = x_ref.shape[0]
    h = _modulate(x_ref[...], _mod_rows(sh_ref, tm), _mod_rows(sc_ref, tm)).astype(BF16)
    q = jnp.dot(h, w_ref[:, 0:NQ], preferred_element_type=F32)
    q_ref[...] = (q * (HEAD_DIM ** -0.5)).astype(BF16)
    k_ref[...] = jnp.dot(h, w_ref[:, NQ:NQ + NKV], preferred_element_type=F32)
    v_ref[...] = jnp.dot(h, w_ref[:, NQ + NKV:NQ + 2 * NKV], preferred_element_type=F32)


def _attn_proj(x, mod, w_in):
    m = x.shape[0]
    tm = min(ROW_TILE, m)
    row = functools.partial(_row_spec, tm)
    return pl.pallas_call(
        _attn_proj_kernel,
        grid=(m // tm,),
        in_specs=[row(D_MODEL), mod.spec(tm, 0), mod.spec(tm, 1), _full(w_in.shape)],
        out_specs=[row(NQ), row(NKV), row(NKV)],
        out_shape=[
            jax.ShapeDtypeStruct((m, NQ), BF16),
            jax.ShapeDtypeStruct((m, NKV), F32),
            jax.ShapeDtypeStruct((m, NKV), F32),
        ],
        compiler_params=_cparams(1),
        name="attn_proj",
    )(x, mod.arr, mod.arr, w_in)


def _gla_proj_kernel(x_ref, sh_ref, sc_ref, w_ref, wgd_ref, wgu_ref, bg_ref,
                     q_ref, k_ref, lg_ref, v_ref, r_ref):
    tm = x_ref.shape[0]
    h = _modulate(x_ref[...], _mod_rows(sh_ref, tm), _mod_rows(sc_ref, tm)).astype(BF16)
    dot = functools.partial(jnp.dot, preferred_element_type=F32)
    gdown = dot(h, wgd_ref[...])
    q_ref[...] = dot(h, w_ref[:, 0:NK]) * (GLA_DK ** -0.5)
    k_ref[...] = dot(h, w_ref[:, NK:2 * NK])
    pre = dot(gdown.astype(BF16), wgu_ref[...]) + bg_ref[...]
    v_ref[...] = dot(h, w_ref[:, 2 * NK:2 * NK + NV]).astype(v_ref.dtype)
    r_ref[...] = dot(h, w_ref[:, 2 * NK + NV:2 * NK + 2 * NV]).astype(r_ref.dtype)
    log_sig = jnp.minimum(pre, 0.0) - jnp.log1p(jnp.exp(-jnp.abs(pre)))
    lg_ref[...] = log_sig / GLA_TAU


def _gla_proj(x, mod, w_main, w_gd, w_gu, b_gate, vr_dtype):
    m = x.shape[0]
    tm = min(ROW_TILE, m)
    row = functools.partial(_row_spec, tm)
    return pl.pallas_call(
        _gla_proj_kernel,
        grid=(m // tm,),
        in_specs=[row(D_MODEL), mod.spec(tm, 0), mod.spec(tm, 1), _full(w_main.shape),
                  _full(w_gd.shape), _full(w_gu.shape), _full(b_gate.shape)],
        out_specs=[row(NK), row(NK), row(NK), row(NV), row(NV)],
        out_shape=[
            jax.ShapeDtypeStruct((m, NK), F32),
            jax.ShapeDtypeStruct((m, NK), F32),
            jax.ShapeDtypeStruct((m, NK), F32),
            jax.ShapeDtypeStruct((m, NV), vr_dtype),
            jax.ShapeDtypeStruct((m, NV), vr_dtype),
        ],
        compiler_params=_cparams(1),
        name="gla_proj",
    )(x, mod.arr, mod.arr, w_main, w_gd, w_gu, b_gate)


def _mix_mlp_kernel(a_ref, x_ref, gt0_ref, sh_ref, sc_ref, gt1_ref, wo_ref, w1_ref, w2_ref,
                    g0_ref, b0_ref, g1_ref, b1_ref, o_ref, acc_ref):
    tm = x_ref.shape[0]
    o = jnp.dot(a_ref[...].astype(BF16), wo_ref[...], preferred_element_type=F32)
    x1 = _res_ln(x_ref[...], _mod_rows(gt0_ref, tm), o, g0_ref[...], b0_ref[...])
    h = _modulate(x1, _mod_rows(sh_ref, tm), _mod_rows(sc_ref, tm)).astype(BF16)
    _mlp_chunks(h, w1_ref, w2_ref, acc_ref, range(D_FF // FF_CHUNK))
    o_ref[...] = _res_ln(x1, _mod_rows(gt1_ref, tm), acc_ref[...], g1_ref[...], b1_ref[...])


def _mix_mlp(a, x, mod_mix, mod_mlp, w_out, w1, w2, ln_g, ln_b, layer):
    m = x.shape[0]
    tm = min(ROW_TILE, m)
    row = functools.partial(_row_spec, tm)
    return pl.pallas_call(
        _mix_mlp_kernel,
        grid=(m // tm,),
        in_specs=[row(a.shape[1]), row(D_MODEL), mod_mix.spec(tm, 2), mod_mlp.spec(tm, 0),
                  mod_mlp.spec(tm, 1), mod_mlp.spec(tm, 2), _full(w_out.shape),
                  _full(w1.shape), _full(w2.shape),
                  _layer(ln_g, 2 * layer), _layer(ln_b, 2 * layer),
                  _layer(ln_g, 2 * layer + 1), _layer(ln_b, 2 * layer + 1)],
        out_specs=row(D_MODEL),
        out_shape=jax.ShapeDtypeStruct((m, D_MODEL), F32),
        scratch_shapes=[pltpu.VMEM((tm, D_MODEL), F32)],
        compiler_params=_cparams(1),
        name="mix_mlp",
    )(a, x, mod_mix.arr, mod_mlp.arr, mod_mlp.arr, mod_mlp.arr, w_out, w1, w2,
      ln_g, ln_b, ln_g, ln_b)


def _mix_mlp_stream_kernel(a_ref, x_ref, gt0_ref, sh_ref, sc_ref, gt1_ref, wo_ref, w1_ref, w2_ref,
                           g0_ref, b0_ref, g1_ref, b1_ref, y_ref, w1b_ref, w2b_ref,
                           x1_s, h_s, acc_ref):
    c = pl.program_id(0)

    @pl.when(c == 0)
    def _():
        o = jnp.dot(a_ref[...].astype(BF16), wo_ref[...], preferred_element_type=F32)
        x1 = _res_ln(x_ref[...], gt0_ref[...], o, g0_ref[...], b0_ref[...])
        x1_s[...] = x1
        h_s[...] = _modulate(x1, sh_ref[...], sc_ref[...]).astype(BF16)
        acc_ref[...] = jnp.zeros_like(acc_ref)

    w1c = w1_ref[...].astype(BF16)
    w2c = w2_ref[...].astype(BF16)
    w1b_ref[...] = w1c
    w2b_ref[...] = w2c
    a = jnp.dot(h_s[...], w1c, preferred_element_type=F32)
    a = jnp.square(jnp.maximum(a, 0.0)).astype(BF16)
    acc_ref[...] += jnp.dot(a, w2c, preferred_element_type=F32)

    @pl.when(c == pl.num_programs(0) - 1)
    def _():
        y_ref[...] = _res_ln(x1_s[...], gt1_ref[...], acc_ref[...], g1_ref[...], b1_ref[...])


def _mix_mlp_stream(a, x, mod_mix, mod_mlp, w_out, w1_f32, w2_f32, ln_g, ln_b, layer):
    m = x.shape[0]
    assert m <= ROW_TILE and mod_mix.per_row and mod_mlp.per_row
    fc = FF_CHUNK
    whole = lambda n: pl.BlockSpec((m, n), lambda c: (0, 0))
    mspec = lambda mod, col: pl.BlockSpec((None, m, D_MODEL), lambda c: (mod.p, 0, col))
    return pl.pallas_call(
        _mix_mlp_stream_kernel,
        grid=(D_FF // fc,),
        in_specs=[whole(a.shape[1]), whole(D_MODEL), mspec(mod_mix, 2), mspec(mod_mlp, 0),
                  mspec(mod_mlp, 1), mspec(mod_mlp, 2), _full(w_out.shape),
                  pl.BlockSpec((None, D_MODEL, fc), lambda c: (layer, 0, c)),
                  pl.BlockSpec((None, fc, D_MODEL), lambda c: (layer, c, 0)),
                  _layer(ln_g, 2 * layer), _layer(ln_b, 2 * layer),
                  _layer(ln_g, 2 * layer + 1), _layer(ln_b, 2 * layer + 1)],
        out_specs=[whole(D_MODEL),
                   pl.BlockSpec((D_MODEL, fc), lambda c: (0, c)),
                   pl.BlockSpec((fc, D_MODEL), lambda c: (c, 0))],
        out_shape=[jax.ShapeDtypeStruct((m, D_MODEL), F32),
                   jax.ShapeDtypeStruct((D_MODEL, D_FF), BF16),
                   jax.ShapeDtypeStruct((D_FF, D_MODEL), BF16)],
        scratch_shapes=[pltpu.VMEM((m, D_MODEL), F32), pltpu.VMEM((m, D_MODEL), BF16),
                        pltpu.VMEM((m, D_MODEL), F32)],
        compiler_params=_cparams(1),
        name="mix_mlp_stream",
    )(a, x, mod_mix.arr, mod_mlp.arr, mod_mlp.arr, mod_mlp.arr, w_out, w1_f32, w2_f32,
      ln_g, ln_b, ln_g, ln_b)


def _alibi_slope(head):
    return 2.0 ** (-8.0 * (head + 1) / N_Q_HEADS)


def _softmax_sink(s, sink):
    m = jnp.maximum(jnp.max(s, axis=-1, keepdims=True), sink)
    e = jnp.exp(s - m)
    den = jnp.sum(e, axis=-1, keepdims=True) + jnp.exp(sink - m)
    return e / den


def _band_bias_init(bias_ref):
    blk = ATTN_BLOCK
    c = lax.broadcasted_iota(jnp.int32, (2 * blk, blk), 0)
    r = lax.broadcasted_iota(jnp.int32, (2 * blk, blk), 1)
    dist = blk + r - c
    valid = (dist >= 0) & (dist <= WINDOW)
    distf = dist.astype(F32)
    for head in range(N_Q_HEADS):
        pen = -_alibi_slope(head) * distf
        bias_ref[0, head] = jnp.where(valid, pen, NEG_BIG)
        bias_ref[1, head] = jnp.where(valid & (c >= blk), pen, NEG_BIG)


def _band_scores(q, kk):
    blk = ATTN_BLOCK
    head_of_lane = lax.broadcasted_iota(jnp.int32, (blk, NKV), 1) // HEAD_DIM
    scores = []
    for b in range(q.shape[0] // blk):
        keys = kk[b * blk:(b + 2) * blk]
        for g in range(GROUP):
            qg = q[b * blk:(b + 1) * blk, g * NKV:(g + 1) * NKV]
            qm = jnp.concatenate(
                [jnp.where(head_of_lane == j, qg, jnp.zeros_like(qg))
                 for j in range(N_KV_HEADS)], axis=0)
            scores.append(lax.dot_general(keys, qm, (((1,), (1,)), ((), ())),
                                          preferred_element_type=F32))
    return scores


def _band_outputs(scores, vvt, first_tile, sinks_ref, bias_ref, o_ref):
    blk = ATTN_BLOCK
    for b in range(len(scores) // GROUP):
        table = jnp.where(first_tile, 1, 0) if b == 0 else 0
        vals_t = vvt[:, b * blk:(b + 2) * blk]
        for g in range(GROUP):
            st_all = scores[b * GROUP + g]
            ps = []
            for j in range(N_KV_HEADS):
                head = j * GROUP + g
                sink = sinks_ref[head]
                st = st_all[:, j * blk:(j + 1) * blk] + bias_ref[table, head]
                m = jnp.maximum(jnp.max(st, axis=0, keepdims=True), sink)
                e = jnp.exp(st - m)
                den = jnp.sum(e, axis=0, keepdims=True) + jnp.exp(sink - m)
                ps.append((e * (1.0 / den)).astype(BF16))
            ot_all = jnp.dot(vals_t, jnp.concatenate(ps, axis=1),
                             preferred_element_type=F32)
            ot = jnp.concatenate(
                [ot_all[j * HEAD_DIM:(j + 1) * HEAD_DIM, j * blk:(j + 1) * blk]
                 for j in range(N_KV_HEADS)], axis=0)
            o_ref[b * blk:(b + 1) * blk, g * NKV:(g + 1) * NKV] = ot.T.astype(BF16)


def _mlp_chunks(h, w1_ref, w2_ref, acc_ref, chunks):
    for c in chunks:
        cols = slice(c * FF_CHUNK, (c + 1) * FF_CHUNK)
        a = jnp.dot(h, w1_ref[:, cols], preferred_element_type=F32)
        a = jnp.square(jnp.maximum(a, 0.0)).astype(BF16)
        d = jnp.dot(a, w2_ref[cols, :], preferred_element_type=F32)
        if c == 0:
            acc_ref[...] = d
        else:
            acc_ref[...] += d


def _attn_layer_kernel(sinks_ref, xc_ref, xp_ref, sh0_ref, sc0_ref, gt0_ref, sh1_ref, sc1_ref,
                       gt1_ref, win_ref, wo_ref, w1_ref, w2_ref, g0_ref, b0_ref, g1_ref, b1_ref,
                       y_ref, kl_ref, vl_ref, o_s, kprev_s, vtprev_s, acc_ref, bias_ref):
    i = pl.program_id(0)
    slot = i % 2
    blk = ATTN_BLOCK
    tm = xc_ref.shape[0]
    dot = functools.partial(jnp.dot, preferred_element_type=F32)
    n_chunks = D_FF // FF_CHUNK

    @pl.when(i == 0)
    def _():
        _band_bias_init(bias_ref)
        o_s[...] = jnp.zeros_like(o_s)
        kprev_s[...] = jnp.zeros_like(kprev_s)
        vtprev_s[...] = jnp.zeros_like(vtprev_s)

    x1 = _res_ln(xp_ref[...], gt0_ref[0:1, :], dot(o_s[1 - slot], wo_ref[...]),
                 g0_ref[...], b0_ref[...])
    h_mlp = _modulate(x1, sh1_ref[0:1, :], sc1_ref[0:1, :]).astype(BF16)

    h_in = _modulate(xc_ref[...], sh0_ref[0:1, :], sc0_ref[0:1, :]).astype(BF16)
    q = (dot(h_in, win_ref[:, 0:NQ]) * (HEAD_DIM ** -0.5)).astype(BF16)
    k = dot(h_in, win_ref[:, NQ:NQ + NKV])
    v = dot(h_in, win_ref[:, NQ + NKV:NQ + 2 * NKV])
    kl_ref[...] = k[tm - blk:, :]
    vl_ref[...] = v[tm - blk:, :]
    k_bf = k.astype(BF16)
    vt = v.T.astype(BF16)
    kk = jnp.concatenate([kprev_s[...], k_bf], axis=0)
    vvt = jnp.concatenate([vtprev_s[...], vt], axis=1)
    scores = _band_scores(q, kk)
    kprev_s[...] = k_bf[tm - blk:, :]
    vtprev_s[...] = vt[:, tm - blk:]

    _mlp_chunks(h_mlp, w1_ref, w2_ref, acc_ref, range(n_chunks))
    _band_outputs(scores, vvt, i == 0, sinks_ref, bias_ref, o_s.at[slot])
    y_ref[...] = _res_ln(x1, gt1_ref[0:1, :], acc_ref[...], g1_ref[...], b1_ref[...])


def _attn_layer(x, mod_mix, mod_mlp, w_in, w_out, w1, w2, ln_g, ln_b, sinks, layer):
    m = x.shape[0]
    tm = ATTN_TILE
    blk = ATTN_BLOCK
    n = m // tm
    assert not mod_mix.per_row and not mod_mlp.per_row
    cur = pl.BlockSpec((tm, D_MODEL), lambda i: (jnp.minimum(i, n - 1), 0))
    prev = pl.BlockSpec((tm, D_MODEL), lambda i: (jnp.maximum(i - 1, 0), 0))
    last = pl.BlockSpec((blk, NKV), lambda i: (0, 0))
    return pl.pallas_call(
        _attn_layer_kernel,
        grid=(n + 1,),
        in_specs=[pl.BlockSpec(memory_space=pltpu.SMEM), cur, prev,
                  mod_mix.spec(tm, 0), mod_mix.spec(tm, 1), mod_mix.spec(tm, 2),
                  mod_mlp.spec(tm, 0), mod_mlp.spec(tm, 1), mod_mlp.spec(tm, 2),
                  _full(w_in.shape), _full(w_out.shape), _full(w1.shape), _full(w2.shape),
                  _layer(ln_g, 2 * layer), _layer(ln_b, 2 * layer),
                  _layer(ln_g, 2 * layer + 1), _layer(ln_b, 2 * layer + 1)],
        out_specs=[prev, last, last],
        out_shape=[jax.ShapeDtypeStruct((m, D_MODEL), F32),
                   jax.ShapeDtypeStruct((blk, NKV), F32),
                   jax.ShapeDtypeStruct((blk, NKV), F32)],
        scratch_shapes=[pltpu.VMEM((2, tm, NQ), BF16),
                        pltpu.VMEM((blk, NKV), BF16),
                        pltpu.VMEM((NKV, blk), BF16),
                        pltpu.VMEM((tm, D_MODEL), F32),
                        pltpu.VMEM((2, N_Q_HEADS, 2 * blk, blk), F32)],
        compiler_params=_cparams(1),
        name="attn_layer",
    )(sinks, x, x, mod_mix.arr, mod_mix.arr, mod_mix.arr, mod_mlp.arr, mod_mlp.arr,
      mod_mlp.arr, w_in, w_out, w1, w2, ln_g, ln_b, ln_g, ln_b)


def _dec_attn_kernel(sinks_ref, q_ref, kn_ref, vn_ref, ck_ref, cv_ref,
                     o_ref, nk_ref, nv_ref, zk_ref, zv_ref):
    n_new = kn_ref.shape[1]
    win = ck_ref.shape[2]
    rows = n_new * GROUP
    n_rows = N_KV_HEADS * rows
    keep = win - n_new
    row = lax.broadcasted_iota(jnp.int32, (n_rows, 2 * win), 0)
    col = lax.broadcasted_iota(jnp.int32, (n_rows, 2 * win), 1)
    j_r = row // rows
    t_r = (row // GROUP) % n_new
    g_r = row % GROUP
    h_r = j_r * GROUP + g_r
    slope = jnp.exp2(-8.0 * (h_r + 1).astype(F32) / N_Q_HEADS)
    sink = jnp.zeros((n_rows, 1), F32)
    h_col = h_r[:, 0:1]
    for h in range(N_Q_HEADS):
        sink = jnp.where(h_col == h, sinks_ref[h], sink)
    is_key = (col < win) | (col >= win + keep)
    frame = jnp.where(col < win, col, col - keep)
    dist = t_r + win - frame
    valid = is_key & (dist >= 0) & (dist <= WINDOW)
    bias = jnp.where(valid, -slope * dist.astype(F32), NEG_BIG)
    head_of_lane = lax.broadcasted_iota(jnp.int32, (rows, NKV), 1) // HEAD_DIM
    lane_w = lax.broadcasted_iota(jnp.int32, (NKV, win), 1)

    @pl.when(pl.program_id(0) == 0)
    def _():
        zk_ref[...] = jnp.zeros_like(zk_ref)
        zv_ref[...] = jnp.zeros_like(zv_ref)

    def per_seq(s, u):
        k_t = ck_ref[s]
        v_t = cv_ref[s]
        zk_ref[u, keep:win, :] = kn_ref[s]
        zv_ref[u, keep:win, :] = vn_ref[s]
        zk_t = zk_ref[u].T
        zv_t = zv_ref[u].T
        nk_ref[s] = jnp.where(lane_w < keep, pltpu.roll(k_t, keep, 1), zk_t)
        nv_ref[s] = jnp.where(lane_w < keep, pltpu.roll(v_t, keep, 1), zv_t)
        keys = jnp.concatenate([k_t, zk_t], axis=1).astype(BF16)
        vals = jnp.concatenate([v_t, zv_t], axis=1).astype(BF16)
        qs = q_ref[s]
        qbd = jnp.concatenate(
            [jnp.where(head_of_lane == j, qs, jnp.zeros_like(qs)) for j in range(N_KV_HEADS)],
            axis=0)
        sc = jnp.dot(qbd, keys, preferred_element_type=F32) + bias
        p = _softmax_sink(sc, sink).astype(BF16)
        pv = lax.dot_general(p, vals, (((1,), (1,)), ((), ())), preferred_element_type=F32)
        o = jnp.zeros((rows, NKV), F32)
        for j in range(N_KV_HEADS):
            o = o + jnp.where(head_of_lane == j, pv[j * rows:(j + 1) * rows], 0.0)
        o_ref[s] = o.astype(BF16)

    def body(it, carry):
        for u in range(SEQ_UNROLL):
            per_seq(it * SEQ_UNROLL + u, u)
        return carry

    lax.fori_loop(0, q_ref.shape[0] // SEQ_UNROLL, body, 0)


def _dec_attn(q, k_new, v_new, cache_kt, cache_vt, sinks):
    nseq, win = cache_kt.shape[0], cache_kt.shape[2]
    n_new = k_new.shape[1]
    assert win == LANES and n_new < SUBLANES
    sb = SEQ_BLOCK
    blk = lambda a: pl.BlockSpec((sb,) + a.shape[1:], lambda i: (i, 0, 0))
    zshape = (SEQ_UNROLL, win, NKV)
    return pl.pallas_call(
        _dec_attn_kernel,
        grid=(nseq // sb,),
        in_specs=[pl.BlockSpec(memory_space=pltpu.SMEM), blk(q), blk(k_new), blk(v_new),
                  blk(cache_kt), blk(cache_vt)],
        out_specs=[blk(q), blk(cache_kt), blk(cache_vt)],
        out_shape=[
            jax.ShapeDtypeStruct(q.shape, BF16),
            jax.ShapeDtypeStruct(cache_kt.shape, F32),
            jax.ShapeDtypeStruct(cache_vt.shape, F32),
        ],
        scratch_shapes=[pltpu.VMEM(zshape, F32), pltpu.VMEM(zshape, F32)],
        compiler_params=_cparams(1),
        name="dec_attn",
    )(sinks, q, k_new, v_new, cache_kt, cache_vt)


def _split2(x):
    hi = x.astype(BF16)
    lo = (x - hi.astype(F32)).astype(BF16)
    return hi, lo


def _gla_out(o, r, norm_g):
    ms = jnp.mean(o * o, axis=-1, keepdims=True)
    o = o * lax.rsqrt(ms + LN_EPS) * norm_g
    return o * (r * jax.nn.sigmoid(r))


def _gla_prompt_kernel(q_ref, k_ref, lg_ref, v_ref, r_ref, ng_ref, o_ref, s_out_ref, s_ref):
    i = pl.program_id(0)
    sub = GLA_SUB
    n_sub = q_ref.shape[0] // sub
    ch = GLA_CHUNK
    n_ch = sub // ch
    dot = functools.partial(jnp.dot, preferred_element_type=F32)
    heads = range(GLA_HEADS)
    ks = [slice(h * GLA_DK, (h + 1) * GLA_DK) for h in heads]
    vs = [slice(h * GLA_DV, (h + 1) * GLA_DV) for h in heads]

    @pl.when(i == 0)
    def _():
        s_ref[...] = jnp.zeros_like(s_ref)

    row = lax.broadcasted_iota(jnp.int32, (sub, sub), 0)
    col = lax.broadcasted_iota(jnp.int32, (sub, sub), 1)
    causal = ((row // ch) == (col // ch)) & (col <= row)
    tril = jnp.where(causal, 1.0, 0.0).astype(BF16)
    chunk_of_col = lax.broadcasted_iota(jnp.int32, (GLA_DK, sub), 1) // ch

    qd, kd, kdec, dec_rows = [], [], [], []
    for t in range(n_sub):
        rows = slice(t * sub, (t + 1) * sub)
        hi, lo = _split2(lg_ref[rows, :])
        b = dot(tril, hi) + dot(tril, lo)
        ends = [b[(c + 1) * ch - 1:(c + 1) * ch, :] for c in range(n_ch)]
        b_end = jnp.concatenate([jnp.broadcast_to(e, (ch, NK)) for e in ends], axis=0)
        q = q_ref[rows, :]
        k = k_ref[rows, :]
        qd.append((q * jnp.exp(b)).astype(BF16))
        kd.append((k * jnp.exp(-b)).astype(BF16))
        kdec.append(k * jnp.exp(b_end - b))
        dec_rows.append(jnp.concatenate(
            [jnp.exp(e) for e in ends] + [jnp.zeros((LANES - n_ch, NK), F32)], axis=0))

    a = [[lax.dot_general(qd[t][:, ks[h]], kd[t][:, ks[h]], (((1,), (1,)), ((), ())),
                          preferred_element_type=F32) for h in heads] for t in range(n_sub)]

    u = []
    for t in range(n_sub):
        rows = slice(t * sub, (t + 1) * sub)
        u_t = []
        for h in heads:
            kdec_t = kdec[t][:, ks[h]].T.astype(BF16)
            stacked = jnp.concatenate(
                [jnp.where(chunk_of_col == c, kdec_t, jnp.zeros_like(kdec_t))
                 for c in range(n_ch)], axis=0)
            u_t.append(dot(stacked, v_ref[rows, vs[h]]))
        u.append(u_t)

    o_intra = []
    for t in range(n_sub):
        rows = slice(t * sub, (t + 1) * sub)
        o_intra.append([dot(jnp.where(causal, a[t][h], 0.0).astype(BF16), v_ref[rows, vs[h]])
                        for h in heads])

    for h in heads:
        s = s_ref[h]
        for t in range(n_sub):
            rows = slice(t * sub, (t + 1) * sub)
            dec_t = dec_rows[t][:, ks[h]].T
            o_inter = []
            for c in range(n_ch):
                o_inter.append(dot(qd[t][c * ch:(c + 1) * ch, ks[h]], s.astype(BF16)))
                s = dec_t[:, c:c + 1] * s + u[t][h][c * GLA_DK:(c + 1) * GLA_DK]
            o = o_intra[t][h] + jnp.concatenate(o_inter, axis=0)
            o_ref[rows, vs[h]] = _gla_out(
                o, r_ref[rows, vs[h]].astype(F32), ng_ref[...]).astype(o_ref.dtype)
        s_ref[h] = s

    s_out_ref[...] = s_ref[...]


def _gla_prompt(q, k, lg, v, r, norm_g):
    m = q.shape[0]
    tt = GLA_TILE
    row = functools.partial(_row_spec, tt)
    state = (GLA_HEADS, GLA_DK, GLA_DV)
    return pl.pallas_call(
        _gla_prompt_kernel,
        grid=(m // tt,),
        in_specs=[row(NK), row(NK), row(NK), row(NV), row(NV), _full(norm_g.shape)],
        out_specs=[row(NV), _full(state)],
        out_shape=[jax.ShapeDtypeStruct((m, NV), BF16), jax.ShapeDtypeStruct(state, F32)],
        scratch_shapes=[pltpu.VMEM(state, F32)],
        compiler_params=_cparams(1),
        name="gla_prompt",
    )(q, k, lg, v, r, norm_g)


def _gla_step_kernel(q_ref, k_ref, lg_ref, v_ref, r_ref, ng_ref, s0_ref,
                     o_ref, s1_ref, z_ref, v_pad_ref, q_pad_ref):
    n_new = q_ref.shape[1]
    dot = functools.partial(jnp.dot, preferred_element_type=F32)
    rowi = lax.broadcasted_iota(jnp.int32, (n_new, NK), 0)
    rowv = lax.broadcasted_iota(jnp.int32, (n_new, GLA_DV), 0)

    @pl.when(pl.program_id(0) == 0)
    def _():
        z_ref[...] = jnp.zeros_like(z_ref)
        v_pad_ref[...] = jnp.zeros_like(v_pad_ref)
        q_pad_ref[...] = jnp.zeros_like(q_pad_ref)

    def per_seq(s, u):
        lg = lg_ref[s]
        b = jnp.zeros_like(lg)
        for t in range(n_new):
            b = b + jnp.where(rowi >= t, jnp.broadcast_to(lg[t:t + 1, :], lg.shape), 0.0)
        b_end = b[n_new - 1:n_new, :]
        q = q_ref[s]
        k = k_ref[s]
        qd = q * jnp.exp(b)
        kd = k * jnp.exp(-b)
        kdec = k * jnp.exp(b_end - b)
        dec = jnp.exp(b_end)
        v = v_ref[s]
        r = r_ref[s]
        for h in range(GLA_HEADS):
            ks = slice(h * GLA_DK, (h + 1) * GLA_DK)
            vs = slice(h * GLA_DV, (h + 1) * GLA_DV)
            v_h = v[:, vs]
            qd_h = qd[:, ks]
            o = jnp.zeros((n_new, GLA_DV), F32)
            for t in range(n_new):
                a_t = jnp.sum(qd_h * kd[t:t + 1, ks], axis=-1, keepdims=True)
                o = o + jnp.where(rowv >= t, a_t * v_h[t:t + 1, :], 0.0)
            s0 = s0_ref[s, h]
            q_pad_ref[u, h, 0:n_new, :] = qd_h
            o = o + dot(q_pad_ref[u, h], s0)[0:n_new]
            z_ref[u, h, 0:n_new, :] = kdec[:, ks]
            z_ref[u, h, n_new:n_new + 1, :] = dec[:, ks]
            z_t = z_ref[u, h].T
            v_pad_ref[u, h, 0:n_new, :] = v_h
            s1_ref[s, h] = z_t[:, n_new:n_new + 1] * s0 + dot(z_t, v_pad_ref[u, h])
            o_ref[s, :, vs] = _gla_out(o, r[:, vs], ng_ref[...])

    def body(it, carry):
        for u in range(SEQ_UNROLL):
            per_seq(it * SEQ_UNROLL + u, u)
        return carry

    lax.fori_loop(0, q_ref.shape[0] // SEQ_UNROLL, body, 0)


def _gla_step(q, k, lg, v, r, norm_g, s0):
    nseq, n_new = q.shape[0], q.shape[1]
    assert n_new <= GLA_CHUNK and n_new < SUBLANES
    sb = SEQ_BLOCK
    blk3 = lambda a: pl.BlockSpec((sb,) + a.shape[1:], lambda i: (i, 0, 0))
    blk4 = lambda a: pl.BlockSpec((sb,) + a.shape[1:], lambda i: (i, 0, 0, 0))
    per = (SEQ_UNROLL, GLA_HEADS)
    return pl.pallas_call(
        _gla_step_kernel,
        grid=(nseq // sb,),
        in_specs=[blk3(q), blk3(k), blk3(lg), blk3(v), blk3(r), _full(norm_g.shape), blk4(s0)],
        out_specs=[blk3(v), blk4(s0)],
        out_shape=[jax.ShapeDtypeStruct(v.shape, F32), jax.ShapeDtypeStruct(s0.shape, F32)],
        scratch_shapes=[pltpu.VMEM(per + (LANES, GLA_DK), F32),
                        pltpu.VMEM(per + (LANES, GLA_DV), F32),
                        pltpu.VMEM(per + (SUBLANES, GLA_DK), F32)],
        compiler_params=_cparams(1),
        name="gla_step",
    )(q, k, lg, v, r, norm_g, s0)


def kernel(x_prompt, x_sample, cache_k, cache_v, state_gla, c_prompt, c_sample, w_mod, b_mod,
           ln_g, ln_b, attn_w_in, attn_w_out, attn_sinks, gla_w_in, gla_w_gate_up, gla_b_gate,
           gla_norm_g, gla_w_out, mlp_w1, mlp_w2):
    assert x_prompt.shape[0] == 1 and w_mod.shape[0] == DEPTH == 2
    seq = x_prompt.shape[1]
    nseq, n_new = x_sample.shape[0], x_sample.shape[1]
    win = cache_k.shape[2]
    m_s = nseq * n_new

    wq = attn_w_in[0][:, :NQ].reshape(D_MODEL, N_KV_HEADS, GROUP, HEAD_DIM)
    wq = wq.transpose(0, 2, 1, 3).reshape(D_MODEL, NQ)
    w_attn_in = jnp.concatenate([wq, attn_w_in[0][:, NQ:]], axis=1).astype(BF16)
    w_attn_out = attn_w_out[0].reshape(N_KV_HEADS, GROUP, HEAD_DIM, D_MODEL)
    w_attn_out = w_attn_out.transpose(1, 0, 2, 3).reshape(NQ, D_MODEL).astype(BF16)
    n_main = 2 * NK + 2 * NV
    w_gla_main = gla_w_in[0][:, :n_main].astype(BF16)
    w_gla_gd = jnp.pad(gla_w_in[0][:, n_main:], ((0, 0), (0, LANES - GLA_GATE_RANK))).astype(BF16)
    w_gla_gu = jnp.pad(gla_w_gate_up[0], ((0, LANES - GLA_GATE_RANK), (0, 0))).astype(BF16)
    w_gla_out = gla_w_out[0].astype(BF16)
    b_gate = gla_b_gate[0].reshape(1, NK)
    norm_g = gla_norm_g[0].reshape(1, GLA_DV)
    ln_g4 = ln_g.reshape(2 * DEPTH, 1, D_MODEL)
    ln_b4 = ln_b.reshape(2 * DEPTH, 1, D_MODEL)
    sinks = attn_sinks[0]

    pad_rows = (-(m_s + 1)) % SUBLANES
    c_all = jnp.concatenate([jnp.repeat(c_sample, n_new, axis=0), c_prompt,
                             jnp.zeros((pad_rows, D_MODEL), F32)], axis=0)
    mod_all = _adaln_all(c_all, w_mod.reshape(2 * DEPTH, D_MODEL, 3 * D_MODEL),
                         b_mod.reshape(2 * DEPTH, 1, 3 * D_MODEL))
    mods_p = [_Mod(mod_all, p, per_row=False, row0=m_s) for p in range(2 * DEPTH)]
    mods_s = [_Mod(mod_all, p, per_row=True) for p in range(2 * DEPTH)]

    assert seq >= WINDOW == ATTN_BLOCK
    win_p = WINDOW
    three = lambda a: a.reshape(nseq, n_new, a.shape[-1])
    to_slab = lambda c: c[0].transpose(0, 2, 3, 1).reshape(nseq, NKV, win)
    from_slab = lambda c: c.reshape(nseq, N_KV_HEADS, HEAD_DIM, win).transpose(0, 3, 1, 2)[None]

    x_s = x_sample.reshape(m_s, D_MODEL)
    q, k, v = _attn_proj(x_s, mods_s[0], w_attn_in)
    o, k_s, v_s = _dec_attn(q.reshape(nseq, n_new * GROUP, NKV), three(k), three(v),
                            to_slab(cache_k), to_slab(cache_v), sinks)
    x1_s, w1_0, w2_0 = _mix_mlp_stream(o.reshape(m_s, NQ), x_s, mods_s[0], mods_s[1], w_attn_out,
                                       mlp_w1, mlp_w2, ln_g4, ln_b4, 0)
    x1_p, k_p, v_p = _attn_layer(x_prompt[0], mods_p[0], mods_p[1], w_attn_in, w_attn_out,
                                 w1_0, w2_0, ln_g4, ln_b4, sinks, 0)

    gla_w = (w_gla_main, w_gla_gd, w_gla_gu, b_gate)
    q, k, lg, v, r = _gla_proj(x1_s, mods_s[2], *gla_w, F32)
    o, s_s = _gla_step(three(q), three(k), three(lg), three(v), three(r), norm_g, state_gla[0])
    y_s, w1_1, w2_1 = _mix_mlp_stream(o.reshape(m_s, NV), x1_s, mods_s[2], mods_s[3], w_gla_out,
                                      mlp_w1, mlp_w2, ln_g4, ln_b4, 1)
    q, k, lg, v, r = _gla_proj(x1_p, mods_p[2], *gla_w, BF16)
    o, s_p = _gla_prompt(q, k, lg, v, r, norm_g)
    y_p = _mix_mlp(o, x1_p, mods_p[2], mods_p[3], w_gla_out, w1_1, w2_1, ln_g4, ln_b4, 1)
    k_s, v_s = from_slab(k_s), from_slab(v_s)

    kv_shape_p = (1, 1, win_p, N_KV_HEADS, HEAD_DIM)
    return (y_p[None], y_s.reshape(nseq, n_new, D_MODEL),
            k_p.reshape(kv_shape_p), v_p.reshape(kv_shape_p), s_p[None, None],
            k_s, v_s, s_s[None])
```

```python
import functools

import jax
import jax.numpy as jnp
from jax import lax
from jax.experimental import pallas as pl
from jax.experimental.pallas import tpu as pltpu

F32 = jnp.float32
BF16 = jnp.bfloat16

D_MODEL = 1024
DEPTH = 2
HEAD_DIM = 64
N_Q_HEADS = 16
N_KV_HEADS = 4
GROUP = 4
WINDOW = 128
ATTN_BLOCK = 128
GLA_HEADS = 4
GLA_DK = 128
GLA_DV = 256
GLA_GATE_RANK = 16
GLA_TAU = 16.0
GLA_CHUNK = 64
D_FF = 4 * D_MODEL
ALPHA = (2.0 * DEPTH) ** 0.25
LN_EPS = 1e-5

NQ = N_Q_HEADS * HEAD_DIM
NKV = N_KV_HEADS * HEAD_DIM
NK = GLA_HEADS * GLA_DK
NV = GLA_HEADS * GLA_DV
LANES = 128
SUBLANES = 8
NEG_BIG = -1e30

ROW_TILE = 512
FF_CHUNK = 1024
ATTN_TILE = 512
GLA_TILE = 512
GLA_SUB = 256
SEQ_BLOCK = 8
SEQ_UNROLL = 2
VMEM_LIMIT = 56 * 1024 * 1024


def _cparams(n_axes):
    return pltpu.CompilerParams(
        dimension_semantics=("arbitrary",) * n_axes,
        vmem_limit_bytes=VMEM_LIMIT,
    )


def _full(shape):
    zeros = (0,) * len(shape)
    return pl.BlockSpec(shape, lambda *_: zeros)


def _layer(arr, idx):
    tail = (0,) * (arr.ndim - 1)
    return pl.BlockSpec((None,) + arr.shape[1:], lambda *_: (idx,) + tail,
                        pipeline_mode=pl.Buffered(1))


def _row_spec(tm, n):
    return pl.BlockSpec((tm, n), lambda i: (i, 0))


class _Mod:
    def __init__(self, arr, p, per_row, row0=0):
        self.arr, self.p, self.per_row, self.row0 = arr, p, per_row, row0

    def spec(self, tm, col):
        p = self.p
        if self.per_row:
            return pl.BlockSpec((None, tm, D_MODEL), lambda i: (p, i, col))
        blk = self.row0 // SUBLANES
        return pl.BlockSpec((None, SUBLANES, D_MODEL), lambda i: (p, blk, col))


def _mod_rows(ref, tm):
    return ref[...] if ref.shape[0] == tm else ref[0:1, :]


def _modulate(x, shift, scale):
    return x * (1.0 + scale) + shift


def _res_ln(x, gate, o, g, b):
    y = ALPHA * x + gate * o
    mu = jnp.mean(y, axis=-1, keepdims=True)
    yc = y - mu
    var = jnp.mean(yc * yc, axis=-1, keepdims=True)
    return yc * lax.rsqrt(var + LN_EPS) * g + b


def _mod_kernel(c_ref, w_ref, b_ref, o_ref):
    c = c_ref[...]
    a = (c * jax.nn.sigmoid(c)).astype(BF16)
    o_ref[...] = jnp.dot(a, w_ref[...].astype(BF16), preferred_element_type=F32) + b_ref[...]


def _adaln_all(c_all, w_mod, b_mod):
    rows = c_all.shape[0]
    tn = 1024
    return pl.pallas_call(
        _mod_kernel,
        grid=(4, 3 * D_MODEL // tn),
        in_specs=[
            pl.BlockSpec((rows, D_MODEL), lambda p, n: (0, 0)),
            pl.BlockSpec((None, D_MODEL, tn), lambda p, n: (p, 0, n)),
            pl.BlockSpec((None, 1, tn), lambda p, n: (p, 0, n)),
        ],
        out_specs=pl.BlockSpec((None, rows, tn), lambda p, n: (p, 0, n)),
        out_shape=jax.ShapeDtypeStruct((4, rows, 3 * D_MODEL), F32),
        compiler_params=_cparams(2),
        name="adaln_mod",
    )(c_all, w_mod, b_mod)


def _attn_proj_kernel(x_ref, sh_ref, sc_ref, w_ref, q_ref, k_ref, v_ref):
    tm = x_ref.shape[0]
    h = _modulate(x_ref[...], _mod_rows(sh_ref, tm), _mod_rows(sc_ref, tm)).astype(BF16)
    q = jnp.dot(h, w_ref[:, 0:NQ], preferred_element_type=F32)
    q_ref[...] = (q * (HEAD_DIM ** -0.5)).astype(BF16)
    k_ref[...] = jnp.dot(h, w_ref[:, NQ:NQ + NKV], preferred_element_type=F32)
    v_ref[...] = jnp.dot(h, w_ref[:, NQ + NKV:NQ + 2 * NKV], preferred_element_type=F32)


def _attn_proj(x, mod, w_in):
    m = x.shape[0]
    tm = min(ROW_TILE, m)
    row = functools.partial(_row_spec, tm)
    return pl.pallas_call(
        _attn_proj_kernel,
        grid=(m // tm,),
        in_specs=[row(D_MODEL), mod.spec(tm, 0), mod.spec(tm, 1), _full(w_in.shape)],
        out_specs=[row(NQ), row(NKV), row(NKV)],
        out_shape=[
            jax.ShapeDtypeStruct((m, NQ), BF16),
            jax.ShapeDtypeStruct((m, NKV), F32),
            jax.ShapeDtypeStruct((m, NKV), F32),
        ],
        compiler_params=_cparams(1),
        name="attn_proj",
    )(x, mod.arr, mod.arr, w_in)


def _gla_proj_kernel(x_ref, sh_ref, sc_ref, wt_ref, wgu_ref, bg_ref,
                     q_ref, k_ref, lg_ref, v_ref, r_ref):
    tm = x_ref.shape[0]
    h = _modulate(x_ref[...], _mod_rows(sh_ref, tm), _mod_rows(sc_ref, tm)).astype(BF16)
    dot = functools.partial(jnp.dot, preferred_element_type=F32)

    def proj(lo, hi):
        return lax.dot_general(h, wt_ref[lo:hi, :], (((1,), (1,)), ((), ())),
                               preferred_element_type=F32)

    n_all = wt_ref.shape[0]
    gdown = proj(n_all - LANES, n_all)
    q_ref[...] = proj(0, NK) * (GLA_DK ** -0.5)
    k_ref[...] = proj(NK, 2 * NK)
    pre = dot(gdown.astype(BF16), wgu_ref[...]) + bg_ref[...]
    v_ref[...] = proj(2 * NK, 2 * NK + NV).astype(v_ref.dtype)
    r_ref[...] = proj(2 * NK + NV, 2 * NK + 2 * NV).astype(r_ref.dtype)
    log_sig = jnp.minimum(pre, 0.0) - jnp.log1p(jnp.exp(-jnp.abs(pre)))
    lg_ref[...] = log_sig / GLA_TAU


def _gla_proj(x, mod, w_t, w_gu, b_gate, vr_dtype):
    m = x.shape[0]
    tm = min(ROW_TILE, m)
    row = functools.partial(_row_spec, tm)
    return pl.pallas_call(
        _gla_proj_kernel,
        grid=(m // tm,),
        in_specs=[row(D_MODEL), mod.spec(tm, 0), mod.spec(tm, 1), _full(w_t.shape),
                  _full(w_gu.shape), _full(b_gate.shape)],
        out_specs=[row(NK), row(NK), row(NK), row(NV), row(NV)],
        out_shape=[
            jax.ShapeDtypeStruct((m, NK), F32),
            jax.ShapeDtypeStruct((m, NK), F32),
            jax.ShapeDtypeStruct((m, NK), F32),
            jax.ShapeDtypeStruct((m, NV), vr_dtype),
            jax.ShapeDtypeStruct((m, NV), vr_dtype),
        ],
        compiler_params=_cparams(1),
        name="gla_proj",
    )(x, mod.arr, mod.arr, w_t, w_gu, b_gate)


def _mix_mlp_kernel(a_ref, x_ref, gt0_ref, sh_ref, sc_ref, gt1_ref, wo_ref, w1_ref, w2_ref,
                    g0_ref, b0_ref, g1_ref, b1_ref, o_ref, acc_ref):
    tm = x_ref.shape[0]
    o = jnp.dot(a_ref[...].astype(BF16), wo_ref[...], preferred_element_type=F32)
    x1 = _res_ln(x_ref[...], _mod_rows(gt0_ref, tm), o, g0_ref[...], b0_ref[...])
    h = _modulate(x1, _mod_rows(sh_ref, tm), _mod_rows(sc_ref, tm)).astype(BF16)
    _mlp_chunks(h, w1_ref, w2_ref, acc_ref, range(D_FF // FF_CHUNK))
    o_ref[...] = _res_ln(x1, _mod_rows(gt1_ref, tm), acc_ref[...], g1_ref[...], b1_ref[...])


def _mix_mlp(a, x, mod_mix, mod_mlp, w_out, w1, w2, ln_g, ln_b, layer):
    m = x.shape[0]
    tm = min(ROW_TILE, m)
    row = functools.partial(_row_spec, tm)
    return pl.pallas_call(
        _mix_mlp_kernel,
        grid=(m // tm,),
        in_specs=[row(a.shape[1]), row(D_MODEL), mod_mix.spec(tm, 2), mod_mlp.spec(tm, 0),
                  mod_mlp.spec(tm, 1), mod_mlp.spec(tm, 2), _full(w_out.shape),
                  _full(w1.shape), _full(w2.shape),
                  _layer(ln_g, 2 * layer), _layer(ln_b, 2 * layer),
                  _layer(ln_g, 2 * layer + 1), _layer(ln_b, 2 * layer + 1)],
        out_specs=row(D_MODEL),
        out_shape=jax.ShapeDtypeStruct((m, D_MODEL), F32),
        scratch_shapes=[pltpu.VMEM((tm, D_MODEL), F32)],
        compiler_params=_cparams(1),
        name="mix_mlp",
    )(a, x, mod_mix.arr, mod_mlp.arr, mod_mlp.arr, mod_mlp.arr, w_out, w1, w2,
      ln_g, ln_b, ln_g, ln_b)


def _mix_mlp_stream_kernel(a_ref, x_ref, gt0_ref, sh_ref, sc_ref, gt1_ref, wo_ref, w1_ref, w2_ref,
                           g0_ref, b0_ref, g1_ref, b1_ref, y_ref, w1b_ref, w2b_ref,
                           x1_s, h_s, acc_ref):
    c = pl.program_id(0)

    @pl.when(c == 0)
    def _():
        o = jnp.dot(a_ref[...].astype(BF16), wo_ref[...], preferred_element_type=F32)
        x1 = _res_ln(x_ref[...], gt0_ref[...], o, g0_ref[...], b0_ref[...])
        x1_s[...] = x1
        h_s[...] = _modulate(x1, sh_ref[...], sc_ref[...]).astype(BF16)
        acc_ref[...] = jnp.zeros_like(acc_ref)

    w1c = w1_ref[...].astype(BF16)
    w2c = w2_ref[...].astype(BF16)
    w1b_ref[...] = w1c
    w2b_ref[...] = w2c
    a = jnp.dot(h_s[...], w1c, preferred_element_type=F32)
    a = jnp.square(jnp.maximum(a, 0.0)).astype(BF16)
    acc_ref[...] += jnp.dot(a, w2c, preferred_element_type=F32)

    @pl.when(c == pl.num_programs(0) - 1)
    def _():
        y_ref[...] = _res_ln(x1_s[...], gt1_ref[...], acc_ref[...], g1_ref[...], b1_ref[...])


def _mix_mlp_stream(a, x, mod_mix, mod_mlp, w_out, w1_f32, w2_f32, ln_g, ln_b, layer):
    m = x.shape[0]
    assert m <= ROW_TILE and mod_mix.per_row and mod_mlp.per_row
    fc = FF_CHUNK
    whole = lambda n: pl.BlockSpec((m, n), lambda c: (0, 0))
    mspec = lambda mod, col: pl.BlockSpec((None, m, D_MODEL), lambda c: (mod.p, 0, col))
    return pl.pallas_call(
        _mix_mlp_stream_kernel,
        grid=(D_FF // fc,),
        in_specs=[whole(a.shape[1]), whole(D_MODEL), mspec(mod_mix, 2), mspec(mod_mlp, 0),
                  mspec(mod_mlp, 1), mspec(mod_mlp, 2), _full(w_out.shape),
                  pl.BlockSpec((None, D_MODEL, fc), lambda c: (layer, 0, c)),
                  pl.BlockSpec((None, fc, D_MODEL), lambda c: (layer, c, 0)),
                  _layer(ln_g, 2 * layer), _layer(ln_b, 2 * layer),
                  _layer(ln_g, 2 * layer + 1), _layer(ln_b, 2 * layer + 1)],
        out_specs=[whole(D_MODEL),
                   pl.BlockSpec((D_MODEL, fc), lambda c: (0, c)),
                   pl.BlockSpec((fc, D_MODEL), lambda c: (c, 0))],
        out_shape=[jax.ShapeDtypeStruct((m, D_MODEL), F32),
                   jax.ShapeDtypeStruct((D_MODEL, D_FF), BF16),
                   jax.ShapeDtypeStruct((D_FF, D_MODEL), BF16)],
        scratch_shapes=[pltpu.VMEM((m, D_MODEL), F32), pltpu.VMEM((m, D_MODEL), BF16),
                        pltpu.VMEM((m, D_MODEL), F32)],
        compiler_params=_cparams(1),
        name="mix_mlp_stream",
    )(a, x, mod_mix.arr, mod_mlp.arr, mod_mlp.arr, mod_mlp.arr, w_out, w1_f32, w2_f32,
      ln_g, ln_b, ln_g, ln_b)


def _alibi_slope(head):
    return 2.0 ** (-8.0 * (head + 1) / N_Q_HEADS)


def _softmax_sink(s, sink):
    m = jnp.maximum(jnp.max(s, axis=-1, keepdims=True), sink)
    e = jnp.exp(s - m)
    den = jnp.sum(e, axis=-1, keepdims=True) + jnp.exp(sink - m)
    return e / den


def _band_bias_init(bias_ref):
    blk = ATTN_BLOCK
    c = lax.broadcasted_iota(jnp.int32, (2 * blk, blk), 0)
    r = lax.broadcasted_iota(jnp.int32, (2 * blk, blk), 1)
    dist = blk + r - c
    valid = (dist >= 0) & (dist <= WINDOW)
    distf = dist.astype(F32)
    for head in range(N_Q_HEADS):
        pen = -_alibi_slope(head) * distf
        bias_ref[0, head] = jnp.where(valid, pen, NEG_BIG)
        bias_ref[1, head] = jnp.where(valid & (c >= blk), pen, NEG_BIG)


def _band_scores(q, kk):
    blk = ATTN_BLOCK
    head_of_lane = lax.broadcasted_iota(jnp.int32, (blk, NKV), 1) // HEAD_DIM
    scores = []
    for b in range(q.shape[0] // blk):
        keys = kk[b * blk:(b + 2) * blk]
        for g in range(GROUP):
            qg = q[b * blk:(b + 1) * blk, g * NKV:(g + 1) * NKV]
            qm = jnp.concatenate(
                [jnp.where(head_of_lane == j, qg, jnp.zeros_like(qg))
                 for j in range(N_KV_HEADS)], axis=0)
            scores.append(lax.dot_general(keys, qm, (((1,), (1,)), ((), ())),
                                          preferred_element_type=F32))
    return scores


def _band_outputs(scores, vvt, first_tile, sinks_ref, bias_ref, o_ref):
    blk = ATTN_BLOCK
    for b in range(len(scores) // GROUP):
        table = jnp.where(first_tile, 1, 0) if b == 0 else 0
        vals_t = vvt[:, b * blk:(b + 2) * blk]
        for g in range(GROUP):
            st_all = scores[b * GROUP + g]
            ps = []
            for j in range(N_KV_HEADS):
                head = j * GROUP + g
                sink = sinks_ref[head]
                st = st_all[:, j * blk:(j + 1) * blk] + bias_ref[table, head]
                m = jnp.maximum(jnp.max(st, axis=0, keepdims=True), sink)
                e = jnp.exp(st - m)
                den = jnp.sum(e, axis=0, keepdims=True) + jnp.exp(sink - m)
                ps.append((e * (1.0 / den)).astype(BF16))
            ot_all = jnp.dot(vals_t, jnp.concatenate(ps, axis=1),
                             preferred_element_type=F32)
            ot = jnp.concatenate(
                [ot_all[j * HEAD_DIM:(j + 1) * HEAD_DIM, j * blk:(j + 1) * blk]
                 for j in range(N_KV_HEADS)], axis=0)
            o_ref[b * blk:(b + 1) * blk, g * NKV:(g + 1) * NKV] = ot.T.astype(BF16)


def _mlp_chunks(h, w1_ref, w2_ref, acc_ref, chunks):
    for c in chunks:
        cols = slice(c * FF_CHUNK, (c + 1) * FF_CHUNK)
        a = jnp.dot(h, w1_ref[:, cols], preferred_element_type=F32)
        a = jnp.square(jnp.maximum(a, 0.0)).astype(BF16)
        d = jnp.dot(a, w2_ref[cols, :], preferred_element_type=F32)
        if c == 0:
            acc_ref[...] = d
        else:
            acc_ref[...] += d


def _attn_layer_kernel(sinks_ref, xc_ref, xp_ref, sh0_ref, sc0_ref, gt0_ref, sh1_ref, sc1_ref,
                       gt1_ref, win_ref, wo_ref, w1_ref, w2_ref, g0_ref, b0_ref, g1_ref, b1_ref,
                       y_ref, kl_ref, vl_ref, o_s, kprev_s, vtprev_s, acc_ref, bias_ref):
    i = pl.program_id(0)
    slot = i % 2
    blk = ATTN_BLOCK
    tm = xc_ref.shape[0]
    dot = functools.partial(jnp.dot, preferred_element_type=F32)
    n_chunks = D_FF // FF_CHUNK

    @pl.when(i == 0)
    def _():
        _band_bias_init(bias_ref)
        o_s[...] = jnp.zeros_like(o_s)
        kprev_s[...] = jnp.zeros_like(kprev_s)
        vtprev_s[...] = jnp.zeros_like(vtprev_s)

    x1 = _res_ln(xp_ref[...], gt0_ref[0:1, :], dot(o_s[1 - slot], wo_ref[...]),
                 g0_ref[...], b0_ref[...])
    h_mlp = _modulate(x1, sh1_ref[0:1, :], sc1_ref[0:1, :]).astype(BF16)

    h_in = _modulate(xc_ref[...], sh0_ref[0:1, :], sc0_ref[0:1, :]).astype(BF16)
    q = (dot(h_in, win_ref[:, 0:NQ]) * (HEAD_DIM ** -0.5)).astype(BF16)
    k = dot(h_in, win_ref[:, NQ:NQ + NKV])
    v = dot(h_in, win_ref[:, NQ + NKV:NQ + 2 * NKV])
    kl_ref[...] = k[tm - blk:, :]
    vl_ref[...] = v[tm - blk:, :]
    k_bf = k.astype(BF16)
    vt = v.T.astype(BF16)
    kk = jnp.concatenate([kprev_s[...], k_bf], axis=0)
    vvt = jnp.concatenate([vtprev_s[...], vt], axis=1)
    scores = _band_scores(q, kk)
    kprev_s[...] = k_bf[tm - blk:, :]
    vtprev_s[...] = vt[:, tm - blk:]

    _mlp_chunks(h_mlp, w1_ref, w2_ref, acc_ref, range(n_chunks))
    _band_outputs(scores, vvt, i == 0, sinks_ref, bias_ref, o_s.at[slot])
    y_ref[...] = _res_ln(x1, gt1_ref[0:1, :], acc_ref[...], g1_ref[...], b1_ref[...])


def _attn_layer(x, mod_mix, mod_mlp, w_in, w_out, w1, w2, ln_g, ln_b, sinks, layer):
    m = x.shape[0]
    tm = ATTN_TILE
    blk = ATTN_BLOCK
    n = m // tm
    assert not mod_mix.per_row and not mod_mlp.per_row
    cur = pl.BlockSpec((tm, D_MODEL), lambda i: (jnp.minimum(i, n - 1), 0))
    prev = pl.BlockSpec((tm, D_MODEL), lambda i: (jnp.maximum(i - 1, 0), 0))
    last = pl.BlockSpec((blk, NKV), lambda i: (0, 0))
    return pl.pallas_call(
        _attn_layer_kernel,
        grid=(n + 1,),
        in_specs=[pl.BlockSpec(memory_space=pltpu.SMEM), cur, prev,
                  mod_mix.spec(tm, 0), mod_mix.spec(tm, 1), mod_mix.spec(tm, 2),
                  mod_mlp.spec(tm, 0), mod_mlp.spec(tm, 1), mod_mlp.spec(tm, 2),
                  _full(w_in.shape), _full(w_out.shape), _full(w1.shape), _full(w2.shape),
                  _layer(ln_g, 2 * layer), _layer(ln_b, 2 * layer),
                  _layer(ln_g, 2 * layer + 1), _layer(ln_b, 2 * layer + 1)],
        out_specs=[prev, last, last],
        out_shape=[jax.ShapeDtypeStruct((m, D_MODEL), F32),
                   jax.ShapeDtypeStruct((blk, NKV), F32),
                   jax.ShapeDtypeStruct((blk, NKV), F32)],
        scratch_shapes=[pltpu.VMEM((2, tm, NQ), BF16),
                        pltpu.VMEM((blk, NKV), BF16),
                        pltpu.VMEM((NKV, blk), BF16),
                        pltpu.VMEM((tm, D_MODEL), F32),
                        pltpu.VMEM((2, N_Q_HEADS, 2 * blk, blk), F32)],
        compiler_params=_cparams(1),
        name="attn_layer",
    )(sinks, x, x, mod_mix.arr, mod_mix.arr, mod_mix.arr, mod_mlp.arr, mod_mlp.arr,
      mod_mlp.arr, w_in, w_out, w1, w2, ln_g, ln_b, ln_g, ln_b)


def _dec_attn_kernel(sinks_ref, q_ref, kn_ref, vn_ref, ck_ref, cv_ref,
                     o_ref, nk_ref, nv_ref, zk_ref, zv_ref):
    n_new = kn_ref.shape[1]
    win = ck_ref.shape[2]
    rows = n_new * GROUP
    n_rows = N_KV_HEADS * rows
    keep = win - n_new
    row = lax.broadcasted_iota(jnp.int32, (n_rows, 2 * win), 0)
    col = lax.broadcasted_iota(jnp.int32, (n_rows, 2 * win), 1)
    j_r = row // rows
    t_r = (row // GROUP) % n_new
    g_r = row % GROUP
    h_r = j_r * GROUP + g_r
    slope = jnp.exp2(-8.0 * (h_r + 1).astype(F32) / N_Q_HEADS)
    sink = jnp.zeros((n_rows, 1), F32)
    h_col = h_r[:, 0:1]
    for h in range(N_Q_HEADS):
        sink = jnp.where(h_col == h, sinks_ref[h], sink)
    is_key = (col < win) | (col >= win + keep)
    frame = jnp.where(col < win, col, col - keep)
    dist = t_r + win - frame
    valid = is_key & (dist >= 0) & (dist <= WINDOW)
    bias = jnp.where(valid, -slope * dist.astype(F32), NEG_BIG)
    head_of_lane = lax.broadcasted_iota(jnp.int32, (rows, NKV), 1) // HEAD_DIM
    lane_w = lax.broadcasted_iota(jnp.int32, (NKV, win), 1)

    @pl.when(pl.program_id(0) == 0)
    def _():
        zk_ref[...] = jnp.zeros_like(zk_ref)
        zv_ref[...] = jnp.zeros_like(zv_ref)

    def per_seq(s, u):
        k_t = ck_ref[s]
        v_t = cv_ref[s]
        zk_ref[u, keep:win, :] = kn_ref[s]
        zv_ref[u, keep:win, :] = vn_ref[s]
        zk_t = zk_ref[u].T
        zv_t = zv_ref[u].T
        nk_ref[s] = jnp.where(lane_w < keep, pltpu.roll(k_t, keep, 1), zk_t)
        nv_ref[s] = jnp.where(lane_w < keep, pltpu.roll(v_t, keep, 1), zv_t)
        keys = jnp.concatenate([k_t, zk_t], axis=1).astype(BF16)
        vals = jnp.concatenate([v_t, zv_t], axis=1).astype(BF16)
        qs = q_ref[s]
        qbd = jnp.concatenate(
            [jnp.where(head_of_lane == j, qs, jnp.zeros_like(qs)) for j in range(N_KV_HEADS)],
            axis=0)
        sc = jnp.dot(qbd, keys, preferred_element_type=F32) + bias
        p = _softmax_sink(sc, sink).astype(BF16)
        pv = lax.dot_general(p, vals, (((1,), (1,)), ((), ())), preferred_element_type=F32)
        o = jnp.zeros((rows, NKV), F32)
        for j in range(N_KV_HEADS):
            o = o + jnp.where(head_of_lane == j, pv[j * rows:(j + 1) * rows], 0.0)
        o_ref[s] = o.astype(BF16)

    def body(it, carry):
        for u in range(SEQ_UNROLL):
            per_seq(it * SEQ_UNROLL + u, u)
        return carry

    lax.fori_loop(0, q_ref.shape[0] // SEQ_UNROLL, body, 0)


def _dec_attn(q, k_new, v_new, cache_kt, cache_vt, sinks):
    nseq, win = cache_kt.shape[0], cache_kt.shape[2]
    n_new = k_new.shape[1]
    assert win == LANES and n_new < SUBLANES
    sb = SEQ_BLOCK
    blk = lambda a: pl.BlockSpec((sb,) + a.shape[1:], lambda i: (i, 0, 0))
    zshape = (SEQ_UNROLL, win, NKV)
    return pl.pallas_call(
        _dec_attn_kernel,
        grid=(nseq // sb,),
        in_specs=[pl.BlockSpec(memory_space=pltpu.SMEM), blk(q), blk(k_new), blk(v_new),
                  blk(cache_kt), blk(cache_vt)],
        out_specs=[blk(q), blk(cache_kt), blk(cache_vt)],
        out_shape=[
            jax.ShapeDtypeStruct(q.shape, BF16),
            jax.ShapeDtypeStruct(cache_kt.shape, F32),
            jax.ShapeDtypeStruct(cache_vt.shape, F32),
        ],
        scratch_shapes=[pltpu.VMEM(zshape, F32), pltpu.VMEM(zshape, F32)],
        compiler_params=_cparams(1),
        name="dec_attn",
    )(sinks, q, k_new, v_new, cache_kt, cache_vt)


def _split2(x):
    hi = x.astype(BF16)
    lo = (x - hi.astype(F32)).astype(BF16)
    return hi, lo


def _gla_out(o, r, norm_g):
    ms = jnp.mean(o * o, axis=-1, keepdims=True)
    o = o * lax.rsqrt(ms + LN_EPS) * norm_g
    return o * (r * jax.nn.sigmoid(r))


def _gla_prompt_kernel(q_ref, k_ref, lg_ref, v_ref, r_ref, ng_ref, o_ref, s_out_ref, s_ref):
    i = pl.program_id(0)
    sub = GLA_SUB
    n_sub = q_ref.shape[0] // sub
    ch = GLA_CHUNK
    n_ch = sub // ch
    dot = functools.partial(jnp.dot, preferred_element_type=F32)
    heads = range(GLA_HEADS)
    ks = [slice(h * GLA_DK, (h + 1) * GLA_DK) for h in heads]
    vs = [slice(h * GLA_DV, (h + 1) * GLA_DV) for h in heads]

    @pl.when(i == 0)
    def _():
        s_ref[...] = jnp.zeros_like(s_ref)

    row = lax.broadcasted_iota(jnp.int32, (sub, sub), 0)
    col = lax.broadcasted_iota(jnp.int32, (sub, sub), 1)
    causal = ((row // ch) == (col // ch)) & (col <= row)
    tril = jnp.where(causal, 1.0, 0.0).astype(BF16)
    chunk_of_col = lax.broadcasted_iota(jnp.int32, (GLA_DK, sub), 1) // ch

    qd, kd, kdec, dec_rows = [], [], [], []
    for t in range(n_sub):
        rows = slice(t * sub, (t + 1) * sub)
        hi, lo = _split2(lg_ref[rows, :])
        b = dot(tril, hi) + dot(tril, lo)
        ends = [b[(c + 1) * ch - 1:(c + 1) * ch, :] for c in range(n_ch)]
        b_end = jnp.concatenate([jnp.broadcast_to(e, (ch, NK)) for e in ends], axis=0)
        q = q_ref[rows, :]
        k = k_ref[rows, :]
        qd.append((q * jnp.exp(b)).astype(BF16))
        kd.append((k * jnp.exp(-b)).astype(BF16))
        kdec.append(k * jnp.exp(b_end - b))
        dec_rows.append(jnp.concatenate(
            [jnp.exp(e) for e in ends] + [jnp.zeros((LANES - n_ch, NK), F32)], axis=0))

    a = [[lax.dot_general(qd[t][:, ks[h]], kd[t][:, ks[h]], (((1,), (1,)), ((), ())),
                          preferred_element_type=F32) for h in heads] for t in range(n_sub)]

    u = []
    for t in range(n_sub):
        rows = slice(t * sub, (t + 1) * sub)
        u_t = []
        for h in heads:
            kdec_t = kdec[t][:, ks[h]].T.astype(BF16)
            stacked = jnp.concatenate(
                [jnp.where(chunk_of_col == c, kdec_t, jnp.zeros_like(kdec_t))
                 for c in range(n_ch)], axis=0)
            u_t.append(dot(stacked, v_ref[rows, vs[h]]))
        u.append(u_t)

    o_intra = []
    for t in range(n_sub):
        rows = slice(t * sub, (t + 1) * sub)
        o_intra.append([dot(jnp.where(causal, a[t][h], 0.0).astype(BF16), v_ref[rows, vs[h]])
                        for h in heads])

    for h in heads:
        s = s_ref[h]
        for t in range(n_sub):
            rows = slice(t * sub, (t + 1) * sub)
            dec_t = dec_rows[t][:, ks[h]].T
            o_inter = []
            for c in range(n_ch):
                o_inter.append(dot(qd[t][c * ch:(c + 1) * ch, ks[h]], s.astype(BF16)))
                s = dec_t[:, c:c + 1] * s + u[t][h][c * GLA_DK:(c + 1) * GLA_DK]
            o = o_intra[t][h] + jnp.concatenate(o_inter, axis=0)
            o_ref[rows, vs[h]] = _gla_out(
                o, r_ref[rows, vs[h]].astype(F32), ng_ref[...]).astype(o_ref.dtype)
        s_ref[h] = s

    s_out_ref[...] = s_ref[...]


def _gla_prompt(q, k, lg, v, r, norm_g):
    m = q.shape[0]
    tt = GLA_TILE
    row = functools.partial(_row_spec, tt)
    state = (GLA_HEADS, GLA_DK, GLA_DV)
    return pl.pallas_call(
        _gla_prompt_kernel,
        grid=(m // tt,),
        in_specs=[row(NK), row(NK), row(NK), row(NV), row(NV), _full(norm_g.shape)],
        out_specs=[row(NV), _full(state)],
        out_shape=[jax.ShapeDtypeStruct((m, NV), BF16), jax.ShapeDtypeStruct(state, F32)],
        scratch_shapes=[pltpu.VMEM(state, F32)],
        compiler_params=_cparams(1),
        name="gla_prompt",
    )(q, k, lg, v, r, norm_g)


def _gla_step_kernel(q_ref, k_ref, lg_ref, v_ref, r_ref, ng_ref, s0_ref,
                     o_ref, s1_ref, z_ref, v_all_ref, q_pad_ref):
    n_seq, n_new = q_ref.shape[0], q_ref.shape[1]
    grp = SUBLANES
    dot = functools.partial(jnp.dot, preferred_element_type=F32)
    rowi = lax.broadcasted_iota(jnp.int32, (n_new, NK), 0)
    rowv = lax.broadcasted_iota(jnp.int32, (n_new, GLA_DV), 0)
    heads = range(GLA_HEADS)
    ks = [slice(h * GLA_DK, (h + 1) * GLA_DK) for h in heads]
    vs = [slice(h * GLA_DV, (h + 1) * GLA_DV) for h in heads]

    @pl.when(pl.program_id(0) == 0)
    def _():
        z_ref[...] = jnp.zeros_like(z_ref)
        v_all_ref[...] = jnp.zeros_like(v_all_ref)
        q_pad_ref[...] = jnp.zeros_like(q_pad_ref)

    o_intra = []
    for s in range(n_seq):
        lg = lg_ref[s]
        b = jnp.zeros_like(lg)
        for t in range(n_new):
            b = b + jnp.where(rowi >= t, jnp.broadcast_to(lg[t:t + 1, :], lg.shape), 0.0)
        b_end = b[n_new - 1:n_new, :]
        qd = q_ref[s] * jnp.exp(b)
        k = k_ref[s]
        kd = k * jnp.exp(-b)
        kdec = k * jnp.exp(b_end - b)
        dec = jnp.exp(b_end)
        v = v_ref[s]
        row0 = s * grp
        o_s = []
        for h in heads:
            v_h = v[:, vs[h]]
            qd_h = qd[:, ks[h]]
            o = jnp.zeros((n_new, GLA_DV), F32)
            for t in range(n_new):
                a_t = jnp.sum(qd_h * kd[t:t + 1, ks[h]], axis=-1, keepdims=True)
                o = o + jnp.where(rowv >= t, a_t * v_h[t:t + 1, :], 0.0)
            o_s.append(o)
            q_pad_ref[s, h, 0:n_new, :] = qd_h
            z_ref[h, row0:row0 + n_new, :] = kdec[:, ks[h]]
            z_ref[h, row0 + n_new:row0 + n_new + 1, :] = dec[:, ks[h]]
            v_all_ref[h, row0:row0 + n_new, :] = v_h
        o_intra.append(o_s)

    group_of_lane = lax.broadcasted_iota(jnp.int32, (GLA_DK, LANES), 1) // grp
    for h in heads:
        z_t = z_ref[h].T
        v_all = v_all_ref[h]
        for s in range(n_seq):
            s0 = s0_ref[s, h]
            mine = jnp.where(group_of_lane == s, z_t, 0.0)
            col = s * grp + n_new
            s1_ref[s, h] = z_t[:, col:col + 1] * s0 + dot(mine, v_all)
            o = o_intra[s][h] + dot(q_pad_ref[s, h], s0)[0:n_new]
            o_ref[s, :, vs[h]] = _gla_out(o, r_ref[s][:, vs[h]], ng_ref[...])


def _gla_step(q, k, lg, v, r, norm_g, s0):
    nseq, n_new = q.shape[0], q.shape[1]
    sb = SEQ_BLOCK
    assert n_new <= GLA_CHUNK and n_new < SUBLANES and sb * SUBLANES <= LANES
    blk3 = lambda a: pl.BlockSpec((sb,) + a.shape[1:], lambda i: (i, 0, 0))
    blk4 = lambda a: pl.BlockSpec((sb,) + a.shape[1:], lambda i: (i, 0, 0, 0))
    return pl.pallas_call(
        _gla_step_kernel,
        grid=(nseq // sb,),
        in_specs=[blk3(q), blk3(k), blk3(lg), blk3(v), blk3(r), _full(norm_g.shape), blk4(s0)],
        out_specs=[blk3(v), blk4(s0)],
        out_shape=[jax.ShapeDtypeStruct(v.shape, F32), jax.ShapeDtypeStruct(s0.shape, F32)],
        scratch_shapes=[pltpu.VMEM((GLA_HEADS, LANES, GLA_DK), F32),
                        pltpu.VMEM((GLA_HEADS, LANES, GLA_DV), F32),
                        pltpu.VMEM((sb, GLA_HEADS, SUBLANES, GLA_DK), F32)],
        compiler_params=_cparams(1),
        name="gla_step",
    )(q, k, lg, v, r, norm_g, s0)


def kernel(x_prompt, x_sample, cache_k, cache_v, state_gla, c_prompt, c_sample, w_mod, b_mod,
           ln_g, ln_b, attn_w_in, attn_w_out, attn_sinks, gla_w_in, gla_w_gate_up, gla_b_gate,
           gla_norm_g, gla_w_out, mlp_w1, mlp_w2):
    assert x_prompt.shape[0] == 1 and w_mod.shape[0] == DEPTH == 2
    seq = x_prompt.shape[1]
    nseq, n_new = x_sample.shape[0], x_sample.shape[1]
    win = cache_k.shape[2]
    m_s = nseq * n_new

    wq = attn_w_in[0][:, :NQ].reshape(D_MODEL, N_KV_HEADS, GROUP, HEAD_DIM)
    wq = wq.transpose(0, 2, 1, 3).reshape(D_MODEL, NQ)
    w_attn_in = jnp.concatenate([wq, attn_w_in[0][:, NQ:]], axis=1).astype(BF16)
    w_attn_out = attn_w_out[0].reshape(N_KV_HEADS, GROUP, HEAD_DIM, D_MODEL)
    w_attn_out = w_attn_out.transpose(1, 0, 2, 3).reshape(NQ, D_MODEL).astype(BF16)
    w_gla_t = gla_w_in[0].T.astype(BF16)
    assert w_gla_t.shape[0] == 2 * NK + 2 * NV + GLA_GATE_RANK
    w_gla_gu = jnp.pad(gla_w_gate_up[0], ((LANES - GLA_GATE_RANK, 0), (0, 0))).astype(BF16)
    w_gla_out = gla_w_out[0].astype(BF16)
    b_gate = gla_b_gate[0].reshape(1, NK)
    norm_g = gla_norm_g[0].reshape(1, GLA_DV)
    ln_g4 = ln_g.reshape(2 * DEPTH, 1, D_MODEL)
    ln_b4 = ln_b.reshape(2 * DEPTH, 1, D_MODEL)
    sinks = attn_sinks[0]

    pad_rows = (-(m_s + 1)) % SUBLANES
    c_all = jnp.concatenate([jnp.repeat(c_sample, n_new, axis=0), c_prompt,
                             jnp.zeros((pad_rows, D_MODEL), F32)], axis=0)
    mod_all = _adaln_all(c_all, w_mod.reshape(2 * DEPTH, D_MODEL, 3 * D_MODEL),
                         b_mod.reshape(2 * DEPTH, 1, 3 * D_MODEL))
    mods_p = [_Mod(mod_all, p, per_row=False, row0=m_s) for p in range(2 * DEPTH)]
    mods_s = [_Mod(mod_all, p, per_row=True) for p in range(2 * DEPTH)]

    assert seq >= WINDOW == ATTN_BLOCK
    win_p = WINDOW
    three = lambda a: a.reshape(nseq, n_new, a.shape[-1])
    to_slab = lambda c: c[0].transpose(0, 2, 3, 1).reshape(nseq, NKV, win)
    from_slab = lambda c: c.reshape(nseq, N_KV_HEADS, HEAD_DIM, win).transpose(0, 3, 1, 2)[None]

    x_s = x_sample.reshape(m_s, D_MODEL)
    q, k, v = _attn_proj(x_s, mods_s[0], w_attn_in)
    o, k_s, v_s = _dec_attn(q.reshape(nseq, n_new * GROUP, NKV), three(k), three(v),
                            to_slab(cache_k), to_slab(cache_v), sinks)
    x1_s, w1_0, w2_0 = _mix_mlp_stream(o.reshape(m_s, NQ), x_s, mods_s[0], mods_s[1], w_attn_out,
                                       mlp_w1, mlp_w2, ln_g4, ln_b4, 0)
    x1_p, k_p, v_p = _attn_layer(x_prompt[0], mods_p[0], mods_p[1], w_attn_in, w_attn_out,
                                 w1_0, w2_0, ln_g4, ln_b4, sinks, 0)

    gla_w = (w_gla_t, w_gla_gu, b_gate)
    q, k, lg, v, r = _gla_proj(x1_s, mods_s[2], *gla_w, F32)
    o, s_s = _gla_step(three(q), three(k), three(lg), three(v), three(r), norm_g, state_gla[0])
    y_s, w1_1, w2_1 = _mix_mlp_stream(o.reshape(m_s, NV), x1_s, mods_s[2], mods_s[3], w_gla_out,
                                      mlp_w1, mlp_w2, ln_g4, ln_b4, 1)
    q, k, lg, v, r = _gla_proj(x1_p, mods_p[2], *gla_w, BF16)
    o, s_p = _gla_prompt(q, k, lg, v, r, norm_g)
    y_p = _mix_mlp(o, x1_p, mods_p[2], mods_p[3], w_gla_out, w1_1, w2_1, ln_g4, ln_b4, 1)
    k_s, v_s = from_slab(k_s), from_slab(v_s)

    kv_shape_p = (1, 1, win_p, N_KV_HEADS, HEAD_DIM)
    return (y_p[None], y_s.reshape(nseq, n_new, D_MODEL),
            k_p.reshape(kv_shape_p), v_p.reshape(kv_shape_p), s_p[None, None],
            k_s, v_s, s_s[None])
```

```python
import functools

import jax
import jax.numpy as jnp
from jax import lax
from jax.experimental import pallas as pl
from jax.experimental.pallas import tpu as pltpu

F32 = jnp.float32
BF16 = jnp.bfloat16

D_MODEL = 1024
DEPTH = 2
HEAD_DIM = 64
N_Q_HEADS = 16
N_KV_HEADS = 4
GROUP = 4
WINDOW = 128
ATTN_BLOCK = 128
GLA_HEADS = 4
GLA_DK = 128
GLA_DV = 256
GLA_GATE_RANK = 16
GLA_TAU = 16.0
GLA_CHUNK = 64
D_FF = 4 * D_MODEL
ALPHA = (2.0 * DEPTH) ** 0.25
LN_EPS = 1e-5

NQ = N_Q_HEADS * HEAD_DIM
NKV = N_KV_HEADS * HEAD_DIM
NK = GLA_HEADS * GLA_DK
NV = GLA_HEADS * GLA_DV
LANES = 128
SUBLANES = 8
NEG_BIG = -1e30

ROW_TILE = 512
FF_CHUNK = 1024
ATTN_TILE = 512
GLA_TILE = 512
GLA_SUB = 256
SEQ_BLOCK = 8
VMEM_LIMIT = 56 * 1024 * 1024


def _cparams(n_axes):
    return pltpu.CompilerParams(
        dimension_semantics=("arbitrary",) * n_axes,
        vmem_limit_bytes=VMEM_LIMIT,
    )


def _full(shape):
    zeros = (0,) * len(shape)
    return pl.BlockSpec(shape, lambda *_: zeros)


def _layer(arr, idx):
    tail = (0,) * (arr.ndim - 1)
    return pl.BlockSpec((None,) + arr.shape[1:], lambda *_: (idx,) + tail,
                        pipeline_mode=pl.Buffered(1))


def _row_spec(tm, n):
    return pl.BlockSpec((tm, n), lambda i: (i, 0))


class _Mod:
    def __init__(self, arr, p, per_row, row0=0):
        self.arr, self.p, self.per_row, self.row0 = arr, p, per_row, row0

    def spec(self, tm, col):
        p = self.p
        if self.per_row:
            return pl.BlockSpec((None, tm, D_MODEL), lambda i: (p, i, col))
        blk = self.row0 // SUBLANES
        return pl.BlockSpec((None, SUBLANES, D_MODEL), lambda i: (p, blk, col))


def _mod_rows(ref, tm):
    return ref[...] if ref.shape[0] == tm else ref[0:1, :]


def _modulate(x, shift, scale):
    return x * (1.0 + scale) + shift


def _res_ln(x, gate, o, g, b):
    y = ALPHA * x + gate * o
    mu = jnp.mean(y, axis=-1, keepdims=True)
    yc = y - mu
    var = jnp.mean(yc * yc, axis=-1, keepdims=True)
    return yc * lax.rsqrt(var + LN_EPS) * g + b


def _mod_kernel(c_ref, w_ref, b_ref, o_ref):
    c = c_ref[...]
    a = (c * jax.nn.sigmoid(c)).astype(BF16)
    o_ref[...] = jnp.dot(a, w_ref[...].astype(BF16), preferred_element_type=F32) + b_ref[...]


def _adaln_all(c_all, w_mod, b_mod):
    rows = c_all.shape[0]
    tn = 1024
    return pl.pallas_call(
        _mod_kernel,
        grid=(4, 3 * D_MODEL // tn),
        in_specs=[
            pl.BlockSpec((rows, D_MODEL), lambda p, n: (0, 0)),
            pl.BlockSpec((None, D_MODEL, tn), lambda p, n: (p, 0, n)),
            pl.BlockSpec((None, 1, tn), lambda p, n: (p, 0, n)),
        ],
        out_specs=pl.BlockSpec((None, rows, tn), lambda p, n: (p, 0, n)),
        out_shape=jax.ShapeDtypeStruct((4, rows, 3 * D_MODEL), F32),
        compiler_params=_cparams(2),
        name="adaln_mod",
    )(c_all, w_mod, b_mod)


def _attn_proj_kernel(x_ref, sh_ref, sc_ref, w_ref, q_ref, k_ref, v_ref):
    tm = x_ref.shape[0]
    h = _modulate(x_ref[...], _mod_rows(sh_ref, tm), _mod_rows(sc_ref, tm)).astype(BF16)
    q = jnp.dot(h, w_ref[:, 0:NQ], preferred_element_type=F32)
    q_ref[...] = (q * (HEAD_DIM ** -0.5)).astype(BF16)
    k_ref[...] = jnp.dot(h, w_ref[:, NQ:NQ + NKV], preferred_element_type=F32)
    v_ref[...] = jnp.dot(h, w_ref[:, NQ + NKV:NQ + 2 * NKV], preferred_element_type=F32)


def _attn_proj(x, mod, w_in):
    m = x.shape[0]
    tm = min(ROW_TILE, m)
    row = functools.partial(_row_spec, tm)
    return pl.pallas_call(
        _attn_proj_kernel,
        grid=(m // tm,),
        in_specs=[row(D_MODEL), mod.spec(tm, 0), mod.spec(tm, 1), _full(w_in.shape)],
        out_specs=[row(NQ), row(NKV), row(NKV)],
        out_shape=[
            jax.ShapeDtypeStruct((m, NQ), BF16),
            jax.ShapeDtypeStruct((m, NKV), F32),
            jax.ShapeDtypeStruct((m, NKV), F32),
        ],
        compiler_params=_cparams(1),
        name="attn_proj",
    )(x, mod.arr, mod.arr, w_in)


def _gla_proj_kernel(x_ref, sh_ref, sc_ref, wt_ref, wgu_ref, bg_ref,
                     q_ref, k_ref, lg_ref, v_ref, r_ref):
    tm = x_ref.shape[0]
    h = _modulate(x_ref[...], _mod_rows(sh_ref, tm), _mod_rows(sc_ref, tm)).astype(BF16)
    dot = functools.partial(jnp.dot, preferred_element_type=F32)

    def proj(lo, hi):
        return lax.dot_general(h, wt_ref[lo:hi, :], (((1,), (1,)), ((), ())),
                               preferred_element_type=F32)

    n_all = wt_ref.shape[0]
    gdown = proj(n_all - LANES, n_all)
    q_ref[...] = proj(0, NK) * (GLA_DK ** -0.5)
    k_ref[...] = proj(NK, 2 * NK)
    pre = dot(gdown.astype(BF16), wgu_ref[...]) + bg_ref[...]
    v_ref[...] = proj(2 * NK, 2 * NK + NV).astype(v_ref.dtype)
    r_ref[...] = proj(2 * NK + NV, 2 * NK + 2 * NV).astype(r_ref.dtype)
    log_sig = jnp.minimum(pre, 0.0) - jnp.log1p(jnp.exp(-jnp.abs(pre)))
    lg_ref[...] = log_sig / GLA_TAU


def _gla_proj(x, mod, w_t, w_gu, b_gate, vr_dtype):
    m = x.shape[0]
    tm = min(ROW_TILE, m)
    row = functools.partial(_row_spec, tm)
    return pl.pallas_call(
        _gla_proj_kernel,
        grid=(m // tm,),
        in_specs=[row(D_MODEL), mod.spec(tm, 0), mod.spec(tm, 1), _full(w_t.shape),
                  _full(w_gu.shape), _full(b_gate.shape)],
        out_specs=[row(NK), row(NK), row(NK), row(NV), row(NV)],
        out_shape=[
            jax.ShapeDtypeStruct((m, NK), F32),
            jax.ShapeDtypeStruct((m, NK), F32),
            jax.ShapeDtypeStruct((m, NK), F32),
            jax.ShapeDtypeStruct((m, NV), vr_dtype),
            jax.ShapeDtypeStruct((m, NV), vr_dtype),
        ],
        compiler_params=_cparams(1),
        name="gla_proj",
    )(x, mod.arr, mod.arr, w_t, w_gu, b_gate)


def _mix_mlp_kernel(a_ref, x_ref, gt0_ref, sh_ref, sc_ref, gt1_ref, wo_ref, w1_ref, w2_ref,
                    g0_ref, b0_ref, g1_ref, b1_ref, o_ref, acc_ref):
    tm = x_ref.shape[0]
    o = jnp.dot(a_ref[...].astype(BF16), wo_ref[...], preferred_element_type=F32)
    x1 = _res_ln(x_ref[...], _mod_rows(gt0_ref, tm), o, g0_ref[...], b0_ref[...])
    h = _modulate(x1, _mod_rows(sh_ref, tm), _mod_rows(sc_ref, tm)).astype(BF16)
    _mlp_chunks(h, w1_ref, w2_ref, acc_ref, range(D_FF // FF_CHUNK))
    o_ref[...] = _res_ln(x1, _mod_rows(gt1_ref, tm), acc_ref[...], g1_ref[...], b1_ref[...])


def _mix_mlp(a, x, mod_mix, mod_mlp, w_out, w1, w2, ln_g, ln_b, layer):
    m = x.shape[0]
    tm = min(ROW_TILE, m)
    row = functools.partial(_row_spec, tm)
    return pl.pallas_call(
        _mix_mlp_kernel,
        grid=(m // tm,),
        in_specs=[row(a.shape[1]), row(D_MODEL), mod_mix.spec(tm, 2), mod_mlp.spec(tm, 0),
                  mod_mlp.spec(tm, 1), mod_mlp.spec(tm, 2), _full(w_out.shape),
                  _full(w1.shape), _full(w2.shape),
                  _layer(ln_g, 2 * layer), _layer(ln_b, 2 * layer),
                  _layer(ln_g, 2 * layer + 1), _layer(ln_b, 2 * layer + 1)],
        out_specs=row(D_MODEL),
        out_shape=jax.ShapeDtypeStruct((m, D_MODEL), F32),
        scratch_shapes=[pltpu.VMEM((tm, D_MODEL), F32)],
        compiler_params=_cparams(1),
        name="mix_mlp",
    )(a, x, mod_mix.arr, mod_mlp.arr, mod_mlp.arr, mod_mlp.arr, w_out, w1, w2,
      ln_g, ln_b, ln_g, ln_b)


def _mix_mlp_stream_kernel(a_ref, x_ref, gt0_ref, sh_ref, sc_ref, gt1_ref, wo_ref, w1_ref, w2_ref,
                           g0_ref, b0_ref, g1_ref, b1_ref, y_ref, w1b_ref, w2b_ref,
                           x1_s, h_s, acc_ref):
    c = pl.program_id(0)

    @pl.when(c == 0)
    def _():
        o = jnp.dot(a_ref[...].astype(BF16), wo_ref[...], preferred_element_type=F32)
        x1 = _res_ln(x_ref[...], gt0_ref[...], o, g0_ref[...], b0_ref[...])
        x1_s[...] = x1
        h_s[...] = _modulate(x1, sh_ref[...], sc_ref[...]).astype(BF16)
        acc_ref[...] = jnp.zeros_like(acc_ref)

    w1c = w1_ref[...].astype(BF16)
    w2c = w2_ref[...].astype(BF16)
    w1b_ref[...] = w1c
    w2b_ref[...] = w2c
    a = jnp.dot(h_s[...], w1c, preferred_element_type=F32)
    a = jnp.square(jnp.maximum(a, 0.0)).astype(BF16)
    acc_ref[...] += jnp.dot(a, w2c, preferred_element_type=F32)

    @pl.when(c == pl.num_programs(0) - 1)
    def _():
        y_ref[...] = _res_ln(x1_s[...], gt1_ref[...], acc_ref[...], g1_ref[...], b1_ref[...])


def _mix_mlp_stream(a, x, mod_mix, mod_mlp, w_out, w1_f32, w2_f32, ln_g, ln_b, layer):
    m = x.shape[0]
    assert m <= ROW_TILE and mod_mix.per_row and mod_mlp.per_row
    fc = FF_CHUNK
    whole = lambda n: pl.BlockSpec((m, n), lambda c: (0, 0))
    mspec = lambda mod, col: pl.BlockSpec((None, m, D_MODEL), lambda c: (mod.p, 0, col))
    return pl.pallas_call(
        _mix_mlp_stream_kernel,
        grid=(D_FF // fc,),
        in_specs=[whole(a.shape[1]), whole(D_MODEL), mspec(mod_mix, 2), mspec(mod_mlp, 0),
                  mspec(mod_mlp, 1), mspec(mod_mlp, 2), _full(w_out.shape),
                  pl.BlockSpec((None, D_MODEL, fc), lambda c: (layer, 0, c)),
                  pl.BlockSpec((None, fc, D_MODEL), lambda c: (layer, c, 0)),
                  _layer(ln_g, 2 * layer), _layer(ln_b, 2 * layer),
                  _layer(ln_g, 2 * layer + 1), _layer(ln_b, 2 * layer + 1)],
        out_specs=[whole(D_MODEL),
                   pl.BlockSpec((D_MODEL, fc), lambda c: (0, c)),
                   pl.BlockSpec((fc, D_MODEL), lambda c: (c, 0))],
        out_shape=[jax.ShapeDtypeStruct((m, D_MODEL), F32),
                   jax.ShapeDtypeStruct((D_MODEL, D_FF), BF16),
                   jax.ShapeDtypeStruct((D_FF, D_MODEL), BF16)],
        scratch_shapes=[pltpu.VMEM((m, D_MODEL), F32), pltpu.VMEM((m, D_MODEL), BF16),
                        pltpu.VMEM((m, D_MODEL), F32)],
        compiler_params=_cparams(1),
        name="mix_mlp_stream",
    )(a, x, mod_mix.arr, mod_mlp.arr, mod_mlp.arr, mod_mlp.arr, w_out, w1_f32, w2_f32,
      ln_g, ln_b, ln_g, ln_b)


def _alibi_slope(head):
    return 2.0 ** (-8.0 * (head + 1) / N_Q_HEADS)


def _softmax_sink(s, sink):
    m = jnp.maximum(jnp.max(s, axis=-1, keepdims=True), sink)
    e = jnp.exp(s - m)
    den = jnp.sum(e, axis=-1, keepdims=True) + jnp.exp(sink - m)
    return e / den


def _band_bias_init(bias_ref):
    blk = ATTN_BLOCK
    c = lax.broadcasted_iota(jnp.int32, (2 * blk, blk), 0)
    r = lax.broadcasted_iota(jnp.int32, (2 * blk, blk), 1)
    dist = blk + r - c
    valid = (dist >= 0) & (dist <= WINDOW)
    distf = dist.astype(F32)
    for head in range(N_Q_HEADS):
        pen = -_alibi_slope(head) * distf
        bias_ref[0, head] = jnp.where(valid, pen, NEG_BIG)
        bias_ref[1, head] = jnp.where(valid & (c >= blk), pen, NEG_BIG)


def _band_scores(q, kk):
    blk = ATTN_BLOCK
    head_of_lane = lax.broadcasted_iota(jnp.int32, (blk, NKV), 1) // HEAD_DIM
    scores = []
    for b in range(q.shape[0] // blk):
        keys = kk[b * blk:(b + 2) * blk]
        for g in range(GROUP):
            qg = q[b * blk:(b + 1) * blk, g * NKV:(g + 1) * NKV]
            qm = jnp.concatenate(
                [jnp.where(head_of_lane == j, qg, jnp.zeros_like(qg))
                 for j in range(N_KV_HEADS)], axis=0)
            scores.append(lax.dot_general(keys, qm, (((1,), (1,)), ((), ())),
                                          preferred_element_type=F32))
    return scores


def _band_outputs(scores, vvt, first_tile, sinks_ref, bias_ref, o_ref):
    blk = ATTN_BLOCK
    for b in range(len(scores) // GROUP):
        table = jnp.where(first_tile, 1, 0) if b == 0 else 0
        vals_t = vvt[:, b * blk:(b + 2) * blk]
        for g in range(GROUP):
            st_all = scores[b * GROUP + g]
            ps = []
            for j in range(N_KV_HEADS):
                head = j * GROUP + g
                sink = sinks_ref[head]
                st = st_all[:, j * blk:(j + 1) * blk] + bias_ref[table, head]
                m = jnp.maximum(jnp.max(st, axis=0, keepdims=True), sink)
                e = jnp.exp(st - m)
                den = jnp.sum(e, axis=0, keepdims=True) + jnp.exp(sink - m)
                ps.append((e * (1.0 / den)).astype(BF16))
            ot_all = jnp.dot(vals_t, jnp.concatenate(ps, axis=1),
                             preferred_element_type=F32)
            ot = jnp.concatenate(
                [ot_all[j * HEAD_DIM:(j + 1) * HEAD_DIM, j * blk:(j + 1) * blk]
                 for j in range(N_KV_HEADS)], axis=0)
            o_ref[b * blk:(b + 1) * blk, g * NKV:(g + 1) * NKV] = ot.T.astype(BF16)


def _mlp_chunks(h, w1_ref, w2_ref, acc_ref, chunks):
    for c in chunks:
        cols = slice(c * FF_CHUNK, (c + 1) * FF_CHUNK)
        a = jnp.dot(h, w1_ref[:, cols], preferred_element_type=F32)
        a = jnp.square(jnp.maximum(a, 0.0)).astype(BF16)
        d = jnp.dot(a, w2_ref[cols, :], preferred_element_type=F32)
        if c == 0:
            acc_ref[...] = d
        else:
            acc_ref[...] += d


def _attn_layer_kernel(sinks_ref, xc_ref, xp_ref, sh0_ref, sc0_ref, gt0_ref, sh1_ref, sc1_ref,
                       gt1_ref, win_ref, wo_ref, w1_ref, w2_ref, g0_ref, b0_ref, g1_ref, b1_ref,
                       y_ref, kl_ref, vl_ref, o_s, kprev_s, vtprev_s, acc_ref, bias_ref):
    i = pl.program_id(0)
    slot = i % 2
    blk = ATTN_BLOCK
    tm = xc_ref.shape[0]
    dot = functools.partial(jnp.dot, preferred_element_type=F32)
    n_chunks = D_FF // FF_CHUNK

    @pl.when(i == 0)
    def _():
        _band_bias_init(bias_ref)
        o_s[...] = jnp.zeros_like(o_s)
        kprev_s[...] = jnp.zeros_like(kprev_s)
        vtprev_s[...] = jnp.zeros_like(vtprev_s)

    x1 = _res_ln(xp_ref[...], gt0_ref[0:1, :], dot(o_s[1 - slot], wo_ref[...]),
                 g0_ref[...], b0_ref[...])
    h_mlp = _modulate(x1, sh1_ref[0:1, :], sc1_ref[0:1, :]).astype(BF16)

    h_in = _modulate(xc_ref[...], sh0_ref[0:1, :], sc0_ref[0:1, :]).astype(BF16)
    q = (dot(h_in, win_ref[:, 0:NQ]) * (HEAD_DIM ** -0.5)).astype(BF16)
    k = dot(h_in, win_ref[:, NQ:NQ + NKV])
    v = dot(h_in, win_ref[:, NQ + NKV:NQ + 2 * NKV])
    kl_ref[...] = k[tm - blk:, :]
    vl_ref[...] = v[tm - blk:, :]
    k_bf = k.astype(BF16)
    vt = v.T.astype(BF16)
    kk = jnp.concatenate([kprev_s[...], k_bf], axis=0)
    vvt = jnp.concatenate([vtprev_s[...], vt], axis=1)
    scores = _band_scores(q, kk)
    kprev_s[...] = k_bf[tm - blk:, :]
    vtprev_s[...] = vt[:, tm - blk:]

    _mlp_chunks(h_mlp, w1_ref, w2_ref, acc_ref, range(n_chunks))
    _band_outputs(scores, vvt, i == 0, sinks_ref, bias_ref, o_s.at[slot])
    y_ref[...] = _res_ln(x1, gt1_ref[0:1, :], acc_ref[...], g1_ref[...], b1_ref[...])


def _attn_layer(x, mod_mix, mod_mlp, w_in, w_out, w1, w2, ln_g, ln_b, sinks, layer):
    m = x.shape[0]
    tm = ATTN_TILE
    blk = ATTN_BLOCK
    n = m // tm
    assert not mod_mix.per_row and not mod_mlp.per_row
    cur = pl.BlockSpec((tm, D_MODEL), lambda i: (jnp.minimum(i, n - 1), 0))
    prev = pl.BlockSpec((tm, D_MODEL), lambda i: (jnp.maximum(i - 1, 0), 0))
    last = pl.BlockSpec((blk, NKV), lambda i: (0, 0))
    return pl.pallas_call(
        _attn_layer_kernel,
        grid=(n + 1,),
        in_specs=[pl.BlockSpec(memory_space=pltpu.SMEM), cur, prev,
                  mod_mix.spec(tm, 0), mod_mix.spec(tm, 1), mod_mix.spec(tm, 2),
                  mod_mlp.spec(tm, 0), mod_mlp.spec(tm, 1), mod_mlp.spec(tm, 2),
                  _full(w_in.shape), _full(w_out.shape), _full(w1.shape), _full(w2.shape),
                  _layer(ln_g, 2 * layer), _layer(ln_b, 2 * layer),
                  _layer(ln_g, 2 * layer + 1), _layer(ln_b, 2 * layer + 1)],
        out_specs=[prev, last, last],
        out_shape=[jax.ShapeDtypeStruct((m, D_MODEL), F32),
                   jax.ShapeDtypeStruct((blk, NKV), F32),
                   jax.ShapeDtypeStruct((blk, NKV), F32)],
        scratch_shapes=[pltpu.VMEM((2, tm, NQ), BF16),
                        pltpu.VMEM((blk, NKV), BF16),
                        pltpu.VMEM((NKV, blk), BF16),
                        pltpu.VMEM((tm, D_MODEL), F32),
                        pltpu.VMEM((2, N_Q_HEADS, 2 * blk, blk), F32)],
        compiler_params=_cparams(1),
        name="attn_layer",
    )(sinks, x, x, mod_mix.arr, mod_mix.arr, mod_mix.arr, mod_mlp.arr, mod_mlp.arr,
      mod_mlp.arr, w_in, w_out, w1, w2, ln_g, ln_b, ln_g, ln_b)


def _dec_attn_kernel(sinks_ref, q_ref, kn_ref, vn_ref, ck_ref, cv_ref,
                     o_ref, nk_ref, nv_ref, zk_ref, zv_ref):
    n_seq = ck_ref.shape[0]
    n_new = kn_ref.shape[0] // n_seq
    win = ck_ref.shape[2]
    rows = n_new * GROUP
    n_rows = N_KV_HEADS * rows
    keep = win - n_new
    row = lax.broadcasted_iota(jnp.int32, (n_rows, 2 * win), 0)
    col = lax.broadcasted_iota(jnp.int32, (n_rows, 2 * win), 1)
    j_r = row // rows
    t_r = (row // GROUP) % n_new
    g_r = row % GROUP
    h_r = j_r * GROUP + g_r
    slope = jnp.exp2(-8.0 * (h_r + 1).astype(F32) / N_Q_HEADS)
    sink = jnp.zeros((n_rows, 1), F32)
    h_col = h_r[:, 0:1]
    for h in range(N_Q_HEADS):
        sink = jnp.where(h_col == h, sinks_ref[h], sink)
    is_key = (col < win) | (col >= win + keep)
    frame = jnp.where(col < win, col, col - keep)
    dist = t_r + win - frame
    valid = is_key & (dist >= 0) & (dist <= WINDOW)
    bias = jnp.where(valid, -slope * dist.astype(F32), NEG_BIG)
    head_of_lane = lax.broadcasted_iota(jnp.int32, (rows, NKV), 1) // HEAD_DIM
    lane_w = lax.broadcasted_iota(jnp.int32, (NKV, win), 1)

    @pl.when(pl.program_id(0) == 0)
    def _():
        zk_ref[...] = jnp.zeros_like(zk_ref)
        zv_ref[...] = jnp.zeros_like(zv_ref)

    for s in range(n_seq):
        u = s
        tok = slice(s * n_new, (s + 1) * n_new)
        k_t = ck_ref[s]
        v_t = cv_ref[s]
        zk_ref[u, keep:win, :] = kn_ref[tok, :]
        zv_ref[u, keep:win, :] = vn_ref[tok, :]
        zk_t = zk_ref[u].T
        zv_t = zv_ref[u].T
        nk_ref[s] = jnp.where(lane_w < keep, pltpu.roll(k_t, keep, 1), zk_t)
        nv_ref[s] = jnp.where(lane_w < keep, pltpu.roll(v_t, keep, 1), zv_t)
        keys = jnp.concatenate([k_t, zk_t], axis=1).astype(BF16)
        vals = jnp.concatenate([v_t, zv_t], axis=1).astype(BF16)
        qs = q_ref[s]
        qbd = jnp.concatenate(
            [jnp.where(head_of_lane == j, qs, jnp.zeros_like(qs)) for j in range(N_KV_HEADS)],
            axis=0)
        sc = jnp.dot(qbd, keys, preferred_element_type=F32) + bias
        p = _softmax_sink(sc, sink).astype(BF16)
        pv = lax.dot_general(p, vals, (((1,), (1,)), ((), ())), preferred_element_type=F32)
        o = jnp.zeros((rows, NKV), F32)
        for j in range(N_KV_HEADS):
            o = o + jnp.where(head_of_lane == j, pv[j * rows:(j + 1) * rows], 0.0)
        o_ref[s] = o.astype(BF16)


def _dec_attn(q, k_new, v_new, cache_kt, cache_vt, sinks):
    nseq, win = cache_kt.shape[0], cache_kt.shape[2]
    n_new = k_new.shape[0] // nseq
    assert win == LANES and n_new < SUBLANES
    sb = SEQ_BLOCK
    blk = lambda a: pl.BlockSpec((sb,) + a.shape[1:], lambda i: (i, 0, 0))
    rows = lambda a: pl.BlockSpec((sb * n_new, a.shape[1]), lambda i: (i, 0))
    zshape = (sb, win, NKV)
    return pl.pallas_call(
        _dec_attn_kernel,
        grid=(nseq // sb,),
        in_specs=[pl.BlockSpec(memory_space=pltpu.SMEM), blk(q), rows(k_new), rows(v_new),
                  blk(cache_kt), blk(cache_vt)],
        out_specs=[blk(q), blk(cache_kt), blk(cache_vt)],
        out_shape=[
            jax.ShapeDtypeStruct(q.shape, BF16),
            jax.ShapeDtypeStruct(cache_kt.shape, F32),
            jax.ShapeDtypeStruct(cache_vt.shape, F32),
        ],
        scratch_shapes=[pltpu.VMEM(zshape, F32), pltpu.VMEM(zshape, F32)],
        compiler_params=_cparams(1),
        name="dec_attn",
    )(sinks, q, k_new, v_new, cache_kt, cache_vt)


def _split2(x):
    hi = x.astype(BF16)
    lo = (x - hi.astype(F32)).astype(BF16)
    return hi, lo


def _gla_out(o, r, norm_g):
    ms = jnp.mean(o * o, axis=-1, keepdims=True)
    o = o * lax.rsqrt(ms + LN_EPS) * norm_g
    return o * (r * jax.nn.sigmoid(r))


def _gla_prompt_kernel(q_ref, k_ref, lg_ref, v_ref, r_ref, ng_ref, o_ref, s_out_ref, s_ref):
    i = pl.program_id(0)
    sub = GLA_SUB
    n_sub = q_ref.shape[0] // sub
    ch = GLA_CHUNK
    n_ch = sub // ch
    dot = functools.partial(jnp.dot, preferred_element_type=F32)
    heads = range(GLA_HEADS)
    ks = [slice(h * GLA_DK, (h + 1) * GLA_DK) for h in heads]
    vs = [slice(h * GLA_DV, (h + 1) * GLA_DV) for h in heads]

    @pl.when(i == 0)
    def _():
        s_ref[...] = jnp.zeros_like(s_ref)

    row = lax.broadcasted_iota(jnp.int32, (sub, sub), 0)
    col = lax.broadcasted_iota(jnp.int32, (sub, sub), 1)
    causal = ((row // ch) == (col // ch)) & (col <= row)
    tril = jnp.where(causal, 1.0, 0.0).astype(BF16)
    chunk_of_col = lax.broadcasted_iota(jnp.int32, (GLA_DK, sub), 1) // ch

    qd, kd, kdec, dec_rows = [], [], [], []
    for t in range(n_sub):
        rows = slice(t * sub, (t + 1) * sub)
        hi, lo = _split2(lg_ref[rows, :])
        b = dot(tril, hi) + dot(tril, lo)
        ends = [b[(c + 1) * ch - 1:(c + 1) * ch, :] for c in range(n_ch)]
        b_end = jnp.concatenate([jnp.broadcast_to(e, (ch, NK)) for e in ends], axis=0)
        q = q_ref[rows, :]
        k = k_ref[rows, :]
        qd.append((q * jnp.exp(b)).astype(BF16))
        kd.append((k * jnp.exp(-b)).astype(BF16))
        kdec.append(k * jnp.exp(b_end - b))
        dec_rows.append(jnp.concatenate(
            [jnp.exp(e) for e in ends] + [jnp.zeros((LANES - n_ch, NK), F32)], axis=0))

    a = [[lax.dot_general(qd[t][:, ks[h]], kd[t][:, ks[h]], (((1,), (1,)), ((), ())),
                          preferred_element_type=F32) for h in heads] for t in range(n_sub)]

    u = []
    for t in range(n_sub):
        rows = slice(t * sub, (t + 1) * sub)
        u_t = []
        for h in heads:
            kdec_t = kdec[t][:, ks[h]].T.astype(BF16)
            stacked = jnp.concatenate(
                [jnp.where(chunk_of_col == c, kdec_t, jnp.zeros_like(kdec_t))
                 for c in range(n_ch)], axis=0)
            u_t.append(dot(stacked, v_ref[rows, vs[h]]))
        u.append(u_t)

    o_intra = []
    for t in range(n_sub):
        rows = slice(t * sub, (t + 1) * sub)
        o_intra.append([dot(jnp.where(causal, a[t][h], 0.0).astype(BF16), v_ref[rows, vs[h]])
                        for h in heads])

    for h in heads:
        s = s_ref[h]
        for t in range(n_sub):
            rows = slice(t * sub, (t + 1) * sub)
            dec_t = dec_rows[t][:, ks[h]].T
            o_inter = []
            for c in range(n_ch):
                o_inter.append(dot(qd[t][c * ch:(c + 1) * ch, ks[h]], s.astype(BF16)))
                s = dec_t[:, c:c + 1] * s + u[t][h][c * GLA_DK:(c + 1) * GLA_DK]
            o = o_intra[t][h] + jnp.concatenate(o_inter, axis=0)
            o_ref[rows, vs[h]] = _gla_out(
                o, r_ref[rows, vs[h]].astype(F32), ng_ref[...]).astype(o_ref.dtype)
        s_ref[h] = s

    s_out_ref[...] = s_ref[...]


def _gla_prompt(q, k, lg, v, r, norm_g):
    m = q.shape[0]
    tt = GLA_TILE
    row = functools.partial(_row_spec, tt)
    state = (GLA_HEADS, GLA_DK, GLA_DV)
    return pl.pallas_call(
        _gla_prompt_kernel,
        grid=(m // tt,),
        in_specs=[row(NK), row(NK), row(NK), row(NV), row(NV), _full(norm_g.shape)],
        out_specs=[row(NV), _full(state)],
        out_shape=[jax.ShapeDtypeStruct((m, NV), BF16), jax.ShapeDtypeStruct(state, F32)],
        scratch_shapes=[pltpu.VMEM(state, F32)],
        compiler_params=_cparams(1),
        name="gla_prompt",
    )(q, k, lg, v, r, norm_g)


def _gla_step_kernel(q_ref, k_ref, lg_ref, v_ref, r_ref, ng_ref, s0_ref,
                     o_ref, s1_ref, z_ref, v_all_ref, q_pad_ref):
    n_seq = s0_ref.shape[0]
    n_new = q_ref.shape[0] // n_seq
    tok = [slice(s * n_new, (s + 1) * n_new) for s in range(n_seq)]
    grp = SUBLANES
    dot = functools.partial(jnp.dot, preferred_element_type=F32)
    rowi = lax.broadcasted_iota(jnp.int32, (n_new, NK), 0)
    rowv = lax.broadcasted_iota(jnp.int32, (n_new, GLA_DV), 0)
    heads = range(GLA_HEADS)
    ks = [slice(h * GLA_DK, (h + 1) * GLA_DK) for h in heads]
    vs = [slice(h * GLA_DV, (h + 1) * GLA_DV) for h in heads]

    @pl.when(pl.program_id(0) == 0)
    def _():
        z_ref[...] = jnp.zeros_like(z_ref)
        v_all_ref[...] = jnp.zeros_like(v_all_ref)
        q_pad_ref[...] = jnp.zeros_like(q_pad_ref)

    o_intra = []
    for s in range(n_seq):
        lg = lg_ref[tok[s], :]
        b = jnp.zeros_like(lg)
        for t in range(n_new):
            b = b + jnp.where(rowi >= t, jnp.broadcast_to(lg[t:t + 1, :], lg.shape), 0.0)
        b_end = b[n_new - 1:n_new, :]
        qd = q_ref[tok[s], :] * jnp.exp(b)
        k = k_ref[tok[s], :]
        kd = k * jnp.exp(-b)
        kdec = k * jnp.exp(b_end - b)
        dec = jnp.exp(b_end)
        v = v_ref[tok[s], :]
        row0 = s * grp
        o_s = []
        for h in heads:
            v_h = v[:, vs[h]]
            qd_h = qd[:, ks[h]]
            o = jnp.zeros((n_new, GLA_DV), F32)
            for t in range(n_new):
                a_t = jnp.sum(qd_h * kd[t:t + 1, ks[h]], axis=-1, keepdims=True)
                o = o + jnp.where(rowv >= t, a_t * v_h[t:t + 1, :], 0.0)
            o_s.append(o)
            q_pad_ref[s, h, 0:n_new, :] = qd_h
            z_ref[h, row0:row0 + n_new, :] = kdec[:, ks[h]]
            z_ref[h, row0 + n_new:row0 + n_new + 1, :] = dec[:, ks[h]]
            v_all_ref[h, row0:row0 + n_new, :] = v_h
        o_intra.append(o_s)

    group_of_lane = lax.broadcasted_iota(jnp.int32, (GLA_DK, LANES), 1) // grp
    for h in heads:
        z_t = z_ref[h].T
        v_all = v_all_ref[h]
        for s in range(n_seq):
            s0 = s0_ref[s, h]
            mine = jnp.where(group_of_lane == s, z_t, 0.0)
            col = s * grp + n_new
            s1_ref[s, h] = z_t[:, col:col + 1] * s0 + dot(mine, v_all)
            o = o_intra[s][h] + dot(q_pad_ref[s, h], s0)[0:n_new]
            o_ref[tok[s], vs[h]] = _gla_out(o, r_ref[tok[s], vs[h]], ng_ref[...])


def _gla_step(q, k, lg, v, r, norm_g, s0):
    nseq = s0.shape[0]
    n_new = q.shape[0] // nseq
    sb = SEQ_BLOCK
    assert n_new <= GLA_CHUNK and n_new < SUBLANES and sb * SUBLANES <= LANES
    rows = lambda a: pl.BlockSpec((sb * n_new, a.shape[1]), lambda i: (i, 0))
    blk4 = lambda a: pl.BlockSpec((sb,) + a.shape[1:], lambda i: (i, 0, 0, 0))
    return pl.pallas_call(
        _gla_step_kernel,
        grid=(nseq // sb,),
        in_specs=[rows(q), rows(k), rows(lg), rows(v), rows(r), _full(norm_g.shape), blk4(s0)],
        out_specs=[rows(v), blk4(s0)],
        out_shape=[jax.ShapeDtypeStruct(v.shape, F32), jax.ShapeDtypeStruct(s0.shape, F32)],
        scratch_shapes=[pltpu.VMEM((GLA_HEADS, LANES, GLA_DK), F32),
                        pltpu.VMEM((GLA_HEADS, LANES, GLA_DV), F32),
                        pltpu.VMEM((sb, GLA_HEADS, SUBLANES, GLA_DK), F32)],
        compiler_params=_cparams(1),
        name="gla_step",
    )(q, k, lg, v, r, norm_g, s0)


def kernel(x_prompt, x_sample, cache_k, cache_v, state_gla, c_prompt, c_sample, w_mod, b_mod,
           ln_g, ln_b, attn_w_in, attn_w_out, attn_sinks, gla_w_in, gla_w_gate_up, gla_b_gate,
           gla_norm_g, gla_w_out, mlp_w1, mlp_w2):
    assert x_prompt.shape[0] == 1 and w_mod.shape[0] == DEPTH == 2
    seq = x_prompt.shape[1]
    nseq, n_new = x_sample.shape[0], x_sample.shape[1]
    win = cache_k.shape[2]
    m_s = nseq * n_new

    wq = attn_w_in[0][:, :NQ].reshape(D_MODEL, N_KV_HEADS, GROUP, HEAD_DIM)
    wq = wq.transpose(0, 2, 1, 3).reshape(D_MODEL, NQ)
    w_attn_in = jnp.concatenate([wq, attn_w_in[0][:, NQ:]], axis=1).astype(BF16)
    w_attn_out = attn_w_out[0].reshape(N_KV_HEADS, GROUP, HEAD_DIM, D_MODEL)
    w_attn_out = w_attn_out.transpose(1, 0, 2, 3).reshape(NQ, D_MODEL).astype(BF16)
    w_gla_t = gla_w_in[0].T.astype(BF16)
    assert w_gla_t.shape[0] == 2 * NK + 2 * NV + GLA_GATE_RANK
    w_gla_gu = jnp.pad(gla_w_gate_up[0], ((LANES - GLA_GATE_RANK, 0), (0, 0))).astype(BF16)
    w_gla_out = gla_w_out[0].astype(BF16)
    b_gate = gla_b_gate[0].reshape(1, NK)
    norm_g = gla_norm_g[0].reshape(1, GLA_DV)
    ln_g4 = ln_g.reshape(2 * DEPTH, 1, D_MODEL)
    ln_b4 = ln_b.reshape(2 * DEPTH, 1, D_MODEL)
    sinks = attn_sinks[0]

    pad_rows = (-(m_s + 1)) % SUBLANES
    c_all = jnp.concatenate([jnp.repeat(c_sample, n_new, axis=0), c_prompt,
                             jnp.zeros((pad_rows, D_MODEL), F32)], axis=0)
    mod_all = _adaln_all(c_all, w_mod.reshape(2 * DEPTH, D_MODEL, 3 * D_MODEL),
                         b_mod.reshape(2 * DEPTH, 1, 3 * D_MODEL))
    mods_p = [_Mod(mod_all, p, per_row=False, row0=m_s) for p in range(2 * DEPTH)]
    mods_s = [_Mod(mod_all, p, per_row=True) for p in range(2 * DEPTH)]

    assert seq >= WINDOW == ATTN_BLOCK
    win_p = WINDOW
    to_slab = lambda c: c[0].transpose(0, 2, 3, 1).reshape(nseq, NKV, win)
    from_slab = lambda c: c.reshape(nseq, N_KV_HEADS, HEAD_DIM, win).transpose(0, 3, 1, 2)[None]

    x_s = x_sample.reshape(m_s, D_MODEL)
    q, k, v = _attn_proj(x_s, mods_s[0], w_attn_in)
    o, k_s, v_s = _dec_attn(q.reshape(nseq, n_new * GROUP, NKV), k, v,
                            to_slab(cache_k), to_slab(cache_v), sinks)
    x1_s, w1_0, w2_0 = _mix_mlp_stream(o.reshape(m_s, NQ), x_s, mods_s[0], mods_s[1], w_attn_out,
                                       mlp_w1, mlp_w2, ln_g4, ln_b4, 0)
    x1_p, k_p, v_p = _attn_layer(x_prompt[0], mods_p[0], mods_p[1], w_attn_in, w_attn_out,
                                 w1_0, w2_0, ln_g4, ln_b4, sinks, 0)

    gla_w = (w_gla_t, w_gla_gu, b_gate)
    q, k, lg, v, r = _gla_proj(x1_s, mods_s[2], *gla_w, F32)
    o, s_s = _gla_step(q, k, lg, v, r, norm_g, state_gla[0])
    y_s, w1_1, w2_1 = _mix_mlp_stream(o, x1_s, mods_s[2], mods_s[3], w_gla_out,
                                      mlp_w1, mlp_w2, ln_g4, ln_b4, 1)
    q, k, lg, v, r = _gla_proj(x1_p, mods_p[2], *gla_w, BF16)
    o, s_p = _gla_prompt(q, k, lg, v, r, norm_g)
    y_p = _mix_mlp(o, x1_p, mods_p[2], mods_p[3], w_gla_out, w1_1, w2_1, ln_g4, ln_b4, 1)
    k_s, v_s = from_slab(k_s), from_slab(v_s)

    kv_shape_p = (1, 1, win_p, N_KV_HEADS, HEAD_DIM)
    return (y_p[None], y_s.reshape(nseq, n_new, D_MODEL),
            k_p.reshape(kv_shape_p), v_p.reshape(kv_shape_p), s_p[None, None],
            k_s, v_s, s_s[None])
```

```python
import functools

import jax
import jax.numpy as jnp
from jax import lax
from jax.experimental import pallas as pl
from jax.experimental.pallas import tpu as pltpu

F32 = jnp.float32
BF16 = jnp.bfloat16

D_MODEL = 1024
DEPTH = 2
HEAD_DIM = 64
N_Q_HEADS = 16
N_KV_HEADS = 4
GROUP = 4
WINDOW = 128
ATTN_BLOCK = 128
GLA_HEADS = 4
GLA_DK = 128
GLA_DV = 256
GLA_GATE_RANK = 16
GLA_TAU = 16.0
GLA_CHUNK = 64
D_FF = 4 * D_MODEL
ALPHA = (2.0 * DEPTH) ** 0.25
LN_EPS = 1e-5

NQ = N_Q_HEADS * HEAD_DIM
NKV = N_KV_HEADS * HEAD_DIM
NK = GLA_HEADS * GLA_DK
NV = GLA_HEADS * GLA_DV
LANES = 128
SUBLANES = 8
NEG_BIG = -1e30

ROW_TILE = 512
FF_CHUNK = 1024
ATTN_TILE = 512
GLA_TILE = 512
GLA_SUB = 256
SEQ_BLOCK = 8
VMEM_LIMIT = 56 * 1024 * 1024


def _cparams(n_axes):
    return pltpu.CompilerParams(
        dimension_semantics=("arbitrary",) * n_axes,
        vmem_limit_bytes=VMEM_LIMIT,
    )


def _full(shape):
    zeros = (0,) * len(shape)
    return pl.BlockSpec(shape, lambda *_: zeros)


def _layer(arr, idx):
    tail = (0,) * (arr.ndim - 1)
    return pl.BlockSpec((None,) + arr.shape[1:], lambda *_: (idx,) + tail,
                        pipeline_mode=pl.Buffered(1))


def _row_spec(tm, n):
    return pl.BlockSpec((tm, n), lambda i: (i, 0))


class _Mod:
    def __init__(self, arr, p, per_row, row0=0):
        self.arr, self.p, self.per_row, self.row0 = arr, p, per_row, row0

    def spec(self, tm, col):
        p = self.p
        if self.per_row:
            return pl.BlockSpec((None, tm, D_MODEL), lambda i: (p, i, col))
        blk = self.row0 // SUBLANES
        return pl.BlockSpec((None, SUBLANES, D_MODEL), lambda i: (p, blk, col))


def _mod_rows(ref, tm):
    return ref[...] if ref.shape[0] == tm else ref[0:1, :]


def _modulate(x, shift, scale):
    return x * (1.0 + scale) + shift


def _res_ln(x, gate, o, g, b):
    y = ALPHA * x + gate * o
    mu = jnp.mean(y, axis=-1, keepdims=True)
    yc = y - mu
    var = jnp.mean(yc * yc, axis=-1, keepdims=True)
    return yc * lax.rsqrt(var + LN_EPS) * g + b


def _mod_kernel(c_ref, w_ref, b_ref, o_ref):
    c = c_ref[...]
    a = (c * jax.nn.sigmoid(c)).astype(BF16)
    o_ref[...] = jnp.dot(a, w_ref[...].astype(BF16), preferred_element_type=F32) + b_ref[...]


def _adaln_all(c_all, w_mod, b_mod):
    rows = c_all.shape[0]
    tn = 1024
    return pl.pallas_call(
        _mod_kernel,
        grid=(4, 3 * D_MODEL // tn),
        in_specs=[
            pl.BlockSpec((rows, D_MODEL), lambda p, n: (0, 0)),
            pl.BlockSpec((None, D_MODEL, tn), lambda p, n: (p, 0, n)),
            pl.BlockSpec((None, 1, tn), lambda p, n: (p, 0, n)),
        ],
        out_specs=pl.BlockSpec((None, rows, tn), lambda p, n: (p, 0, n)),
        out_shape=jax.ShapeDtypeStruct((4, rows, 3 * D_MODEL), F32),
        compiler_params=_cparams(2),
        name="adaln_mod",
    )(c_all, w_mod, b_mod)


def _attn_proj_kernel(x_ref, sh_ref, sc_ref, w_ref, q_ref, k_ref, v_ref):
    tm = x_ref.shape[0]
    h = _modulate(x_ref[...], _mod_rows(sh_ref, tm), _mod_rows(sc_ref, tm)).astype(BF16)
    q = jnp.dot(h, w_ref[:, 0:NQ], preferred_element_type=F32)
    q_ref[...] = (q * (HEAD_DIM ** -0.5)).astype(BF16)
    k_ref[...] = jnp.dot(h, w_ref[:, NQ:NQ + NKV], preferred_element_type=F32)
    v_ref[...] = jnp.dot(h, w_ref[:, NQ + NKV:NQ + 2 * NKV], preferred_element_type=F32)


def _attn_proj(x, mod, w_in):
    m = x.shape[0]
    tm = min(ROW_TILE, m)
    row = functools.partial(_row_spec, tm)
    return pl.pallas_call(
        _attn_proj_kernel,
        grid=(m // tm,),
        in_specs=[row(D_MODEL), mod.spec(tm, 0), mod.spec(tm, 1), _full(w_in.shape)],
        out_specs=[row(NQ), row(NKV), row(NKV)],
        out_shape=[
            jax.ShapeDtypeStruct((m, NQ), BF16),
            jax.ShapeDtypeStruct((m, NKV), F32),
            jax.ShapeDtypeStruct((m, NKV), F32),
        ],
        compiler_params=_cparams(1),
        name="attn_proj",
    )(x, mod.arr, mod.arr, w_in)


def _gla_proj_kernel(x_ref, sh_ref, sc_ref, wt_ref, wgu_ref, bg_ref,
                     q_ref, k_ref, lg_ref, v_ref, r_ref):
    tm = x_ref.shape[0]
    h = _modulate(x_ref[...], _mod_rows(sh_ref, tm), _mod_rows(sc_ref, tm)).astype(BF16)
    dot = functools.partial(jnp.dot, preferred_element_type=F32)

    def proj(lo, hi):
        return lax.dot_general(h, wt_ref[lo:hi, :], (((1,), (1,)), ((), ())),
                               preferred_element_type=F32)

    n_all = wt_ref.shape[0]
    gdown = proj(n_all - LANES, n_all)
    q_ref[...] = proj(0, NK) * (GLA_DK ** -0.5)
    k_ref[...] = proj(NK, 2 * NK)
    pre = dot(gdown.astype(BF16), wgu_ref[...]) + bg_ref[...]
    v_ref[...] = proj(2 * NK, 2 * NK + NV).astype(v_ref.dtype)
    r_ref[...] = proj(2 * NK + NV, 2 * NK + 2 * NV).astype(r_ref.dtype)
    log_sig = jnp.minimum(pre, 0.0) - jnp.log1p(jnp.exp(-jnp.abs(pre)))
    lg_ref[...] = log_sig / GLA_TAU


def _gla_proj(x, mod, w_t, w_gu, b_gate, vr_dtype):
    m = x.shape[0]
    tm = min(ROW_TILE, m)
    row = functools.partial(_row_spec, tm)
    return pl.pallas_call(
        _gla_proj_kernel,
        grid=(m // tm,),
        in_specs=[row(D_MODEL), mod.spec(tm, 0), mod.spec(tm, 1), _full(w_t.shape),
                  _full(w_gu.shape), _full(b_gate.shape)],
        out_specs=[row(NK), row(NK), row(NK), row(NV), row(NV)],
        out_shape=[
            jax.ShapeDtypeStruct((m, NK), F32),
            jax.ShapeDtypeStruct((m, NK), F32),
            jax.ShapeDtypeStruct((m, NK), F32),
            jax.ShapeDtypeStruct((m, NV), vr_dtype),
            jax.ShapeDtypeStruct((m, NV), vr_dtype),
        ],
        compiler_params=_cparams(1),
        name="gla_proj",
    )(x, mod.arr, mod.arr, w_t, w_gu, b_gate)


def _mix_mlp_kernel(a_ref, x_ref, gt0_ref, sh_ref, sc_ref, gt1_ref, wo_ref, w1_ref, w2_ref,
                    g0_ref, b0_ref, g1_ref, b1_ref, o_ref, acc_ref):
    tm = x_ref.shape[0]
    o = jnp.dot(a_ref[...].astype(BF16), wo_ref[...], preferred_element_type=F32)
    x1 = _res_ln(x_ref[...], _mod_rows(gt0_ref, tm), o, g0_ref[...], b0_ref[...])
    h = _modulate(x1, _mod_rows(sh_ref, tm), _mod_rows(sc_ref, tm)).astype(BF16)
    _mlp_chunks(h, w1_ref, w2_ref, acc_ref, range(D_FF // FF_CHUNK))
    o_ref[...] = _res_ln(x1, _mod_rows(gt1_ref, tm), acc_ref[...], g1_ref[...], b1_ref[...])


def _mix_mlp(a, x, mod_mix, mod_mlp, w_out, w1, w2, ln_g, ln_b, layer):
    m = x.shape[0]
    tm = min(ROW_TILE, m)
    row = functools.partial(_row_spec, tm)
    return pl.pallas_call(
        _mix_mlp_kernel,
        grid=(m // tm,),
        in_specs=[row(a.shape[1]), row(D_MODEL), mod_mix.spec(tm, 2), mod_mlp.spec(tm, 0),
                  mod_mlp.spec(tm, 1), mod_mlp.spec(tm, 2), _full(w_out.shape),
                  _full(w1.shape), _full(w2.shape),
                  _layer(ln_g, 2 * layer), _layer(ln_b, 2 * layer),
                  _layer(ln_g, 2 * layer + 1), _layer(ln_b, 2 * layer + 1)],
        out_specs=row(D_MODEL),
        out_shape=jax.ShapeDtypeStruct((m, D_MODEL), F32),
        scratch_shapes=[pltpu.VMEM((tm, D_MODEL), F32)],
        compiler_params=_cparams(1),
        name="mix_mlp",
    )(a, x, mod_mix.arr, mod_mlp.arr, mod_mlp.arr, mod_mlp.arr, w_out, w1, w2,
      ln_g, ln_b, ln_g, ln_b)


def _mix_mlp_stream_kernel(a_ref, x_ref, gt0_ref, sh_ref, sc_ref, gt1_ref, wo_ref, w1_ref, w2_ref,
                           g0_ref, b0_ref, g1_ref, b1_ref, y_ref, w1b_ref, w2b_ref,
                           x1_s, h_s, acc_ref):
    c = pl.program_id(0)

    @pl.when(c == 0)
    def _():
        o = jnp.dot(a_ref[...].astype(BF16), wo_ref[...], preferred_element_type=F32)
        x1 = _res_ln(x_ref[...], gt0_ref[...], o, g0_ref[...], b0_ref[...])
        x1_s[...] = x1
        h_s[...] = _modulate(x1, sh_ref[...], sc_ref[...]).astype(BF16)
        acc_ref[...] = jnp.zeros_like(acc_ref)

    w1c = w1_ref[...].astype(BF16)
    w2c = w2_ref[...].astype(BF16)
    w1b_ref[...] = w1c
    w2b_ref[...] = w2c
    a = jnp.dot(h_s[...], w1c, preferred_element_type=F32)
    a = jnp.square(jnp.maximum(a, 0.0)).astype(BF16)
    acc_ref[...] += jnp.dot(a, w2c, preferred_element_type=F32)

    @pl.when(c == pl.num_programs(0) - 1)
    def _():
        y_ref[...] = _res_ln(x1_s[...], gt1_ref[...], acc_ref[...], g1_ref[...], b1_ref[...])


def _mix_mlp_stream(a, x, mod_mix, mod_mlp, w_out, w1_f32, w2_f32, ln_g, ln_b, layer):
    m = x.shape[0]
    assert m <= ROW_TILE and mod_mix.per_row and mod_mlp.per_row
    fc = FF_CHUNK
    whole = lambda n: pl.BlockSpec((m, n), lambda c: (0, 0))
    mspec = lambda mod, col: pl.BlockSpec((None, m, D_MODEL), lambda c: (mod.p, 0, col))
    return pl.pallas_call(
        _mix_mlp_stream_kernel,
        grid=(D_FF // fc,),
        in_specs=[whole(a.shape[1]), whole(D_MODEL), mspec(mod_mix, 2), mspec(mod_mlp, 0),
                  mspec(mod_mlp, 1), mspec(mod_mlp, 2), _full(w_out.shape),
                  pl.BlockSpec((None, D_MODEL, fc), lambda c: (layer, 0, c)),
                  pl.BlockSpec((None, fc, D_MODEL), lambda c: (layer, c, 0)),
                  _layer(ln_g, 2 * layer), _layer(ln_b, 2 * layer),
                  _layer(ln_g, 2 * layer + 1), _layer(ln_b, 2 * layer + 1)],
        out_specs=[whole(D_MODEL),
                   pl.BlockSpec((D_MODEL, fc), lambda c: (0, c)),
                   pl.BlockSpec((fc, D_MODEL), lambda c: (c, 0))],
        out_shape=[jax.ShapeDtypeStruct((m, D_MODEL), F32),
                   jax.ShapeDtypeStruct((D_MODEL, D_FF), BF16),
                   jax.ShapeDtypeStruct((D_FF, D_MODEL), BF16)],
        scratch_shapes=[pltpu.VMEM((m, D_MODEL), F32), pltpu.VMEM((m, D_MODEL), BF16),
                        pltpu.VMEM((m, D_MODEL), F32)],
        compiler_params=_cparams(1),
        name="mix_mlp_stream",
    )(a, x, mod_mix.arr, mod_mlp.arr, mod_mlp.arr, mod_mlp.arr, w_out, w1_f32, w2_f32,
      ln_g, ln_b, ln_g, ln_b)


def _alibi_slope(head):
    return 2.0 ** (-8.0 * (head + 1) / N_Q_HEADS)


def _softmax_sink(s, sink):
    m = jnp.maximum(jnp.max(s, axis=-1, keepdims=True), sink)
    e = jnp.exp(s - m)
    den = jnp.sum(e, axis=-1, keepdims=True) + jnp.exp(sink - m)
    return e / den


def _band_bias_init(bias_ref):
    blk = ATTN_BLOCK
    c = lax.broadcasted_iota(jnp.int32, (2 * blk, blk), 0)
    r = lax.broadcasted_iota(jnp.int32, (2 * blk, blk), 1)
    dist = blk + r - c
    valid = (dist >= 0) & (dist <= WINDOW)
    distf = dist.astype(F32)
    for head in range(N_Q_HEADS):
        pen = -_alibi_slope(head) * distf
        bias_ref[0, head] = jnp.where(valid, pen, NEG_BIG)
        bias_ref[1, head] = jnp.where(valid & (c >= blk), pen, NEG_BIG)


def _band_scores(q, kk):
    blk = ATTN_BLOCK
    head_of_lane = lax.broadcasted_iota(jnp.int32, (blk, NKV), 1) // HEAD_DIM
    scores = []
    for b in range(q.shape[0] // blk):
        keys = kk[b * blk:(b + 2) * blk]
        for g in range(GROUP):
            qg = q[b * blk:(b + 1) * blk, g * NKV:(g + 1) * NKV]
            qm = jnp.concatenate(
                [jnp.where(head_of_lane == j, qg, jnp.zeros_like(qg))
                 for j in range(N_KV_HEADS)], axis=0)
            scores.append(lax.dot_general(keys, qm, (((1,), (1,)), ((), ())),
                                          preferred_element_type=F32))
    return scores


def _band_outputs(scores, vvt, first_tile, sinks_ref, bias_ref, o_ref):
    blk = ATTN_BLOCK
    for b in range(len(scores) // GROUP):
        table = 1 if first_tile and b == 0 else 0
        vals_t = vvt[:, b * blk:(b + 2) * blk]
        for g in range(GROUP):
            st_all = scores[b * GROUP + g]
            ps = []
            for j in range(N_KV_HEADS):
                head = j * GROUP + g
                sink = sinks_ref[head]
                st = st_all[:, j * blk:(j + 1) * blk] + bias_ref[table, head]
                m = jnp.maximum(jnp.max(st, axis=0, keepdims=True), sink)
                e = jnp.exp(st - m)
                den = jnp.sum(e, axis=0, keepdims=True) + jnp.exp(sink - m)
                ps.append((e * (1.0 / den)).astype(BF16))
            ot_all = jnp.dot(vals_t, jnp.concatenate(ps, axis=1),
                             preferred_element_type=F32)
            ot = jnp.concatenate(
                [ot_all[j * HEAD_DIM:(j + 1) * HEAD_DIM, j * blk:(j + 1) * blk]
                 for j in range(N_KV_HEADS)], axis=0)
            o_ref[b * blk:(b + 1) * blk, g * NKV:(g + 1) * NKV] = ot.T.astype(BF16)


def _mlp_chunks(h, w1_ref, w2_ref, acc_ref, chunks):
    for c in chunks:
        cols = slice(c * FF_CHUNK, (c + 1) * FF_CHUNK)
        a = jnp.dot(h, w1_ref[:, cols], preferred_element_type=F32)
        a = jnp.square(jnp.maximum(a, 0.0)).astype(BF16)
        d = jnp.dot(a, w2_ref[cols, :], preferred_element_type=F32)
        if c == 0:
            acc_ref[...] = d
        else:
            acc_ref[...] += d


def _attn_layer_kernel(sinks_ref, xc_ref, xp_ref, sh0_ref, sc0_ref, gt0_ref, sh1_ref, sc1_ref,
                       gt1_ref, win_ref, wo_ref, w1_ref, w2_ref, g0_ref, b0_ref, g1_ref, b1_ref,
                       y_ref, kl_ref, vl_ref, o_s, kprev_s, vtprev_s, acc_ref, bias_ref):
    i = pl.program_id(0)
    last = pl.num_programs(0) - 1
    slot = i % 2
    blk = ATTN_BLOCK
    tm = xc_ref.shape[0]
    dot = functools.partial(jnp.dot, preferred_element_type=F32)
    n_chunks = D_FF // FF_CHUNK

    def mlp_in():
        x1 = _res_ln(xp_ref[...], gt0_ref[0:1, :], dot(o_s[1 - slot], wo_ref[...]),
                     g0_ref[...], b0_ref[...])
        return x1, _modulate(x1, sh1_ref[0:1, :], sc1_ref[0:1, :]).astype(BF16)

    def mlp_out(x1):
        y_ref[...] = _res_ln(x1, gt1_ref[0:1, :], acc_ref[...], g1_ref[...], b1_ref[...])

    def mix_in():
        h_in = _modulate(xc_ref[...], sh0_ref[0:1, :], sc0_ref[0:1, :]).astype(BF16)
        q = (dot(h_in, win_ref[:, 0:NQ]) * (HEAD_DIM ** -0.5)).astype(BF16)
        k = dot(h_in, win_ref[:, NQ:NQ + NKV])
        v = dot(h_in, win_ref[:, NQ + NKV:NQ + 2 * NKV])
        kl_ref[...] = k[tm - blk:, :]
        vl_ref[...] = v[tm - blk:, :]
        k_bf = k.astype(BF16)
        vt = v.T.astype(BF16)
        kk = jnp.concatenate([kprev_s[...], k_bf], axis=0)
        vvt = jnp.concatenate([vtprev_s[...], vt], axis=1)
        scores = _band_scores(q, kk)
        kprev_s[...] = k_bf[tm - blk:, :]
        vtprev_s[...] = vt[:, tm - blk:]
        return scores, vvt

    @pl.when(i == 0)
    def _():
        _band_bias_init(bias_ref)
        kprev_s[...] = jnp.zeros_like(kprev_s)
        vtprev_s[...] = jnp.zeros_like(vtprev_s)
        scores, vvt = mix_in()
        _band_outputs(scores, vvt, True, sinks_ref, bias_ref, o_s.at[slot])

    @pl.when((i > 0) & (i < last))
    def _():
        x1, h_mlp = mlp_in()
        scores, vvt = mix_in()
        _mlp_chunks(h_mlp, w1_ref, w2_ref, acc_ref, range(n_chunks))
        _band_outputs(scores, vvt, False, sinks_ref, bias_ref, o_s.at[slot])
        mlp_out(x1)

    @pl.when(i == last)
    def _():
        x1, h_mlp = mlp_in()
        _mlp_chunks(h_mlp, w1_ref, w2_ref, acc_ref, range(n_chunks))
        mlp_out(x1)


def _attn_layer(x, mod_mix, mod_mlp, w_in, w_out, w1, w2, ln_g, ln_b, sinks, layer):
    m = x.shape[0]
    tm = ATTN_TILE
    blk = ATTN_BLOCK
    n = m // tm
    assert not mod_mix.per_row and not mod_mlp.per_row
    cur = pl.BlockSpec((tm, D_MODEL), lambda i: (jnp.minimum(i, n - 1), 0))
    prev = pl.BlockSpec((tm, D_MODEL), lambda i: (jnp.maximum(i - 1, 0), 0))
    last = pl.BlockSpec((blk, NKV), lambda i: (0, 0))
    return pl.pallas_call(
        _attn_layer_kernel,
        grid=(n + 1,),
        in_specs=[pl.BlockSpec(memory_space=pltpu.SMEM), cur, prev,
                  mod_mix.spec(tm, 0), mod_mix.spec(tm, 1), mod_mix.spec(tm, 2),
                  mod_mlp.spec(tm, 0), mod_mlp.spec(tm, 1), mod_mlp.spec(tm, 2),
                  _full(w_in.shape), _full(w_out.shape), _full(w1.shape), _full(w2.shape),
                  _layer(ln_g, 2 * layer), _layer(ln_b, 2 * layer),
                  _layer(ln_g, 2 * layer + 1), _layer(ln_b, 2 * layer + 1)],
        out_specs=[prev, last, last],
        out_shape=[jax.ShapeDtypeStruct((m, D_MODEL), F32),
                   jax.ShapeDtypeStruct((blk, NKV), F32),
                   jax.ShapeDtypeStruct((blk, NKV), F32)],
        scratch_shapes=[pltpu.VMEM((2, tm, NQ), BF16),
                        pltpu.VMEM((blk, NKV), BF16),
                        pltpu.VMEM((NKV, blk), BF16),
                        pltpu.VMEM((tm, D_MODEL), F32),
                        pltpu.VMEM((2, N_Q_HEADS, 2 * blk, blk), F32)],
        compiler_params=_cparams(1),
        name="attn_layer",
    )(sinks, x, x, mod_mix.arr, mod_mix.arr, mod_mix.arr, mod_mlp.arr, mod_mlp.arr,
      mod_mlp.arr, w_in, w_out, w1, w2, ln_g, ln_b, ln_g, ln_b)


def _dec_attn_kernel(sinks_ref, q_ref, kn_ref, vn_ref, ck_ref, cv_ref,
                     o_ref, nk_ref, nv_ref, zk_ref, zv_ref):
    n_seq = ck_ref.shape[0]
    n_new = kn_ref.shape[0] // n_seq
    win = ck_ref.shape[2]
    rows = n_new * GROUP
    n_rows = N_KV_HEADS * rows
    keep = win - n_new
    row = lax.broadcasted_iota(jnp.int32, (n_rows, 2 * win), 0)
    col = lax.broadcasted_iota(jnp.int32, (n_rows, 2 * win), 1)
    j_r = row // rows
    t_r = (row // GROUP) % n_new
    g_r = row % GROUP
    h_r = j_r * GROUP + g_r
    slope = jnp.exp2(-8.0 * (h_r + 1).astype(F32) / N_Q_HEADS)
    sink = jnp.zeros((n_rows, 1), F32)
    h_col = h_r[:, 0:1]
    for h in range(N_Q_HEADS):
        sink = jnp.where(h_col == h, sinks_ref[h], sink)
    is_key = (col < win) | (col >= win + keep)
    frame = jnp.where(col < win, col, col - keep)
    dist = t_r + win - frame
    valid = is_key & (dist >= 0) & (dist <= WINDOW)
    bias = jnp.where(valid, -slope * dist.astype(F32), NEG_BIG)
    head_of_lane = lax.broadcasted_iota(jnp.int32, (rows, NKV), 1) // HEAD_DIM
    lane_w = lax.broadcasted_iota(jnp.int32, (NKV, win), 1)

    @pl.when(pl.program_id(0) == 0)
    def _():
        zk_ref[...] = jnp.zeros_like(zk_ref)
        zv_ref[...] = jnp.zeros_like(zv_ref)

    for s in range(n_seq):
        u = s
        tok = slice(s * n_new, (s + 1) * n_new)
        k_t = ck_ref[s]
        v_t = cv_ref[s]
        zk_ref[u, keep:win, :] = kn_ref[tok, :]
        zv_ref[u, keep:win, :] = vn_ref[tok, :]
        zk_t = zk_ref[u].T
        zv_t = zv_ref[u].T
        nk_ref[s] = jnp.where(lane_w < keep, pltpu.roll(k_t, keep, 1), zk_t)
        nv_ref[s] = jnp.where(lane_w < keep, pltpu.roll(v_t, keep, 1), zv_t)
        keys = jnp.concatenate([k_t, zk_t], axis=1).astype(BF16)
        vals = jnp.concatenate([v_t, zv_t], axis=1).astype(BF16)
        qs = q_ref[s]
        qbd = jnp.concatenate(
            [jnp.where(head_of_lane == j, qs, jnp.zeros_like(qs)) for j in range(N_KV_HEADS)],
            axis=0)
        sc = jnp.dot(qbd, keys, preferred_element_type=F32) + bias
        p = _softmax_sink(sc, sink).astype(BF16)
        pv = lax.dot_general(p, vals, (((1,), (1,)), ((), ())), preferred_element_type=F32)
        o = jnp.zeros((rows, NKV), F32)
        for j in range(N_KV_HEADS):
            o = o + jnp.where(head_of_lane == j, pv[j * rows:(j + 1) * rows], 0.0)
        o_ref[s] = o.astype(BF16)


def _dec_attn(q, k_new, v_new, cache_kt, cache_vt, sinks):
    nseq, win = cache_kt.shape[0], cache_kt.shape[2]
    n_new = k_new.shape[0] // nseq
    assert win == LANES and n_new < SUBLANES
    sb = SEQ_BLOCK
    blk = lambda a: pl.BlockSpec((sb,) + a.shape[1:], lambda i: (i, 0, 0))
    rows = lambda a: pl.BlockSpec((sb * n_new, a.shape[1]), lambda i: (i, 0))
    zshape = (sb, win, NKV)
    return pl.pallas_call(
        _dec_attn_kernel,
        grid=(nseq // sb,),
        in_specs=[pl.BlockSpec(memory_space=pltpu.SMEM), blk(q), rows(k_new), rows(v_new),
                  blk(cache_kt), blk(cache_vt)],
        out_specs=[blk(q), blk(cache_kt), blk(cache_vt)],
        out_shape=[
            jax.ShapeDtypeStruct(q.shape, BF16),
            jax.ShapeDtypeStruct(cache_kt.shape, F32),
            jax.ShapeDtypeStruct(cache_vt.shape, F32),
        ],
        scratch_shapes=[pltpu.VMEM(zshape, F32), pltpu.VMEM(zshape, F32)],
        compiler_params=_cparams(1),
        name="dec_attn",
    )(sinks, q, k_new, v_new, cache_kt, cache_vt)


def _split2(x):
    hi = x.astype(BF16)
    lo = (x - hi.astype(F32)).astype(BF16)
    return hi, lo


def _gla_out(o, r, norm_g):
    ms = jnp.mean(o * o, axis=-1, keepdims=True)
    o = o * lax.rsqrt(ms + LN_EPS) * norm_g
    return o * (r * jax.nn.sigmoid(r))


def _gla_prompt_kernel(q_ref, k_ref, lg_ref, v_ref, r_ref, ng_ref, o_ref, s_out_ref, s_ref):
    i = pl.program_id(0)
    sub = GLA_SUB
    n_sub = q_ref.shape[0] // sub
    ch = GLA_CHUNK
    n_ch = sub // ch
    dot = functools.partial(jnp.dot, preferred_element_type=F32)
    heads = range(GLA_HEADS)
    ks = [slice(h * GLA_DK, (h + 1) * GLA_DK) for h in heads]
    vs = [slice(h * GLA_DV, (h + 1) * GLA_DV) for h in heads]

    @pl.when(i == 0)
    def _():
        s_ref[...] = jnp.zeros_like(s_ref)

    row = lax.broadcasted_iota(jnp.int32, (sub, sub), 0)
    col = lax.broadcasted_iota(jnp.int32, (sub, sub), 1)
    causal = ((row // ch) == (col // ch)) & (col <= row)
    tril = jnp.where(causal, 1.0, 0.0).astype(BF16)
    chunk_of_col = lax.broadcasted_iota(jnp.int32, (GLA_DK, sub), 1) // ch

    qd, kd, kdec, dec_rows = [], [], [], []
    for t in range(n_sub):
        rows = slice(t * sub, (t + 1) * sub)
        hi, lo = _split2(lg_ref[rows, :])
        b = dot(tril, hi) + dot(tril, lo)
        ends = [b[(c + 1) * ch - 1:(c + 1) * ch, :] for c in range(n_ch)]
        b_end = jnp.concatenate([jnp.broadcast_to(e, (ch, NK)) for e in ends], axis=0)
        q = q_ref[rows, :]
        k = k_ref[rows, :]
        qd.append((q * jnp.exp(b)).astype(BF16))
        kd.append((k * jnp.exp(-b)).astype(BF16))
        kdec.append(k * jnp.exp(b_end - b))
        dec_rows.append(jnp.concatenate(
            [jnp.exp(e) for e in ends] + [jnp.zeros((LANES - n_ch, NK), F32)], axis=0))

    a = [[lax.dot_general(qd[t][:, ks[h]], kd[t][:, ks[h]], (((1,), (1,)), ((), ())),
                          preferred_element_type=F32) for h in heads] for t in range(n_sub)]

    u = []
    for t in range(n_sub):
        rows = slice(t * sub, (t + 1) * sub)
        u_t = []
        for h in heads:
            kdec_t = kdec[t][:, ks[h]].T.astype(BF16)
            stacked = jnp.concatenate(
                [jnp.where(chunk_of_col == c, kdec_t, jnp.zeros_like(kdec_t))
                 for c in range(n_ch)], axis=0)
            u_t.append(dot(stacked, v_ref[rows, vs[h]]))
        u.append(u_t)

    o_intra = []
    for t in range(n_sub):
        rows = slice(t * sub, (t + 1) * sub)
        o_intra.append([dot(jnp.where(causal, a[t][h], 0.0).astype(BF16), v_ref[rows, vs[h]])
                        for h in heads])

    for h in heads:
        s = s_ref[h]
        for t in range(n_sub):
            rows = slice(t * sub, (t + 1) * sub)
            dec_t = dec_rows[t][:, ks[h]].T
            o_inter = []
            for c in range(n_ch):
                o_inter.append(dot(qd[t][c * ch:(c + 1) * ch, ks[h]], s.astype(BF16)))
                s = dec_t[:, c:c + 1] * s + u[t][h][c * GLA_DK:(c + 1) * GLA_DK]
            o = o_intra[t][h] + jnp.concatenate(o_inter, axis=0)
            o_ref[rows, vs[h]] = _gla_out(
                o, r_ref[rows, vs[h]].astype(F32), ng_ref[...]).astype(o_ref.dtype)
        s_ref[h] = s

    s_out_ref[...] = s_ref[...]


def _gla_prompt(q, k, lg, v, r, norm_g):
    m = q.shape[0]
    tt = GLA_TILE
    row = functools.partial(_row_spec, tt)
    state = (GLA_HEADS, GLA_DK, GLA_DV)
    return pl.pallas_call(
        _gla_prompt_kernel,
        grid=(m // tt,),
        in_specs=[row(NK), row(NK), row(NK), row(NV), row(NV), _full(norm_g.shape)],
        out_specs=[row(NV), _full(state)],
        out_shape=[jax.ShapeDtypeStruct((m, NV), BF16), jax.ShapeDtypeStruct(state, F32)],
        scratch_shapes=[pltpu.VMEM(state, F32)],
        compiler_params=_cparams(1),
        name="gla_prompt",
    )(q, k, lg, v, r, norm_g)


def _gla_step_kernel(q_ref, k_ref, lg_ref, v_ref, r_ref, ng_ref, s0_ref,
                     o_ref, s1_ref, z_ref, v_all_ref, q_pad_ref):
    n_seq = s0_ref.shape[0]
    n_new = q_ref.shape[0] // n_seq
    tok = [slice(s * n_new, (s + 1) * n_new) for s in range(n_seq)]
    grp = SUBLANES
    dot = functools.partial(jnp.dot, preferred_element_type=F32)
    rowi = lax.broadcasted_iota(jnp.int32, (n_new, NK), 0)
    rowv = lax.broadcasted_iota(jnp.int32, (n_new, GLA_DV), 0)
    heads = range(GLA_HEADS)
    ks = [slice(h * GLA_DK, (h + 1) * GLA_DK) for h in heads]
    vs = [slice(h * GLA_DV, (h + 1) * GLA_DV) for h in heads]

    @pl.when(pl.program_id(0) == 0)
    def _():
        z_ref[...] = jnp.zeros_like(z_ref)
        v_all_ref[...] = jnp.zeros_like(v_all_ref)
        q_pad_ref[...] = jnp.zeros_like(q_pad_ref)

    o_intra = []
    for s in range(n_seq):
        lg = lg_ref[tok[s], :]
        b = jnp.zeros_like(lg)
        for t in range(n_new):
            b = b + jnp.where(rowi >= t, jnp.broadcast_to(lg[t:t + 1, :], lg.shape), 0.0)
        b_end = b[n_new - 1:n_new, :]
        qd = q_ref[tok[s], :] * jnp.exp(b)
        k = k_ref[tok[s], :]
        kd = k * jnp.exp(-b)
        kdec = k * jnp.exp(b_end - b)
        dec = jnp.exp(b_end)
        v = v_ref[tok[s], :]
        row0 = s * grp
        o_s = []
        for h in heads:
            v_h = v[:, vs[h]]
            qd_h = qd[:, ks[h]]
            o = jnp.zeros((n_new, GLA_DV), F32)
            for t in range(n_new):
                a_t = jnp.sum(qd_h * kd[t:t + 1, ks[h]], axis=-1, keepdims=True)
                o = o + jnp.where(rowv >= t, a_t * v_h[t:t + 1, :], 0.0)
            o_s.append(o)
            q_pad_ref[s, h, 0:n_new, :] = qd_h
            z_ref[h, row0:row0 + n_new, :] = kdec[:, ks[h]]
            z_ref[h, row0 + n_new:row0 + n_new + 1, :] = dec[:, ks[h]]
            v_all_ref[h, row0:row0 + n_new, :] = v_h
        o_intra.append(o_s)

    group_of_lane = lax.broadcasted_iota(jnp.int32, (GLA_DK, LANES), 1) // grp
    for h in heads:
        z_t = z_ref[h].T
        v_all = v_all_ref[h]
        for s in range(n_seq):
            s0 = s0_ref[s, h]
            mine = jnp.where(group_of_lane == s, z_t, 0.0)
            col = s * grp + n_new
            s1_ref[s, h] = z_t[:, col:col + 1] * s0 + dot(mine, v_all)
            o = o_intra[s][h] + dot(q_pad_ref[s, h], s0)[0:n_new]
            o_ref[tok[s], vs[h]] = _gla_out(o, r_ref[tok[s], vs[h]], ng_ref[...])


def _gla_step(q, k, lg, v, r, norm_g, s0):
    nseq = s0.shape[0]
    n_new = q.shape[0] // nseq
    sb = SEQ_BLOCK
    assert n_new <= GLA_CHUNK and n_new < SUBLANES and sb * SUBLANES <= LANES
    rows = lambda a: pl.BlockSpec((sb * n_new, a.shape[1]), lambda i: (i, 0))
    blk4 = lambda a: pl.BlockSpec((sb,) + a.shape[1:], lambda i: (i, 0, 0, 0))
    return pl.pallas_call(
        _gla_step_kernel,
        grid=(nseq // sb,),
        in_specs=[rows(q), rows(k), rows(lg), rows(v), rows(r), _full(norm_g.shape), blk4(s0)],
        out_specs=[rows(v), blk4(s0)],
        out_shape=[jax.ShapeDtypeStruct(v.shape, F32), jax.ShapeDtypeStruct(s0.shape, F32)],
        scratch_shapes=[pltpu.VMEM((GLA_HEADS, LANES, GLA_DK), F32),
                        pltpu.VMEM((GLA_HEADS, LANES, GLA_DV), F32),
                        pltpu.VMEM((sb, GLA_HEADS, SUBLANES, GLA_DK), F32)],
        compiler_params=_cparams(1),
        name="gla_step",
    )(q, k, lg, v, r, norm_g, s0)


def kernel(x_prompt, x_sample, cache_k, cache_v, state_gla, c_prompt, c_sample, w_mod, b_mod,
           ln_g, ln_b, attn_w_in, attn_w_out, attn_sinks, gla_w_in, gla_w_gate_up, gla_b_gate,
           gla_norm_g, gla_w_out, mlp_w1, mlp_w2):
    assert x_prompt.shape[0] == 1 and w_mod.shape[0] == DEPTH == 2
    seq = x_prompt.shape[1]
    nseq, n_new = x_sample.shape[0], x_sample.shape[1]
    win = cache_k.shape[2]
    m_s = nseq * n_new

    wq = attn_w_in[0][:, :NQ].reshape(D_MODEL, N_KV_HEADS, GROUP, HEAD_DIM)
    wq = wq.transpose(0, 2, 1, 3).reshape(D_MODEL, NQ)
    w_attn_in = jnp.concatenate([wq, attn_w_in[0][:, NQ:]], axis=1).astype(BF16)
    w_attn_out = attn_w_out[0].reshape(N_KV_HEADS, GROUP, HEAD_DIM, D_MODEL)
    w_attn_out = w_attn_out.transpose(1, 0, 2, 3).reshape(NQ, D_MODEL).astype(BF16)
    w_gla_t = gla_w_in[0].T.astype(BF16)
    assert w_gla_t.shape[0] == 2 * NK + 2 * NV + GLA_GATE_RANK
    w_gla_gu = jnp.pad(gla_w_gate_up[0], ((LANES - GLA_GATE_RANK, 0), (0, 0))).astype(BF16)
    w_gla_out = gla_w_out[0].astype(BF16)
    b_gate = gla_b_gate[0].reshape(1, NK)
    norm_g = gla_norm_g[0].reshape(1, GLA_DV)
    ln_g4 = ln_g.reshape(2 * DEPTH, 1, D_MODEL)
    ln_b4 = ln_b.reshape(2 * DEPTH, 1, D_MODEL)
    sinks = attn_sinks[0]

    pad_rows = (-(m_s + 1)) % SUBLANES
    c_all = jnp.concatenate([jnp.repeat(c_sample, n_new, axis=0), c_prompt,
                             jnp.zeros((pad_rows, D_MODEL), F32)], axis=0)
    mod_all = _adaln_all(c_all, w_mod.reshape(2 * DEPTH, D_MODEL, 3 * D_MODEL),
                         b_mod.reshape(2 * DEPTH, 1, 3 * D_MODEL))
    mods_p = [_Mod(mod_all, p, per_row=False, row0=m_s) for p in range(2 * DEPTH)]
    mods_s = [_Mod(mod_all, p, per_row=True) for p in range(2 * DEPTH)]

    assert seq >= WINDOW == ATTN_BLOCK
    win_p = WINDOW
    to_slab = lambda c: c[0].transpose(0, 2, 3, 1).reshape(nseq, NKV, win)
    from_slab = lambda c: c.reshape(nseq, N_KV_HEADS, HEAD_DIM, win).transpose(0, 3, 1, 2)[None]

    x_s = x_sample.reshape(m_s, D_MODEL)
    q, k, v = _attn_proj(x_s, mods_s[0], w_attn_in)
    o, k_s, v_s = _dec_attn(q.reshape(nseq, n_new * GROUP, NKV), k, v,
                            to_slab(cache_k), to_slab(cache_v), sinks)
    x1_s, w1_0, w2_0 = _mix_mlp_stream(o.reshape(m_s, NQ), x_s, mods_s[0], mods_s[1], w_attn_out,
                                       mlp_w1, mlp_w2, ln_g4, ln_b4, 0)
    x1_p, k_p, v_p = _attn_layer(x_prompt[0], mods_p[0], mods_p[1], w_attn_in, w_attn_out,
                                 w1_0, w2_0, ln_g4, ln_b4, sinks, 0)

    gla_w = (w_gla_t, w_gla_gu, b_gate)
    q, k, lg, v, r = _gla_proj(x1_s, mods_s[2], *gla_w, F32)
    o, s_s = _gla_step(q, k, lg, v, r, norm_g, state_gla[0])
    y_s, w1_1, w2_1 = _mix_mlp_stream(o, x1_s, mods_s[2], mods_s[3], w_gla_out,
                                      mlp_w1, mlp_w2, ln_g4, ln_b4, 1)
    q, k, lg, v, r = _gla_proj(x1_p, mods_p[2], *gla_w, BF16)
    o, s_p = _gla_prompt(q, k, lg, v, r, norm_g)
    y_p = _mix_mlp(o, x1_p, mods_p[2], mods_p[3], w_gla_out, w1_1, w2_1, ln_g4, ln_b4, 1)
    k_s, v_s = from_slab(k_s), from_slab(v_s)

    kv_shape_p = (1, 1, win_p, N_KV_HEADS, HEAD_DIM)
    return (y_p[None], y_s.reshape(nseq, n_new, D_MODEL),
            k_p.reshape(kv_shape_p), v_p.reshape(kv_shape_p), s_p[None, None],
            k_s, v_s, s_s[None])
```

```python
import functools

import jax
import jax.numpy as jnp
from jax import lax
from jax.experimental import pallas as pl
from jax.experimental.pallas import tpu as pltpu

F32 = jnp.float32
BF16 = jnp.bfloat16

D_MODEL = 1024
DEPTH = 2
HEAD_DIM = 64
N_Q_HEADS = 16
N_KV_HEADS = 4
GROUP = 4
WINDOW = 128
ATTN_BLOCK = 128
GLA_HEADS = 4
GLA_DK = 128
GLA_DV = 256
GLA_GATE_RANK = 16
GLA_TAU = 16.0
GLA_CHUNK = 64
D_FF = 4 * D_MODEL
ALPHA = (2.0 * DEPTH) ** 0.25
LN_EPS = 1e-5

NQ = N_Q_HEADS * HEAD_DIM
NKV = N_KV_HEADS * HEAD_DIM
NK = GLA_HEADS * GLA_DK
NV = GLA_HEADS * GLA_DV
LANES = 128
SUBLANES = 8
NEG_BIG = -1e30

ROW_TILE = 512
FF_CHUNK = 1024
ATTN_TILE = 512
GLA_TILE = 256
SEQ_BLOCK = 8
VMEM_LIMIT = 56 * 1024 * 1024


def _cparams(n_axes):
    return pltpu.CompilerParams(
        dimension_semantics=("arbitrary",) * n_axes,
        vmem_limit_bytes=VMEM_LIMIT,
    )


def _full(shape):
    zeros = (0,) * len(shape)
    return pl.BlockSpec(shape, lambda *_: zeros)


def _layer(arr, idx):
    tail = (0,) * (arr.ndim - 1)
    return pl.BlockSpec((None,) + arr.shape[1:], lambda *_: (idx,) + tail,
                        pipeline_mode=pl.Buffered(1))


def _row_spec(tm, n):
    return pl.BlockSpec((tm, n), lambda i: (i, 0))


class _Mod:
    def __init__(self, arr, p, per_row, row0=0):
        self.arr, self.p, self.per_row, self.row0 = arr, p, per_row, row0

    def spec(self, tm, col):
        p = self.p
        if self.per_row:
            return pl.BlockSpec((None, tm, D_MODEL), lambda i: (p, i, col))
        blk = self.row0 // SUBLANES
        return pl.BlockSpec((None, SUBLANES, D_MODEL), lambda i: (p, blk, col))


def _mod_rows(ref, tm):
    return ref[...] if ref.shape[0] == tm else ref[0:1, :]


def _modulate(x, shift, scale):
    return x * (1.0 + scale) + shift


def _res_ln(x, gate, o, g, b):
    y = ALPHA * x + gate * o
    mu = jnp.mean(y, axis=-1, keepdims=True)
    yc = y - mu
    var = jnp.mean(yc * yc, axis=-1, keepdims=True)
    return yc * lax.rsqrt(var + LN_EPS) * g + b


def _mod_kernel(c_ref, w_ref, b_ref, o_ref):
    c = c_ref[...]
    a = (c * jax.nn.sigmoid(c)).astype(BF16)
    o_ref[...] = jnp.dot(a, w_ref[...].astype(BF16), preferred_element_type=F32) + b_ref[...]


def _adaln_all(c_all, w_mod, b_mod):
    rows = c_all.shape[0]
    tn = 1024
    return pl.pallas_call(
        _mod_kernel,
        grid=(4, 3 * D_MODEL // tn),
        in_specs=[
            pl.BlockSpec((rows, D_MODEL), lambda p, n: (0, 0)),
            pl.BlockSpec((None, D_MODEL, tn), lambda p, n: (p, 0, n)),
            pl.BlockSpec((None, 1, tn), lambda p, n: (p, 0, n)),
        ],
        out_specs=pl.BlockSpec((None, rows, tn), lambda p, n: (p, 0, n)),
        out_shape=jax.ShapeDtypeStruct((4, rows, 3 * D_MODEL), F32),
        compiler_params=_cparams(2),
        name="adaln_mod",
    )(c_all, w_mod, b_mod)


def _attn_proj_kernel(x_ref, sh_ref, sc_ref, w_ref, q_ref, k_ref, v_ref):
    tm = x_ref.shape[0]
    h = _modulate(x_ref[...], _mod_rows(sh_ref, tm), _mod_rows(sc_ref, tm)).astype(BF16)
    q = jnp.dot(h, w_ref[:, 0:NQ], preferred_element_type=F32)
    q_ref[...] = (q * (HEAD_DIM ** -0.5)).astype(BF16)
    k_ref[...] = jnp.dot(h, w_ref[:, NQ:NQ + NKV], preferred_element_type=F32)
    v_ref[...] = jnp.dot(h, w_ref[:, NQ + NKV:NQ + 2 * NKV], preferred_element_type=F32)


def _attn_proj(x, mod, w_in):
    m = x.shape[0]
    tm = min(ROW_TILE, m)
    row = functools.partial(_row_spec, tm)
    return pl.pallas_call(
        _attn_proj_kernel,
        grid=(m // tm,),
        in_specs=[row(D_MODEL), mod.spec(tm, 0), mod.spec(tm, 1), _full(w_in.shape)],
        out_specs=[row(NQ), row(NKV), row(NKV)],
        out_shape=[
            jax.ShapeDtypeStruct((m, NQ), BF16),
            jax.ShapeDtypeStruct((m, NKV), F32),
            jax.ShapeDtypeStruct((m, NKV), F32),
        ],
        compiler_params=_cparams(1),
        name="attn_proj",
    )(x, mod.arr, mod.arr, w_in)


def _gla_project(h, wt_ref, wgu_ref, bg_ref, vr_dtype):
    def proj(lo, hi):
        return lax.dot_general(h, wt_ref[lo:hi, :], (((1,), (1,)), ((), ())),
                               preferred_element_type=F32)

    n_all = wt_ref.shape[0]
    gdown = proj(n_all - LANES, n_all)
    q = proj(0, NK) * (GLA_DK ** -0.5)
    k = proj(NK, 2 * NK)
    pre = jnp.dot(gdown.astype(BF16), wgu_ref[...], preferred_element_type=F32) + bg_ref[...]
    v = proj(2 * NK, 2 * NK + NV).astype(vr_dtype)
    r = proj(2 * NK + NV, 2 * NK + 2 * NV).astype(vr_dtype)
    log_sig = jnp.minimum(pre, 0.0) - jnp.log1p(jnp.exp(-jnp.abs(pre)))
    return q, k, log_sig / GLA_TAU, v, r


def _gla_proj_kernel(x_ref, sh_ref, sc_ref, wt_ref, wgu_ref, bg_ref,
                     q_ref, k_ref, lg_ref, v_ref, r_ref):
    tm = x_ref.shape[0]
    h = _modulate(x_ref[...], _mod_rows(sh_ref, tm), _mod_rows(sc_ref, tm)).astype(BF16)
    q_ref[...], k_ref[...], lg_ref[...], v_ref[...], r_ref[...] = _gla_project(
        h, wt_ref, wgu_ref, bg_ref, v_ref.dtype)


def _gla_proj(x, mod, w_t, w_gu, b_gate, vr_dtype):
    m = x.shape[0]
    tm = min(ROW_TILE, m)
    row = functools.partial(_row_spec, tm)
    return pl.pallas_call(
        _gla_proj_kernel,
        grid=(m // tm,),
        in_specs=[row(D_MODEL), mod.spec(tm, 0), mod.spec(tm, 1), _full(w_t.shape),
                  _full(w_gu.shape), _full(b_gate.shape)],
        out_specs=[row(NK), row(NK), row(NK), row(NV), row(NV)],
        out_shape=[
            jax.ShapeDtypeStruct((m, NK), F32),
            jax.ShapeDtypeStruct((m, NK), F32),
            jax.ShapeDtypeStruct((m, NK), F32),
            jax.ShapeDtypeStruct((m, NV), vr_dtype),
            jax.ShapeDtypeStruct((m, NV), vr_dtype),
        ],
        compiler_params=_cparams(1),
        name="gla_proj",
    )(x, mod.arr, mod.arr, w_t, w_gu, b_gate)


def _mix_mlp_kernel(a_ref, x_ref, gt0_ref, sh_ref, sc_ref, gt1_ref, wo_ref, w1_ref, w2_ref,
                    g0_ref, b0_ref, g1_ref, b1_ref, o_ref, acc_ref):
    tm = x_ref.shape[0]
    o = jnp.dot(a_ref[...].astype(BF16), wo_ref[...], preferred_element_type=F32)
    x1 = _res_ln(x_ref[...], _mod_rows(gt0_ref, tm), o, g0_ref[...], b0_ref[...])
    h = _modulate(x1, _mod_rows(sh_ref, tm), _mod_rows(sc_ref, tm)).astype(BF16)
    _mlp_chunks(h, w1_ref, w2_ref, acc_ref, range(D_FF // FF_CHUNK))
    o_ref[...] = _res_ln(x1, _mod_rows(gt1_ref, tm), acc_ref[...], g1_ref[...], b1_ref[...])


def _mix_mlp(a, x, mod_mix, mod_mlp, w_out, w1, w2, ln_g, ln_b, layer):
    m = x.shape[0]
    tm = min(ROW_TILE, m)
    row = functools.partial(_row_spec, tm)
    return pl.pallas_call(
        _mix_mlp_kernel,
        grid=(m // tm,),
        in_specs=[row(a.shape[1]), row(D_MODEL), mod_mix.spec(tm, 2), mod_mlp.spec(tm, 0),
                  mod_mlp.spec(tm, 1), mod_mlp.spec(tm, 2), _full(w_out.shape),
                  _full(w1.shape), _full(w2.shape),
                  _layer(ln_g, 2 * layer), _layer(ln_b, 2 * layer),
                  _layer(ln_g, 2 * layer + 1), _layer(ln_b, 2 * layer + 1)],
        out_specs=row(D_MODEL),
        out_shape=jax.ShapeDtypeStruct((m, D_MODEL), F32),
        scratch_shapes=[pltpu.VMEM((tm, D_MODEL), F32)],
        compiler_params=_cparams(1),
        name="mix_mlp",
    )(a, x, mod_mix.arr, mod_mlp.arr, mod_mlp.arr, mod_mlp.arr, w_out, w1, w2,
      ln_g, ln_b, ln_g, ln_b)


def _mix_mlp_stream_kernel(a_ref, x_ref, gt0_ref, sh_ref, sc_ref, gt1_ref, wo_ref, w1_ref, w2_ref,
                           g0_ref, b0_ref, g1_ref, b1_ref, y_ref, w1b_ref, w2b_ref,
                           x1_s, h_s, acc_ref):
    c = pl.program_id(0)

    @pl.when(c == 0)
    def _():
        o = jnp.dot(a_ref[...].astype(BF16), wo_ref[...], preferred_element_type=F32)
        x1 = _res_ln(x_ref[...], gt0_ref[...], o, g0_ref[...], b0_ref[...])
        x1_s[...] = x1
        h_s[...] = _modulate(x1, sh_ref[...], sc_ref[...]).astype(BF16)
        acc_ref[...] = jnp.zeros_like(acc_ref)

    w1c = w1_ref[...].astype(BF16)
    w2c = w2_ref[...].astype(BF16)
    w1b_ref[...] = w1c
    w2b_ref[...] = w2c
    a = jnp.dot(h_s[...], w1c, preferred_element_type=F32)
    a = jnp.square(jnp.maximum(a, 0.0)).astype(BF16)
    acc_ref[...] += jnp.dot(a, w2c, preferred_element_type=F32)

    @pl.when(c == pl.num_programs(0) - 1)
    def _():
        y_ref[...] = _res_ln(x1_s[...], gt1_ref[...], acc_ref[...], g1_ref[...], b1_ref[...])


def _mix_mlp_stream(a, x, mod_mix, mod_mlp, w_out, w1_f32, w2_f32, ln_g, ln_b, layer):
    m = x.shape[0]
    assert m <= ROW_TILE and mod_mix.per_row and mod_mlp.per_row
    fc = FF_CHUNK
    whole = lambda n: pl.BlockSpec((m, n), lambda c: (0, 0))
    mspec = lambda mod, col: pl.BlockSpec((None, m, D_MODEL), lambda c: (mod.p, 0, col))
    return pl.pallas_call(
        _mix_mlp_stream_kernel,
        grid=(D_FF // fc,),
        in_specs=[whole(a.shape[1]), whole(D_MODEL), mspec(mod_mix, 2), mspec(mod_mlp, 0),
                  mspec(mod_mlp, 1), mspec(mod_mlp, 2), _full(w_out.shape),
                  pl.BlockSpec((None, D_MODEL, fc), lambda c: (layer, 0, c)),
                  pl.BlockSpec((None, fc, D_MODEL), lambda c: (layer, c, 0)),
                  _layer(ln_g, 2 * layer), _layer(ln_b, 2 * layer),
                  _layer(ln_g, 2 * layer + 1), _layer(ln_b, 2 * layer + 1)],
        out_specs=[whole(D_MODEL),
                   pl.BlockSpec((D_MODEL, fc), lambda c: (0, c)),
                   pl.BlockSpec((fc, D_MODEL), lambda c: (c, 0))],
        out_shape=[jax.ShapeDtypeStruct((m, D_MODEL), F32),
                   jax.ShapeDtypeStruct((D_MODEL, D_FF), BF16),
                   jax.ShapeDtypeStruct((D_FF, D_MODEL), BF16)],
        scratch_shapes=[pltpu.VMEM((m, D_MODEL), F32), pltpu.VMEM((m, D_MODEL), BF16),
                        pltpu.VMEM((m, D_MODEL), F32)],
        compiler_params=_cparams(1),
        name="mix_mlp_stream",
    )(a, x, mod_mix.arr, mod_mlp.arr, mod_mlp.arr, mod_mlp.arr, w_out, w1_f32, w2_f32,
      ln_g, ln_b, ln_g, ln_b)


def _alibi_slope(head):
    return 2.0 ** (-8.0 * (head + 1) / N_Q_HEADS)


def _softmax_sink(s, sink):
    m = jnp.maximum(jnp.max(s, axis=-1, keepdims=True), sink)
    e = jnp.exp(s - m)
    den = jnp.sum(e, axis=-1, keepdims=True) + jnp.exp(sink - m)
    return e / den


def _band_bias_init(bias_ref):
    blk = ATTN_BLOCK
    c = lax.broadcasted_iota(jnp.int32, (2 * blk, blk), 0)
    r = lax.broadcasted_iota(jnp.int32, (2 * blk, blk), 1)
    dist = blk + r - c
    valid = (dist >= 0) & (dist <= WINDOW)
    distf = dist.astype(F32)
    for head in range(N_Q_HEADS):
        pen = -_alibi_slope(head) * distf
        bias_ref[0, head] = jnp.where(valid, pen, NEG_BIG)
        bias_ref[1, head] = jnp.where(valid & (c >= blk), pen, NEG_BIG)


def _band_scores(q, kk):
    blk = ATTN_BLOCK
    head_of_lane = lax.broadcasted_iota(jnp.int32, (blk, NKV), 1) // HEAD_DIM
    scores = []
    for b in range(q.shape[0] // blk):
        keys = kk[b * blk:(b + 2) * blk]
        for g in range(GROUP):
            qg = q[b * blk:(b + 1) * blk, g * NKV:(g + 1) * NKV]
            qm = jnp.concatenate(
                [jnp.where(head_of_lane == j, qg, jnp.zeros_like(qg))
                 for j in range(N_KV_HEADS)], axis=0)
            scores.append(lax.dot_general(keys, qm, (((1,), (1,)), ((), ())),
                                          preferred_element_type=F32))
    return scores


def _band_outputs(scores, vvt, first_tile, sinks_ref, bias_ref, o_ref):
    blk = ATTN_BLOCK
    for b in range(len(scores) // GROUP):
        table = 1 if first_tile and b == 0 else 0
        vals_t = vvt[:, b * blk:(b + 2) * blk]
        for g in range(GROUP):
            st_all = scores[b * GROUP + g]
            ps = []
            for j in range(N_KV_HEADS):
                head = j * GROUP + g
                sink = sinks_ref[head]
                st = st_all[:, j * blk:(j + 1) * blk] + bias_ref[table, head]
                m = jnp.maximum(jnp.max(st, axis=0, keepdims=True), sink)
                e = jnp.exp(st - m)
                den = jnp.sum(e, axis=0, keepdims=True) + jnp.exp(sink - m)
                ps.append((e * (1.0 / den)).astype(BF16))
            ot_all = jnp.dot(vals_t, jnp.concatenate(ps, axis=1),
                             preferred_element_type=F32)
            ot = jnp.concatenate(
                [ot_all[j * HEAD_DIM:(j + 1) * HEAD_DIM, j * blk:(j + 1) * blk]
                 for j in range(N_KV_HEADS)], axis=0)
            o_ref[b * blk:(b + 1) * blk, g * NKV:(g + 1) * NKV] = ot.T.astype(BF16)


def _mlp_chunks(h, w1_ref, w2_ref, acc_ref, chunks):
    for c in chunks:
        cols = slice(c * FF_CHUNK, (c + 1) * FF_CHUNK)
        a = jnp.dot(h, w1_ref[:, cols], preferred_element_type=F32)
        a = jnp.square(jnp.maximum(a, 0.0)).astype(BF16)
        d = jnp.dot(a, w2_ref[cols, :], preferred_element_type=F32)
        if c == 0:
            acc_ref[...] = d
        else:
            acc_ref[...] += d


def _attn_layer_kernel(sinks_ref, xc_ref, xp_ref, sh0_ref, sc0_ref, gt0_ref, sh1_ref, sc1_ref,
                       gt1_ref, win_ref, wo_ref, w1_ref, w2_ref, g0_ref, b0_ref, g1_ref, b1_ref,
                       y_ref, kl_ref, vl_ref, o_s, kprev_s, vtprev_s, acc_ref, bias_ref):
    i = pl.program_id(0)
    last = pl.num_programs(0) - 1
    slot = i % 2
    blk = ATTN_BLOCK
    tm = xc_ref.shape[0]
    dot = functools.partial(jnp.dot, preferred_element_type=F32)
    n_chunks = D_FF // FF_CHUNK

    def mlp_in():
        x1 = _res_ln(xp_ref[...], gt0_ref[0:1, :], dot(o_s[1 - slot], wo_ref[...]),
                     g0_ref[...], b0_ref[...])
        return x1, _modulate(x1, sh1_ref[0:1, :], sc1_ref[0:1, :]).astype(BF16)

    def mlp_out(x1):
        y_ref[...] = _res_ln(x1, gt1_ref[0:1, :], acc_ref[...], g1_ref[...], b1_ref[...])

    def mix_in():
        h_in = _modulate(xc_ref[...], sh0_ref[0:1, :], sc0_ref[0:1, :]).astype(BF16)
        q = (dot(h_in, win_ref[:, 0:NQ]) * (HEAD_DIM ** -0.5)).astype(BF16)
        k = dot(h_in, win_ref[:, NQ:NQ + NKV])
        v = dot(h_in, win_ref[:, NQ + NKV:NQ + 2 * NKV])
        kl_ref[...] = k[tm - blk:, :]
        vl_ref[...] = v[tm - blk:, :]
        k_bf = k.astype(BF16)
        vt = v.T.astype(BF16)
        kk = jnp.concatenate([kprev_s[...], k_bf], axis=0)
        vvt = jnp.concatenate([vtprev_s[...], vt], axis=1)
        scores = _band_scores(q, kk)
        kprev_s[...] = k_bf[tm - blk:, :]
        vtprev_s[...] = vt[:, tm - blk:]
        return scores, vvt

    @pl.when(i == 0)
    def _():
        _band_bias_init(bias_ref)
        kprev_s[...] = jnp.zeros_like(kprev_s)
        vtprev_s[...] = jnp.zeros_like(vtprev_s)
        scores, vvt = mix_in()
        _band_outputs(scores, vvt, True, sinks_ref, bias_ref, o_s.at[slot])

    @pl.when((i > 0) & (i < last))
    def _():
        x1, h_mlp = mlp_in()
        scores, vvt = mix_in()
        _mlp_chunks(h_mlp, w1_ref, w2_ref, acc_ref, range(n_chunks))
        _band_outputs(scores, vvt, False, sinks_ref, bias_ref, o_s.at[slot])
        mlp_out(x1)

    @pl.when(i == last)
    def _():
        x1, h_mlp = mlp_in()
        _mlp_chunks(h_mlp, w1_ref, w2_ref, acc_ref, range(n_chunks))
        mlp_out(x1)


def _attn_layer(x, mod_mix, mod_mlp, w_in, w_out, w1, w2, ln_g, ln_b, sinks, layer):
    m = x.shape[0]
    tm = ATTN_TILE
    blk = ATTN_BLOCK
    n = m // tm
    assert not mod_mix.per_row and not mod_mlp.per_row
    cur = pl.BlockSpec((tm, D_MODEL), lambda i: (jnp.minimum(i, n - 1), 0))
    prev = pl.BlockSpec((tm, D_MODEL), lambda i: (jnp.maximum(i - 1, 0), 0))
    last = pl.BlockSpec((blk, NKV), lambda i: (0, 0))
    return pl.pallas_call(
        _attn_layer_kernel,
        grid=(n + 1,),
        in_specs=[pl.BlockSpec(memory_space=pltpu.SMEM), cur, prev,
                  mod_mix.spec(tm, 0), mod_mix.spec(tm, 1), mod_mix.spec(tm, 2),
                  mod_mlp.spec(tm, 0), mod_mlp.spec(tm, 1), mod_mlp.spec(tm, 2),
                  _full(w_in.shape), _full(w_out.shape), _full(w1.shape), _full(w2.shape),
                  _layer(ln_g, 2 * layer), _layer(ln_b, 2 * layer),
                  _layer(ln_g, 2 * layer + 1), _layer(ln_b, 2 * layer + 1)],
        out_specs=[prev, last, last],
        out_shape=[jax.ShapeDtypeStruct((m, D_MODEL), F32),
                   jax.ShapeDtypeStruct((blk, NKV), F32),
                   jax.ShapeDtypeStruct((blk, NKV), F32)],
        scratch_shapes=[pltpu.VMEM((2, tm, NQ), BF16),
                        pltpu.VMEM((blk, NKV), BF16),
                        pltpu.VMEM((NKV, blk), BF16),
                        pltpu.VMEM((tm, D_MODEL), F32),
                        pltpu.VMEM((2, N_Q_HEADS, 2 * blk, blk), F32)],
        compiler_params=_cparams(1),
        name="attn_layer",
    )(sinks, x, x, mod_mix.arr, mod_mix.arr, mod_mix.arr, mod_mlp.arr, mod_mlp.arr,
      mod_mlp.arr, w_in, w_out, w1, w2, ln_g, ln_b, ln_g, ln_b)


def _dec_attn_kernel(sinks_ref, q_ref, kn_ref, vn_ref, ck_ref, cv_ref,
                     o_ref, nk_ref, nv_ref, zk_ref, zv_ref):
    n_seq = ck_ref.shape[0]
    n_new = kn_ref.shape[0] // n_seq
    win = ck_ref.shape[2]
    rows = n_new * GROUP
    n_rows = N_KV_HEADS * rows
    keep = win - n_new
    row = lax.broadcasted_iota(jnp.int32, (n_rows, 2 * win), 0)
    col = lax.broadcasted_iota(jnp.int32, (n_rows, 2 * win), 1)
    j_r = row // rows
    t_r = (row // GROUP) % n_new
    g_r = row % GROUP
    h_r = j_r * GROUP + g_r
    slope = jnp.exp2(-8.0 * (h_r + 1).astype(F32) / N_Q_HEADS)
    sink = jnp.zeros((n_rows, 1), F32)
    h_col = h_r[:, 0:1]
    for h in range(N_Q_HEADS):
        sink = jnp.where(h_col == h, sinks_ref[h], sink)
    is_key = (col < win) | (col >= win + keep)
    frame = jnp.where(col < win, col, col - keep)
    dist = t_r + win - frame
    valid = is_key & (dist >= 0) & (dist <= WINDOW)
    bias = jnp.where(valid, -slope * dist.astype(F32), NEG_BIG)
    head_of_lane = lax.broadcasted_iota(jnp.int32, (rows, NKV), 1) // HEAD_DIM
    lane_w = lax.broadcasted_iota(jnp.int32, (NKV, win), 1)

    @pl.when(pl.program_id(0) == 0)
    def _():
        zk_ref[...] = jnp.zeros_like(zk_ref)
        zv_ref[...] = jnp.zeros_like(zv_ref)

    for s in range(n_seq):
        u = s
        tok = slice(s * n_new, (s + 1) * n_new)
        k_t = ck_ref[s]
        v_t = cv_ref[s]
        zk_ref[u, keep:win, :] = kn_ref[tok, :]
        zv_ref[u, keep:win, :] = vn_ref[tok, :]
        zk_t = zk_ref[u].T
        zv_t = zv_ref[u].T
        nk_ref[s] = jnp.where(lane_w < keep, pltpu.roll(k_t, keep, 1), zk_t)
        nv_ref[s] = jnp.where(lane_w < keep, pltpu.roll(v_t, keep, 1), zv_t)
        keys = jnp.concatenate([k_t, zk_t], axis=1).astype(BF16)
        vals = jnp.concatenate([v_t, zv_t], axis=1).astype(BF16)
        qs = q_ref[s]
        qbd = jnp.concatenate(
            [jnp.where(head_of_lane == j, qs, jnp.zeros_like(qs)) for j in range(N_KV_HEADS)],
            axis=0)
        sc = jnp.dot(qbd, keys, preferred_element_type=F32) + bias
        p = _softmax_sink(sc, sink).astype(BF16)
        pv = lax.dot_general(p, vals, (((1,), (1,)), ((), ())), preferred_element_type=F32)
        o = jnp.zeros((rows, NKV), F32)
        for j in range(N_KV_HEADS):
            o = o + jnp.where(head_of_lane == j, pv[j * rows:(j + 1) * rows], 0.0)
        o_ref[s] = o.astype(BF16)


def _dec_attn(q, k_new, v_new, cache_kt, cache_vt, sinks):
    nseq, win = cache_kt.shape[0], cache_kt.shape[2]
    n_new = k_new.shape[0] // nseq
    assert win == LANES and n_new < SUBLANES
    sb = SEQ_BLOCK
    blk = lambda a: pl.BlockSpec((sb,) + a.shape[1:], lambda i: (i, 0, 0))
    rows = lambda a: pl.BlockSpec((sb * n_new, a.shape[1]), lambda i: (i, 0))
    zshape = (sb, win, NKV)
    return pl.pallas_call(
        _dec_attn_kernel,
        grid=(nseq // sb,),
        in_specs=[pl.BlockSpec(memory_space=pltpu.SMEM), blk(q), rows(k_new), rows(v_new),
                  blk(cache_kt), blk(cache_vt)],
        out_specs=[blk(q), blk(cache_kt), blk(cache_vt)],
        out_shape=[
            jax.ShapeDtypeStruct(q.shape, BF16),
            jax.ShapeDtypeStruct(cache_kt.shape, F32),
            jax.ShapeDtypeStruct(cache_vt.shape, F32),
        ],
        scratch_shapes=[pltpu.VMEM(zshape, F32), pltpu.VMEM(zshape, F32)],
        compiler_params=_cparams(1),
        name="dec_attn",
    )(sinks, q, k_new, v_new, cache_kt, cache_vt)


def _split2(x):
    hi = x.astype(BF16)
    lo = (x - hi.astype(F32)).astype(BF16)
    return hi, lo


def _gla_out(o, r, norm_g):
    ms = jnp.mean(o * o, axis=-1, keepdims=True)
    o = o * lax.rsqrt(ms + LN_EPS) * norm_g
    return o * (r * jax.nn.sigmoid(r))


def _gla_layer_kernel(xc_ref, xp_ref, sh0_ref, sc0_ref, gt0_ref, sh1_ref, sc1_ref, gt1_ref,
                      wt_ref, wgu_ref, bg_ref, ng_ref, wo_ref, w1_ref, w2_ref,
                      g0_ref, b0_ref, g1_ref, b1_ref, y_ref, s_out_ref, o_s, s_ref, acc_ref):
    i = pl.program_id(0)
    last = pl.num_programs(0) - 1
    sub = xc_ref.shape[0]
    ch = GLA_CHUNK
    n_ch = sub // ch
    n_chunks = D_FF // FF_CHUNK
    dot = functools.partial(jnp.dot, preferred_element_type=F32)
    heads = range(GLA_HEADS)
    ks = [slice(h * GLA_DK, (h + 1) * GLA_DK) for h in heads]
    vs = [slice(h * GLA_DV, (h + 1) * GLA_DV) for h in heads]

    def mlp_in():
        x1 = _res_ln(xp_ref[...], gt0_ref[0:1, :], dot(o_s[...], wo_ref[...]),
                     g0_ref[...], b0_ref[...])
        return x1, _modulate(x1, sh1_ref[0:1, :], sc1_ref[0:1, :]).astype(BF16)

    def mlp_out(x1):
        y_ref[...] = _res_ln(x1, gt1_ref[0:1, :], acc_ref[...], g1_ref[...], b1_ref[...])

    def decays():
        h_in = _modulate(xc_ref[...], sh0_ref[0:1, :], sc0_ref[0:1, :]).astype(BF16)
        q, k, lg, v, r = _gla_project(h_in, wt_ref, wgu_ref, bg_ref, BF16)
        row = lax.broadcasted_iota(jnp.int32, (sub, sub), 0)
        col = lax.broadcasted_iota(jnp.int32, (sub, sub), 1)
        causal = ((row // ch) == (col // ch)) & (col <= row)
        tril = jnp.where(causal, 1.0, 0.0).astype(BF16)
        hi, lo = _split2(lg)
        b = dot(tril, hi) + dot(tril, lo)
        ends = [b[(c + 1) * ch - 1:(c + 1) * ch, :] for c in range(n_ch)]
        b_end = jnp.concatenate([jnp.broadcast_to(e, (ch, NK)) for e in ends], axis=0)
        qd = (q * jnp.exp(b)).astype(BF16)
        kd = (k * jnp.exp(-b)).astype(BF16)
        kdec = k * jnp.exp(b_end - b)
        dec_rows = jnp.concatenate(
            [jnp.exp(e) for e in ends] + [jnp.zeros((LANES - n_ch, NK), F32)], axis=0)
        return causal, qd, kd, kdec, dec_rows, v, r

    def chunk_products(causal, qd, kd, kdec, v):
        chunk_of_col = lax.broadcasted_iota(jnp.int32, (GLA_DK, sub), 1) // ch
        a = [lax.dot_general(qd[:, ks[h]], kd[:, ks[h]], (((1,), (1,)), ((), ())),
                             preferred_element_type=F32) for h in heads]
        u = []
        for h in heads:
            kdec_t = kdec[:, ks[h]].T.astype(BF16)
            stacked = jnp.concatenate(
                [jnp.where(chunk_of_col == c, kdec_t, jnp.zeros_like(kdec_t))
                 for c in range(n_ch)], axis=0)
            u.append(dot(stacked, v[:, vs[h]]))
        o_intra = [dot(jnp.where(causal, a[h], 0.0).astype(BF16), v[:, vs[h]]) for h in heads]
        return u, o_intra

    def recurrence(qd, dec_rows, u, o_intra, r):
        for h in heads:
            s = s_ref[h]
            dec_t = dec_rows[:, ks[h]].T
            o_inter = []
            for c in range(n_ch):
                o_inter.append(dot(qd[c * ch:(c + 1) * ch, ks[h]], s.astype(BF16)))
                s = dec_t[:, c:c + 1] * s + u[h][c * GLA_DK:(c + 1) * GLA_DK]
            s_ref[h] = s
            o = o_intra[h] + jnp.concatenate(o_inter, axis=0)
            o_s[:, vs[h]] = _gla_out(o, r[:, vs[h]].astype(F32), ng_ref[...]).astype(BF16)

    @pl.when(i == 0)
    def _():
        s_ref[...] = jnp.zeros_like(s_ref)
        causal, qd, kd, kdec, dec_rows, v, r = decays()
        u, o_intra = chunk_products(causal, qd, kd, kdec, v)
        recurrence(qd, dec_rows, u, o_intra, r)

    @pl.when((i > 0) & (i < last))
    def _():
        x1, h_mlp = mlp_in()
        causal, qd, kd, kdec, dec_rows, v, r = decays()
        _mlp_chunks(h_mlp, w1_ref, w2_ref, acc_ref, range(0, 1))
        u, o_intra = chunk_products(causal, qd, kd, kdec, v)
        _mlp_chunks(h_mlp, w1_ref, w2_ref, acc_ref, range(1, n_chunks))
        recurrence(qd, dec_rows, u, o_intra, r)
        mlp_out(x1)

    @pl.when(i == last)
    def _():
        x1, h_mlp = mlp_in()
        _mlp_chunks(h_mlp, w1_ref, w2_ref, acc_ref, range(n_chunks))
        mlp_out(x1)
        s_out_ref[...] = s_ref[...]


def _gla_layer(x, mod_mix, mod_mlp, w_t, w_gu, b_gate, norm_g, w_out, w1, w2, ln_g, ln_b, layer):
    m = x.shape[0]
    tm = GLA_TILE
    n = m // tm
    assert not mod_mix.per_row and not mod_mlp.per_row and tm % GLA_CHUNK == 0
    cur = pl.BlockSpec((tm, D_MODEL), lambda i: (jnp.minimum(i, n - 1), 0))
    prev = pl.BlockSpec((tm, D_MODEL), lambda i: (jnp.maximum(i - 1, 0), 0))
    state = (GLA_HEADS, GLA_DK, GLA_DV)
    return pl.pallas_call(
        _gla_layer_kernel,
        grid=(n + 1,),
        in_specs=[cur, prev,
                  mod_mix.spec(tm, 0), mod_mix.spec(tm, 1), mod_mix.spec(tm, 2),
                  mod_mlp.spec(tm, 0), mod_mlp.spec(tm, 1), mod_mlp.spec(tm, 2),
                  _full(w_t.shape), _full(w_gu.shape), _full(b_gate.shape), _full(norm_g.shape),
                  _full(w_out.shape), _full(w1.shape), _full(w2.shape),
                  _layer(ln_g, 2 * layer), _layer(ln_b, 2 * layer),
                  _layer(ln_g, 2 * layer + 1), _layer(ln_b, 2 * layer + 1)],
        out_specs=[prev, _full(state)],
        out_shape=[jax.ShapeDtypeStruct((m, D_MODEL), F32), jax.ShapeDtypeStruct(state, F32)],
        scratch_shapes=[pltpu.VMEM((tm, NV), BF16),
                        pltpu.VMEM(state, F32),
                        pltpu.VMEM((tm, D_MODEL), F32)],
        compiler_params=_cparams(1),
        name="gla_layer",
    )(x, x, mod_mix.arr, mod_mix.arr, mod_mix.arr, mod_mlp.arr, mod_mlp.arr, mod_mlp.arr,
      w_t, w_gu, b_gate, norm_g, w_out, w1, w2, ln_g, ln_b, ln_g, ln_b)


def _gla_step_kernel(q_ref, k_ref, lg_ref, v_ref, r_ref, ng_ref, s0_ref,
                     o_ref, s1_ref, z_ref, v_all_ref, q_pad_ref):
    n_seq = s0_ref.shape[0]
    n_new = q_ref.shape[0] // n_seq
    tok = [slice(s * n_new, (s + 1) * n_new) for s in range(n_seq)]
    grp = SUBLANES
    dot = functools.partial(jnp.dot, preferred_element_type=F32)
    rowi = lax.broadcasted_iota(jnp.int32, (n_new, NK), 0)
    rowv = lax.broadcasted_iota(jnp.int32, (n_new, GLA_DV), 0)
    heads = range(GLA_HEADS)
    ks = [slice(h * GLA_DK, (h + 1) * GLA_DK) for h in heads]
    vs = [slice(h * GLA_DV, (h + 1) * GLA_DV) for h in heads]

    @pl.when(pl.program_id(0) == 0)
    def _():
        z_ref[...] = jnp.zeros_like(z_ref)
        v_all_ref[...] = jnp.zeros_like(v_all_ref)
        q_pad_ref[...] = jnp.zeros_like(q_pad_ref)

    o_intra = []
    for s in range(n_seq):
        lg = lg_ref[tok[s], :]
        b = jnp.zeros_like(lg)
        for t in range(n_new):
            b = b + jnp.where(rowi >= t, jnp.broadcast_to(lg[t:t + 1, :], lg.shape), 0.0)
        b_end = b[n_new - 1:n_new, :]
        qd = q_ref[tok[s], :] * jnp.exp(b)
        k = k_ref[tok[s], :]
        kd = k * jnp.exp(-b)
        kdec = k * jnp.exp(b_end - b)
        dec = jnp.exp(b_end)
        v = v_ref[tok[s], :]
        row0 = s * grp
        o_s = []
        for h in heads:
            v_h = v[:, vs[h]]
            qd_h = qd[:, ks[h]]
            o = jnp.zeros((n_new, GLA_DV), F32)
            for t in range(n_new):
                a_t = jnp.sum(qd_h * kd[t:t + 1, ks[h]], axis=-1, keepdims=True)
                o = o + jnp.where(rowv >= t, a_t * v_h[t:t + 1, :], 0.0)
            o_s.append(o)
            q_pad_ref[s, h, 0:n_new, :] = qd_h
            z_ref[h, row0:row0 + n_new, :] = kdec[:, ks[h]]
            z_ref[h, row0 + n_new:row0 + n_new + 1, :] = dec[:, ks[h]]
            v_all_ref[h, row0:row0 + n_new, :] = v_h
        o_intra.append(o_s)

    group_of_lane = lax.broadcasted_iota(jnp.int32, (GLA_DK, LANES), 1) // grp
    for h in heads:
        z_t = z_ref[h].T
        v_all = v_all_ref[h]
        for s in range(n_seq):
            s0 = s0_ref[s, h]
            mine = jnp.where(group_of_lane == s, z_t, 0.0)
            col = s * grp + n_new
            s1_ref[s, h] = z_t[:, col:col + 1] * s0 + dot(mine, v_all)
            o = o_intra[s][h] + dot(q_pad_ref[s, h], s0)[0:n_new]
            o_ref[tok[s], vs[h]] = _gla_out(o, r_ref[tok[s], vs[h]], ng_ref[...])


def _gla_step(q, k, lg, v, r, norm_g, s0):
    nseq = s0.shape[0]
    n_new = q.shape[0] // nseq
    sb = SEQ_BLOCK
    assert n_new <= GLA_CHUNK and n_new < SUBLANES and sb * SUBLANES <= LANES
    rows = lambda a: pl.BlockSpec((sb * n_new, a.shape[1]), lambda i: (i, 0))
    blk4 = lambda a: pl.BlockSpec((sb,) + a.shape[1:], lambda i: (i, 0, 0, 0))
    return pl.pallas_call(
        _gla_step_kernel,
        grid=(nseq // sb,),
        in_specs=[rows(q), rows(k), rows(lg), rows(v), rows(r), _full(norm_g.shape), blk4(s0)],
        out_specs=[rows(v), blk4(s0)],
        out_shape=[jax.ShapeDtypeStruct(v.shape, F32), jax.ShapeDtypeStruct(s0.shape, F32)],
        scratch_shapes=[pltpu.VMEM((GLA_HEADS, LANES, GLA_DK), F32),
                        pltpu.VMEM((GLA_HEADS, LANES, GLA_DV), F32),
                        pltpu.VMEM((sb, GLA_HEADS, SUBLANES, GLA_DK), F32)],
        compiler_params=_cparams(1),
        name="gla_step",
    )(q, k, lg, v, r, norm_g, s0)


def kernel(x_prompt, x_sample, cache_k, cache_v, state_gla, c_prompt, c_sample, w_mod, b_mod,
           ln_g, ln_b, attn_w_in, attn_w_out, attn_sinks, gla_w_in, gla_w_gate_up, gla_b_gate,
           gla_norm_g, gla_w_out, mlp_w1, mlp_w2):
    assert x_prompt.shape[0] == 1 and w_mod.shape[0] == DEPTH == 2
    seq = x_prompt.shape[1]
    nseq, n_new = x_sample.shape[0], x_sample.shape[1]
    win = cache_k.shape[2]
    m_s = nseq * n_new

    wq = attn_w_in[0][:, :NQ].reshape(D_MODEL, N_KV_HEADS, GROUP, HEAD_DIM)
    wq = wq.transpose(0, 2, 1, 3).reshape(D_MODEL, NQ)
    w_attn_in = jnp.concatenate([wq, attn_w_in[0][:, NQ:]], axis=1).astype(BF16)
    w_attn_out = attn_w_out[0].reshape(N_KV_HEADS, GROUP, HEAD_DIM, D_MODEL)
    w_attn_out = w_attn_out.transpose(1, 0, 2, 3).reshape(NQ, D_MODEL).astype(BF16)
    w_gla_t = gla_w_in[0].T.astype(BF16)
    assert w_gla_t.shape[0] == 2 * NK + 2 * NV + GLA_GATE_RANK
    w_gla_gu = jnp.pad(gla_w_gate_up[0], ((LANES - GLA_GATE_RANK, 0), (0, 0))).astype(BF16)
    w_gla_out = gla_w_out[0].astype(BF16)
    b_gate = gla_b_gate[0].reshape(1, NK)
    norm_g = gla_norm_g[0].reshape(1, GLA_DV)
    ln_g4 = ln_g.reshape(2 * DEPTH, 1, D_MODEL)
    ln_b4 = ln_b.reshape(2 * DEPTH, 1, D_MODEL)
    sinks = attn_sinks[0]

    pad_rows = (-(m_s + 1)) % SUBLANES
    c_all = jnp.concatenate([jnp.repeat(c_sample, n_new, axis=0), c_prompt,
                             jnp.zeros((pad_rows, D_MODEL), F32)], axis=0)
    mod_all = _adaln_all(c_all, w_mod.reshape(2 * DEPTH, D_MODEL, 3 * D_MODEL),
                         b_mod.reshape(2 * DEPTH, 1, 3 * D_MODEL))
    mods_p = [_Mod(mod_all, p, per_row=False, row0=m_s) for p in range(2 * DEPTH)]
    mods_s = [_Mod(mod_all, p, per_row=True) for p in range(2 * DEPTH)]

    assert seq >= WINDOW == ATTN_BLOCK
    win_p = WINDOW
    to_slab = lambda c: c[0].transpose(0, 2, 3, 1).reshape(nseq, NKV, win)
    from_slab = lambda c: c.reshape(nseq, N_KV_HEADS, HEAD_DIM, win).transpose(0, 3, 1, 2)[None]

    x_s = x_sample.reshape(m_s, D_MODEL)
    q, k, v = _attn_proj(x_s, mods_s[0], w_attn_in)
    o, k_s, v_s = _dec_attn(q.reshape(nseq, n_new * GROUP, NKV), k, v,
                            to_slab(cache_k), to_slab(cache_v), sinks)
    x1_s, w1_0, w2_0 = _mix_mlp_stream(o.reshape(m_s, NQ), x_s, mods_s[0], mods_s[1], w_attn_out,
                                       mlp_w1, mlp_w2, ln_g4, ln_b4, 0)
    x1_p, k_p, v_p = _attn_layer(x_prompt[0], mods_p[0], mods_p[1], w_attn_in, w_attn_out,
                                 w1_0, w2_0, ln_g4, ln_b4, sinks, 0)

    gla_w = (w_gla_t, w_gla_gu, b_gate)
    q, k, lg, v, r = _gla_proj(x1_s, mods_s[2], *gla_w, F32)
    o, s_s = _gla_step(q, k, lg, v, r, norm_g, state_gla[0])
    y_s, w1_1, w2_1 = _mix_mlp_stream(o, x1_s, mods_s[2], mods_s[3], w_gla_out,
                                      mlp_w1, mlp_w2, ln_g4, ln_b4, 1)
    y_p, s_p = _gla_layer(x1_p, mods_p[2], mods_p[3], *gla_w, norm_g, w_gla_out, w1_1, w2_1,
                          ln_g4, ln_b4, 1)
    k_s, v_s = from_slab(k_s), from_slab(v_s)

    kv_shape_p = (1, 1, win_p, N_KV_HEADS, HEAD_DIM)
    return (y_p[None], y_s.reshape(nseq, n_new, D_MODEL),
            k_p.reshape(kv_shape_p), v_p.reshape(kv_shape_p), s_p[None, None],
            k_s, v_s, s_s[None])
```

```python
import functools

import jax
import jax.numpy as jnp
from jax import lax
from jax.experimental import pallas as pl
from jax.experimental.pallas import tpu as pltpu

F32 = jnp.float32
BF16 = jnp.bfloat16

D_MODEL = 1024
DEPTH = 2
HEAD_DIM = 64
N_Q_HEADS = 16
N_KV_HEADS = 4
GROUP = 4
WINDOW = 128
ATTN_BLOCK = 128
GLA_HEADS = 4
GLA_DK = 128
GLA_DV = 256
GLA_GATE_RANK = 16
GLA_TAU = 16.0
GLA_CHUNK = 64
D_FF = 4 * D_MODEL
ALPHA = (2.0 * DEPTH) ** 0.25
LN_EPS = 1e-5

NQ = N_Q_HEADS * HEAD_DIM
NKV = N_KV_HEADS * HEAD_DIM
NK = GLA_HEADS * GLA_DK
NV = GLA_HEADS * GLA_DV
LANES = 128
SUBLANES = 8
NEG_BIG = -1e30

ROW_TILE = 512
FF_CHUNK = 1024
ATTN_TILE = 512
GLA_TILE = 256
SEQ_BLOCK = 8
VMEM_LIMIT = 56 * 1024 * 1024


def _cparams(n_axes):
    return pltpu.CompilerParams(
        dimension_semantics=("arbitrary",) * n_axes,
        vmem_limit_bytes=VMEM_LIMIT,
    )


def _full(shape):
    zeros = (0,) * len(shape)
    return pl.BlockSpec(shape, lambda *_: zeros)


def _layer(arr, idx):
    tail = (0,) * (arr.ndim - 1)
    return pl.BlockSpec((None,) + arr.shape[1:], lambda *_: (idx,) + tail,
                        pipeline_mode=pl.Buffered(1))


def _row_spec(tm, n):
    return pl.BlockSpec((tm, n), lambda i: (i, 0))


class _Mod:
    def __init__(self, arr, p, per_row, row0=0):
        self.arr, self.p, self.per_row, self.row0 = arr, p, per_row, row0

    def spec(self, tm, col):
        p = self.p
        if self.per_row:
            return pl.BlockSpec((None, tm, D_MODEL), lambda i: (p, i, col))
        blk = self.row0 // SUBLANES
        return pl.BlockSpec((None, SUBLANES, D_MODEL), lambda i: (p, blk, col))


def _mod_rows(ref, tm):
    return ref[...] if ref.shape[0] == tm else ref[0:1, :]


def _modulate(x, shift, scale):
    return x * (1.0 + scale) + shift


def _res_ln(x, gate, o, g, b):
    y = ALPHA * x + gate * o
    mu = jnp.mean(y, axis=-1, keepdims=True)
    yc = y - mu
    var = jnp.mean(yc * yc, axis=-1, keepdims=True)
    return yc * lax.rsqrt(var + LN_EPS) * g + b


def _mod_kernel(c_ref, w_ref, b_ref, o_ref):
    c = c_ref[...]
    a = (c * jax.nn.sigmoid(c)).astype(BF16)
    o_ref[...] = jnp.dot(a, w_ref[...].astype(BF16), preferred_element_type=F32) + b_ref[...]


def _adaln_all(c_all, w_mod, b_mod):
    rows = c_all.shape[0]
    tn = 1024
    return pl.pallas_call(
        _mod_kernel,
        grid=(4, 3 * D_MODEL // tn),
        in_specs=[
            pl.BlockSpec((rows, D_MODEL), lambda p, n: (0, 0)),
            pl.BlockSpec((None, D_MODEL, tn), lambda p, n: (p, 0, n)),
            pl.BlockSpec((None, 1, tn), lambda p, n: (p, 0, n)),
        ],
        out_specs=pl.BlockSpec((None, rows, tn), lambda p, n: (p, 0, n)),
        out_shape=jax.ShapeDtypeStruct((4, rows, 3 * D_MODEL), F32),
        compiler_params=_cparams(2),
        name="adaln_mod",
    )(c_all, w_mod, b_mod)


def _attn_proj_kernel(x_ref, sh_ref, sc_ref, w_ref, q_ref, k_ref, v_ref):
    tm = x_ref.shape[0]
    h = _modulate(x_ref[...], _mod_rows(sh_ref, tm), _mod_rows(sc_ref, tm)).astype(BF16)
    q = jnp.dot(h, w_ref[:, 0:NQ], preferred_element_type=F32)
    q_ref[...] = (q * (HEAD_DIM ** -0.5)).astype(BF16)
    k_ref[...] = jnp.dot(h, w_ref[:, NQ:NQ + NKV], preferred_element_type=F32)
    v_ref[...] = jnp.dot(h, w_ref[:, NQ + NKV:NQ + 2 * NKV], preferred_element_type=F32)


def _attn_proj(x, mod, w_in):
    m = x.shape[0]
    tm = min(ROW_TILE, m)
    row = functools.partial(_row_spec, tm)
    return pl.pallas_call(
        _attn_proj_kernel,
        grid=(m // tm,),
        in_specs=[row(D_MODEL), mod.spec(tm, 0), mod.spec(tm, 1), _full(w_in.shape)],
        out_specs=[row(NQ), row(NKV), row(NKV)],
        out_shape=[
            jax.ShapeDtypeStruct((m, NQ), BF16),
            jax.ShapeDtypeStruct((m, NKV), F32),
            jax.ShapeDtypeStruct((m, NKV), F32),
        ],
        compiler_params=_cparams(1),
        name="attn_proj",
    )(x, mod.arr, mod.arr, w_in)


def _gla_project(h, wt_ref, wgu_ref, bg_ref, vr_dtype):
    def proj(lo, hi):
        return lax.dot_general(h, wt_ref[lo:hi, :], (((1,), (1,)), ((), ())),
                               preferred_element_type=F32)

    n_all = wt_ref.shape[0]
    gdown = proj(n_all - LANES, n_all)
    q = proj(0, NK) * (GLA_DK ** -0.5)
    k = proj(NK, 2 * NK)
    pre = jnp.dot(gdown.astype(BF16), wgu_ref[...], preferred_element_type=F32) + bg_ref[...]
    v = proj(2 * NK, 2 * NK + NV).astype(vr_dtype)
    r = proj(2 * NK + NV, 2 * NK + 2 * NV).astype(vr_dtype)
    log_sig = jnp.minimum(pre, 0.0) - jnp.log1p(jnp.exp(-jnp.abs(pre)))
    return q, k, log_sig / GLA_TAU, v, r


def _gla_proj_kernel(x_ref, sh_ref, sc_ref, wt_ref, wgu_ref, bg_ref,
                     q_ref, k_ref, lg_ref, v_ref, r_ref):
    tm = x_ref.shape[0]
    h = _modulate(x_ref[...], _mod_rows(sh_ref, tm), _mod_rows(sc_ref, tm)).astype(BF16)
    q_ref[...], k_ref[...], lg_ref[...], v_ref[...], r_ref[...] = _gla_project(
        h, wt_ref, wgu_ref, bg_ref, v_ref.dtype)


def _gla_proj(x, mod, w_t, w_gu, b_gate, vr_dtype):
    m = x.shape[0]
    tm = min(ROW_TILE, m)
    row = functools.partial(_row_spec, tm)
    return pl.pallas_call(
        _gla_proj_kernel,
        grid=(m // tm,),
        in_specs=[row(D_MODEL), mod.spec(tm, 0), mod.spec(tm, 1), _full(w_t.shape),
                  _full(w_gu.shape), _full(b_gate.shape)],
        out_specs=[row(NK), row(NK), row(NK), row(NV), row(NV)],
        out_shape=[
            jax.ShapeDtypeStruct((m, NK), F32),
            jax.ShapeDtypeStruct((m, NK), F32),
            jax.ShapeDtypeStruct((m, NK), F32),
            jax.ShapeDtypeStruct((m, NV), vr_dtype),
            jax.ShapeDtypeStruct((m, NV), vr_dtype),
        ],
        compiler_params=_cparams(1),
        name="gla_proj",
    )(x, mod.arr, mod.arr, w_t, w_gu, b_gate)


def _mix_mlp_stream_kernel(a_ref, x_ref, gt0_ref, sh_ref, sc_ref, gt1_ref, wo_ref, w1_ref, w2_ref,
                           g0_ref, b0_ref, g1_ref, b1_ref, y_ref, *rest):
    if len(rest) == 5:
        w1b_ref, w2b_ref, x1_s, h_s, acc_ref = rest
    else:
        (x1_s, h_s, acc_ref), w1b_ref, w2b_ref = rest, None, None
    c = pl.program_id(0)

    @pl.when(c == 0)
    def _():
        o = jnp.dot(a_ref[...].astype(BF16), wo_ref[...], preferred_element_type=F32)
        x1 = _res_ln(x_ref[...], gt0_ref[...], o, g0_ref[...], b0_ref[...])
        x1_s[...] = x1
        h_s[...] = _modulate(x1, sh_ref[...], sc_ref[...]).astype(BF16)
        acc_ref[...] = jnp.zeros_like(acc_ref)

    w1c = w1_ref[...].astype(BF16)
    w2c = w2_ref[...].astype(BF16)
    if w1b_ref is not None:
        w1b_ref[...] = w1c
        w2b_ref[...] = w2c
    a = jnp.dot(h_s[...], w1c, preferred_element_type=F32)
    a = jnp.square(jnp.maximum(a, 0.0)).astype(BF16)
    acc_ref[...] += jnp.dot(a, w2c, preferred_element_type=F32)

    @pl.when(c == pl.num_programs(0) - 1)
    def _():
        y_ref[...] = _res_ln(x1_s[...], gt1_ref[...], acc_ref[...], g1_ref[...], b1_ref[...])


def _mix_mlp_stream(a, x, mod_mix, mod_mlp, w_out, w1, w2, ln_g, ln_b, layer):
    m = x.shape[0]
    assert m <= ROW_TILE and mod_mix.per_row and mod_mlp.per_row
    fc = FF_CHUNK
    emit = w1.dtype != BF16
    whole = lambda n: pl.BlockSpec((m, n), lambda c: (0, 0))
    mspec = lambda mod, col: pl.BlockSpec((None, m, D_MODEL), lambda c: (mod.p, 0, col))
    w1_cols = pl.BlockSpec((D_MODEL, fc), lambda c: (0, c))
    w2_rows = pl.BlockSpec((fc, D_MODEL), lambda c: (c, 0))
    if emit:
        w_specs = [pl.BlockSpec((None, D_MODEL, fc), lambda c: (layer, 0, c)),
                   pl.BlockSpec((None, fc, D_MODEL), lambda c: (layer, c, 0))]
    else:
        w_specs = [w1_cols, w2_rows]
    out = pl.pallas_call(
        _mix_mlp_stream_kernel,
        grid=(D_FF // fc,),
        in_specs=[whole(a.shape[1]), whole(D_MODEL), mspec(mod_mix, 2), mspec(mod_mlp, 0),
                  mspec(mod_mlp, 1), mspec(mod_mlp, 2), _full(w_out.shape)] + w_specs
                 + [_layer(ln_g, 2 * layer), _layer(ln_b, 2 * layer),
                    _layer(ln_g, 2 * layer + 1), _layer(ln_b, 2 * layer + 1)],
        out_specs=[whole(D_MODEL)] + ([w1_cols, w2_rows] if emit else []),
        out_shape=[jax.ShapeDtypeStruct((m, D_MODEL), F32)]
                  + ([jax.ShapeDtypeStruct((D_MODEL, D_FF), BF16),
                      jax.ShapeDtypeStruct((D_FF, D_MODEL), BF16)] if emit else []),
        scratch_shapes=[pltpu.VMEM((m, D_MODEL), F32), pltpu.VMEM((m, D_MODEL), BF16),
                        pltpu.VMEM((m, D_MODEL), F32)],
        compiler_params=_cparams(1),
        name="mix_mlp_stream",
    )(a, x, mod_mix.arr, mod_mlp.arr, mod_mlp.arr, mod_mlp.arr, w_out, w1, w2,
      ln_g, ln_b, ln_g, ln_b)
    return out if emit else out[0]


def _alibi_slope(head):
    return 2.0 ** (-8.0 * (head + 1) / N_Q_HEADS)


def _softmax_sink(s, sink):
    m = jnp.maximum(jnp.max(s, axis=-1, keepdims=True), sink)
    e = jnp.exp(s - m)
    den = jnp.sum(e, axis=-1, keepdims=True) + jnp.exp(sink - m)
    return e / den


def _band_bias_init(bias_ref):
    blk = ATTN_BLOCK
    c = lax.broadcasted_iota(jnp.int32, (2 * blk, blk), 0)
    r = lax.broadcasted_iota(jnp.int32, (2 * blk, blk), 1)
    dist = blk + r - c
    valid = (dist >= 0) & (dist <= WINDOW)
    distf = dist.astype(F32)
    for head in range(N_Q_HEADS):
        pen = -_alibi_slope(head) * distf
        bias_ref[0, head] = jnp.where(valid, pen, NEG_BIG)
        bias_ref[1, head] = jnp.where(valid & (c >= blk), pen, NEG_BIG)


def _band_scores(q, kk):
    blk = ATTN_BLOCK
    head_of_lane = lax.broadcasted_iota(jnp.int32, (blk, NKV), 1) // HEAD_DIM
    scores = []
    for b in range(q.shape[0] // blk):
        keys = kk[b * blk:(b + 2) * blk]
        for g in range(GROUP):
            qg = q[b * blk:(b + 1) * blk, g * NKV:(g + 1) * NKV]
            qm = jnp.concatenate(
                [jnp.where(head_of_lane == j, qg, jnp.zeros_like(qg))
                 for j in range(N_KV_HEADS)], axis=0)
            scores.append(lax.dot_general(keys, qm, (((1,), (1,)), ((), ())),
                                          preferred_element_type=F32))
    return scores


def _band_outputs(scores, vvt, first_tile, sinks_ref, bias_ref, o_ref):
    blk = ATTN_BLOCK
    for b in range(len(scores) // GROUP):
        table = 1 if first_tile and b == 0 else 0
        vals_t = vvt[:, b * blk:(b + 2) * blk]
        for g in range(GROUP):
            st_all = scores[b * GROUP + g]
            ps = []
            for j in range(N_KV_HEADS):
                head = j * GROUP + g
                sink = sinks_ref[head]
                st = st_all[:, j * blk:(j + 1) * blk] + bias_ref[table, head]
                m = jnp.maximum(jnp.max(st, axis=0, keepdims=True), sink)
                e = jnp.exp(st - m)
                den = jnp.sum(e, axis=0, keepdims=True) + jnp.exp(sink - m)
                ps.append((e * (1.0 / den)).astype(BF16))
            ot_all = jnp.dot(vals_t, jnp.concatenate(ps, axis=1),
                             preferred_element_type=F32)
            ot = jnp.concatenate(
                [ot_all[j * HEAD_DIM:(j + 1) * HEAD_DIM, j * blk:(j + 1) * blk]
                 for j in range(N_KV_HEADS)], axis=0)
            o_ref[b * blk:(b + 1) * blk, g * NKV:(g + 1) * NKV] = ot.T.astype(BF16)


def _mlp_chunks(h, w1_ref, w2_ref, acc_ref, chunks):
    for c in chunks:
        cols = slice(c * FF_CHUNK, (c + 1) * FF_CHUNK)
        a = jnp.dot(h, w1_ref[:, cols], preferred_element_type=F32)
        a = jnp.square(jnp.maximum(a, 0.0)).astype(BF16)
        d = jnp.dot(a, w2_ref[cols, :], preferred_element_type=F32)
        if c == 0:
            acc_ref[...] = d
        else:
            acc_ref[...] += d


def _attn_layer_kernel(sinks_ref, xc_ref, xp_ref, sh0_ref, sc0_ref, gt0_ref, sh1_ref, sc1_ref,
                       gt1_ref, win_ref, wo_ref, w1_ref, w2_ref, g0_ref, b0_ref, g1_ref, b1_ref,
                       y_ref, kl_ref, vl_ref, o_s, kprev_s, vtprev_s, acc_ref, bias_ref):
    i = pl.program_id(0)
    last = pl.num_programs(0) - 1
    slot = i % 2
    blk = ATTN_BLOCK
    tm = xc_ref.shape[0]
    dot = functools.partial(jnp.dot, preferred_element_type=F32)
    n_chunks = D_FF // FF_CHUNK

    def mlp_in():
        x1 = _res_ln(xp_ref[...], gt0_ref[0:1, :], dot(o_s[1 - slot], wo_ref[...]),
                     g0_ref[...], b0_ref[...])
        return x1, _modulate(x1, sh1_ref[0:1, :], sc1_ref[0:1, :]).astype(BF16)

    def mlp_out(x1):
        y_ref[...] = _res_ln(x1, gt1_ref[0:1, :], acc_ref[...], g1_ref[...], b1_ref[...])

    def mix_in():
        h_in = _modulate(xc_ref[...], sh0_ref[0:1, :], sc0_ref[0:1, :]).astype(BF16)
        q = (dot(h_in, win_ref[:, 0:NQ]) * (HEAD_DIM ** -0.5)).astype(BF16)
        k = dot(h_in, win_ref[:, NQ:NQ + NKV])
        v = dot(h_in, win_ref[:, NQ + NKV:NQ + 2 * NKV])
        kl_ref[...] = k[tm - blk:, :]
        vl_ref[...] = v[tm - blk:, :]
        k_bf = k.astype(BF16)
        vt = v.T.astype(BF16)
        kk = jnp.concatenate([kprev_s[...], k_bf], axis=0)
        vvt = jnp.concatenate([vtprev_s[...], vt], axis=1)
        scores = _band_scores(q, kk)
        kprev_s[...] = k_bf[tm - blk:, :]
        vtprev_s[...] = vt[:, tm - blk:]
        return scores, vvt

    @pl.when(i == 0)
    def _():
        _band_bias_init(bias_ref)
        kprev_s[...] = jnp.zeros_like(kprev_s)
        vtprev_s[...] = jnp.zeros_like(vtprev_s)
        scores, vvt = mix_in()
        _band_outputs(scores, vvt, True, sinks_ref, bias_ref, o_s.at[slot])

    @pl.when((i > 0) & (i < last))
    def _():
        x1, h_mlp = mlp_in()
        scores, vvt = mix_in()
        _mlp_chunks(h_mlp, w1_ref, w2_ref, acc_ref, range(n_chunks))
        _band_outputs(scores, vvt, False, sinks_ref, bias_ref, o_s.at[slot])
        mlp_out(x1)

    @pl.when(i == last)
    def _():
        x1, h_mlp = mlp_in()
        _mlp_chunks(h_mlp, w1_ref, w2_ref, acc_ref, range(n_chunks))
        mlp_out(x1)


def _attn_layer(x, mod_mix, mod_mlp, w_in, w_out, w1, w2, ln_g, ln_b, sinks, layer):
    m = x.shape[0]
    tm = ATTN_TILE
    blk = ATTN_BLOCK
    n = m // tm
    assert not mod_mix.per_row and not mod_mlp.per_row
    cur = pl.BlockSpec((tm, D_MODEL), lambda i: (jnp.minimum(i, n - 1), 0))
    prev = pl.BlockSpec((tm, D_MODEL), lambda i: (jnp.maximum(i - 1, 0), 0))
    last = pl.BlockSpec((blk, NKV), lambda i: (0, 0))
    return pl.pallas_call(
        _attn_layer_kernel,
        grid=(n + 1,),
        in_specs=[pl.BlockSpec(memory_space=pltpu.SMEM), cur, prev,
                  mod_mix.spec(tm, 0), mod_mix.spec(tm, 1), mod_mix.spec(tm, 2),
                  mod_mlp.spec(tm, 0), mod_mlp.spec(tm, 1), mod_mlp.spec(tm, 2),
                  _full(w_in.shape), _full(w_out.shape), _full(w1.shape), _full(w2.shape),
                  _layer(ln_g, 2 * layer), _layer(ln_b, 2 * layer),
                  _layer(ln_g, 2 * layer + 1), _layer(ln_b, 2 * layer + 1)],
        out_specs=[prev, last, last],
        out_shape=[jax.ShapeDtypeStruct((m, D_MODEL), F32),
                   jax.ShapeDtypeStruct((blk, NKV), F32),
                   jax.ShapeDtypeStruct((blk, NKV), F32)],
        scratch_shapes=[pltpu.VMEM((2, tm, NQ), BF16),
                        pltpu.VMEM((blk, NKV), BF16),
                        pltpu.VMEM((NKV, blk), BF16),
                        pltpu.VMEM((tm, D_MODEL), F32),
                        pltpu.VMEM((2, N_Q_HEADS, 2 * blk, blk), F32)],
        compiler_params=_cparams(1),
        name="attn_layer",
    )(sinks, x, x, mod_mix.arr, mod_mix.arr, mod_mix.arr, mod_mlp.arr, mod_mlp.arr,
      mod_mlp.arr, w_in, w_out, w1, w2, ln_g, ln_b, ln_g, ln_b)


def _dec_attn_kernel(sinks_ref, q_ref, kn_ref, vn_ref, ck_ref, cv_ref,
                     o_ref, nk_ref, nv_ref, zk_ref, zv_ref):
    n_seq = ck_ref.shape[0]
    n_new = kn_ref.shape[0] // n_seq
    win = ck_ref.shape[2]
    rows = n_new * GROUP
    n_rows = N_KV_HEADS * rows
    keep = win - n_new
    row = lax.broadcasted_iota(jnp.int32, (n_rows, 2 * win), 0)
    col = lax.broadcasted_iota(jnp.int32, (n_rows, 2 * win), 1)
    j_r = row // rows
    t_r = (row // GROUP) % n_new
    g_r = row % GROUP
    h_r = j_r * GROUP + g_r
    slope = jnp.exp2(-8.0 * (h_r + 1).astype(F32) / N_Q_HEADS)
    sink = jnp.zeros((n_rows, 1), F32)
    h_col = h_r[:, 0:1]
    for h in range(N_Q_HEADS):
        sink = jnp.where(h_col == h, sinks_ref[h], sink)
    is_key = (col < win) | (col >= win + keep)
    frame = jnp.where(col < win, col, col - keep)
    dist = t_r + win - frame
    valid = is_key & (dist >= 0) & (dist <= WINDOW)
    bias = jnp.where(valid, -slope * dist.astype(F32), NEG_BIG)
    head_of_lane = lax.broadcasted_iota(jnp.int32, (rows, NKV), 1) // HEAD_DIM
    lane_w = lax.broadcasted_iota(jnp.int32, (NKV, win), 1)

    @pl.when(pl.program_id(0) == 0)
    def _():
        zk_ref[...] = jnp.zeros_like(zk_ref)
        zv_ref[...] = jnp.zeros_like(zv_ref)

    for s in range(n_seq):
        u = s
        tok = slice(s * n_new, (s + 1) * n_new)
        k_t = ck_ref[s]
        v_t = cv_ref[s]
        zk_ref[u, keep:win, :] = kn_ref[tok, :]
        zv_ref[u, keep:win, :] = vn_ref[tok, :]
        zk_t = zk_ref[u].T
        zv_t = zv_ref[u].T
        nk_ref[s] = jnp.where(lane_w < keep, pltpu.roll(k_t, keep, 1), zk_t)
        nv_ref[s] = jnp.where(lane_w < keep, pltpu.roll(v_t, keep, 1), zv_t)
        keys = jnp.concatenate([k_t, zk_t], axis=1).astype(BF16)
        vals = jnp.concatenate([v_t, zv_t], axis=1).astype(BF16)
        qs = q_ref[s]
        qbd = jnp.concatenate(
            [jnp.where(head_of_lane == j, qs, jnp.zeros_like(qs)) for j in range(N_KV_HEADS)],
            axis=0)
        sc = jnp.dot(qbd, keys, preferred_element_type=F32) + bias
        p = _softmax_sink(sc, sink).astype(BF16)
        pv = lax.dot_general(p, vals, (((1,), (1,)), ((), ())), preferred_element_type=F32)
        o = jnp.zeros((rows, NKV), F32)
        for j in range(N_KV_HEADS):
            o = o + jnp.where(head_of_lane == j, pv[j * rows:(j + 1) * rows], 0.0)
        o_ref[s] = o.astype(BF16)


def _dec_attn(q, k_new, v_new, cache_kt, cache_vt, sinks):
    nseq, win = cache_kt.shape[0], cache_kt.shape[2]
    n_new = k_new.shape[0] // nseq
    assert win == LANES and n_new < SUBLANES
    sb = SEQ_BLOCK
    blk = lambda a: pl.BlockSpec((sb,) + a.shape[1:], lambda i: (i, 0, 0))
    rows = lambda a: pl.BlockSpec((sb * n_new, a.shape[1]), lambda i: (i, 0))
    zshape = (sb, win, NKV)
    return pl.pallas_call(
        _dec_attn_kernel,
        grid=(nseq // sb,),
        in_specs=[pl.BlockSpec(memory_space=pltpu.SMEM), blk(q), rows(k_new), rows(v_new),
                  blk(cache_kt), blk(cache_vt)],
        out_specs=[blk(q), blk(cache_kt), blk(cache_vt)],
        out_shape=[
            jax.ShapeDtypeStruct(q.shape, BF16),
            jax.ShapeDtypeStruct(cache_kt.shape, F32),
            jax.ShapeDtypeStruct(cache_vt.shape, F32),
        ],
        scratch_shapes=[pltpu.VMEM(zshape, F32), pltpu.VMEM(zshape, F32)],
        compiler_params=_cparams(1),
        name="dec_attn",
    )(sinks, q, k_new, v_new, cache_kt, cache_vt)


def _split2(x):
    hi = x.astype(BF16)
    lo = (x - hi.astype(F32)).astype(BF16)
    return hi, lo


def _gla_out(o, r, norm_g):
    ms = jnp.mean(o * o, axis=-1, keepdims=True)
    o = o * lax.rsqrt(ms + LN_EPS) * norm_g
    return o * (r * jax.nn.sigmoid(r))


def _gla_layer_kernel(xc_ref, xp_ref, sh0_ref, sc0_ref, gt0_ref, sh1_ref, sc1_ref, gt1_ref,
                      wt_ref, wgu_ref, bg_ref, ng_ref, wo_ref, w1_ref, w2_ref,
                      g0_ref, b0_ref, g1_ref, b1_ref,
                      dq_ref, dk_ref, dlg_ref, dv_ref, dr_ref, ds0_ref,
                      y_ref, s_out_ref, do_ref, ds1_ref,
                      o_s, s_ref, acc_ref, z_ref, v_all_ref, q_pad_ref):
    i = pl.program_id(0)
    last = pl.num_programs(0) - 1
    dec = _GlaStep(dq_ref, dk_ref, dlg_ref, dv_ref, dr_ref, ng_ref, ds0_ref, do_ref, ds1_ref,
                   z_ref, v_all_ref, q_pad_ref)
    sub = xc_ref.shape[0]
    ch = GLA_CHUNK
    n_ch = sub // ch
    n_chunks = D_FF // FF_CHUNK
    dot = functools.partial(jnp.dot, preferred_element_type=F32)
    heads = range(GLA_HEADS)
    ks = [slice(h * GLA_DK, (h + 1) * GLA_DK) for h in heads]
    vs = [slice(h * GLA_DV, (h + 1) * GLA_DV) for h in heads]

    def mlp_in():
        x1 = _res_ln(xp_ref[...], gt0_ref[0:1, :], dot(o_s[...], wo_ref[...]),
                     g0_ref[...], b0_ref[...])
        return x1, _modulate(x1, sh1_ref[0:1, :], sc1_ref[0:1, :]).astype(BF16)

    def mlp_out(x1):
        y_ref[...] = _res_ln(x1, gt1_ref[0:1, :], acc_ref[...], g1_ref[...], b1_ref[...])

    def decays():
        h_in = _modulate(xc_ref[...], sh0_ref[0:1, :], sc0_ref[0:1, :]).astype(BF16)
        q, k, lg, v, r = _gla_project(h_in, wt_ref, wgu_ref, bg_ref, BF16)
        row = lax.broadcasted_iota(jnp.int32, (sub, sub), 0)
        col = lax.broadcasted_iota(jnp.int32, (sub, sub), 1)
        causal = ((row // ch) == (col // ch)) & (col <= row)
        tril = jnp.where(causal, 1.0, 0.0).astype(BF16)
        hi, lo = _split2(lg)
        b = dot(tril, hi) + dot(tril, lo)
        ends = [b[(c + 1) * ch - 1:(c + 1) * ch, :] for c in range(n_ch)]
        b_end = jnp.concatenate([jnp.broadcast_to(e, (ch, NK)) for e in ends], axis=0)
        qd = (q * jnp.exp(b)).astype(BF16)
        kd = (k * jnp.exp(-b)).astype(BF16)
        kdec = k * jnp.exp(b_end - b)
        dec_rows = jnp.concatenate(
            [jnp.exp(e) for e in ends] + [jnp.zeros((LANES - n_ch, NK), F32)], axis=0)
        return causal, qd, kd, kdec, dec_rows, v, r

    def chunk_products(causal, qd, kd, kdec, v):
        chunk_of_col = lax.broadcasted_iota(jnp.int32, (GLA_DK, sub), 1) // ch
        a = [lax.dot_general(qd[:, ks[h]], kd[:, ks[h]], (((1,), (1,)), ((), ())),
                             preferred_element_type=F32) for h in heads]
        u = []
        for h in heads:
            kdec_t = kdec[:, ks[h]].T.astype(BF16)
            stacked = jnp.concatenate(
                [jnp.where(chunk_of_col == c, kdec_t, jnp.zeros_like(kdec_t))
                 for c in range(n_ch)], axis=0)
            u.append(dot(stacked, v[:, vs[h]]))
        o_intra = [dot(jnp.where(causal, a[h], 0.0).astype(BF16), v[:, vs[h]]) for h in heads]
        return u, o_intra

    def recurrence(qd, dec_rows, u, o_intra, r):
        for h in heads:
            s = s_ref[h]
            dec_t = dec_rows[:, ks[h]].T
            o_inter = []
            for c in range(n_ch):
                o_inter.append(dot(qd[c * ch:(c + 1) * ch, ks[h]], s.astype(BF16)))
                s = dec_t[:, c:c + 1] * s + u[h][c * GLA_DK:(c + 1) * GLA_DK]
            s_ref[h] = s
            o = o_intra[h] + jnp.concatenate(o_inter, axis=0)
            o_s[:, vs[h]] = _gla_out(o, r[:, vs[h]].astype(F32), ng_ref[...]).astype(BF16)

    @pl.when(i == 0)
    def _():
        s_ref[...] = jnp.zeros_like(s_ref)
        dec.zero()
        dec_intra = dec.stage()
        causal, qd, kd, kdec, dec_rows, v, r = decays()
        u, o_intra = chunk_products(causal, qd, kd, kdec, v)
        recurrence(qd, dec_rows, u, o_intra, r)
        dec.finish(dec_intra)

    @pl.when((i > 0) & (i < last))
    def _():
        x1, h_mlp = mlp_in()
        dec_intra = dec.stage()
        causal, qd, kd, kdec, dec_rows, v, r = decays()
        _mlp_chunks(h_mlp, w1_ref, w2_ref, acc_ref, range(0, 1))
        u, o_intra = chunk_products(causal, qd, kd, kdec, v)
        _mlp_chunks(h_mlp, w1_ref, w2_ref, acc_ref, range(1, n_chunks))
        recurrence(qd, dec_rows, u, o_intra, r)
        dec.finish(dec_intra)
        mlp_out(x1)

    @pl.when(i == last)
    def _():
        x1, h_mlp = mlp_in()
        _mlp_chunks(h_mlp, w1_ref, w2_ref, acc_ref, range(n_chunks))
        mlp_out(x1)
        s_out_ref[...] = s_ref[...]


def _gla_layer(x, mod_mix, mod_mlp, w_t, w_gu, b_gate, norm_g, w_out, w1, w2, ln_g, ln_b, layer,
               dec_rows, dec_state):
    m = x.shape[0]
    tm = GLA_TILE
    n = m // tm
    assert not mod_mix.per_row and not mod_mlp.per_row and tm % GLA_CHUNK == 0
    nseq = dec_state.shape[0]
    sb = nseq // n
    n_new = dec_rows[0].shape[0] // nseq
    assert sb * n == nseq and (sb * n_new) % SUBLANES == 0 and n_new <= GLA_CHUNK
    cur = pl.BlockSpec((tm, D_MODEL), lambda i: (jnp.minimum(i, n - 1), 0))
    prev = pl.BlockSpec((tm, D_MODEL), lambda i: (jnp.maximum(i - 1, 0), 0))
    drow = lambda a: pl.BlockSpec((sb * n_new, a.shape[1]), lambda i: (jnp.minimum(i, n - 1), 0))
    dstate = pl.BlockSpec((sb,) + dec_state.shape[1:],
                          lambda i: (jnp.minimum(i, n - 1), 0, 0, 0))
    state = (GLA_HEADS, GLA_DK, GLA_DV)
    dv = dec_rows[3]
    return pl.pallas_call(
        _gla_layer_kernel,
        grid=(n + 1,),
        in_specs=[cur, prev,
                  mod_mix.spec(tm, 0), mod_mix.spec(tm, 1), mod_mix.spec(tm, 2),
                  mod_mlp.spec(tm, 0), mod_mlp.spec(tm, 1), mod_mlp.spec(tm, 2),
                  _full(w_t.shape), _full(w_gu.shape), _full(b_gate.shape), _full(norm_g.shape),
                  _full(w_out.shape), _full(w1.shape), _full(w2.shape),
                  _layer(ln_g, 2 * layer), _layer(ln_b, 2 * layer),
                  _layer(ln_g, 2 * layer + 1), _layer(ln_b, 2 * layer + 1)]
                 + [drow(a) for a in dec_rows] + [dstate],
        out_specs=[prev, _full(state), drow(dv), dstate],
        out_shape=[jax.ShapeDtypeStruct((m, D_MODEL), F32), jax.ShapeDtypeStruct(state, F32),
                   jax.ShapeDtypeStruct(dv.shape, F32),
                   jax.ShapeDtypeStruct(dec_state.shape, F32)],
        scratch_shapes=[pltpu.VMEM((tm, NV), BF16),
                        pltpu.VMEM(state, F32),
                        pltpu.VMEM((tm, D_MODEL), F32),
                        pltpu.VMEM((GLA_HEADS, LANES, GLA_DK), F32),
                        pltpu.VMEM((GLA_HEADS, LANES, GLA_DV), F32),
                        pltpu.VMEM((sb, GLA_HEADS, SUBLANES, GLA_DK), F32)],
        compiler_params=_cparams(1),
        name="gla_layer",
    )(x, x, mod_mix.arr, mod_mix.arr, mod_mix.arr, mod_mlp.arr, mod_mlp.arr, mod_mlp.arr,
      w_t, w_gu, b_gate, norm_g, w_out, w1, w2, ln_g, ln_b, ln_g, ln_b, *dec_rows, dec_state)


class _GlaStep:
    def __init__(self, q_ref, k_ref, lg_ref, v_ref, r_ref, ng_ref, s0_ref, o_ref, s1_ref,
                 z_ref, v_all_ref, q_pad_ref):
        self.refs = (q_ref, k_ref, lg_ref, v_ref, r_ref, ng_ref, s0_ref, o_ref, s1_ref,
                     z_ref, v_all_ref, q_pad_ref)
        self.n_seq = s0_ref.shape[0]
        self.n_new = q_ref.shape[0] // self.n_seq
        assert self.n_new < SUBLANES and self.n_seq * SUBLANES <= LANES

    def zero(self):
        for ref in self.refs[9:]:
            ref[...] = jnp.zeros_like(ref)

    def stage(self):
        return _gla_step_stage(self.n_seq, self.n_new, *self.refs)

    def finish(self, o_intra):
        _gla_step_finish(o_intra, self.n_seq, self.n_new, *self.refs)


def _gla_step_stage(n_seq, n_new, q_ref, k_ref, lg_ref, v_ref, r_ref, ng_ref, s0_ref,
                    o_ref, s1_ref, z_ref, v_all_ref, q_pad_ref):
    tok = [slice(s * n_new, (s + 1) * n_new) for s in range(n_seq)]
    grp = SUBLANES
    rowi = lax.broadcasted_iota(jnp.int32, (n_new, NK), 0)
    rowv = lax.broadcasted_iota(jnp.int32, (n_new, GLA_DV), 0)
    heads = range(GLA_HEADS)
    ks = [slice(h * GLA_DK, (h + 1) * GLA_DK) for h in heads]
    vs = [slice(h * GLA_DV, (h + 1) * GLA_DV) for h in heads]
    o_intra = []
    for s in range(n_seq):
        lg = lg_ref[tok[s], :]
        b = jnp.zeros_like(lg)
        for t in range(n_new):
            b = b + jnp.where(rowi >= t, jnp.broadcast_to(lg[t:t + 1, :], lg.shape), 0.0)
        b_end = b[n_new - 1:n_new, :]
        qd = q_ref[tok[s], :] * jnp.exp(b)
        k = k_ref[tok[s], :]
        kd = k * jnp.exp(-b)
        kdec = k * jnp.exp(b_end - b)
        dec = jnp.exp(b_end)
        v = v_ref[tok[s], :]
        row0 = s * grp
        o_s = []
        for h in heads:
            v_h = v[:, vs[h]]
            qd_h = qd[:, ks[h]]
            o = jnp.zeros((n_new, GLA_DV), F32)
            for t in range(n_new):
                a_t = jnp.sum(qd_h * kd[t:t + 1, ks[h]], axis=-1, keepdims=True)
                o = o + jnp.where(rowv >= t, a_t * v_h[t:t + 1, :], 0.0)
            o_s.append(o)
            q_pad_ref[s, h, 0:n_new, :] = qd_h
            z_ref[h, row0:row0 + n_new, :] = kdec[:, ks[h]]
            z_ref[h, row0 + n_new:row0 + n_new + 1, :] = dec[:, ks[h]]
            v_all_ref[h, row0:row0 + n_new, :] = v_h
        o_intra.append(o_s)
    return o_intra


def _gla_step_finish(o_intra, n_seq, n_new, q_ref, k_ref, lg_ref, v_ref, r_ref, ng_ref, s0_ref,
                     o_ref, s1_ref, z_ref, v_all_ref, q_pad_ref):
    tok = [slice(s * n_new, (s + 1) * n_new) for s in range(n_seq)]
    grp = SUBLANES
    dot = functools.partial(jnp.dot, preferred_element_type=F32)
    heads = range(GLA_HEADS)
    vs = [slice(h * GLA_DV, (h + 1) * GLA_DV) for h in heads]
    group_of_lane = lax.broadcasted_iota(jnp.int32, (GLA_DK, LANES), 1) // grp
    for h in heads:
        z_t = z_ref[h].T
        v_all = v_all_ref[h]
        for s in range(n_seq):
            s0 = s0_ref[s, h]
            mine = jnp.where(group_of_lane == s, z_t, 0.0)
            col = s * grp + n_new
            s1_ref[s, h] = z_t[:, col:col + 1] * s0 + dot(mine, v_all)
            o = o_intra[s][h] + dot(q_pad_ref[s, h], s0)[0:n_new]
            o_ref[tok[s], vs[h]] = _gla_out(o, r_ref[tok[s], vs[h]], ng_ref[...])


def kernel(x_prompt, x_sample, cache_k, cache_v, state_gla, c_prompt, c_sample, w_mod, b_mod,
           ln_g, ln_b, attn_w_in, attn_w_out, attn_sinks, gla_w_in, gla_w_gate_up, gla_b_gate,
           gla_norm_g, gla_w_out, mlp_w1, mlp_w2):
    assert x_prompt.shape[0] == 1 and w_mod.shape[0] == DEPTH == 2
    seq = x_prompt.shape[1]
    nseq, n_new = x_sample.shape[0], x_sample.shape[1]
    win = cache_k.shape[2]
    m_s = nseq * n_new

    wq = attn_w_in[0][:, :NQ].reshape(D_MODEL, N_KV_HEADS, GROUP, HEAD_DIM)
    wq = wq.transpose(0, 2, 1, 3).reshape(D_MODEL, NQ)
    w_attn_in = jnp.concatenate([wq, attn_w_in[0][:, NQ:]], axis=1).astype(BF16)
    w_attn_out = attn_w_out[0].reshape(N_KV_HEADS, GROUP, HEAD_DIM, D_MODEL)
    w_attn_out = w_attn_out.transpose(1, 0, 2, 3).reshape(NQ, D_MODEL).astype(BF16)
    w_gla_t = gla_w_in[0].T.astype(BF16)
    assert w_gla_t.shape[0] == 2 * NK + 2 * NV + GLA_GATE_RANK
    w_gla_gu = jnp.pad(gla_w_gate_up[0], ((LANES - GLA_GATE_RANK, 0), (0, 0))).astype(BF16)
    w_gla_out = gla_w_out[0].astype(BF16)
    b_gate = gla_b_gate[0].reshape(1, NK)
    norm_g = gla_norm_g[0].reshape(1, GLA_DV)
    ln_g4 = ln_g.reshape(2 * DEPTH, 1, D_MODEL)
    ln_b4 = ln_b.reshape(2 * DEPTH, 1, D_MODEL)
    sinks = attn_sinks[0]

    pad_rows = (-(m_s + 1)) % SUBLANES
    c_all = jnp.concatenate([jnp.repeat(c_sample, n_new, axis=0), c_prompt,
                             jnp.zeros((pad_rows, D_MODEL), F32)], axis=0)
    mod_all = _adaln_all(c_all, w_mod.reshape(2 * DEPTH, D_MODEL, 3 * D_MODEL),
                         b_mod.reshape(2 * DEPTH, 1, 3 * D_MODEL))
    mods_p = [_Mod(mod_all, p, per_row=False, row0=m_s) for p in range(2 * DEPTH)]
    mods_s = [_Mod(mod_all, p, per_row=True) for p in range(2 * DEPTH)]

    assert seq >= WINDOW == ATTN_BLOCK
    win_p = WINDOW
    to_slab = lambda c: c[0].transpose(0, 2, 3, 1).reshape(nseq, NKV, win)
    from_slab = lambda c: c.reshape(nseq, N_KV_HEADS, HEAD_DIM, win).transpose(0, 3, 1, 2)[None]

    x_s = x_sample.reshape(m_s, D_MODEL)
    q, k, v = _attn_proj(x_s, mods_s[0], w_attn_in)
    o, k_s, v_s = _dec_attn(q.reshape(nseq, n_new * GROUP, NKV), k, v,
                            to_slab(cache_k), to_slab(cache_v), sinks)
    x1_s, w1_0, w2_0 = _mix_mlp_stream(o.reshape(m_s, NQ), x_s, mods_s[0], mods_s[1], w_attn_out,
                                       mlp_w1, mlp_w2, ln_g4, ln_b4, 0)
    x1_p, k_p, v_p = _attn_layer(x_prompt[0], mods_p[0], mods_p[1], w_attn_in, w_attn_out,
                                 w1_0, w2_0, ln_g4, ln_b4, sinks, 0)

    gla_w = (w_gla_t, w_gla_gu, b_gate)
    w1_1, w2_1 = mlp_w1[1].astype(BF16), mlp_w2[1].astype(BF16)
    dec_rows = _gla_proj(x1_s, mods_s[2], *gla_w, F32)
    y_p, s_p, o, s_s = _gla_layer(x1_p, mods_p[2], mods_p[3], *gla_w, norm_g, w_gla_out,
                                  w1_1, w2_1, ln_g4, ln_b4, 1, dec_rows, state_gla[0])
    y_s = _mix_mlp_stream(o, x1_s, mods_s[2], mods_s[3], w_gla_out, w1_1, w2_1,
                          ln_g4, ln_b4, 1)
    k_s, v_s = from_slab(k_s), from_slab(v_s)

    kv_shape_p = (1, 1, win_p, N_KV_HEADS, HEAD_DIM)
    return (y_p[None], y_s.reshape(nseq, n_new, D_MODEL),
            k_p.reshape(kv_shape_p), v_p.reshape(kv_shape_p), s_p[None, None],
            k_s, v_s, s_s[None])
```

```python
import functools

import jax
import jax.numpy as jnp
from jax import lax
from jax.experimental import pallas as pl
from jax.experimental.pallas import tpu as pltpu

F32 = jnp.float32
BF16 = jnp.bfloat16

D_MODEL = 1024
DEPTH = 2
HEAD_DIM = 64
N_Q_HEADS = 16
N_KV_HEADS = 4
GROUP = 4
WINDOW = 128
ATTN_BLOCK = 128
GLA_HEADS = 4
GLA_DK = 128
GLA_DV = 256
GLA_GATE_RANK = 16
GLA_TAU = 16.0
GLA_CHUNK = 64
D_FF = 4 * D_MODEL
ALPHA = (2.0 * DEPTH) ** 0.25
LN_EPS = 1e-5

NQ = N_Q_HEADS * HEAD_DIM
NKV = N_KV_HEADS * HEAD_DIM
NK = GLA_HEADS * GLA_DK
NV = GLA_HEADS * GLA_DV
LANES = 128
SUBLANES = 8
NEG_BIG = -1e30

ROW_TILE = 512
FF_CHUNK = 1024
ATTN_TILE = 256
GLA_TILE = 256
VMEM_LIMIT = 56 * 1024 * 1024


def _cparams(n_axes):
    return pltpu.CompilerParams(
        dimension_semantics=("arbitrary",) * n_axes,
        vmem_limit_bytes=VMEM_LIMIT,
    )


def _full(shape):
    zeros = (0,) * len(shape)
    return pl.BlockSpec(shape, lambda *_: zeros)


def _layer(arr, idx):
    tail = (0,) * (arr.ndim - 1)
    return pl.BlockSpec((None,) + arr.shape[1:], lambda *_: (idx,) + tail,
                        pipeline_mode=pl.Buffered(1))


def _row_spec(tm, n):
    return pl.BlockSpec((tm, n), lambda i: (i, 0))


class _Mod:
    def __init__(self, arr, p, per_row, row0=0):
        self.arr, self.p, self.per_row, self.row0 = arr, p, per_row, row0

    def spec(self, tm, col):
        p = self.p
        if self.per_row:
            return pl.BlockSpec((None, tm, D_MODEL), lambda i: (p, i, col))
        blk = self.row0 // SUBLANES
        return pl.BlockSpec((None, SUBLANES, D_MODEL), lambda i: (p, blk, col))


def _mod_rows(ref, tm):
    return ref[...] if ref.shape[0] == tm else ref[0:1, :]


def _modulate(x, shift, scale):
    return x * (1.0 + scale) + shift


def _res_ln(x, gate, o, g, b):
    y = ALPHA * x + gate * o
    mu = jnp.mean(y, axis=-1, keepdims=True)
    yc = y - mu
    var = jnp.mean(yc * yc, axis=-1, keepdims=True)
    return yc * lax.rsqrt(var + LN_EPS) * g + b


def _mod_kernel(c_ref, w_ref, b_ref, o_ref):
    c = c_ref[...]
    a = (c * jax.nn.sigmoid(c)).astype(BF16)
    o_ref[...] = jnp.dot(a, w_ref[...].astype(BF16), preferred_element_type=F32) + b_ref[...]


def _adaln_all(c_all, w_mod, b_mod):
    rows = c_all.shape[0]
    tn = 1024
    return pl.pallas_call(
        _mod_kernel,
        grid=(4, 3 * D_MODEL // tn),
        in_specs=[
            pl.BlockSpec((rows, D_MODEL), lambda p, n: (0, 0)),
            pl.BlockSpec((None, D_MODEL, tn), lambda p, n: (p, 0, n)),
            pl.BlockSpec((None, 1, tn), lambda p, n: (p, 0, n)),
        ],
        out_specs=pl.BlockSpec((None, rows, tn), lambda p, n: (p, 0, n)),
        out_shape=jax.ShapeDtypeStruct((4, rows, 3 * D_MODEL), F32),
        compiler_params=_cparams(2),
        name="adaln_mod",
    )(c_all, w_mod, b_mod)


def _attn_proj_kernel(x_ref, sh_ref, sc_ref, w_ref, q_ref, k_ref, v_ref):
    tm = x_ref.shape[0]
    h = _modulate(x_ref[...], _mod_rows(sh_ref, tm), _mod_rows(sc_ref, tm)).astype(BF16)
    q = jnp.dot(h, w_ref[:, 0:NQ], preferred_element_type=F32)
    q_ref[...] = (q * (HEAD_DIM ** -0.5)).astype(BF16)
    k_ref[...] = jnp.dot(h, w_ref[:, NQ:NQ + NKV], preferred_element_type=F32)
    v_ref[...] = jnp.dot(h, w_ref[:, NQ + NKV:NQ + 2 * NKV], preferred_element_type=F32)


def _attn_proj(x, mod, w_in):
    m = x.shape[0]
    tm = min(ROW_TILE, m)
    row = functools.partial(_row_spec, tm)
    return pl.pallas_call(
        _attn_proj_kernel,
        grid=(m // tm,),
        in_specs=[row(D_MODEL), mod.spec(tm, 0), mod.spec(tm, 1), _full(w_in.shape)],
        out_specs=[row(NQ), row(NKV), row(NKV)],
        out_shape=[
            jax.ShapeDtypeStruct((m, NQ), BF16),
            jax.ShapeDtypeStruct((m, NKV), F32),
            jax.ShapeDtypeStruct((m, NKV), F32),
        ],
        compiler_params=_cparams(1),
        name="attn_proj",
    )(x, mod.arr, mod.arr, w_in)


def _gla_project(h, wt_ref, wgu_ref, bg_ref, vr_dtype):
    def proj(lo, hi):
        return lax.dot_general(h, wt_ref[lo:hi, :], (((1,), (1,)), ((), ())),
                               preferred_element_type=F32)

    n_all = wt_ref.shape[0]
    gdown = proj(n_all - LANES, n_all)
    q = proj(0, NK) * (GLA_DK ** -0.5)
    k = proj(NK, 2 * NK)
    pre = jnp.dot(gdown.astype(BF16), wgu_ref[...], preferred_element_type=F32) + bg_ref[...]
    v = proj(2 * NK, 2 * NK + NV).astype(vr_dtype)
    r = proj(2 * NK + NV, 2 * NK + 2 * NV).astype(vr_dtype)
    log_sig = jnp.minimum(pre, 0.0) - jnp.log1p(jnp.exp(-jnp.abs(pre)))
    return q, k, log_sig / GLA_TAU, v, r


def _gla_proj_kernel(x_ref, sh_ref, sc_ref, wt_ref, wgu_ref, bg_ref,
                     q_ref, k_ref, lg_ref, v_ref, r_ref):
    tm = x_ref.shape[0]
    h = _modulate(x_ref[...], _mod_rows(sh_ref, tm), _mod_rows(sc_ref, tm)).astype(BF16)
    q_ref[...], k_ref[...], lg_ref[...], v_ref[...], r_ref[...] = _gla_project(
        h, wt_ref, wgu_ref, bg_ref, v_ref.dtype)


def _gla_proj(x, mod, w_t, w_gu, b_gate, vr_dtype):
    m = x.shape[0]
    tm = min(ROW_TILE, m)
    row = functools.partial(_row_spec, tm)
    return pl.pallas_call(
        _gla_proj_kernel,
        grid=(m // tm,),
        in_specs=[row(D_MODEL), mod.spec(tm, 0), mod.spec(tm, 1), _full(w_t.shape),
                  _full(w_gu.shape), _full(b_gate.shape)],
        out_specs=[row(NK), row(NK), row(NK), row(NV), row(NV)],
        out_shape=[
            jax.ShapeDtypeStruct((m, NK), F32),
            jax.ShapeDtypeStruct((m, NK), F32),
            jax.ShapeDtypeStruct((m, NK), F32),
            jax.ShapeDtypeStruct((m, NV), vr_dtype),
            jax.ShapeDtypeStruct((m, NV), vr_dtype),
        ],
        compiler_params=_cparams(1),
        name="gla_proj",
    )(x, mod.arr, mod.arr, w_t, w_gu, b_gate)


def _mix_mlp_stream_kernel(a_ref, x_ref, gt0_ref, sh_ref, sc_ref, gt1_ref, wo_ref, w1_ref, w2_ref,
                           g0_ref, b0_ref, g1_ref, b1_ref, y_ref, x1_s, h_s, acc_ref):
    c = pl.program_id(0)

    @pl.when(c == 0)
    def _():
        o = jnp.dot(a_ref[...].astype(BF16), wo_ref[...], preferred_element_type=F32)
        x1 = _res_ln(x_ref[...], gt0_ref[...], o, g0_ref[...], b0_ref[...])
        x1_s[...] = x1
        h_s[...] = _modulate(x1, sh_ref[...], sc_ref[...]).astype(BF16)
        acc_ref[...] = jnp.zeros_like(acc_ref)

    a = jnp.dot(h_s[...], w1_ref[...], preferred_element_type=F32)
    a = jnp.square(jnp.maximum(a, 0.0)).astype(BF16)
    acc_ref[...] += jnp.dot(a, w2_ref[...], preferred_element_type=F32)

    @pl.when(c == pl.num_programs(0) - 1)
    def _():
        y_ref[...] = _res_ln(x1_s[...], gt1_ref[...], acc_ref[...], g1_ref[...], b1_ref[...])


def _mix_mlp_stream(a, x, mod_mix, mod_mlp, w_out, w1, w2, ln_g, ln_b, layer):
    m = x.shape[0]
    assert m <= ROW_TILE and mod_mix.per_row and mod_mlp.per_row
    fc = FF_CHUNK
    whole = lambda n: pl.BlockSpec((m, n), lambda c: (0, 0))
    mspec = lambda mod, col: pl.BlockSpec((None, m, D_MODEL), lambda c: (mod.p, 0, col))
    return pl.pallas_call(
        _mix_mlp_stream_kernel,
        grid=(D_FF // fc,),
        in_specs=[whole(a.shape[1]), whole(D_MODEL), mspec(mod_mix, 2), mspec(mod_mlp, 0),
                  mspec(mod_mlp, 1), mspec(mod_mlp, 2), _full(w_out.shape),
                  pl.BlockSpec((D_MODEL, fc), lambda c: (0, c)),
                  pl.BlockSpec((fc, D_MODEL), lambda c: (c, 0)),
                  _layer(ln_g, 2 * layer), _layer(ln_b, 2 * layer),
                  _layer(ln_g, 2 * layer + 1), _layer(ln_b, 2 * layer + 1)],
        out_specs=whole(D_MODEL),
        out_shape=jax.ShapeDtypeStruct((m, D_MODEL), F32),
        scratch_shapes=[pltpu.VMEM((m, D_MODEL), F32), pltpu.VMEM((m, D_MODEL), BF16),
                        pltpu.VMEM((m, D_MODEL), F32)],
        compiler_params=_cparams(1),
        name="mix_mlp_stream",
    )(a, x, mod_mix.arr, mod_mlp.arr, mod_mlp.arr, mod_mlp.arr, w_out, w1, w2,
      ln_g, ln_b, ln_g, ln_b)


def _alibi_slope(head):
    return 2.0 ** (-8.0 * (head + 1) / N_Q_HEADS)


def _softmax_sink(s, sink):
    m = jnp.maximum(jnp.max(s, axis=-1, keepdims=True), sink)
    e = jnp.exp(s - m)
    den = jnp.sum(e, axis=-1, keepdims=True) + jnp.exp(sink - m)
    return e / den


def _band_bias_init(bias_ref):
    blk = ATTN_BLOCK
    c = lax.broadcasted_iota(jnp.int32, (2 * blk, blk), 0)
    r = lax.broadcasted_iota(jnp.int32, (2 * blk, blk), 1)
    dist = blk + r - c
    valid = (dist >= 0) & (dist <= WINDOW)
    distf = dist.astype(F32)
    for head in range(N_Q_HEADS):
        pen = -_alibi_slope(head) * distf
        bias_ref[0, head] = jnp.where(valid, pen, NEG_BIG)
        bias_ref[1, head] = jnp.where(valid & (c >= blk), pen, NEG_BIG)


def _band_scores(q, kk):
    blk = ATTN_BLOCK
    head_of_lane = lax.broadcasted_iota(jnp.int32, (blk, NKV), 1) // HEAD_DIM
    scores = []
    for b in range(q.shape[0] // blk):
        keys = kk[b * blk:(b + 2) * blk]
        for g in range(GROUP):
            qg = q[b * blk:(b + 1) * blk, g * NKV:(g + 1) * NKV]
            qm = jnp.concatenate(
                [jnp.where(head_of_lane == j, qg, jnp.zeros_like(qg))
                 for j in range(N_KV_HEADS)], axis=0)
            scores.append(lax.dot_general(keys, qm, (((1,), (1,)), ((), ())),
                                          preferred_element_type=F32))
    return scores


def _band_outputs(scores, vvt, first_tile, sinks_ref, bias_ref, o_ref):
    blk = ATTN_BLOCK
    for b in range(len(scores) // GROUP):
        table = 1 if first_tile and b == 0 else 0
        vals_t = vvt[:, b * blk:(b + 2) * blk]
        for g in range(GROUP):
            st_all = scores[b * GROUP + g]
            ps = []
            for j in range(N_KV_HEADS):
                head = j * GROUP + g
                sink = sinks_ref[head]
                st = st_all[:, j * blk:(j + 1) * blk] + bias_ref[table, head]
                m = jnp.maximum(jnp.max(st, axis=0, keepdims=True), sink)
                e = jnp.exp(st - m)
                den = jnp.sum(e, axis=0, keepdims=True) + jnp.exp(sink - m)
                ps.append((e * (1.0 / den)).astype(BF16))
            ot_all = jnp.dot(vals_t, jnp.concatenate(ps, axis=1),
                             preferred_element_type=F32)
            ot = jnp.concatenate(
                [ot_all[j * HEAD_DIM:(j + 1) * HEAD_DIM, j * blk:(j + 1) * blk]
                 for j in range(N_KV_HEADS)], axis=0)
            o_ref[b * blk:(b + 1) * blk, g * NKV:(g + 1) * NKV] = ot.T.astype(BF16)


def _mlp_chunks(h, w1_ref, w2_ref, acc_ref, chunks):
    for c in chunks:
        cols = slice(c * FF_CHUNK, (c + 1) * FF_CHUNK)
        a = jnp.dot(h, w1_ref[:, cols], preferred_element_type=F32)
        a = jnp.square(jnp.maximum(a, 0.0)).astype(BF16)
        d = jnp.dot(a, w2_ref[cols, :], preferred_element_type=F32)
        if c == 0:
            acc_ref[...] = d
        else:
            acc_ref[...] += d


def _attn_layer_kernel(sinks_ref, xc_ref, xp_ref, sh0_ref, sc0_ref, gt0_ref, sh1_ref, sc1_ref,
                       gt1_ref, win_ref, wo_ref, w1_ref, w2_ref, g0_ref, b0_ref, g1_ref, b1_ref,
                       dq_ref, dkn_ref, dvn_ref, dck_ref, dcv_ref,
                       y_ref, kl_ref, vl_ref, do_ref, dnk_ref, dnv_ref,
                       o_s, kprev_s, vtprev_s, acc_ref, bias_ref, zk_ref, zv_ref):
    i = pl.program_id(0)
    last = pl.num_programs(0) - 1
    dec = _DecAttn(sinks_ref, dq_ref, dkn_ref, dvn_ref, dck_ref, dcv_ref, do_ref, dnk_ref,
                   dnv_ref, zk_ref, zv_ref)
    slot = i % 2
    blk = ATTN_BLOCK
    tm = xc_ref.shape[0]
    dot = functools.partial(jnp.dot, preferred_element_type=F32)
    n_chunks = D_FF // FF_CHUNK

    def mlp_in():
        x1 = _res_ln(xp_ref[...], gt0_ref[0:1, :], dot(o_s[1 - slot], wo_ref[...]),
                     g0_ref[...], b0_ref[...])
        return x1, _modulate(x1, sh1_ref[0:1, :], sc1_ref[0:1, :]).astype(BF16)

    def mlp_out(x1):
        y_ref[...] = _res_ln(x1, gt1_ref[0:1, :], acc_ref[...], g1_ref[...], b1_ref[...])

    def mix_in():
        h_in = _modulate(xc_ref[...], sh0_ref[0:1, :], sc0_ref[0:1, :]).astype(BF16)
        q = (dot(h_in, win_ref[:, 0:NQ]) * (HEAD_DIM ** -0.5)).astype(BF16)
        k = dot(h_in, win_ref[:, NQ:NQ + NKV])
        v = dot(h_in, win_ref[:, NQ + NKV:NQ + 2 * NKV])
        kl_ref[...] = k[tm - blk:, :]
        vl_ref[...] = v[tm - blk:, :]
        k_bf = k.astype(BF16)
        vt = v.T.astype(BF16)
        kk = jnp.concatenate([kprev_s[...], k_bf], axis=0)
        vvt = jnp.concatenate([vtprev_s[...], vt], axis=1)
        scores = _band_scores(q, kk)
        kprev_s[...] = k_bf[tm - blk:, :]
        vtprev_s[...] = vt[:, tm - blk:]
        return scores, vvt

    @pl.when(i == 0)
    def _():
        _band_bias_init(bias_ref)
        kprev_s[...] = jnp.zeros_like(kprev_s)
        vtprev_s[...] = jnp.zeros_like(vtprev_s)
        dec.zero()
        scores, vvt = mix_in()
        dec_scored = dec.scores()
        _band_outputs(scores, vvt, True, sinks_ref, bias_ref, o_s.at[slot])
        dec.finish(dec_scored)

    @pl.when((i > 0) & (i < last))
    def _():
        x1, h_mlp = mlp_in()
        scores, vvt = mix_in()
        _mlp_chunks(h_mlp, w1_ref, w2_ref, acc_ref, range(0, 1))
        dec_scored = dec.scores()
        _mlp_chunks(h_mlp, w1_ref, w2_ref, acc_ref, range(1, n_chunks))
        _band_outputs(scores, vvt, False, sinks_ref, bias_ref, o_s.at[slot])
        dec.finish(dec_scored)
        mlp_out(x1)

    @pl.when(i == last)
    def _():
        x1, h_mlp = mlp_in()
        _mlp_chunks(h_mlp, w1_ref, w2_ref, acc_ref, range(n_chunks))
        mlp_out(x1)


def _attn_layer(x, mod_mix, mod_mlp, w_in, w_out, w1, w2, ln_g, ln_b, sinks, layer,
                dec_q, dec_k, dec_v, cache_kt, cache_vt):
    m = x.shape[0]
    tm = ATTN_TILE
    blk = ATTN_BLOCK
    n = m // tm
    assert not mod_mix.per_row and not mod_mlp.per_row
    nseq, win = cache_kt.shape[0], cache_kt.shape[2]
    sb = nseq // n
    n_new = dec_k.shape[0] // nseq
    assert sb * n == nseq and (sb * n_new) % SUBLANES == 0
    cur = pl.BlockSpec((tm, D_MODEL), lambda i: (jnp.minimum(i, n - 1), 0))
    prev = pl.BlockSpec((tm, D_MODEL), lambda i: (jnp.maximum(i - 1, 0), 0))
    last = pl.BlockSpec((blk, NKV), lambda i: (0, 0))
    dblk = lambda a: pl.BlockSpec((sb,) + a.shape[1:], lambda i: (jnp.minimum(i, n - 1), 0, 0))
    drow = lambda a: pl.BlockSpec((sb * n_new, a.shape[1]), lambda i: (jnp.minimum(i, n - 1), 0))
    return pl.pallas_call(
        _attn_layer_kernel,
        grid=(n + 1,),
        in_specs=[pl.BlockSpec(memory_space=pltpu.SMEM), cur, prev,
                  mod_mix.spec(tm, 0), mod_mix.spec(tm, 1), mod_mix.spec(tm, 2),
                  mod_mlp.spec(tm, 0), mod_mlp.spec(tm, 1), mod_mlp.spec(tm, 2),
                  _full(w_in.shape), _full(w_out.shape), _full(w1.shape), _full(w2.shape),
                  _layer(ln_g, 2 * layer), _layer(ln_b, 2 * layer),
                  _layer(ln_g, 2 * layer + 1), _layer(ln_b, 2 * layer + 1),
                  dblk(dec_q), drow(dec_k), drow(dec_v), dblk(cache_kt), dblk(cache_vt)],
        out_specs=[prev, last, last, dblk(dec_q), dblk(cache_kt), dblk(cache_vt)],
        out_shape=[jax.ShapeDtypeStruct((m, D_MODEL), F32),
                   jax.ShapeDtypeStruct((blk, NKV), F32),
                   jax.ShapeDtypeStruct((blk, NKV), F32),
                   jax.ShapeDtypeStruct(dec_q.shape, BF16),
                   jax.ShapeDtypeStruct(cache_kt.shape, F32),
                   jax.ShapeDtypeStruct(cache_vt.shape, F32)],
        scratch_shapes=[pltpu.VMEM((2, tm, NQ), BF16),
                        pltpu.VMEM((blk, NKV), BF16),
                        pltpu.VMEM((NKV, blk), BF16),
                        pltpu.VMEM((tm, D_MODEL), F32),
                        pltpu.VMEM((2, N_Q_HEADS, 2 * blk, blk), F32),
                        pltpu.VMEM((sb, win, NKV), F32),
                        pltpu.VMEM((sb, win, NKV), F32)],
        compiler_params=_cparams(1),
        name="attn_layer",
    )(sinks, x, x, mod_mix.arr, mod_mix.arr, mod_mix.arr, mod_mlp.arr, mod_mlp.arr,
      mod_mlp.arr, w_in, w_out, w1, w2, ln_g, ln_b, ln_g, ln_b,
      dec_q, dec_k, dec_v, cache_kt, cache_vt)


class _DecAttn:
    def __init__(self, sinks_ref, q_ref, kn_ref, vn_ref, ck_ref, cv_ref, o_ref, nk_ref, nv_ref,
                 zk_ref, zv_ref):
        self.refs = (sinks_ref, q_ref, kn_ref, vn_ref, ck_ref, cv_ref, o_ref, nk_ref, nv_ref,
                     zk_ref, zv_ref)
        self.n_seq = ck_ref.shape[0]
        self.n_new = kn_ref.shape[0] // self.n_seq
        self.win = ck_ref.shape[2]
        assert self.win == LANES and self.n_new < SUBLANES

    def zero(self):
        for ref in self.refs[9:]:
            ref[...] = jnp.zeros_like(ref)

    def scores(self):
        (sinks_ref, q_ref, kn_ref, vn_ref, ck_ref, cv_ref, _, nk_ref, nv_ref,
         zk_ref, zv_ref) = self.refs
        n_new, win = self.n_new, self.win
        rows = n_new * GROUP
        keep = win - n_new
        head_of_lane = lax.broadcasted_iota(jnp.int32, (rows, NKV), 1) // HEAD_DIM
        lane_w = lax.broadcasted_iota(jnp.int32, (NKV, win), 1)
        out = []
        for s in range(self.n_seq):
            tok = slice(s * n_new, (s + 1) * n_new)
            k_t = ck_ref[s]
            v_t = cv_ref[s]
            zk_ref[s, keep:win, :] = kn_ref[tok, :]
            zv_ref[s, keep:win, :] = vn_ref[tok, :]
            zk_t = zk_ref[s].T
            zv_t = zv_ref[s].T
            nk_ref[s] = jnp.where(lane_w < keep, pltpu.roll(k_t, keep, 1), zk_t)
            nv_ref[s] = jnp.where(lane_w < keep, pltpu.roll(v_t, keep, 1), zv_t)
            keys = jnp.concatenate([k_t, zk_t], axis=1).astype(BF16)
            vals = jnp.concatenate([v_t, zv_t], axis=1).astype(BF16)
            qs = q_ref[s]
            qbd = jnp.concatenate(
                [jnp.where(head_of_lane == j, qs, jnp.zeros_like(qs))
                 for j in range(N_KV_HEADS)], axis=0)
            out.append((jnp.dot(qbd, keys, preferred_element_type=F32), vals))
        return out

    def finish(self, scored):
        sinks_ref, o_ref = self.refs[0], self.refs[6]
        n_new, win = self.n_new, self.win
        rows = n_new * GROUP
        n_rows = N_KV_HEADS * rows
        keep = win - n_new
        row = lax.broadcasted_iota(jnp.int32, (n_rows, 2 * win), 0)
        col = lax.broadcasted_iota(jnp.int32, (n_rows, 2 * win), 1)
        j_r = row // rows
        t_r = (row // GROUP) % n_new
        g_r = row % GROUP
        h_r = j_r * GROUP + g_r
        slope = jnp.exp2(-8.0 * (h_r + 1).astype(F32) / N_Q_HEADS)
        sink = jnp.zeros((n_rows, 1), F32)
        h_col = h_r[:, 0:1]
        for h in range(N_Q_HEADS):
            sink = jnp.where(h_col == h, sinks_ref[h], sink)
        is_key = (col < win) | (col >= win + keep)
        frame = jnp.where(col < win, col, col - keep)
        dist = t_r + win - frame
        valid = is_key & (dist >= 0) & (dist <= WINDOW)
        bias = jnp.where(valid, -slope * dist.astype(F32), NEG_BIG)
        head_of_lane = lax.broadcasted_iota(jnp.int32, (rows, NKV), 1) // HEAD_DIM
        for s, (sc, vals) in enumerate(scored):
            p = _softmax_sink(sc + bias, sink).astype(BF16)
            pv = lax.dot_general(p, vals, (((1,), (1,)), ((), ())),
                                 preferred_element_type=F32)
            o = jnp.zeros((rows, NKV), F32)
            for j in range(N_KV_HEADS):
                o = o + jnp.where(head_of_lane == j, pv[j * rows:(j + 1) * rows], 0.0)
            o_ref[s] = o.astype(BF16)


def _split2(x):
    hi = x.astype(BF16)
    lo = (x - hi.astype(F32)).astype(BF16)
    return hi, lo


def _gla_out(o, r, norm_g):
    ms = jnp.mean(o * o, axis=-1, keepdims=True)
    o = o * lax.rsqrt(ms + LN_EPS) * norm_g
    return o * (r * jax.nn.sigmoid(r))


def _gla_layer_kernel(xc_ref, xp_ref, sh0_ref, sc0_ref, gt0_ref, sh1_ref, sc1_ref, gt1_ref,
                      wt_ref, wgu_ref, bg_ref, ng_ref, wo_ref, w1_ref, w2_ref,
                      g0_ref, b0_ref, g1_ref, b1_ref,
                      dq_ref, dk_ref, dlg_ref, dv_ref, dr_ref, ds0_ref,
                      y_ref, s_out_ref, do_ref, ds1_ref,
                      o_s, s_ref, acc_ref, z_ref, v_all_ref, q_pad_ref):
    i = pl.program_id(0)
    last = pl.num_programs(0) - 1
    dec = _GlaStep(dq_ref, dk_ref, dlg_ref, dv_ref, dr_ref, ng_ref, ds0_ref, do_ref, ds1_ref,
                   z_ref, v_all_ref, q_pad_ref)
    sub = xc_ref.shape[0]
    ch = GLA_CHUNK
    n_ch = sub // ch
    n_chunks = D_FF // FF_CHUNK
    dot = functools.partial(jnp.dot, preferred_element_type=F32)
    heads = range(GLA_HEADS)
    ks = [slice(h * GLA_DK, (h + 1) * GLA_DK) for h in heads]
    vs = [slice(h * GLA_DV, (h + 1) * GLA_DV) for h in heads]

    def mlp_in():
        x1 = _res_ln(xp_ref[...], gt0_ref[0:1, :], dot(o_s[...], wo_ref[...]),
                     g0_ref[...], b0_ref[...])
        return x1, _modulate(x1, sh1_ref[0:1, :], sc1_ref[0:1, :]).astype(BF16)

    def mlp_out(x1):
        y_ref[...] = _res_ln(x1, gt1_ref[0:1, :], acc_ref[...], g1_ref[...], b1_ref[...])

    def decays():
        h_in = _modulate(xc_ref[...], sh0_ref[0:1, :], sc0_ref[0:1, :]).astype(BF16)
        q, k, lg, v, r = _gla_project(h_in, wt_ref, wgu_ref, bg_ref, BF16)
        row = lax.broadcasted_iota(jnp.int32, (sub, sub), 0)
        col = lax.broadcasted_iota(jnp.int32, (sub, sub), 1)
        causal = ((row // ch) == (col // ch)) & (col <= row)
        tril = jnp.where(causal, 1.0, 0.0).astype(BF16)
        hi, lo = _split2(lg)
        b = dot(tril, hi) + dot(tril, lo)
        ends = [b[(c + 1) * ch - 1:(c + 1) * ch, :] for c in range(n_ch)]
        b_end = jnp.concatenate([jnp.broadcast_to(e, (ch, NK)) for e in ends], axis=0)
        qd = (q * jnp.exp(b)).astype(BF16)
        kd = (k * jnp.exp(-b)).astype(BF16)
        kdec = k * jnp.exp(b_end - b)
        dec_rows = jnp.concatenate(
            [jnp.exp(e) for e in ends] + [jnp.zeros((LANES - n_ch, NK), F32)], axis=0)
        return causal, qd, kd, kdec, dec_rows, v, r

    def chunk_products(causal, qd, kd, kdec, v):
        chunk_of_col = lax.broadcasted_iota(jnp.int32, (GLA_DK, sub), 1) // ch
        a = [lax.dot_general(qd[:, ks[h]], kd[:, ks[h]], (((1,), (1,)), ((), ())),
                             preferred_element_type=F32) for h in heads]
        u = []
        for h in heads:
            kdec_t = kdec[:, ks[h]].T.astype(BF16)
            stacked = jnp.concatenate(
                [jnp.where(chunk_of_col == c, kdec_t, jnp.zeros_like(kdec_t))
                 for c in range(n_ch)], axis=0)
            u.append(dot(stacked, v[:, vs[h]]))
        o_intra = [dot(jnp.where(causal, a[h], 0.0).astype(BF16), v[:, vs[h]]) for h in heads]
        return u, o_intra

    def recurrence(qd, dec_rows, u, o_intra, r):
        for h in heads:
            s = s_ref[h]
            dec_t = dec_rows[:, ks[h]].T
            o_inter = []
            for c in range(n_ch):
                o_inter.append(dot(qd[c * ch:(c + 1) * ch, ks[h]], s.astype(BF16)))
                s = dec_t[:, c:c + 1] * s + u[h][c * GLA_DK:(c + 1) * GLA_DK]
            s_ref[h] = s
            o = o_intra[h] + jnp.concatenate(o_inter, axis=0)
            o_s[:, vs[h]] = _gla_out(o, r[:, vs[h]].astype(F32), ng_ref[...]).astype(BF16)

    @pl.when(i == 0)
    def _():
        s_ref[...] = jnp.zeros_like(s_ref)
        dec.zero()
        dec_intra = dec.stage()
        causal, qd, kd, kdec, dec_rows, v, r = decays()
        u, o_intra = chunk_products(causal, qd, kd, kdec, v)
        recurrence(qd, dec_rows, u, o_intra, r)
        dec.finish(dec_intra)

    @pl.when((i > 0) & (i < last))
    def _():
        x1, h_mlp = mlp_in()
        dec_intra = dec.stage()
        causal, qd, kd, kdec, dec_rows, v, r = decays()
        _mlp_chunks(h_mlp, w1_ref, w2_ref, acc_ref, range(0, 1))
        u, o_intra = chunk_products(causal, qd, kd, kdec, v)
        _mlp_chunks(h_mlp, w1_ref, w2_ref, acc_ref, range(1, n_chunks))
        recurrence(qd, dec_rows, u, o_intra, r)
        dec.finish(dec_intra)
        mlp_out(x1)

    @pl.when(i == last)
    def _():
        x1, h_mlp = mlp_in()
        _mlp_chunks(h_mlp, w1_ref, w2_ref, acc_ref, range(n_chunks))
        mlp_out(x1)
        s_out_ref[...] = s_ref[...]


def _gla_layer(x, mod_mix, mod_mlp, w_t, w_gu, b_gate, norm_g, w_out, w1, w2, ln_g, ln_b, layer,
               dec_rows, dec_state):
    m = x.shape[0]
    tm = GLA_TILE
    n = m // tm
    assert not mod_mix.per_row and not mod_mlp.per_row and tm % GLA_CHUNK == 0
    nseq = dec_state.shape[0]
    sb = nseq // n
    n_new = dec_rows[0].shape[0] // nseq
    assert sb * n == nseq and (sb * n_new) % SUBLANES == 0 and n_new <= GLA_CHUNK
    cur = pl.BlockSpec((tm, D_MODEL), lambda i: (jnp.minimum(i, n - 1), 0))
    prev = pl.BlockSpec((tm, D_MODEL), lambda i: (jnp.maximum(i - 1, 0), 0))
    drow = lambda a: pl.BlockSpec((sb * n_new, a.shape[1]), lambda i: (jnp.minimum(i, n - 1), 0))
    dstate = pl.BlockSpec((sb,) + dec_state.shape[1:],
                          lambda i: (jnp.minimum(i, n - 1), 0, 0, 0))
    state = (GLA_HEADS, GLA_DK, GLA_DV)
    dv = dec_rows[3]
    return pl.pallas_call(
        _gla_layer_kernel,
        grid=(n + 1,),
        in_specs=[cur, prev,
                  mod_mix.spec(tm, 0), mod_mix.spec(tm, 1), mod_mix.spec(tm, 2),
                  mod_mlp.spec(tm, 0), mod_mlp.spec(tm, 1), mod_mlp.spec(tm, 2),
                  _full(w_t.shape), _full(w_gu.shape), _full(b_gate.shape), _full(norm_g.shape),
                  _full(w_out.shape), _full(w1.shape), _full(w2.shape),
                  _layer(ln_g, 2 * layer), _layer(ln_b, 2 * layer),
                  _layer(ln_g, 2 * layer + 1), _layer(ln_b, 2 * layer + 1)]
                 + [drow(a) for a in dec_rows] + [dstate],
        out_specs=[prev, _full(state), drow(dv), dstate],
        out_shape=[jax.ShapeDtypeStruct((m, D_MODEL), F32), jax.ShapeDtypeStruct(state, F32),
                   jax.ShapeDtypeStruct(dv.shape, F32),
                   jax.ShapeDtypeStruct(dec_state.shape, F32)],
        scratch_shapes=[pltpu.VMEM((tm, NV), BF16),
                        pltpu.VMEM(state, F32),
                        pltpu.VMEM((tm, D_MODEL), F32),
                        pltpu.VMEM((GLA_HEADS, LANES, GLA_DK), F32),
                        pltpu.VMEM((GLA_HEADS, LANES, GLA_DV), F32),
                        pltpu.VMEM((sb, GLA_HEADS, SUBLANES, GLA_DK), F32)],
        compiler_params=_cparams(1),
        name="gla_layer",
    )(x, x, mod_mix.arr, mod_mix.arr, mod_mix.arr, mod_mlp.arr, mod_mlp.arr, mod_mlp.arr,
      w_t, w_gu, b_gate, norm_g, w_out, w1, w2, ln_g, ln_b, ln_g, ln_b, *dec_rows, dec_state)


class _GlaStep:
    def __init__(self, q_ref, k_ref, lg_ref, v_ref, r_ref, ng_ref, s0_ref, o_ref, s1_ref,
                 z_ref, v_all_ref, q_pad_ref):
        self.refs = (q_ref, k_ref, lg_ref, v_ref, r_ref, ng_ref, s0_ref, o_ref, s1_ref,
                     z_ref, v_all_ref, q_pad_ref)
        self.n_seq = s0_ref.shape[0]
        self.n_new = q_ref.shape[0] // self.n_seq
        assert self.n_new < SUBLANES and self.n_seq * SUBLANES <= LANES

    def zero(self):
        for ref in self.refs[9:]:
            ref[...] = jnp.zeros_like(ref)

    def stage(self):
        return _gla_step_stage(self.n_seq, self.n_new, *self.refs)

    def finish(self, o_intra):
        _gla_step_finish(o_intra, self.n_seq, self.n_new, *self.refs)


def _gla_step_stage(n_seq, n_new, q_ref, k_ref, lg_ref, v_ref, r_ref, ng_ref, s0_ref,
                    o_ref, s1_ref, z_ref, v_all_ref, q_pad_ref):
    tok = [slice(s * n_new, (s + 1) * n_new) for s in range(n_seq)]
    grp = SUBLANES
    rowi = lax.broadcasted_iota(jnp.int32, (n_new, NK), 0)
    rowv = lax.broadcasted_iota(jnp.int32, (n_new, GLA_DV), 0)
    heads = range(GLA_HEADS)
    ks = [slice(h * GLA_DK, (h + 1) * GLA_DK) for h in heads]
    vs = [slice(h * GLA_DV, (h + 1) * GLA_DV) for h in heads]
    o_intra = []
    for s in range(n_seq):
        lg = lg_ref[tok[s], :]
        b = jnp.zeros_like(lg)
        for t in range(n_new):
            b = b + jnp.where(rowi >= t, jnp.broadcast_to(lg[t:t + 1, :], lg.shape), 0.0)
        b_end = b[n_new - 1:n_new, :]
        qd = q_ref[tok[s], :] * jnp.exp(b)
        k = k_ref[tok[s], :]
        kd = k * jnp.exp(-b)
        kdec = k * jnp.exp(b_end - b)
        dec = jnp.exp(b_end)
        v = v_ref[tok[s], :]
        row0 = s * grp
        o_s = []
        for h in heads:
            v_h = v[:, vs[h]]
            qd_h = qd[:, ks[h]]
            o = jnp.zeros((n_new, GLA_DV), F32)
            for t in range(n_new):
                a_t = jnp.sum(qd_h * kd[t:t + 1, ks[h]], axis=-1, keepdims=True)
                o = o + jnp.where(rowv >= t, a_t * v_h[t:t + 1, :], 0.0)
            o_s.append(o)
            q_pad_ref[s, h, 0:n_new, :] = qd_h
            z_ref[h, row0:row0 + n_new, :] = kdec[:, ks[h]]
            z_ref[h, row0 + n_new:row0 + n_new + 1, :] = dec[:, ks[h]]
            v_all_ref[h, row0:row0 + n_new, :] = v_h
        o_intra.append(o_s)
    return o_intra


def _gla_step_finish(o_intra, n_seq, n_new, q_ref, k_ref, lg_ref, v_ref, r_ref, ng_ref, s0_ref,
                     o_ref, s1_ref, z_ref, v_all_ref, q_pad_ref):
    tok = [slice(s * n_new, (s + 1) * n_new) for s in range(n_seq)]
    grp = SUBLANES
    dot = functools.partial(jnp.dot, preferred_element_type=F32)
    heads = range(GLA_HEADS)
    vs = [slice(h * GLA_DV, (h + 1) * GLA_DV) for h in heads]
    group_of_lane = lax.broadcasted_iota(jnp.int32, (GLA_DK, LANES), 1) // grp
    for h in heads:
        z_t = z_ref[h].T
        v_all = v_all_ref[h]
        for s in range(n_seq):
            s0 = s0_ref[s, h]
            mine = jnp.where(group_of_lane == s, z_t, 0.0)
            col = s * grp + n_new
            s1_ref[s, h] = z_t[:, col:col + 1] * s0 + dot(mine, v_all)
            o = o_intra[s][h] + dot(q_pad_ref[s, h], s0)[0:n_new]
            o_ref[tok[s], vs[h]] = _gla_out(o, r_ref[tok[s], vs[h]], ng_ref[...])


def kernel(x_prompt, x_sample, cache_k, cache_v, state_gla, c_prompt, c_sample, w_mod, b_mod,
           ln_g, ln_b, attn_w_in, attn_w_out, attn_sinks, gla_w_in, gla_w_gate_up, gla_b_gate,
           gla_norm_g, gla_w_out, mlp_w1, mlp_w2):
    assert x_prompt.shape[0] == 1 and w_mod.shape[0] == DEPTH == 2
    seq = x_prompt.shape[1]
    nseq, n_new = x_sample.shape[0], x_sample.shape[1]
    win = cache_k.shape[2]
    m_s = nseq * n_new

    wq = attn_w_in[0][:, :NQ].reshape(D_MODEL, N_KV_HEADS, GROUP, HEAD_DIM)
    wq = wq.transpose(0, 2, 1, 3).reshape(D_MODEL, NQ)
    w_attn_in = jnp.concatenate([wq, attn_w_in[0][:, NQ:]], axis=1).astype(BF16)
    w_attn_out = attn_w_out[0].reshape(N_KV_HEADS, GROUP, HEAD_DIM, D_MODEL)
    w_attn_out = w_attn_out.transpose(1, 0, 2, 3).reshape(NQ, D_MODEL).astype(BF16)
    w_gla_t = gla_w_in[0].T.astype(BF16)
    assert w_gla_t.shape[0] == 2 * NK + 2 * NV + GLA_GATE_RANK
    w_gla_gu = jnp.pad(gla_w_gate_up[0], ((LANES - GLA_GATE_RANK, 0), (0, 0))).astype(BF16)
    w_gla_out = gla_w_out[0].astype(BF16)
    b_gate = gla_b_gate[0].reshape(1, NK)
    norm_g = gla_norm_g[0].reshape(1, GLA_DV)
    ln_g4 = ln_g.reshape(2 * DEPTH, 1, D_MODEL)
    ln_b4 = ln_b.reshape(2 * DEPTH, 1, D_MODEL)
    sinks = attn_sinks[0]

    pad_rows = (-(m_s + 1)) % SUBLANES
    c_all = jnp.concatenate([jnp.repeat(c_sample, n_new, axis=0), c_prompt,
                             jnp.zeros((pad_rows, D_MODEL), F32)], axis=0)
    mod_all = _adaln_all(c_all, w_mod.reshape(2 * DEPTH, D_MODEL, 3 * D_MODEL),
                         b_mod.reshape(2 * DEPTH, 1, 3 * D_MODEL))
    mods_p = [_Mod(mod_all, p, per_row=False, row0=m_s) for p in range(2 * DEPTH)]
    mods_s = [_Mod(mod_all, p, per_row=True) for p in range(2 * DEPTH)]

    assert seq >= WINDOW == ATTN_BLOCK
    win_p = WINDOW
    to_slab = lambda c: c[0].transpose(0, 2, 3, 1).reshape(nseq, NKV, win)
    from_slab = lambda c: c.reshape(nseq, N_KV_HEADS, HEAD_DIM, win).transpose(0, 3, 1, 2)[None]

    x_s = x_sample.reshape(m_s, D_MODEL)
    w1_0, w2_0 = mlp_w1[0].astype(BF16), mlp_w2[0].astype(BF16)
    q, k, v = _attn_proj(x_s, mods_s[0], w_attn_in)
    x1_p, k_p, v_p, o, k_s, v_s = _attn_layer(
        x_prompt[0], mods_p[0], mods_p[1], w_attn_in, w_attn_out, w1_0, w2_0,
        ln_g4, ln_b4, sinks, 0, q.reshape(nseq, n_new * GROUP, NKV), k, v,
        to_slab(cache_k), to_slab(cache_v))
    x1_s = _mix_mlp_stream(o.reshape(m_s, NQ), x_s, mods_s[0], mods_s[1], w_attn_out,
                           w1_0, w2_0, ln_g4, ln_b4, 0)

    gla_w = (w_gla_t, w_gla_gu, b_gate)
    w1_1, w2_1 = mlp_w1[1].astype(BF16), mlp_w2[1].astype(BF16)
    dec_rows = _gla_proj(x1_s, mods_s[2], *gla_w, F32)
    y_p, s_p, o, s_s = _gla_layer(x1_p, mods_p[2], mods_p[3], *gla_w, norm_g, w_gla_out,
                                  w1_1, w2_1, ln_g4, ln_b4, 1, dec_rows, state_gla[0])
    y_s = _mix_mlp_stream(o, x1_s, mods_s[2], mods_s[3], w_gla_out, w1_1, w2_1,
                          ln_g4, ln_b4, 1)
    k_s, v_s = from_slab(k_s), from_slab(v_s)

    kv_shape_p = (1, 1, win_p, N_KV_HEADS, HEAD_DIM)
    return (y_p[None], y_s.reshape(nseq, n_new, D_MODEL),
            k_p.reshape(kv_shape_p), v_p.reshape(kv_shape_p), s_p[None, None],
            k_s, v_s, s_s[None])
```

```python
import functools

import jax
import jax.numpy as jnp
from jax import lax
from jax.experimental import pallas as pl
from jax.experimental.pallas import tpu as pltpu

F32 = jnp.float32
BF16 = jnp.bfloat16

D_MODEL = 1024
DEPTH = 2
HEAD_DIM = 64
N_Q_HEADS = 16
N_KV_HEADS = 4
GROUP = 4
WINDOW = 128
ATTN_BLOCK = 128
GLA_HEADS = 4
GLA_DK = 128
GLA_DV = 256
GLA_GATE_RANK = 16
GLA_TAU = 16.0
GLA_CHUNK = 64
D_FF = 4 * D_MODEL
ALPHA = (2.0 * DEPTH) ** 0.25
LN_EPS = 1e-5

NQ = N_Q_HEADS * HEAD_DIM
NKV = N_KV_HEADS * HEAD_DIM
NK = GLA_HEADS * GLA_DK
NV = GLA_HEADS * GLA_DV
LANES = 128
SUBLANES = 8
NEG_BIG = -1e30

ROW_TILE = 512
FF_CHUNK = 1024
ATTN_TILE = 256
GLA_TILE = 256
VMEM_LIMIT = 56 * 1024 * 1024


def _cparams(n_axes):
    return pltpu.CompilerParams(
        dimension_semantics=("arbitrary",) * n_axes,
        vmem_limit_bytes=VMEM_LIMIT,
    )


def _full(shape):
    zeros = (0,) * len(shape)
    return pl.BlockSpec(shape, lambda *_: zeros)


def _layer(arr, idx):
    tail = (0,) * (arr.ndim - 1)
    return pl.BlockSpec((None,) + arr.shape[1:], lambda *_: (idx,) + tail,
                        pipeline_mode=pl.Buffered(1))


def _resident(w, idx):
    return _full(w.shape) if w.ndim == 2 else _layer(w, idx)


def _row_spec(tm, n):
    return pl.BlockSpec((tm, n), lambda i: (i, 0))


class _Mod:
    def __init__(self, arr, p, per_row, row0=0):
        self.arr, self.p, self.per_row, self.row0 = arr, p, per_row, row0

    def spec(self, tm, col):
        p = self.p
        if self.per_row:
            return pl.BlockSpec((None, tm, D_MODEL), lambda i: (p, i, col))
        blk = self.row0 // SUBLANES
        return pl.BlockSpec((None, SUBLANES, D_MODEL), lambda i: (p, blk, col))


def _mod_rows(ref, tm):
    return ref[...] if ref.shape[0] == tm else ref[0:1, :]


def _modulate(x, shift, scale):
    return x * (1.0 + scale) + shift


def _res_ln(x, gate, o, g, b):
    y = ALPHA * x + gate * o
    mu = jnp.mean(y, axis=-1, keepdims=True)
    yc = y - mu
    var = jnp.mean(yc * yc, axis=-1, keepdims=True)
    return yc * lax.rsqrt(var + LN_EPS) * g + b


def _mod_kernel(c_ref, w_ref, b_ref, o_ref):
    c = c_ref[...]
    a = (c * jax.nn.sigmoid(c)).astype(BF16)
    o_ref[...] = jnp.dot(a, w_ref[...].astype(BF16), preferred_element_type=F32) + b_ref[...]


def _adaln_all(c_all, w_mod, b_mod):
    rows = c_all.shape[0]
    tn = 1024
    return pl.pallas_call(
        _mod_kernel,
        grid=(4, 3 * D_MODEL // tn),
        in_specs=[
            pl.BlockSpec((rows, D_MODEL), lambda p, n: (0, 0)),
            pl.BlockSpec((None, D_MODEL, tn), lambda p, n: (p, 0, n)),
            pl.BlockSpec((None, 1, tn), lambda p, n: (p, 0, n)),
        ],
        out_specs=pl.BlockSpec((None, rows, tn), lambda p, n: (p, 0, n)),
        out_shape=jax.ShapeDtypeStruct((4, rows, 3 * D_MODEL), F32),
        compiler_params=_cparams(2),
        name="adaln_mod",
    )(c_all, w_mod, b_mod)


def _attn_proj_kernel(x_ref, sh_ref, sc_ref, w_ref, q_ref, k_ref, v_ref):
    tm = x_ref.shape[0]
    h = _modulate(x_ref[...], _mod_rows(sh_ref, tm), _mod_rows(sc_ref, tm)).astype(BF16)
    q = jnp.dot(h, w_ref[:, 0:NQ], preferred_element_type=F32)
    q_ref[...] = (q * (HEAD_DIM ** -0.5)).astype(BF16)
    k_ref[...] = jnp.dot(h, w_ref[:, NQ:NQ + NKV], preferred_element_type=F32)
    v_ref[...] = jnp.dot(h, w_ref[:, NQ + NKV:NQ + 2 * NKV], preferred_element_type=F32)


def _attn_proj(x, mod, w_in):
    m = x.shape[0]
    tm = min(ROW_TILE, m)
    row = functools.partial(_row_spec, tm)
    return pl.pallas_call(
        _attn_proj_kernel,
        grid=(m // tm,),
        in_specs=[row(D_MODEL), mod.spec(tm, 0), mod.spec(tm, 1), _full(w_in.shape)],
        out_specs=[row(NQ), row(NKV), row(NKV)],
        out_shape=[
            jax.ShapeDtypeStruct((m, NQ), BF16),
            jax.ShapeDtypeStruct((m, NKV), F32),
            jax.ShapeDtypeStruct((m, NKV), F32),
        ],
        compiler_params=_cparams(1),
        name="attn_proj",
    )(x, mod.arr, mod.arr, w_in)


def _gla_project(h, wt_ref, wgu_ref, bg_ref, vr_dtype):
    def proj(lo, hi):
        return lax.dot_general(h, wt_ref[lo:hi, :], (((1,), (1,)), ((), ())),
                               preferred_element_type=F32)

    n_all = wt_ref.shape[0]
    gdown = proj(n_all - LANES, n_all)
    q = proj(0, NK) * (GLA_DK ** -0.5)
    k = proj(NK, 2 * NK)
    pre = jnp.dot(gdown.astype(BF16), wgu_ref[...], preferred_element_type=F32) + bg_ref[...]
    v = proj(2 * NK, 2 * NK + NV).astype(vr_dtype)
    r = proj(2 * NK + NV, 2 * NK + 2 * NV).astype(vr_dtype)
    log_sig = jnp.minimum(pre, 0.0) - jnp.log1p(jnp.exp(-jnp.abs(pre)))
    return q, k, log_sig / GLA_TAU, v, r


def _gla_proj_kernel(x_ref, sh_ref, sc_ref, wt_ref, wgu_ref, bg_ref,
                     q_ref, k_ref, lg_ref, v_ref, r_ref):
    tm = x_ref.shape[0]
    h = _modulate(x_ref[...], _mod_rows(sh_ref, tm), _mod_rows(sc_ref, tm)).astype(BF16)
    q_ref[...], k_ref[...], lg_ref[...], v_ref[...], r_ref[...] = _gla_project(
        h, wt_ref, wgu_ref, bg_ref, v_ref.dtype)


def _gla_proj(x, mod, w_t, w_gu, b_gate, vr_dtype):
    m = x.shape[0]
    tm = min(ROW_TILE, m)
    row = functools.partial(_row_spec, tm)
    return pl.pallas_call(
        _gla_proj_kernel,
        grid=(m // tm,),
        in_specs=[row(D_MODEL), mod.spec(tm, 0), mod.spec(tm, 1), _full(w_t.shape),
                  _full(w_gu.shape), _full(b_gate.shape)],
        out_specs=[row(NK), row(NK), row(NK), row(NV), row(NV)],
        out_shape=[
            jax.ShapeDtypeStruct((m, NK), F32),
            jax.ShapeDtypeStruct((m, NK), F32),
            jax.ShapeDtypeStruct((m, NK), F32),
            jax.ShapeDtypeStruct((m, NV), vr_dtype),
            jax.ShapeDtypeStruct((m, NV), vr_dtype),
        ],
        compiler_params=_cparams(1),
        name="gla_proj",
    )(x, mod.arr, mod.arr, w_t, w_gu, b_gate)


def _mix_mlp_stream_kernel(a_ref, x_ref, gt0_ref, sh_ref, sc_ref, gt1_ref, wo_ref, w1_ref, w2_ref,
                           g0_ref, b0_ref, g1_ref, b1_ref, y_ref, x1_s, h_s, acc_ref):
    c = pl.program_id(0)

    @pl.when(c == 0)
    def _():
        o = jnp.dot(a_ref[...].astype(BF16), wo_ref[...], preferred_element_type=F32)
        x1 = _res_ln(x_ref[...], gt0_ref[...], o, g0_ref[...], b0_ref[...])
        x1_s[...] = x1
        h_s[...] = _modulate(x1, sh_ref[...], sc_ref[...]).astype(BF16)
        acc_ref[...] = jnp.zeros_like(acc_ref)

    a = jnp.dot(h_s[...], w1_ref[...], preferred_element_type=F32)
    a = jnp.square(jnp.maximum(a, 0.0)).astype(BF16)
    acc_ref[...] += jnp.dot(a, w2_ref[...], preferred_element_type=F32)

    @pl.when(c == pl.num_programs(0) - 1)
    def _():
        y_ref[...] = _res_ln(x1_s[...], gt1_ref[...], acc_ref[...], g1_ref[...], b1_ref[...])


def _mix_mlp_stream(a, x, mod_mix, mod_mlp, w_out, w1, w2, ln_g, ln_b, layer):
    m = x.shape[0]
    assert m <= ROW_TILE and mod_mix.per_row and mod_mlp.per_row
    fc = FF_CHUNK
    whole = lambda n: pl.BlockSpec((m, n), lambda c: (0, 0))
    mspec = lambda mod, col: pl.BlockSpec((None, m, D_MODEL), lambda c: (mod.p, 0, col))
    if w1.ndim == 2:
        w1_cols = pl.BlockSpec((D_MODEL, fc), lambda c: (0, c))
        w2_rows = pl.BlockSpec((fc, D_MODEL), lambda c: (c, 0))
    else:
        w1_cols = pl.BlockSpec((None, D_MODEL, fc), lambda c: (layer, 0, c))
        w2_rows = pl.BlockSpec((None, fc, D_MODEL), lambda c: (layer, c, 0))
    return pl.pallas_call(
        _mix_mlp_stream_kernel,
        grid=(D_FF // fc,),
        in_specs=[whole(a.shape[1]), whole(D_MODEL), mspec(mod_mix, 2), mspec(mod_mlp, 0),
                  mspec(mod_mlp, 1), mspec(mod_mlp, 2), _full(w_out.shape),
                  w1_cols, w2_rows,
                  _layer(ln_g, 2 * layer), _layer(ln_b, 2 * layer),
                  _layer(ln_g, 2 * layer + 1), _layer(ln_b, 2 * layer + 1)],
        out_specs=whole(D_MODEL),
        out_shape=jax.ShapeDtypeStruct((m, D_MODEL), F32),
        scratch_shapes=[pltpu.VMEM((m, D_MODEL), F32), pltpu.VMEM((m, D_MODEL), BF16),
                        pltpu.VMEM((m, D_MODEL), F32)],
        compiler_params=_cparams(1),
        name="mix_mlp_stream",
    )(a, x, mod_mix.arr, mod_mlp.arr, mod_mlp.arr, mod_mlp.arr, w_out, w1, w2,
      ln_g, ln_b, ln_g, ln_b)


def _alibi_slope(head):
    return 2.0 ** (-8.0 * (head + 1) / N_Q_HEADS)


def _softmax_sink(s, sink):
    m = jnp.maximum(jnp.max(s, axis=-1, keepdims=True), sink)
    e = jnp.exp(s - m)
    den = jnp.sum(e, axis=-1, keepdims=True) + jnp.exp(sink - m)
    return e / den


def _band_bias_init(bias_ref):
    blk = ATTN_BLOCK
    c = lax.broadcasted_iota(jnp.int32, (2 * blk, blk), 0)
    r = lax.broadcasted_iota(jnp.int32, (2 * blk, blk), 1)
    dist = blk + r - c
    valid = (dist >= 0) & (dist <= WINDOW)
    distf = dist.astype(F32)
    for head in range(N_Q_HEADS):
        pen = -_alibi_slope(head) * distf
        bias_ref[0, head] = jnp.where(valid, pen, NEG_BIG)
        bias_ref[1, head] = jnp.where(valid & (c >= blk), pen, NEG_BIG)


def _band_scores(q, kk):
    blk = ATTN_BLOCK
    head_of_lane = lax.broadcasted_iota(jnp.int32, (blk, NKV), 1) // HEAD_DIM
    scores = []
    for b in range(q.shape[0] // blk):
        keys = kk[b * blk:(b + 2) * blk]
        for g in range(GROUP):
            qg = q[b * blk:(b + 1) * blk, g * NKV:(g + 1) * NKV]
            qm = jnp.concatenate(
                [jnp.where(head_of_lane == j, qg, jnp.zeros_like(qg))
                 for j in range(N_KV_HEADS)], axis=0)
            scores.append(lax.dot_general(keys, qm, (((1,), (1,)), ((), ())),
                                          preferred_element_type=F32))
    return scores


def _band_outputs(scores, vvt, first_tile, sinks_ref, bias_ref, o_ref):
    blk = ATTN_BLOCK
    for b in range(len(scores) // GROUP):
        table = 1 if first_tile and b == 0 else 0
        vals_t = vvt[:, b * blk:(b + 2) * blk]
        for g in range(GROUP):
            st_all = scores[b * GROUP + g]
            ps = []
            for j in range(N_KV_HEADS):
                head = j * GROUP + g
                sink = sinks_ref[head]
                st = st_all[:, j * blk:(j + 1) * blk] + bias_ref[table, head]
                m = jnp.maximum(jnp.max(st, axis=0, keepdims=True), sink)
                e = jnp.exp(st - m)
                den = jnp.sum(e, axis=0, keepdims=True) + jnp.exp(sink - m)
                ps.append((e * (1.0 / den)).astype(BF16))
            ot_all = jnp.dot(vals_t, jnp.concatenate(ps, axis=1),
                             preferred_element_type=F32)
            ot = jnp.concatenate(
                [ot_all[j * HEAD_DIM:(j + 1) * HEAD_DIM, j * blk:(j + 1) * blk]
                 for j in range(N_KV_HEADS)], axis=0)
            o_ref[b * blk:(b + 1) * blk, g * NKV:(g + 1) * NKV] = ot.T.astype(BF16)


def _mlp_chunks(h, w1_ref, w2_ref, acc_ref, chunks):
    for c in chunks:
        cols = slice(c * FF_CHUNK, (c + 1) * FF_CHUNK)
        a = jnp.dot(h, w1_ref[:, cols], preferred_element_type=F32)
        a = jnp.square(jnp.maximum(a, 0.0)).astype(BF16)
        d = jnp.dot(a, w2_ref[cols, :], preferred_element_type=F32)
        if c == 0:
            acc_ref[...] = d
        else:
            acc_ref[...] += d


def _attn_layer_kernel(sinks_ref, xc_ref, xp_ref, sh0_ref, sc0_ref, gt0_ref, sh1_ref, sc1_ref,
                       gt1_ref, win_ref, wo_ref, w1_ref, w2_ref, g0_ref, b0_ref, g1_ref, b1_ref,
                       dq_ref, dkn_ref, dvn_ref, dck_ref, dcv_ref, cw1_ref, cw2_ref,
                       y_ref, kl_ref, vl_ref, do_ref, dnk_ref, dnv_ref, cw1b_ref, cw2b_ref,
                       o_s, kprev_s, vtprev_s, acc_ref, bias_ref, zk_ref, zv_ref):
    i = pl.program_id(0)
    last = pl.num_programs(0) - 1
    dec = _DecAttn(sinks_ref, dq_ref, dkn_ref, dvn_ref, dck_ref, dcv_ref, do_ref, dnk_ref,
                   dnv_ref, zk_ref, zv_ref)

    def cast_weights():
        cw1b_ref[...] = cw1_ref[...].astype(BF16)
        cw2b_ref[...] = cw2_ref[...].astype(BF16)
    slot = i % 2
    blk = ATTN_BLOCK
    tm = xc_ref.shape[0]
    dot = functools.partial(jnp.dot, preferred_element_type=F32)
    n_chunks = D_FF // FF_CHUNK

    def mlp_in():
        x1 = _res_ln(xp_ref[...], gt0_ref[0:1, :], dot(o_s[1 - slot], wo_ref[...]),
                     g0_ref[...], b0_ref[...])
        return x1, _modulate(x1, sh1_ref[0:1, :], sc1_ref[0:1, :]).astype(BF16)

    def mlp_out(x1):
        y_ref[...] = _res_ln(x1, gt1_ref[0:1, :], acc_ref[...], g1_ref[...], b1_ref[...])

    def mix_in():
        h_in = _modulate(xc_ref[...], sh0_ref[0:1, :], sc0_ref[0:1, :]).astype(BF16)
        q = (dot(h_in, win_ref[:, 0:NQ]) * (HEAD_DIM ** -0.5)).astype(BF16)
        k = dot(h_in, win_ref[:, NQ:NQ + NKV])
        v = dot(h_in, win_ref[:, NQ + NKV:NQ + 2 * NKV])
        kl_ref[...] = k[tm - blk:, :]
        vl_ref[...] = v[tm - blk:, :]
        k_bf = k.astype(BF16)
        vt = v.T.astype(BF16)
        kk = jnp.concatenate([kprev_s[...], k_bf], axis=0)
        vvt = jnp.concatenate([vtprev_s[...], vt], axis=1)
        scores = _band_scores(q, kk)
        kprev_s[...] = k_bf[tm - blk:, :]
        vtprev_s[...] = vt[:, tm - blk:]
        return scores, vvt

    @pl.when(i == 0)
    def _():
        _band_bias_init(bias_ref)
        kprev_s[...] = jnp.zeros_like(kprev_s)
        vtprev_s[...] = jnp.zeros_like(vtprev_s)
        dec.zero()
        cast_weights()
        scores, vvt = mix_in()
        dec_scored = dec.scores()
        _band_outputs(scores, vvt, True, sinks_ref, bias_ref, o_s.at[slot])
        dec.finish(dec_scored)

    @pl.when((i > 0) & (i < last))
    def _():
        x1, h_mlp = mlp_in()
        cast_weights()
        scores, vvt = mix_in()
        _mlp_chunks(h_mlp, w1_ref, w2_ref, acc_ref, range(0, 1))
        dec_scored = dec.scores()
        _mlp_chunks(h_mlp, w1_ref, w2_ref, acc_ref, range(1, n_chunks))
        _band_outputs(scores, vvt, False, sinks_ref, bias_ref, o_s.at[slot])
        dec.finish(dec_scored)
        mlp_out(x1)

    @pl.when(i == last)
    def _():
        x1, h_mlp = mlp_in()
        _mlp_chunks(h_mlp, w1_ref, w2_ref, acc_ref, range(n_chunks))
        mlp_out(x1)


def _attn_layer(x, mod_mix, mod_mlp, w_in, w_out, w1, w2, ln_g, ln_b, sinks, layer,
                dec_q, dec_k, dec_v, cache_kt, cache_vt, w1_f32, w2_f32, cast_layer):
    m = x.shape[0]
    tm = ATTN_TILE
    blk = ATTN_BLOCK
    n = m // tm
    assert not mod_mix.per_row and not mod_mlp.per_row
    nseq, win = cache_kt.shape[0], cache_kt.shape[2]
    sb = nseq // n
    n_new = dec_k.shape[0] // nseq
    assert sb * n == nseq and (sb * n_new) % SUBLANES == 0
    cur = pl.BlockSpec((tm, D_MODEL), lambda i: (jnp.minimum(i, n - 1), 0))
    prev = pl.BlockSpec((tm, D_MODEL), lambda i: (jnp.maximum(i - 1, 0), 0))
    last = pl.BlockSpec((blk, NKV), lambda i: (0, 0))
    dblk = lambda a: pl.BlockSpec((sb,) + a.shape[1:], lambda i: (jnp.minimum(i, n - 1), 0, 0))
    drow = lambda a: pl.BlockSpec((sb * n_new, a.shape[1]), lambda i: (jnp.minimum(i, n - 1), 0))
    c_in = lambda w: pl.BlockSpec((None, w.shape[1] // n, w.shape[2]),
                                  lambda i: (cast_layer, jnp.minimum(i, n - 1), 0))
    c_out = lambda w: pl.BlockSpec((w.shape[1] // n, w.shape[2]),
                                   lambda i: (jnp.minimum(i, n - 1), 0))
    assert all(w.shape[1] % (n * 2 * SUBLANES) == 0 for w in (w1_f32, w2_f32))
    return pl.pallas_call(
        _attn_layer_kernel,
        grid=(n + 1,),
        in_specs=[pl.BlockSpec(memory_space=pltpu.SMEM), cur, prev,
                  mod_mix.spec(tm, 0), mod_mix.spec(tm, 1), mod_mix.spec(tm, 2),
                  mod_mlp.spec(tm, 0), mod_mlp.spec(tm, 1), mod_mlp.spec(tm, 2),
                  _full(w_in.shape), _full(w_out.shape), _resident(w1, layer),
                  _resident(w2, layer),
                  _layer(ln_g, 2 * layer), _layer(ln_b, 2 * layer),
                  _layer(ln_g, 2 * layer + 1), _layer(ln_b, 2 * layer + 1),
                  dblk(dec_q), drow(dec_k), drow(dec_v), dblk(cache_kt), dblk(cache_vt),
                  c_in(w1_f32), c_in(w2_f32)],
        out_specs=[prev, last, last, dblk(dec_q), dblk(cache_kt), dblk(cache_vt),
                   c_out(w1_f32), c_out(w2_f32)],
        out_shape=[jax.ShapeDtypeStruct((m, D_MODEL), F32),
                   jax.ShapeDtypeStruct((blk, NKV), F32),
                   jax.ShapeDtypeStruct((blk, NKV), F32),
                   jax.ShapeDtypeStruct(dec_q.shape, BF16),
                   jax.ShapeDtypeStruct(cache_kt.shape, F32),
                   jax.ShapeDtypeStruct(cache_vt.shape, F32),
                   jax.ShapeDtypeStruct(w1_f32.shape[1:], BF16),
                   jax.ShapeDtypeStruct(w2_f32.shape[1:], BF16)],
        scratch_shapes=[pltpu.VMEM((2, tm, NQ), BF16),
                        pltpu.VMEM((blk, NKV), BF16),
                        pltpu.VMEM((NKV, blk), BF16),
                        pltpu.VMEM((tm, D_MODEL), F32),
                        pltpu.VMEM((2, N_Q_HEADS, 2 * blk, blk), F32),
                        pltpu.VMEM((sb, win, NKV), F32),
                        pltpu.VMEM((sb, win, NKV), F32)],
        compiler_params=_cparams(1),
        name="attn_layer",
    )(sinks, x, x, mod_mix.arr, mod_mix.arr, mod_mix.arr, mod_mlp.arr, mod_mlp.arr,
      mod_mlp.arr, w_in, w_out, w1, w2, ln_g, ln_b, ln_g, ln_b,
      dec_q, dec_k, dec_v, cache_kt, cache_vt, w1_f32, w2_f32)


class _DecAttn:
    def __init__(self, sinks_ref, q_ref, kn_ref, vn_ref, ck_ref, cv_ref, o_ref, nk_ref, nv_ref,
                 zk_ref, zv_ref):
        self.refs = (sinks_ref, q_ref, kn_ref, vn_ref, ck_ref, cv_ref, o_ref, nk_ref, nv_ref,
                     zk_ref, zv_ref)
        self.n_seq = ck_ref.shape[0]
        self.n_new = kn_ref.shape[0] // self.n_seq
        self.win = ck_ref.shape[2]
        assert self.win == LANES and self.n_new < SUBLANES

    def zero(self):
        for ref in self.refs[9:]:
            ref[...] = jnp.zeros_like(ref)

    def scores(self):
        (sinks_ref, q_ref, kn_ref, vn_ref, ck_ref, cv_ref, _, nk_ref, nv_ref,
         zk_ref, zv_ref) = self.refs
        n_new, win = self.n_new, self.win
        rows = n_new * GROUP
        keep = win - n_new
        head_of_lane = lax.broadcasted_iota(jnp.int32, (rows, NKV), 1) // HEAD_DIM
        lane_w = lax.broadcasted_iota(jnp.int32, (NKV, win), 1)
        out = []
        for s in range(self.n_seq):
            tok = slice(s * n_new, (s + 1) * n_new)
            k_t = ck_ref[s]
            v_t = cv_ref[s]
            zk_ref[s, keep:win, :] = kn_ref[tok, :]
            zv_ref[s, keep:win, :] = vn_ref[tok, :]
            zk_t = zk_ref[s].T
            zv_t = zv_ref[s].T
            nk_ref[s] = jnp.where(lane_w < keep, pltpu.roll(k_t, keep, 1), zk_t)
            nv_ref[s] = jnp.where(lane_w < keep, pltpu.roll(v_t, keep, 1), zv_t)
            keys = jnp.concatenate([k_t, zk_t], axis=1).astype(BF16)
            vals = jnp.concatenate([v_t, zv_t], axis=1).astype(BF16)
            qs = q_ref[s]
            qbd = jnp.concatenate(
                [jnp.where(head_of_lane == j, qs, jnp.zeros_like(qs))
                 for j in range(N_KV_HEADS)], axis=0)
            out.append((jnp.dot(qbd, keys, preferred_element_type=F32), vals))
        return out

    def finish(self, scored):
        sinks_ref, o_ref = self.refs[0], self.refs[6]
        n_new, win = self.n_new, self.win
        rows = n_new * GROUP
        n_rows = N_KV_HEADS * rows
        keep = win - n_new
        row = lax.broadcasted_iota(jnp.int32, (n_rows, 2 * win), 0)
        col = lax.broadcasted_iota(jnp.int32, (n_rows, 2 * win), 1)
        j_r = row // rows
        t_r = (row // GROUP) % n_new
        g_r = row % GROUP
        h_r = j_r * GROUP + g_r
        slope = jnp.exp2(-8.0 * (h_r + 1).astype(F32) / N_Q_HEADS)
        sink = jnp.zeros((n_rows, 1), F32)
        h_col = h_r[:, 0:1]
        for h in range(N_Q_HEADS):
            sink = jnp.where(h_col == h, sinks_ref[h], sink)
        is_key = (col < win) | (col >= win + keep)
        frame = jnp.where(col < win, col, col - keep)
        dist = t_r + win - frame
        valid = is_key & (dist >= 0) & (dist <= WINDOW)
        bias = jnp.where(valid, -slope * dist.astype(F32), NEG_BIG)
        head_of_lane = lax.broadcasted_iota(jnp.int32, (rows, NKV), 1) // HEAD_DIM
        for s, (sc, vals) in enumerate(scored):
            p = _softmax_sink(sc + bias, sink).astype(BF16)
            pv = lax.dot_general(p, vals, (((1,), (1,)), ((), ())),
                                 preferred_element_type=F32)
            o = jnp.zeros((rows, NKV), F32)
            for j in range(N_KV_HEADS):
                o = o + jnp.where(head_of_lane == j, pv[j * rows:(j + 1) * rows], 0.0)
            o_ref[s] = o.astype(BF16)


def _split2(x):
    hi = x.astype(BF16)
    lo = (x - hi.astype(F32)).astype(BF16)
    return hi, lo


def _gla_out(o, r, norm_g):
    ms = jnp.mean(o * o, axis=-1, keepdims=True)
    o = o * lax.rsqrt(ms + LN_EPS) * norm_g
    return o * (r * jax.nn.sigmoid(r))


def _gla_layer_kernel(xc_ref, xp_ref, sh0_ref, sc0_ref, gt0_ref, sh1_ref, sc1_ref, gt1_ref,
                      wt_ref, wgu_ref, bg_ref, ng_ref, wo_ref, w1_ref, w2_ref,
                      g0_ref, b0_ref, g1_ref, b1_ref,
                      dq_ref, dk_ref, dlg_ref, dv_ref, dr_ref, ds0_ref,
                      y_ref, s_out_ref, do_ref, ds1_ref,
                      o_s, s_ref, acc_ref, z_ref, v_all_ref, q_pad_ref):
    i = pl.program_id(0)
    last = pl.num_programs(0) - 1
    dec = _GlaStep(dq_ref, dk_ref, dlg_ref, dv_ref, dr_ref, ng_ref, ds0_ref, do_ref, ds1_ref,
                   z_ref, v_all_ref, q_pad_ref)
    sub = xc_ref.shape[0]
    ch = GLA_CHUNK
    n_ch = sub // ch
    n_chunks = D_FF // FF_CHUNK
    dot = functools.partial(jnp.dot, preferred_element_type=F32)
    heads = range(GLA_HEADS)
    ks = [slice(h * GLA_DK, (h + 1) * GLA_DK) for h in heads]
    vs = [slice(h * GLA_DV, (h + 1) * GLA_DV) for h in heads]

    def mlp_in():
        x1 = _res_ln(xp_ref[...], gt0_ref[0:1, :], dot(o_s[...], wo_ref[...]),
                     g0_ref[...], b0_ref[...])
        return x1, _modulate(x1, sh1_ref[0:1, :], sc1_ref[0:1, :]).astype(BF16)

    def mlp_out(x1):
        y_ref[...] = _res_ln(x1, gt1_ref[0:1, :], acc_ref[...], g1_ref[...], b1_ref[...])

    def decays():
        h_in = _modulate(xc_ref[...], sh0_ref[0:1, :], sc0_ref[0:1, :]).astype(BF16)
        q, k, lg, v, r = _gla_project(h_in, wt_ref, wgu_ref, bg_ref, BF16)
        row = lax.broadcasted_iota(jnp.int32, (sub, sub), 0)
        col = lax.broadcasted_iota(jnp.int32, (sub, sub), 1)
        causal = ((row // ch) == (col // ch)) & (col <= row)
        tril = jnp.where(causal, 1.0, 0.0).astype(BF16)
        hi, lo = _split2(lg)
        b = dot(tril, hi) + dot(tril, lo)
        ends = [b[(c + 1) * ch - 1:(c + 1) * ch, :] for c in range(n_ch)]
        b_end = jnp.concatenate([jnp.broadcast_to(e, (ch, NK)) for e in ends], axis=0)
        qd = (q * jnp.exp(b)).astype(BF16)
        kd = (k * jnp.exp(-b)).astype(BF16)
        kdec = k * jnp.exp(b_end - b)
        dec_rows = jnp.concatenate(
            [jnp.exp(e) for e in ends] + [jnp.zeros((LANES - n_ch, NK), F32)], axis=0)
        return causal, qd, kd, kdec, dec_rows, v, r

    def chunk_products(causal, qd, kd, kdec, v):
        chunk_of_col = lax.broadcasted_iota(jnp.int32, (GLA_DK, sub), 1) // ch
        a = [lax.dot_general(qd[:, ks[h]], kd[:, ks[h]], (((1,), (1,)), ((), ())),
                             preferred_element_type=F32) for h in heads]
        u = []
        for h in heads:
            kdec_t = kdec[:, ks[h]].T.astype(BF16)
            stacked = jnp.concatenate(
                [jnp.where(chunk_of_col == c, kdec_t, jnp.zeros_like(kdec_t))
                 for c in range(n_ch)], axis=0)
            u.append(dot(stacked, v[:, vs[h]]))
        o_intra = [dot(jnp.where(causal, a[h], 0.0).astype(BF16), v[:, vs[h]]) for h in heads]
        return u, o_intra

    def recurrence(qd, dec_rows, u, o_intra, r):
        for h in heads:
            s = s_ref[h]
            dec_t = dec_rows[:, ks[h]].T
            o_inter = []
            for c in range(n_ch):
                o_inter.append(dot(qd[c * ch:(c + 1) * ch, ks[h]], s.astype(BF16)))
                s = dec_t[:, c:c + 1] * s + u[h][c * GLA_DK:(c + 1) * GLA_DK]
            s_ref[h] = s
            o = o_intra[h] + jnp.concatenate(o_inter, axis=0)
            o_s[:, vs[h]] = _gla_out(o, r[:, vs[h]].astype(F32), ng_ref[...]).astype(BF16)

    @pl.when(i == 0)
    def _():
        s_ref[...] = jnp.zeros_like(s_ref)
        dec.zero()
        dec_intra = dec.stage()
        causal, qd, kd, kdec, dec_rows, v, r = decays()
        u, o_intra = chunk_products(causal, qd, kd, kdec, v)
        recurrence(qd, dec_rows, u, o_intra, r)
        dec.finish(dec_intra)

    @pl.when((i > 0) & (i < last))
    def _():
        x1, h_mlp = mlp_in()
        dec_intra = dec.stage()
        causal, qd, kd, kdec, dec_rows, v, r = decays()
        _mlp_chunks(h_mlp, w1_ref, w2_ref, acc_ref, range(0, 1))
        u, o_intra = chunk_products(causal, qd, kd, kdec, v)
        _mlp_chunks(h_mlp, w1_ref, w2_ref, acc_ref, range(1, n_chunks))
        recurrence(qd, dec_rows, u, o_intra, r)
        dec.finish(dec_intra)
        mlp_out(x1)

    @pl.when(i == last)
    def _():
        x1, h_mlp = mlp_in()
        _mlp_chunks(h_mlp, w1_ref, w2_ref, acc_ref, range(n_chunks))
        mlp_out(x1)
        s_out_ref[...] = s_ref[...]


def _gla_layer(x, mod_mix, mod_mlp, w_t, w_gu, b_gate, norm_g, w_out, w1, w2, ln_g, ln_b, layer,
               dec_rows, dec_state):
    m = x.shape[0]
    tm = GLA_TILE
    n = m // tm
    assert not mod_mix.per_row and not mod_mlp.per_row and tm % GLA_CHUNK == 0
    nseq = dec_state.shape[0]
    sb = nseq // n
    n_new = dec_rows[0].shape[0] // nseq
    assert sb * n == nseq and (sb * n_new) % SUBLANES == 0 and n_new <= GLA_CHUNK
    cur = pl.BlockSpec((tm, D_MODEL), lambda i: (jnp.minimum(i, n - 1), 0))
    prev = pl.BlockSpec((tm, D_MODEL), lambda i: (jnp.maximum(i - 1, 0), 0))
    drow = lambda a: pl.BlockSpec((sb * n_new, a.shape[1]), lambda i: (jnp.minimum(i, n - 1), 0))
    dstate = pl.BlockSpec((sb,) + dec_state.shape[1:],
                          lambda i: (jnp.minimum(i, n - 1), 0, 0, 0))
    state = (GLA_HEADS, GLA_DK, GLA_DV)
    dv = dec_rows[3]
    return pl.pallas_call(
        _gla_layer_kernel,
        grid=(n + 1,),
        in_specs=[cur, prev,
                  mod_mix.spec(tm, 0), mod_mix.spec(tm, 1), mod_mix.spec(tm, 2),
                  mod_mlp.spec(tm, 0), mod_mlp.spec(tm, 1), mod_mlp.spec(tm, 2),
                  _full(w_t.shape), _full(w_gu.shape), _full(b_gate.shape), _full(norm_g.shape),
                  _full(w_out.shape), _resident(w1, layer), _resident(w2, layer),
                  _layer(ln_g, 2 * layer), _layer(ln_b, 2 * layer),
                  _layer(ln_g, 2 * layer + 1), _layer(ln_b, 2 * layer + 1)]
                 + [drow(a) for a in dec_rows] + [dstate],
        out_specs=[prev, _full(state), drow(dv), dstate],
        out_shape=[jax.ShapeDtypeStruct((m, D_MODEL), F32), jax.ShapeDtypeStruct(state, F32),
                   jax.ShapeDtypeStruct(dv.shape, F32),
                   jax.ShapeDtypeStruct(dec_state.shape, F32)],
        scratch_shapes=[pltpu.VMEM((tm, NV), BF16),
                        pltpu.VMEM(state, F32),
                        pltpu.VMEM((tm, D_MODEL), F32),
                        pltpu.VMEM((GLA_HEADS, LANES, GLA_DK), F32),
                        pltpu.VMEM((GLA_HEADS, LANES, GLA_DV), F32),
                        pltpu.VMEM((sb, GLA_HEADS, SUBLANES, GLA_DK), F32)],
        compiler_params=_cparams(1),
        name="gla_layer",
    )(x, x, mod_mix.arr, mod_mix.arr, mod_mix.arr, mod_mlp.arr, mod_mlp.arr, mod_mlp.arr,
      w_t, w_gu, b_gate, norm_g, w_out, w1, w2, ln_g, ln_b, ln_g, ln_b, *dec_rows, dec_state)


class _GlaStep:
    def __init__(self, q_ref, k_ref, lg_ref, v_ref, r_ref, ng_ref, s0_ref, o_ref, s1_ref,
                 z_ref, v_all_ref, q_pad_ref):
        self.refs = (q_ref, k_ref, lg_ref, v_ref, r_ref, ng_ref, s0_ref, o_ref, s1_ref,
                     z_ref, v_all_ref, q_pad_ref)
        self.n_seq = s0_ref.shape[0]
        self.n_new = q_ref.shape[0] // self.n_seq
        assert self.n_new < SUBLANES and self.n_seq * SUBLANES <= LANES

    def zero(self):
        for ref in self.refs[9:]:
            ref[...] = jnp.zeros_like(ref)

    def stage(self):
        return _gla_step_stage(self.n_seq, self.n_new, *self.refs)

    def finish(self, o_intra):
        _gla_step_finish(o_intra, self.n_seq, self.n_new, *self.refs)


def _gla_step_stage(n_seq, n_new, q_ref, k_ref, lg_ref, v_ref, r_ref, ng_ref, s0_ref,
                    o_ref, s1_ref, z_ref, v_all_ref, q_pad_ref):
    tok = [slice(s * n_new, (s + 1) * n_new) for s in range(n_seq)]
    grp = SUBLANES
    rowi = lax.broadcasted_iota(jnp.int32, (n_new, NK), 0)
    rowv = lax.broadcasted_iota(jnp.int32, (n_new, GLA_DV), 0)
    heads = range(GLA_HEADS)
    ks = [slice(h * GLA_DK, (h + 1) * GLA_DK) for h in heads]
    vs = [slice(h * GLA_DV, (h + 1) * GLA_DV) for h in heads]
    o_intra = []
    for s in range(n_seq):
        lg = lg_ref[tok[s], :]
        b = jnp.zeros_like(lg)
        for t in range(n_new):
            b = b + jnp.where(rowi >= t, jnp.broadcast_to(lg[t:t + 1, :], lg.shape), 0.0)
        b_end = b[n_new - 1:n_new, :]
        qd = q_ref[tok[s], :] * jnp.exp(b)
        k = k_ref[tok[s], :]
        kd = k * jnp.exp(-b)
        kdec = k * jnp.exp(b_end - b)
        dec = jnp.exp(b_end)
        v = v_ref[tok[s], :]
        row0 = s * grp
        o_s = []
        for h in heads:
            v_h = v[:, vs[h]]
            qd_h = qd[:, ks[h]]
            o = jnp.zeros((n_new, GLA_DV), F32)
            for t in range(n_new):
                a_t = jnp.sum(qd_h * kd[t:t + 1, ks[h]], axis=-1, keepdims=True)
                o = o + jnp.where(rowv >= t, a_t * v_h[t:t + 1, :], 0.0)
            o_s.append(o)
            q_pad_ref[s, h, 0:n_new, :] = qd_h
            z_ref[h, row0:row0 + n_new, :] = kdec[:, ks[h]]
            z_ref[h, row0 + n_new:row0 + n_new + 1, :] = dec[:, ks[h]]
            v_all_ref[h, row0:row0 + n_new, :] = v_h
        o_intra.append(o_s)
    return o_intra


def _gla_step_finish(o_intra, n_seq, n_new, q_ref, k_ref, lg_ref, v_ref, r_ref, ng_ref, s0_ref,
                     o_ref, s1_ref, z_ref, v_all_ref, q_pad_ref):
    tok = [slice(s * n_new, (s + 1) * n_new) for s in range(n_seq)]
    grp = SUBLANES
    dot = functools.partial(jnp.dot, preferred_element_type=F32)
    heads = range(GLA_HEADS)
    vs = [slice(h * GLA_DV, (h + 1) * GLA_DV) for h in heads]
    group_of_lane = lax.broadcasted_iota(jnp.int32, (GLA_DK, LANES), 1) // grp
    for h in heads:
        z_t = z_ref[h].T
        v_all = v_all_ref[h]
        for s in range(n_seq):
            s0 = s0_ref[s, h]
            mine = jnp.where(group_of_lane == s, z_t, 0.0)
            col = s * grp + n_new
            s1_ref[s, h] = z_t[:, col:col + 1] * s0 + dot(mine, v_all)
            o = o_intra[s][h] + dot(q_pad_ref[s, h], s0)[0:n_new]
            o_ref[tok[s], vs[h]] = _gla_out(o, r_ref[tok[s], vs[h]], ng_ref[...])


def kernel(x_prompt, x_sample, cache_k, cache_v, state_gla, c_prompt, c_sample, w_mod, b_mod,
           ln_g, ln_b, attn_w_in, attn_w_out, attn_sinks, gla_w_in, gla_w_gate_up, gla_b_gate,
           gla_norm_g, gla_w_out, mlp_w1, mlp_w2):
    assert x_prompt.shape[0] == 1 and w_mod.shape[0] == DEPTH == 2
    seq = x_prompt.shape[1]
    nseq, n_new = x_sample.shape[0], x_sample.shape[1]
    win = cache_k.shape[2]
    m_s = nseq * n_new

    wq = attn_w_in[0][:, :NQ].reshape(D_MODEL, N_KV_HEADS, GROUP, HEAD_DIM)
    wq = wq.transpose(0, 2, 1, 3).reshape(D_MODEL, NQ)
    w_attn_in = jnp.concatenate([wq, attn_w_in[0][:, NQ:]], axis=1).astype(BF16)
    w_attn_out = attn_w_out[0].reshape(N_KV_HEADS, GROUP, HEAD_DIM, D_MODEL)
    w_attn_out = w_attn_out.transpose(1, 0, 2, 3).reshape(NQ, D_MODEL).astype(BF16)
    w_gla_t = gla_w_in[0].T.astype(BF16)
    assert w_gla_t.shape[0] == 2 * NK + 2 * NV + GLA_GATE_RANK
    w_gla_gu = jnp.pad(gla_w_gate_up[0], ((LANES - GLA_GATE_RANK, 0), (0, 0))).astype(BF16)
    w_gla_out = gla_w_out[0].astype(BF16)
    b_gate = gla_b_gate[0].reshape(1, NK)
    norm_g = gla_norm_g[0].reshape(1, GLA_DV)
    ln_g4 = ln_g.reshape(2 * DEPTH, 1, D_MODEL)
    ln_b4 = ln_b.reshape(2 * DEPTH, 1, D_MODEL)
    sinks = attn_sinks[0]

    pad_rows = (-(m_s + 1)) % SUBLANES
    c_all = jnp.concatenate([jnp.repeat(c_sample, n_new, axis=0), c_prompt,
                             jnp.zeros((pad_rows, D_MODEL), F32)], axis=0)
    mod_all = _adaln_all(c_all, w_mod.reshape(2 * DEPTH, D_MODEL, 3 * D_MODEL),
                         b_mod.reshape(2 * DEPTH, 1, 3 * D_MODEL))
    mods_p = [_Mod(mod_all, p, per_row=False, row0=m_s) for p in range(2 * DEPTH)]
    mods_s = [_Mod(mod_all, p, per_row=True) for p in range(2 * DEPTH)]

    assert seq >= WINDOW == ATTN_BLOCK
    win_p = WINDOW
    to_slab = lambda c: c[0].transpose(0, 2, 3, 1).reshape(nseq, NKV, win)
    from_slab = lambda c: c.reshape(nseq, N_KV_HEADS, HEAD_DIM, win).transpose(0, 3, 1, 2)[None]

    x_s = x_sample.reshape(m_s, D_MODEL)
    w1_0, w2_0 = mlp_w1[0].astype(BF16), mlp_w2[0].astype(BF16)
    q, k, v = _attn_proj(x_s, mods_s[0], w_attn_in)
    x1_p, k_p, v_p, o, k_s, v_s, w1_1, w2_1 = _attn_layer(
        x_prompt[0], mods_p[0], mods_p[1], w_attn_in, w_attn_out, w1_0, w2_0,
        ln_g4, ln_b4, sinks, 0, q.reshape(nseq, n_new * GROUP, NKV), k, v,
        to_slab(cache_k), to_slab(cache_v), mlp_w1, mlp_w2, 1)
    x1_s = _mix_mlp_stream(o.reshape(m_s, NQ), x_s, mods_s[0], mods_s[1], w_attn_out,
                           w1_0, w2_0, ln_g4, ln_b4, 0)

    gla_w = (w_gla_t, w_gla_gu, b_gate)
    dec_rows = _gla_proj(x1_s, mods_s[2], *gla_w, F32)
    y_p, s_p, o, s_s = _gla_layer(x1_p, mods_p[2], mods_p[3], *gla_w, norm_g, w_gla_out,
                                  w1_1, w2_1, ln_g4, ln_b4, 1, dec_rows, state_gla[0])
    y_s = _mix_mlp_stream(o, x1_s, mods_s[2], mods_s[3], w_gla_out, w1_1, w2_1,
                          ln_g4, ln_b4, 1)
    k_s, v_s = from_slab(k_s), from_slab(v_s)

    kv_shape_p = (1, 1, win_p, N_KV_HEADS, HEAD_DIM)
    return (y_p[None], y_s.reshape(nseq, n_new, D_MODEL),
            k_p.reshape(kv_shape_p), v_p.reshape(kv_shape_p), s_p[None, None],
            k_s, v_s, s_s[None])
```

```python
import functools

import jax
import jax.numpy as jnp
from jax import lax
from jax.experimental import pallas as pl
from jax.experimental.pallas import tpu as pltpu

F32 = jnp.float32
BF16 = jnp.bfloat16

D_MODEL = 1024
DEPTH = 2
HEAD_DIM = 64
N_Q_HEADS = 16
N_KV_HEADS = 4
GROUP = 4
WINDOW = 128
ATTN_BLOCK = 128
GLA_HEADS = 4
GLA_DK = 128
GLA_DV = 256
GLA_GATE_RANK = 16
GLA_TAU = 16.0
GLA_CHUNK = 64
D_FF = 4 * D_MODEL
ALPHA = (2.0 * DEPTH) ** 0.25
LN_EPS = 1e-5

NQ = N_Q_HEADS * HEAD_DIM
NKV = N_KV_HEADS * HEAD_DIM
NK = GLA_HEADS * GLA_DK
NV = GLA_HEADS * GLA_DV
LANES = 128
SUBLANES = 8
NEG_BIG = -1e30

ROW_TILE = 512
FF_CHUNK = 1024
ATTN_TILE = 256
GLA_TILE = 256
VMEM_LIMIT = 56 * 1024 * 1024


def _cparams(n_axes):
    return pltpu.CompilerParams(
        dimension_semantics=("arbitrary",) * n_axes,
        vmem_limit_bytes=VMEM_LIMIT,
    )


def _full(shape):
    zeros = (0,) * len(shape)
    return pl.BlockSpec(shape, lambda *_: zeros)


def _layer(arr, idx):
    tail = (0,) * (arr.ndim - 1)
    return pl.BlockSpec((None,) + arr.shape[1:], lambda *_: (idx,) + tail,
                        pipeline_mode=pl.Buffered(1))


def _resident(w, idx):
    return _full(w.shape) if w.ndim == 2 else _layer(w, idx)


def _row_spec(tm, n):
    return pl.BlockSpec((tm, n), lambda i: (i, 0))


class _Mod:
    def __init__(self, arr, p, per_row, row0=0):
        self.arr, self.p, self.per_row, self.row0 = arr, p, per_row, row0

    def spec(self, tm, col):
        p = self.p
        if self.per_row:
            return pl.BlockSpec((None, tm, D_MODEL), lambda i: (p, i, col))
        blk = self.row0 // SUBLANES
        return pl.BlockSpec((None, SUBLANES, D_MODEL), lambda i: (p, blk, col))


def _mod_rows(ref, tm):
    return ref[...] if ref.shape[0] == tm else ref[0:1, :]


def _modulate(x, shift, scale):
    return x * (1.0 + scale) + shift


def _res_ln(x, gate, o, g, b):
    y = ALPHA * x + gate * o
    mu = jnp.mean(y, axis=-1, keepdims=True)
    yc = y - mu
    var = jnp.mean(yc * yc, axis=-1, keepdims=True)
    return yc * lax.rsqrt(var + LN_EPS) * g + b


def _mod_kernel(c_ref, w_ref, b_ref, o_ref):
    c = c_ref[...]
    a = (c * jax.nn.sigmoid(c)).astype(BF16)
    o_ref[...] = jnp.dot(a, w_ref[...].astype(BF16), preferred_element_type=F32) + b_ref[...]


def _adaln_all(c_all, w_mod, b_mod):
    rows = c_all.shape[0]
    tn = 1024
    return pl.pallas_call(
        _mod_kernel,
        grid=(4, 3 * D_MODEL // tn),
        in_specs=[
            pl.BlockSpec((rows, D_MODEL), lambda p, n: (0, 0)),
            pl.BlockSpec((None, D_MODEL, tn), lambda p, n: (p, 0, n)),
            pl.BlockSpec((None, 1, tn), lambda p, n: (p, 0, n)),
        ],
        out_specs=pl.BlockSpec((None, rows, tn), lambda p, n: (p, 0, n)),
        out_shape=jax.ShapeDtypeStruct((4, rows, 3 * D_MODEL), F32),
        compiler_params=_cparams(2),
        name="adaln_mod",
    )(c_all, w_mod, b_mod)


def _attn_proj_kernel(x_ref, sh_ref, sc_ref, w_ref, q_ref, k_ref, v_ref):
    tm = x_ref.shape[0]
    h = _modulate(x_ref[...], _mod_rows(sh_ref, tm), _mod_rows(sc_ref, tm)).astype(BF16)
    q = jnp.dot(h, w_ref[:, 0:NQ], preferred_element_type=F32)
    q_ref[...] = (q * (HEAD_DIM ** -0.5)).astype(BF16)
    k_ref[...] = jnp.dot(h, w_ref[:, NQ:NQ + NKV], preferred_element_type=F32)
    v_ref[...] = jnp.dot(h, w_ref[:, NQ + NKV:NQ + 2 * NKV], preferred_element_type=F32)


def _attn_proj(x, mod, w_in):
    m = x.shape[0]
    tm = min(ROW_TILE, m)
    row = functools.partial(_row_spec, tm)
    return pl.pallas_call(
        _attn_proj_kernel,
        grid=(m // tm,),
        in_specs=[row(D_MODEL), mod.spec(tm, 0), mod.spec(tm, 1), _full(w_in.shape)],
        out_specs=[row(NQ), row(NKV), row(NKV)],
        out_shape=[
            jax.ShapeDtypeStruct((m, NQ), BF16),
            jax.ShapeDtypeStruct((m, NKV), F32),
            jax.ShapeDtypeStruct((m, NKV), F32),
        ],
        compiler_params=_cparams(1),
        name="attn_proj",
    )(x, mod.arr, mod.arr, w_in)


def _gla_project(h, wt_ref, wgu_ref, bg_ref, vr_dtype):
    def proj(lo, hi):
        return lax.dot_general(h, wt_ref[lo:hi, :], (((1,), (1,)), ((), ())),
                               preferred_element_type=F32)

    n_all = wt_ref.shape[0]
    gdown = proj(n_all - LANES, n_all)
    q = proj(0, NK) * (GLA_DK ** -0.5)
    k = proj(NK, 2 * NK)
    pre = jnp.dot(gdown.astype(BF16), wgu_ref[...], preferred_element_type=F32) + bg_ref[...]
    v = proj(2 * NK, 2 * NK + NV).astype(vr_dtype)
    r = proj(2 * NK + NV, 2 * NK + 2 * NV).astype(vr_dtype)
    log_sig = jnp.minimum(pre, 0.0) - jnp.log1p(jnp.exp(-jnp.abs(pre)))
    return q, k, log_sig / GLA_TAU, v, r


def _gla_proj_kernel(x_ref, sh_ref, sc_ref, wt_ref, wgu_ref, bg_ref,
                     q_ref, k_ref, lg_ref, v_ref, r_ref):
    tm = x_ref.shape[0]
    h = _modulate(x_ref[...], _mod_rows(sh_ref, tm), _mod_rows(sc_ref, tm)).astype(BF16)
    q_ref[...], k_ref[...], lg_ref[...], v_ref[...], r_ref[...] = _gla_project(
        h, wt_ref, wgu_ref, bg_ref, v_ref.dtype)


def _gla_proj(x, mod, w_t, w_gu, b_gate, vr_dtype):
    m = x.shape[0]
    tm = min(ROW_TILE, m)
    row = functools.partial(_row_spec, tm)
    return pl.pallas_call(
        _gla_proj_kernel,
        grid=(m // tm,),
        in_specs=[row(D_MODEL), mod.spec(tm, 0), mod.spec(tm, 1), _full(w_t.shape),
                  _full(w_gu.shape), _full(b_gate.shape)],
        out_specs=[row(NK), row(NK), row(NK), row(NV), row(NV)],
        out_shape=[
            jax.ShapeDtypeStruct((m, NK), F32),
            jax.ShapeDtypeStruct((m, NK), F32),
            jax.ShapeDtypeStruct((m, NK), F32),
            jax.ShapeDtypeStruct((m, NV), vr_dtype),
            jax.ShapeDtypeStruct((m, NV), vr_dtype),
        ],
        compiler_params=_cparams(1),
        name="gla_proj",
    )(x, mod.arr, mod.arr, w_t, w_gu, b_gate)


def _mix_mlp_stream_kernel(a_ref, x_ref, gt0_ref, sh_ref, sc_ref, gt1_ref, wo_ref, w1_ref, w2_ref,
                           g0_ref, b0_ref, g1_ref, b1_ref, y_ref, x1_s, h_s, acc_ref):
    c = pl.program_id(0)

    @pl.when(c == 0)
    def _():
        o = jnp.dot(a_ref[...].astype(BF16), wo_ref[...], preferred_element_type=F32)
        x1 = _res_ln(x_ref[...], gt0_ref[...], o, g0_ref[...], b0_ref[...])
        x1_s[...] = x1
        h_s[...] = _modulate(x1, sh_ref[...], sc_ref[...]).astype(BF16)
        acc_ref[...] = jnp.zeros_like(acc_ref)

    a = jnp.dot(h_s[...], w1_ref[...], preferred_element_type=F32)
    a = jnp.square(jnp.maximum(a, 0.0)).astype(BF16)
    acc_ref[...] += jnp.dot(a, w2_ref[...], preferred_element_type=F32)

    @pl.when(c == pl.num_programs(0) - 1)
    def _():
        y_ref[...] = _res_ln(x1_s[...], gt1_ref[...], acc_ref[...], g1_ref[...], b1_ref[...])


def _mix_mlp_stream(a, x, mod_mix, mod_mlp, w_out, w1, w2, ln_g, ln_b, layer):
    m = x.shape[0]
    assert m <= ROW_TILE and mod_mix.per_row and mod_mlp.per_row
    fc = FF_CHUNK
    whole = lambda n: pl.BlockSpec((m, n), lambda c: (0, 0))
    mspec = lambda mod, col: pl.BlockSpec((None, m, D_MODEL), lambda c: (mod.p, 0, col))
    if w1.ndim == 2:
        w1_cols = pl.BlockSpec((D_MODEL, fc), lambda c: (0, c))
        w2_rows = pl.BlockSpec((fc, D_MODEL), lambda c: (c, 0))
    else:
        w1_cols = pl.BlockSpec((None, D_MODEL, fc), lambda c: (layer, 0, c))
        w2_rows = pl.BlockSpec((None, fc, D_MODEL), lambda c: (layer, c, 0))
    return pl.pallas_call(
        _mix_mlp_stream_kernel,
        grid=(D_FF // fc,),
        in_specs=[whole(a.shape[1]), whole(D_MODEL), mspec(mod_mix, 2), mspec(mod_mlp, 0),
                  mspec(mod_mlp, 1), mspec(mod_mlp, 2), _full(w_out.shape),
                  w1_cols, w2_rows,
                  _layer(ln_g, 2 * layer), _layer(ln_b, 2 * layer),
                  _layer(ln_g, 2 * layer + 1), _layer(ln_b, 2 * layer + 1)],
        out_specs=whole(D_MODEL),
        out_shape=jax.ShapeDtypeStruct((m, D_MODEL), F32),
        scratch_shapes=[pltpu.VMEM((m, D_MODEL), F32), pltpu.VMEM((m, D_MODEL), BF16),
                        pltpu.VMEM((m, D_MODEL), F32)],
        compiler_params=_cparams(1),
        name="mix_mlp_stream",
    )(a, x, mod_mix.arr, mod_mlp.arr, mod_mlp.arr, mod_mlp.arr, w_out, w1, w2,
      ln_g, ln_b, ln_g, ln_b)


def _alibi_slope(head):
    return 2.0 ** (-8.0 * (head + 1) / N_Q_HEADS)


def _softmax_sink(s, sink):
    m = jnp.maximum(jnp.max(s, axis=-1, keepdims=True), sink)
    e = jnp.exp(s - m)
    den = jnp.sum(e, axis=-1, keepdims=True) + jnp.exp(sink - m)
    return e / den


def _band_bias_init(bias_ref):
    blk = ATTN_BLOCK
    c = lax.broadcasted_iota(jnp.int32, (2 * blk, blk), 0)
    r = lax.broadcasted_iota(jnp.int32, (2 * blk, blk), 1)
    dist = blk + r - c
    valid = (dist >= 0) & (dist <= WINDOW)
    distf = dist.astype(F32)
    for head in range(N_Q_HEADS):
        pen = -_alibi_slope(head) * distf
        bias_ref[0, head] = jnp.where(valid, pen, NEG_BIG)
        bias_ref[1, head] = jnp.where(valid & (c >= blk), pen, NEG_BIG)


def _band_scores(q, kk):
    blk = ATTN_BLOCK
    head_of_lane = lax.broadcasted_iota(jnp.int32, (blk, NKV), 1) // HEAD_DIM
    scores = []
    for b in range(q.shape[0] // blk):
        keys = kk[b * blk:(b + 2) * blk]
        for g in range(GROUP):
            qg = q[b * blk:(b + 1) * blk, g * NKV:(g + 1) * NKV]
            qm = jnp.concatenate(
                [jnp.where(head_of_lane == j, qg, jnp.zeros_like(qg))
                 for j in range(N_KV_HEADS)], axis=0)
            scores.append(lax.dot_general(keys, qm, (((1,), (1,)), ((), ())),
                                          preferred_element_type=F32))
    return scores


def _band_outputs(scores, vvt, first_tile, sinks_ref, bias_ref, o_ref):
    blk = ATTN_BLOCK
    for b in range(len(scores) // GROUP):
        table = 1 if first_tile and b == 0 else 0
        vals_t = vvt[:, b * blk:(b + 2) * blk]
        for g in range(GROUP):
            st_all = scores[b * GROUP + g]
            ps = []
            for j in range(N_KV_HEADS):
                head = j * GROUP + g
                sink = sinks_ref[head]
                st = st_all[:, j * blk:(j + 1) * blk] + bias_ref[table, head]
                m = jnp.maximum(jnp.max(st, axis=0, keepdims=True), sink)
                e = jnp.exp(st - m)
                den = jnp.sum(e, axis=0, keepdims=True) + jnp.exp(sink - m)
                ps.append((e * (1.0 / den)).astype(BF16))
            ot_all = jnp.dot(vals_t, jnp.concatenate(ps, axis=1),
                             preferred_element_type=F32)
            ot = jnp.concatenate(
                [ot_all[j * HEAD_DIM:(j + 1) * HEAD_DIM, j * blk:(j + 1) * blk]
                 for j in range(N_KV_HEADS)], axis=0)
            o_ref[b * blk:(b + 1) * blk, g * NKV:(g + 1) * NKV] = ot.T.astype(BF16)


def _mlp_chunks(h, w1_ref, w2_ref, acc_ref, chunks):
    for c in chunks:
        cols = slice(c * FF_CHUNK, (c + 1) * FF_CHUNK)
        a = jnp.dot(h, w1_ref[:, cols], preferred_element_type=F32)
        a = jnp.square(jnp.maximum(a, 0.0)).astype(BF16)
        d = jnp.dot(a, w2_ref[cols, :], preferred_element_type=F32)
        if c == 0:
            acc_ref[...] = d
        else:
            acc_ref[...] += d


def _attn_layer_kernel(sinks_ref, xc_ref, xp_ref, sh0_ref, sc0_ref, gt0_ref, sh1_ref, sc1_ref,
                       gt1_ref, win_ref, wo_ref, w1_ref, w2_ref, g0_ref, b0_ref, g1_ref, b1_ref,
                       dq_ref, dkn_ref, dvn_ref, dck_ref, dcv_ref, cw1_ref, cw2_ref, cw3_ref,
                       y_ref, kl_ref, vl_ref, do_ref, dnk_ref, dnv_ref,
                       cw1b_ref, cw2b_ref, cw3b_ref,
                       o_s, kprev_s, vtprev_s, acc_ref, bias_ref, zk_ref, zv_ref):
    i = pl.program_id(0)
    last = pl.num_programs(0) - 1
    dec = _DecAttn(sinks_ref, dq_ref, dkn_ref, dvn_ref, dck_ref, dcv_ref, do_ref, dnk_ref,
                   dnv_ref, zk_ref, zv_ref)

    def cast_weights():
        cw1b_ref[...] = cw1_ref[...].astype(BF16)
        cw2b_ref[...] = cw2_ref[...].astype(BF16)
        cw3b_ref[...] = cw3_ref[...].astype(BF16)
    slot = i % 2
    blk = ATTN_BLOCK
    tm = xc_ref.shape[0]
    dot = functools.partial(jnp.dot, preferred_element_type=F32)
    n_chunks = D_FF // FF_CHUNK

    def mlp_in():
        x1 = _res_ln(xp_ref[...], gt0_ref[0:1, :], dot(o_s[1 - slot], wo_ref[...]),
                     g0_ref[...], b0_ref[...])
        return x1, _modulate(x1, sh1_ref[0:1, :], sc1_ref[0:1, :]).astype(BF16)

    def mlp_out(x1):
        y_ref[...] = _res_ln(x1, gt1_ref[0:1, :], acc_ref[...], g1_ref[...], b1_ref[...])

    def mix_in():
        h_in = _modulate(xc_ref[...], sh0_ref[0:1, :], sc0_ref[0:1, :]).astype(BF16)
        q = (dot(h_in, win_ref[:, 0:NQ]) * (HEAD_DIM ** -0.5)).astype(BF16)
        k = dot(h_in, win_ref[:, NQ:NQ + NKV])
        v = dot(h_in, win_ref[:, NQ + NKV:NQ + 2 * NKV])
        kl_ref[...] = k[tm - blk:, :]
        vl_ref[...] = v[tm - blk:, :]
        k_bf = k.astype(BF16)
        vt = v.T.astype(BF16)
        kk = jnp.concatenate([kprev_s[...], k_bf], axis=0)
        vvt = jnp.concatenate([vtprev_s[...], vt], axis=1)
        scores = _band_scores(q, kk)
        kprev_s[...] = k_bf[tm - blk:, :]
        vtprev_s[...] = vt[:, tm - blk:]
        return scores, vvt

    @pl.when(i == 0)
    def _():
        _band_bias_init(bias_ref)
        kprev_s[...] = jnp.zeros_like(kprev_s)
        vtprev_s[...] = jnp.zeros_like(vtprev_s)
        dec.zero()
        cast_weights()
        scores, vvt = mix_in()
        dec_scored = dec.scores()
        _band_outputs(scores, vvt, True, sinks_ref, bias_ref, o_s.at[slot])
        dec.finish(dec_scored)

    @pl.when((i > 0) & (i < last))
    def _():
        x1, h_mlp = mlp_in()
        cast_weights()
        scores, vvt = mix_in()
        _mlp_chunks(h_mlp, w1_ref, w2_ref, acc_ref, range(0, 1))
        dec_scored = dec.scores()
        _mlp_chunks(h_mlp, w1_ref, w2_ref, acc_ref, range(1, n_chunks))
        _band_outputs(scores, vvt, False, sinks_ref, bias_ref, o_s.at[slot])
        dec.finish(dec_scored)
        mlp_out(x1)

    @pl.when(i == last)
    def _():
        x1, h_mlp = mlp_in()
        _mlp_chunks(h_mlp, w1_ref, w2_ref, acc_ref, range(n_chunks))
        mlp_out(x1)


def _attn_layer(x, mod_mix, mod_mlp, w_in, w_out, w1, w2, ln_g, ln_b, sinks, layer,
                dec_q, dec_k, dec_v, cache_kt, cache_vt, w1_f32, w2_f32, cast_layer, w3_f32):
    m = x.shape[0]
    tm = ATTN_TILE
    blk = ATTN_BLOCK
    n = m // tm
    assert not mod_mix.per_row and not mod_mlp.per_row
    nseq, win = cache_kt.shape[0], cache_kt.shape[2]
    sb = nseq // n
    n_new = dec_k.shape[0] // nseq
    assert sb * n == nseq and (sb * n_new) % SUBLANES == 0
    cur = pl.BlockSpec((tm, D_MODEL), lambda i: (jnp.minimum(i, n - 1), 0))
    prev = pl.BlockSpec((tm, D_MODEL), lambda i: (jnp.maximum(i - 1, 0), 0))
    last = pl.BlockSpec((blk, NKV), lambda i: (0, 0))
    dblk = lambda a: pl.BlockSpec((sb,) + a.shape[1:], lambda i: (jnp.minimum(i, n - 1), 0, 0))
    drow = lambda a: pl.BlockSpec((sb * n_new, a.shape[1]), lambda i: (jnp.minimum(i, n - 1), 0))
    c_in = lambda w: pl.BlockSpec((None, w.shape[1] // n, w.shape[2]),
                                  lambda i: (cast_layer, jnp.minimum(i, n - 1), 0))
    c_out = lambda w: pl.BlockSpec((w.shape[1] // n, w.shape[2]),
                                   lambda i: (jnp.minimum(i, n - 1), 0))
    assert all(w.shape[1] % (n * 2 * SUBLANES) == 0 for w in (w1_f32, w2_f32))
    r3 = pl.cdiv(pl.cdiv(w3_f32.shape[0], n), 2 * SUBLANES) * 2 * SUBLANES
    n3 = pl.cdiv(w3_f32.shape[0], r3)
    c3 = pl.BlockSpec((r3, w3_f32.shape[1]), lambda i: (jnp.minimum(i, n3 - 1), 0))
    return pl.pallas_call(
        _attn_layer_kernel,
        grid=(n + 1,),
        in_specs=[pl.BlockSpec(memory_space=pltpu.SMEM), cur, prev,
                  mod_mix.spec(tm, 0), mod_mix.spec(tm, 1), mod_mix.spec(tm, 2),
                  mod_mlp.spec(tm, 0), mod_mlp.spec(tm, 1), mod_mlp.spec(tm, 2),
                  _full(w_in.shape), _full(w_out.shape), _resident(w1, layer),
                  _resident(w2, layer),
                  _layer(ln_g, 2 * layer), _layer(ln_b, 2 * layer),
                  _layer(ln_g, 2 * layer + 1), _layer(ln_b, 2 * layer + 1),
                  dblk(dec_q), drow(dec_k), drow(dec_v), dblk(cache_kt), dblk(cache_vt),
                  c_in(w1_f32), c_in(w2_f32), c3],
        out_specs=[prev, last, last, dblk(dec_q), dblk(cache_kt), dblk(cache_vt),
                   c_out(w1_f32), c_out(w2_f32), c3],
        out_shape=[jax.ShapeDtypeStruct((m, D_MODEL), F32),
                   jax.ShapeDtypeStruct((blk, NKV), F32),
                   jax.ShapeDtypeStruct((blk, NKV), F32),
                   jax.ShapeDtypeStruct(dec_q.shape, BF16),
                   jax.ShapeDtypeStruct(cache_kt.shape, F32),
                   jax.ShapeDtypeStruct(cache_vt.shape, F32),
                   jax.ShapeDtypeStruct(w1_f32.shape[1:], BF16),
                   jax.ShapeDtypeStruct(w2_f32.shape[1:], BF16),
                   jax.ShapeDtypeStruct(w3_f32.shape, BF16)],
        scratch_shapes=[pltpu.VMEM((2, tm, NQ), BF16),
                        pltpu.VMEM((blk, NKV), BF16),
                        pltpu.VMEM((NKV, blk), BF16),
                        pltpu.VMEM((tm, D_MODEL), F32),
                        pltpu.VMEM((2, N_Q_HEADS, 2 * blk, blk), F32),
                        pltpu.VMEM((sb, win, NKV), F32),
                        pltpu.VMEM((sb, win, NKV), F32)],
        compiler_params=_cparams(1),
        name="attn_layer",
    )(sinks, x, x, mod_mix.arr, mod_mix.arr, mod_mix.arr, mod_mlp.arr, mod_mlp.arr,
      mod_mlp.arr, w_in, w_out, w1, w2, ln_g, ln_b, ln_g, ln_b,
      dec_q, dec_k, dec_v, cache_kt, cache_vt, w1_f32, w2_f32, w3_f32)


class _DecAttn:
    def __init__(self, sinks_ref, q_ref, kn_ref, vn_ref, ck_ref, cv_ref, o_ref, nk_ref, nv_ref,
                 zk_ref, zv_ref):
        self.refs = (sinks_ref, q_ref, kn_ref, vn_ref, ck_ref, cv_ref, o_ref, nk_ref, nv_ref,
                     zk_ref, zv_ref)
        self.n_seq = ck_ref.shape[0]
        self.n_new = kn_ref.shape[0] // self.n_seq
        self.win = ck_ref.shape[2]
        assert self.win == LANES and self.n_new < SUBLANES

    def zero(self):
        for ref in self.refs[9:]:
            ref[...] = jnp.zeros_like(ref)

    def scores(self):
        (sinks_ref, q_ref, kn_ref, vn_ref, ck_ref, cv_ref, _, nk_ref, nv_ref,
         zk_ref, zv_ref) = self.refs
        n_new, win = self.n_new, self.win
        rows = n_new * GROUP
        keep = win - n_new
        head_of_lane = lax.broadcasted_iota(jnp.int32, (rows, NKV), 1) // HEAD_DIM
        lane_w = lax.broadcasted_iota(jnp.int32, (NKV, win), 1)
        out = []
        for s in range(self.n_seq):
            tok = slice(s * n_new, (s + 1) * n_new)
            k_t = ck_ref[s]
            v_t = cv_ref[s]
            zk_ref[s, keep:win, :] = kn_ref[tok, :]
            zv_ref[s, keep:win, :] = vn_ref[tok, :]
            zk_t = zk_ref[s].T
            zv_t = zv_ref[s].T
            nk_ref[s] = jnp.where(lane_w < keep, pltpu.roll(k_t, keep, 1), zk_t)
            nv_ref[s] = jnp.where(lane_w < keep, pltpu.roll(v_t, keep, 1), zv_t)
            keys = jnp.concatenate([k_t, zk_t], axis=1).astype(BF16)
            vals = jnp.concatenate([v_t, zv_t], axis=1).astype(BF16)
            qs = q_ref[s]
            qbd = jnp.concatenate(
                [jnp.where(head_of_lane == j, qs, jnp.zeros_like(qs))
                 for j in range(N_KV_HEADS)], axis=0)
            out.append((jnp.dot(qbd, keys, preferred_element_type=F32), vals))
        return out

    def finish(self, scored):
        sinks_ref, o_ref = self.refs[0], self.refs[6]
        n_new, win = self.n_new, self.win
        rows = n_new * GROUP
        n_rows = N_KV_HEADS * rows
        keep = win - n_new
        row = lax.broadcasted_iota(jnp.int32, (n_rows, 2 * win), 0)
        col = lax.broadcasted_iota(jnp.int32, (n_rows, 2 * win), 1)
        j_r = row // rows
        t_r = (row // GROUP) % n_new
        g_r = row % GROUP
        h_r = j_r * GROUP + g_r
        slope = jnp.exp2(-8.0 * (h_r + 1).astype(F32) / N_Q_HEADS)
        sink = jnp.zeros((n_rows, 1), F32)
        h_col = h_r[:, 0:1]
        for h in range(N_Q_HEADS):
            sink = jnp.where(h_col == h, sinks_ref[h], sink)
        is_key = (col < win) | (col >= win + keep)
        frame = jnp.where(col < win, col, col - keep)
        dist = t_r + win - frame
        valid = is_key & (dist >= 0) & (dist <= WINDOW)
        bias = jnp.where(valid, -slope * dist.astype(F32), NEG_BIG)
        head_of_lane = lax.broadcasted_iota(jnp.int32, (rows, NKV), 1) // HEAD_DIM
        for s, (sc, vals) in enumerate(scored):
            p = _softmax_sink(sc + bias, sink).astype(BF16)
            pv = lax.dot_general(p, vals, (((1,), (1,)), ((), ())),
                                 preferred_element_type=F32)
            o = jnp.zeros((rows, NKV), F32)
            for j in range(N_KV_HEADS):
                o = o + jnp.where(head_of_lane == j, pv[j * rows:(j + 1) * rows], 0.0)
            o_ref[s] = o.astype(BF16)


def _split2(x):
    hi = x.astype(BF16)
    lo = (x - hi.astype(F32)).astype(BF16)
    return hi, lo


def _gla_out(o, r, norm_g):
    ms = jnp.mean(o * o, axis=-1, keepdims=True)
    o = o * lax.rsqrt(ms + LN_EPS) * norm_g
    return o * (r * jax.nn.sigmoid(r))


def _gla_layer_kernel(xc_ref, xp_ref, sh0_ref, sc0_ref, gt0_ref, sh1_ref, sc1_ref, gt1_ref,
                      wt_ref, wgu_ref, bg_ref, ng_ref, wo_ref, w1_ref, w2_ref,
                      g0_ref, b0_ref, g1_ref, b1_ref,
                      dq_ref, dk_ref, dlg_ref, dv_ref, dr_ref, ds0_ref,
                      y_ref, s_out_ref, do_ref, ds1_ref,
                      o_s, s_ref, acc_ref, z_ref, v_all_ref, q_pad_ref):
    i = pl.program_id(0)
    last = pl.num_programs(0) - 1
    dec = _GlaStep(dq_ref, dk_ref, dlg_ref, dv_ref, dr_ref, ng_ref, ds0_ref, do_ref, ds1_ref,
                   z_ref, v_all_ref, q_pad_ref)
    sub = xc_ref.shape[0]
    ch = GLA_CHUNK
    n_ch = sub // ch
    n_chunks = D_FF // FF_CHUNK
    dot = functools.partial(jnp.dot, preferred_element_type=F32)
    heads = range(GLA_HEADS)
    ks = [slice(h * GLA_DK, (h + 1) * GLA_DK) for h in heads]
    vs = [slice(h * GLA_DV, (h + 1) * GLA_DV) for h in heads]

    def mlp_in():
        x1 = _res_ln(xp_ref[...], gt0_ref[0:1, :], dot(o_s[...], wo_ref[...]),
                     g0_ref[...], b0_ref[...])
        return x1, _modulate(x1, sh1_ref[0:1, :], sc1_ref[0:1, :]).astype(BF16)

    def mlp_out(x1):
        y_ref[...] = _res_ln(x1, gt1_ref[0:1, :], acc_ref[...], g1_ref[...], b1_ref[...])

    def decays():
        h_in = _modulate(xc_ref[...], sh0_ref[0:1, :], sc0_ref[0:1, :]).astype(BF16)
        q, k, lg, v, r = _gla_project(h_in, wt_ref, wgu_ref, bg_ref, BF16)
        row = lax.broadcasted_iota(jnp.int32, (sub, sub), 0)
        col = lax.broadcasted_iota(jnp.int32, (sub, sub), 1)
        causal = ((row // ch) == (col // ch)) & (col <= row)
        tril = jnp.where(causal, 1.0, 0.0).astype(BF16)
        hi, lo = _split2(lg)
        b = dot(tril, hi) + dot(tril, lo)
        ends = [b[(c + 1) * ch - 1:(c + 1) * ch, :] for c in range(n_ch)]
        b_end = jnp.concatenate([jnp.broadcast_to(e, (ch, NK)) for e in ends], axis=0)
        qd = (q * jnp.exp(b)).astype(BF16)
        kd = (k * jnp.exp(-b)).astype(BF16)
        kdec = k * jnp.exp(b_end - b)
        dec_rows = jnp.concatenate(
            [jnp.exp(e) for e in ends] + [jnp.zeros((LANES - n_ch, NK), F32)], axis=0)
        return causal, qd, kd, kdec, dec_rows, v, r

    def chunk_products(causal, qd, kd, kdec, v):
        chunk_of_col = lax.broadcasted_iota(jnp.int32, (GLA_DK, sub), 1) // ch
        a = [lax.dot_general(qd[:, ks[h]], kd[:, ks[h]], (((1,), (1,)), ((), ())),
                             preferred_element_type=F32) for h in heads]
        u = []
        for h in heads:
            kdec_t = kdec[:, ks[h]].T.astype(BF16)
            stacked = jnp.concatenate(
                [jnp.where(chunk_of_col == c, kdec_t, jnp.zeros_like(kdec_t))
                 for c in range(n_ch)], axis=0)
            u.append(dot(stacked, v[:, vs[h]]))
        o_intra = [dot(jnp.where(causal, a[h], 0.0).astype(BF16), v[:, vs[h]]) for h in heads]
        return u, o_intra

    def recurrence(qd, dec_rows, u, o_intra, r):
        for h in heads:
            s = s_ref[h]
            dec_t = dec_rows[:, ks[h]].T
            o_inter = []
            for c in range(n_ch):
                o_inter.append(dot(qd[c * ch:(c + 1) * ch, ks[h]], s.astype(BF16)))
                s = dec_t[:, c:c + 1] * s + u[h][c * GLA_DK:(c + 1) * GLA_DK]
            s_ref[h] = s
            o = o_intra[h] + jnp.concatenate(o_inter, axis=0)
            o_s[:, vs[h]] = _gla_out(o, r[:, vs[h]].astype(F32), ng_ref[...]).astype(BF16)

    @pl.when(i == 0)
    def _():
        s_ref[...] = jnp.zeros_like(s_ref)
        dec.zero()
        dec_intra = dec.stage()
        causal, qd, kd, kdec, dec_rows, v, r = decays()
        u, o_intra = chunk_products(causal, qd, kd, kdec, v)
        recurrence(qd, dec_rows, u, o_intra, r)
        dec.finish(dec_intra)

    @pl.when((i > 0) & (i < last))
    def _():
        x1, h_mlp = mlp_in()
        dec_intra = dec.stage()
        causal, qd, kd, kdec, dec_rows, v, r = decays()
        _mlp_chunks(h_mlp, w1_ref, w2_ref, acc_ref, range(0, 1))
        u, o_intra = chunk_products(causal, qd, kd, kdec, v)
        _mlp_chunks(h_mlp, w1_ref, w2_ref, acc_ref, range(1, n_chunks))
        recurrence(qd, dec_rows, u, o_intra, r)
        dec.finish(dec_intra)
        mlp_out(x1)

    @pl.when(i == last)
    def _():
        x1, h_mlp = mlp_in()
        _mlp_chunks(h_mlp, w1_ref, w2_ref, acc_ref, range(n_chunks))
        mlp_out(x1)
        s_out_ref[...] = s_ref[...]


def _gla_layer(x, mod_mix, mod_mlp, w_t, w_gu, b_gate, norm_g, w_out, w1, w2, ln_g, ln_b, layer,
               dec_rows, dec_state):
    m = x.shape[0]
    tm = GLA_TILE
    n = m // tm
    assert not mod_mix.per_row and not mod_mlp.per_row and tm % GLA_CHUNK == 0
    nseq = dec_state.shape[0]
    sb = nseq // n
    n_new = dec_rows[0].shape[0] // nseq
    assert sb * n == nseq and (sb * n_new) % SUBLANES == 0 and n_new <= GLA_CHUNK
    cur = pl.BlockSpec((tm, D_MODEL), lambda i: (jnp.minimum(i, n - 1), 0))
    prev = pl.BlockSpec((tm, D_MODEL), lambda i: (jnp.maximum(i - 1, 0), 0))
    drow = lambda a: pl.BlockSpec((sb * n_new, a.shape[1]), lambda i: (jnp.minimum(i, n - 1), 0))
    dstate = pl.BlockSpec((sb,) + dec_state.shape[1:],
                          lambda i: (jnp.minimum(i, n - 1), 0, 0, 0))
    state = (GLA_HEADS, GLA_DK, GLA_DV)
    dv = dec_rows[3]
    return pl.pallas_call(
        _gla_layer_kernel,
        grid=(n + 1,),
        in_specs=[cur, prev,
                  mod_mix.spec(tm, 0), mod_mix.spec(tm, 1), mod_mix.spec(tm, 2),
                  mod_mlp.spec(tm, 0), mod_mlp.spec(tm, 1), mod_mlp.spec(tm, 2),
                  _full(w_t.shape), _full(w_gu.shape), _full(b_gate.shape), _full(norm_g.shape),
                  _full(w_out.shape), _resident(w1, layer), _resident(w2, layer),
                  _layer(ln_g, 2 * layer), _layer(ln_b, 2 * layer),
                  _layer(ln_g, 2 * layer + 1), _layer(ln_b, 2 * layer + 1)]
                 + [drow(a) for a in dec_rows] + [dstate],
        out_specs=[prev, _full(state), drow(dv), dstate],
        out_shape=[jax.ShapeDtypeStruct((m, D_MODEL), F32), jax.ShapeDtypeStruct(state, F32),
                   jax.ShapeDtypeStruct(dv.shape, F32),
                   jax.ShapeDtypeStruct(dec_state.shape, F32)],
        scratch_shapes=[pltpu.VMEM((tm, NV), BF16),
                        pltpu.VMEM(state, F32),
                        pltpu.VMEM((tm, D_MODEL), F32),
                        pltpu.VMEM((GLA_HEADS, LANES, GLA_DK), F32),
                        pltpu.VMEM((GLA_HEADS, LANES, GLA_DV), F32),
                        pltpu.VMEM((sb, GLA_HEADS, SUBLANES, GLA_DK), F32)],
        compiler_params=_cparams(1),
        name="gla_layer",
    )(x, x, mod_mix.arr, mod_mix.arr, mod_mix.arr, mod_mlp.arr, mod_mlp.arr, mod_mlp.arr,
      w_t, w_gu, b_gate, norm_g, w_out, w1, w2, ln_g, ln_b, ln_g, ln_b, *dec_rows, dec_state)


class _GlaStep:
    def __init__(self, q_ref, k_ref, lg_ref, v_ref, r_ref, ng_ref, s0_ref, o_ref, s1_ref,
                 z_ref, v_all_ref, q_pad_ref):
        self.refs = (q_ref, k_ref, lg_ref, v_ref, r_ref, ng_ref, s0_ref, o_ref, s1_ref,
                     z_ref, v_all_ref, q_pad_ref)
        self.n_seq = s0_ref.shape[0]
        self.n_new = q_ref.shape[0] // self.n_seq
        assert self.n_new < SUBLANES and self.n_seq * SUBLANES <= LANES

    def zero(self):
        for ref in self.refs[9:]:
            ref[...] = jnp.zeros_like(ref)

    def stage(self):
        return _gla_step_stage(self.n_seq, self.n_new, *self.refs)

    def finish(self, o_intra):
        _gla_step_finish(o_intra, self.n_seq, self.n_new, *self.refs)


def _gla_step_stage(n_seq, n_new, q_ref, k_ref, lg_ref, v_ref, r_ref, ng_ref, s0_ref,
                    o_ref, s1_ref, z_ref, v_all_ref, q_pad_ref):
    tok = [slice(s * n_new, (s + 1) * n_new) for s in range(n_seq)]
    grp = SUBLANES
    rowi = lax.broadcasted_iota(jnp.int32, (n_new, NK), 0)
    rowv = lax.broadcasted_iota(jnp.int32, (n_new, GLA_DV), 0)
    heads = range(GLA_HEADS)
    ks = [slice(h * GLA_DK, (h + 1) * GLA_DK) for h in heads]
    vs = [slice(h * GLA_DV, (h + 1) * GLA_DV) for h in heads]
    o_intra = []
    for s in range(n_seq):
        lg = lg_ref[tok[s], :]
        b = jnp.zeros_like(lg)
        for t in range(n_new):
            b = b + jnp.where(rowi >= t, jnp.broadcast_to(lg[t:t + 1, :], lg.shape), 0.0)
        b_end = b[n_new - 1:n_new, :]
        qd = q_ref[tok[s], :] * jnp.exp(b)
        k = k_ref[tok[s], :]
        kd = k * jnp.exp(-b)
        kdec = k * jnp.exp(b_end - b)
        dec = jnp.exp(b_end)
        v = v_ref[tok[s], :]
        row0 = s * grp
        o_s = []
        for h in heads:
            v_h = v[:, vs[h]]
            qd_h = qd[:, ks[h]]
            o = jnp.zeros((n_new, GLA_DV), F32)
            for t in range(n_new):
                a_t = jnp.sum(qd_h * kd[t:t + 1, ks[h]], axis=-1, keepdims=True)
                o = o + jnp.where(rowv >= t, a_t * v_h[t:t + 1, :], 0.0)
            o_s.append(o)
            q_pad_ref[s, h, 0:n_new, :] = qd_h
            z_ref[h, row0:row0 + n_new, :] = kdec[:, ks[h]]
            z_ref[h, row0 + n_new:row0 + n_new + 1, :] = dec[:, ks[h]]
            v_all_ref[h, row0:row0 + n_new, :] = v_h
        o_intra.append(o_s)
    return o_intra


def _gla_step_finish(o_intra, n_seq, n_new, q_ref, k_ref, lg_ref, v_ref, r_ref, ng_ref, s0_ref,
                     o_ref, s1_ref, z_ref, v_all_ref, q_pad_ref):
    tok = [slice(s * n_new, (s + 1) * n_new) for s in range(n_seq)]
    grp = SUBLANES
    dot = functools.partial(jnp.dot, preferred_element_type=F32)
    heads = range(GLA_HEADS)
    vs = [slice(h * GLA_DV, (h + 1) * GLA_DV) for h in heads]
    group_of_lane = lax.broadcasted_iota(jnp.int32, (GLA_DK, LANES), 1) // grp
    for h in heads:
        z_t = z_ref[h].T
        v_all = v_all_ref[h]
        for s in range(n_seq):
            s0 = s0_ref[s, h]
            mine = jnp.where(group_of_lane == s, z_t, 0.0)
            col = s * grp + n_new
            s1_ref[s, h] = z_t[:, col:col + 1] * s0 + dot(mine, v_all)
            o = o_intra[s][h] + dot(q_pad_ref[s, h], s0)[0:n_new]
            o_ref[tok[s], vs[h]] = _gla_out(o, r_ref[tok[s], vs[h]], ng_ref[...])


def kernel(x_prompt, x_sample, cache_k, cache_v, state_gla, c_prompt, c_sample, w_mod, b_mod,
           ln_g, ln_b, attn_w_in, attn_w_out, attn_sinks, gla_w_in, gla_w_gate_up, gla_b_gate,
           gla_norm_g, gla_w_out, mlp_w1, mlp_w2):
    assert x_prompt.shape[0] == 1 and w_mod.shape[0] == DEPTH == 2
    seq = x_prompt.shape[1]
    nseq, n_new = x_sample.shape[0], x_sample.shape[1]
    win = cache_k.shape[2]
    m_s = nseq * n_new

    wq = attn_w_in[0][:, :NQ].reshape(D_MODEL, N_KV_HEADS, GROUP, HEAD_DIM)
    wq = wq.transpose(0, 2, 1, 3).reshape(D_MODEL, NQ)
    w_attn_in = jnp.concatenate([wq, attn_w_in[0][:, NQ:]], axis=1).astype(BF16)
    w_attn_out = attn_w_out[0].reshape(N_KV_HEADS, GROUP, HEAD_DIM, D_MODEL)
    w_attn_out = w_attn_out.transpose(1, 0, 2, 3).reshape(NQ, D_MODEL).astype(BF16)
    w_gla_t_f32 = gla_w_in[0].T
    assert w_gla_t_f32.shape[0] == 2 * NK + 2 * NV + GLA_GATE_RANK
    w_gla_gu = jnp.pad(gla_w_gate_up[0], ((LANES - GLA_GATE_RANK, 0), (0, 0))).astype(BF16)
    w_gla_out = gla_w_out[0].astype(BF16)
    b_gate = gla_b_gate[0].reshape(1, NK)
    norm_g = gla_norm_g[0].reshape(1, GLA_DV)
    ln_g4 = ln_g.reshape(2 * DEPTH, 1, D_MODEL)
    ln_b4 = ln_b.reshape(2 * DEPTH, 1, D_MODEL)
    sinks = attn_sinks[0]

    pad_rows = (-(m_s + 1)) % SUBLANES
    c_all = jnp.concatenate([jnp.repeat(c_sample, n_new, axis=0), c_prompt,
                             jnp.zeros((pad_rows, D_MODEL), F32)], axis=0)
    mod_all = _adaln_all(c_all, w_mod.reshape(2 * DEPTH, D_MODEL, 3 * D_MODEL),
                         b_mod.reshape(2 * DEPTH, 1, 3 * D_MODEL))
    mods_p = [_Mod(mod_all, p, per_row=False, row0=m_s) for p in range(2 * DEPTH)]
    mods_s = [_Mod(mod_all, p, per_row=True) for p in range(2 * DEPTH)]

    assert seq >= WINDOW == ATTN_BLOCK
    win_p = WINDOW
    to_slab = lambda c: c[0].transpose(0, 2, 3, 1).reshape(nseq, NKV, win)
    from_slab = lambda c: c.reshape(nseq, N_KV_HEADS, HEAD_DIM, win).transpose(0, 3, 1, 2)[None]

    x_s = x_sample.reshape(m_s, D_MODEL)
    w1_0, w2_0 = mlp_w1[0].astype(BF16), mlp_w2[0].astype(BF16)
    q, k, v = _attn_proj(x_s, mods_s[0], w_attn_in)
    x1_p, k_p, v_p, o, k_s, v_s, w1_1, w2_1, w_gla_t = _attn_layer(
        x_prompt[0], mods_p[0], mods_p[1], w_attn_in, w_attn_out, w1_0, w2_0,
        ln_g4, ln_b4, sinks, 0, q.reshape(nseq, n_new * GROUP, NKV), k, v,
        to_slab(cache_k), to_slab(cache_v), mlp_w1, mlp_w2, 1, w_gla_t_f32)
    x1_s = _mix_mlp_stream(o.reshape(m_s, NQ), x_s, mods_s[0], mods_s[1], w_attn_out,
                           w1_0, w2_0, ln_g4, ln_b4, 0)

    gla_w = (w_gla_t, w_gla_gu, b_gate)
    dec_rows = _gla_proj(x1_s, mods_s[2], *gla_w, F32)
    y_p, s_p, o, s_s = _gla_layer(x1_p, mods_p[2], mods_p[3], *gla_w, norm_g, w_gla_out,
                                  w1_1, w2_1, ln_g4, ln_b4, 1, dec_rows, state_gla[0])
    y_s = _mix_mlp_stream(o, x1_s, mods_s[2], mods_s[3], w_gla_out, w1_1, w2_1,
                          ln_g4, ln_b4, 1)
    k_s, v_s = from_slab(k_s), from_slab(v_s)

    kv_shape_p = (1, 1, win_p, N_KV_HEADS, HEAD_DIM)
    return (y_p[None], y_s.reshape(nseq, n_new, D_MODEL),
            k_p.reshape(kv_shape_p), v_p.reshape(kv_shape_p), s_p[None, None],
            k_s, v_s, s_s[None])
```

```python
import functools

import jax
import jax.numpy as jnp
from jax import lax
from jax.experimental import pallas as pl
from jax.experimental.pallas import tpu as pltpu

F32 = jnp.float32
BF16 = jnp.bfloat16

D_MODEL = 1024
DEPTH = 2
HEAD_DIM = 64
N_Q_HEADS = 16
N_KV_HEADS = 4
GROUP = 4
WINDOW = 128
ATTN_BLOCK = 128
GLA_HEADS = 4
GLA_DK = 128
GLA_DV = 256
GLA_GATE_RANK = 16
GLA_TAU = 16.0
GLA_CHUNK = 64
D_FF = 4 * D_MODEL
ALPHA = (2.0 * DEPTH) ** 0.25
LN_EPS = 1e-5

NQ = N_Q_HEADS * HEAD_DIM
NKV = N_KV_HEADS * HEAD_DIM
NK = GLA_HEADS * GLA_DK
NV = GLA_HEADS * GLA_DV
LANES = 128
SUBLANES = 8
NEG_BIG = -1e30

ROW_TILE = 512
FF_CHUNK = 1024
ATTN_TILE = 256
GLA_TILE = 256
VMEM_LIMIT = 56 * 1024 * 1024


def _cparams(n_axes):
    return pltpu.CompilerParams(
        dimension_semantics=("arbitrary",) * n_axes,
        vmem_limit_bytes=VMEM_LIMIT,
    )


def _full(shape):
    zeros = (0,) * len(shape)
    return pl.BlockSpec(shape, lambda *_: zeros)


def _layer(arr, idx):
    tail = (0,) * (arr.ndim - 1)
    return pl.BlockSpec((None,) + arr.shape[1:], lambda *_: (idx,) + tail,
                        pipeline_mode=pl.Buffered(1))


def _resident(w, idx):
    return _full(w.shape) if w.ndim == 2 else _layer(w, idx)


def _row_spec(tm, n):
    return pl.BlockSpec((tm, n), lambda i: (i, 0))


class _Mod:
    def __init__(self, arr, p, per_row, row0=0, n_seq=0):
        self.arr, self.p, self.per_row, self.row0, self.n_seq = arr, p, per_row, row0, n_seq
        assert not per_row or n_seq > SUBLANES

    def spec(self, tm, col):
        p = self.p
        if self.per_row:
            return pl.BlockSpec((None, self.n_seq, D_MODEL), lambda i: (p, 0, col))
        blk = self.row0 // SUBLANES
        return pl.BlockSpec((None, SUBLANES, D_MODEL), lambda i: (p, blk, col))


def _repeat_rows(x, k):
    r = x.shape[0]
    row = lax.broadcasted_iota(jnp.int32, (r * k, r), 0)
    col = lax.broadcasted_iota(jnp.int32, (r * k, r), 1)
    pick = jnp.where(row // k == col, 1.0, 0.0).astype(BF16)
    hi = x.astype(BF16)
    rest = x - hi.astype(F32)
    mid = rest.astype(BF16)
    lo = (rest - mid.astype(F32)).astype(BF16)
    dot = functools.partial(jnp.dot, preferred_element_type=F32)
    return (dot(pick, hi) + dot(pick, mid)) + dot(pick, lo)


def _mod_rows(ref, tm):
    rows = ref.shape[0]
    if rows == SUBLANES:
        return ref[0:1, :]
    return _repeat_rows(ref[...], tm // rows)


def _modulate(x, shift, scale):
    return x * (1.0 + scale) + shift


def _res_ln(x, gate, o, g, b):
    y = ALPHA * x + gate * o
    mu = jnp.mean(y, axis=-1, keepdims=True)
    yc = y - mu
    var = jnp.mean(yc * yc, axis=-1, keepdims=True)
    return yc * lax.rsqrt(var + LN_EPS) * g + b


def _mod_kernel(c_ref, w_ref, b_ref, o_ref):
    c = c_ref[...]
    a = (c * jax.nn.sigmoid(c)).astype(BF16)
    o_ref[...] = jnp.dot(a, w_ref[...].astype(BF16), preferred_element_type=F32) + b_ref[...]


def _adaln_all(c_all, w_mod, b_mod):
    rows = c_all.shape[0]
    tn = 1024
    return pl.pallas_call(
        _mod_kernel,
        grid=(4, 3 * D_MODEL // tn),
        in_specs=[
            pl.BlockSpec((rows, D_MODEL), lambda p, n: (0, 0)),
            pl.BlockSpec((None, D_MODEL, tn), lambda p, n: (p, 0, n)),
            pl.BlockSpec((None, 1, tn), lambda p, n: (p, 0, n)),
        ],
        out_specs=pl.BlockSpec((None, rows, tn), lambda p, n: (p, 0, n)),
        out_shape=jax.ShapeDtypeStruct((4, rows, 3 * D_MODEL), F32),
        compiler_params=_cparams(2),
        name="adaln_mod",
    )(c_all, w_mod, b_mod)


def _attn_proj_kernel(x_ref, sh_ref, sc_ref, w_ref, q_ref, k_ref, v_ref):
    tm = x_ref.shape[0]
    h = _modulate(x_ref[...], _mod_rows(sh_ref, tm), _mod_rows(sc_ref, tm)).astype(BF16)
    q = jnp.dot(h, w_ref[:, 0:NQ], preferred_element_type=F32)
    q_ref[...] = (q * (HEAD_DIM ** -0.5)).astype(BF16)
    k_ref[...] = jnp.dot(h, w_ref[:, NQ:NQ + NKV], preferred_element_type=F32)
    v_ref[...] = jnp.dot(h, w_ref[:, NQ + NKV:NQ + 2 * NKV], preferred_element_type=F32)


def _attn_proj(x, mod, w_in):
    m = x.shape[0]
    tm = min(ROW_TILE, m)
    row = functools.partial(_row_spec, tm)
    return pl.pallas_call(
        _attn_proj_kernel,
        grid=(m // tm,),
        in_specs=[row(D_MODEL), mod.spec(tm, 0), mod.spec(tm, 1), _full(w_in.shape)],
        out_specs=[row(NQ), row(NKV), row(NKV)],
        out_shape=[
            jax.ShapeDtypeStruct((m, NQ), BF16),
            jax.ShapeDtypeStruct((m, NKV), F32),
            jax.ShapeDtypeStruct((m, NKV), F32),
        ],
        compiler_params=_cparams(1),
        name="attn_proj",
    )(x, mod.arr, mod.arr, w_in)


def _gla_project(h, wt_ref, wgu_ref, bg_ref, vr_dtype):
    def proj(lo, hi):
        return lax.dot_general(h, wt_ref[lo:hi, :], (((1,), (1,)), ((), ())),
                               preferred_element_type=F32)

    n_all = wt_ref.shape[0]
    gdown = proj(n_all - LANES, n_all)
    q = proj(0, NK) * (GLA_DK ** -0.5)
    k = proj(NK, 2 * NK)
    pre = jnp.dot(gdown.astype(BF16), wgu_ref[...], preferred_element_type=F32) + bg_ref[...]
    v = proj(2 * NK, 2 * NK + NV).astype(vr_dtype)
    r = proj(2 * NK + NV, 2 * NK + 2 * NV).astype(vr_dtype)
    log_sig = jnp.minimum(pre, 0.0) - jnp.log1p(jnp.exp(-jnp.abs(pre)))
    return q, k, log_sig / GLA_TAU, v, r


def _gla_proj_kernel(x_ref, sh_ref, sc_ref, wt_ref, wgu_ref, bg_ref,
                     q_ref, k_ref, lg_ref, v_ref, r_ref):
    tm = x_ref.shape[0]
    h = _modulate(x_ref[...], _mod_rows(sh_ref, tm), _mod_rows(sc_ref, tm)).astype(BF16)
    q_ref[...], k_ref[...], lg_ref[...], v_ref[...], r_ref[...] = _gla_project(
        h, wt_ref, wgu_ref, bg_ref, v_ref.dtype)


def _gla_proj(x, mod, w_t, w_gu, b_gate, vr_dtype):
    m = x.shape[0]
    tm = min(ROW_TILE, m)
    row = functools.partial(_row_spec, tm)
    return pl.pallas_call(
        _gla_proj_kernel,
        grid=(m // tm,),
        in_specs=[row(D_MODEL), mod.spec(tm, 0), mod.spec(tm, 1), _full(w_t.shape),
                  _full(w_gu.shape), _full(b_gate.shape)],
        out_specs=[row(NK), row(NK), row(NK), row(NV), row(NV)],
        out_shape=[
            jax.ShapeDtypeStruct((m, NK), F32),
            jax.ShapeDtypeStruct((m, NK), F32),
            jax.ShapeDtypeStruct((m, NK), F32),
            jax.ShapeDtypeStruct((m, NV), vr_dtype),
            jax.ShapeDtypeStruct((m, NV), vr_dtype),
        ],
        compiler_params=_cparams(1),
        name="gla_proj",
    )(x, mod.arr, mod.arr, w_t, w_gu, b_gate)


def _mix_mlp_stream_kernel(a_ref, x_ref, gt0_ref, sh_ref, sc_ref, gt1_ref, wo_ref, w1_ref, w2_ref,
                           g0_ref, b0_ref, g1_ref, b1_ref, y_ref, x1_s, h_s, acc_ref):
    c = pl.program_id(0)
    tm = x_ref.shape[0]

    @pl.when(c == 0)
    def _():
        o = jnp.dot(a_ref[...].astype(BF16), wo_ref[...], preferred_element_type=F32)
        x1 = _res_ln(x_ref[...], _mod_rows(gt0_ref, tm), o, g0_ref[...], b0_ref[...])
        x1_s[...] = x1
        h_s[...] = _modulate(x1, _mod_rows(sh_ref, tm), _mod_rows(sc_ref, tm)).astype(BF16)
        acc_ref[...] = jnp.zeros_like(acc_ref)

    a = jnp.dot(h_s[...], w1_ref[...], preferred_element_type=F32)
    a = jnp.square(jnp.maximum(a, 0.0)).astype(BF16)
    acc_ref[...] += jnp.dot(a, w2_ref[...], preferred_element_type=F32)

    @pl.when(c == pl.num_programs(0) - 1)
    def _():
        y_ref[...] = _res_ln(x1_s[...], _mod_rows(gt1_ref, tm), acc_ref[...],
                             g1_ref[...], b1_ref[...])


def _mix_mlp_stream(a, x, mod_mix, mod_mlp, w_out, w1, w2, ln_g, ln_b, layer):
    m = x.shape[0]
    assert m <= ROW_TILE and mod_mix.per_row and mod_mlp.per_row
    fc = FF_CHUNK
    whole = lambda n: pl.BlockSpec((m, n), lambda c: (0, 0))
    mspec = lambda mod, col: mod.spec(m, col)
    if w1.ndim == 2:
        w1_cols = pl.BlockSpec((D_MODEL, fc), lambda c: (0, c))
        w2_rows = pl.BlockSpec((fc, D_MODEL), lambda c: (c, 0))
    else:
        w1_cols = pl.BlockSpec((None, D_MODEL, fc), lambda c: (layer, 0, c))
        w2_rows = pl.BlockSpec((None, fc, D_MODEL), lambda c: (layer, c, 0))
    return pl.pallas_call(
        _mix_mlp_stream_kernel,
        grid=(D_FF // fc,),
        in_specs=[whole(a.shape[1]), whole(D_MODEL), mspec(mod_mix, 2), mspec(mod_mlp, 0),
                  mspec(mod_mlp, 1), mspec(mod_mlp, 2), _full(w_out.shape),
                  w1_cols, w2_rows,
                  _layer(ln_g, 2 * layer), _layer(ln_b, 2 * layer),
                  _layer(ln_g, 2 * layer + 1), _layer(ln_b, 2 * layer + 1)],
        out_specs=whole(D_MODEL),
        out_shape=jax.ShapeDtypeStruct((m, D_MODEL), F32),
        scratch_shapes=[pltpu.VMEM((m, D_MODEL), F32), pltpu.VMEM((m, D_MODEL), BF16),
                        pltpu.VMEM((m, D_MODEL), F32)],
        compiler_params=_cparams(1),
        name="mix_mlp_stream",
    )(a, x, mod_mix.arr, mod_mlp.arr, mod_mlp.arr, mod_mlp.arr, w_out, w1, w2,
      ln_g, ln_b, ln_g, ln_b)


def _alibi_slope(head):
    return 2.0 ** (-8.0 * (head + 1) / N_Q_HEADS)


def _softmax_sink(s, sink):
    m = jnp.maximum(jnp.max(s, axis=-1, keepdims=True), sink)
    e = jnp.exp(s - m)
    den = jnp.sum(e, axis=-1, keepdims=True) + jnp.exp(sink - m)
    return e / den


def _band_bias_init(bias_ref):
    blk = ATTN_BLOCK
    c = lax.broadcasted_iota(jnp.int32, (2 * blk, blk), 0)
    r = lax.broadcasted_iota(jnp.int32, (2 * blk, blk), 1)
    dist = blk + r - c
    valid = (dist >= 0) & (dist <= WINDOW)
    distf = dist.astype(F32)
    for head in range(N_Q_HEADS):
        pen = -_alibi_slope(head) * distf
        bias_ref[0, head] = jnp.where(valid, pen, NEG_BIG)
        bias_ref[1, head] = jnp.where(valid & (c >= blk), pen, NEG_BIG)


def _band_scores(q, kk):
    blk = ATTN_BLOCK
    head_of_lane = lax.broadcasted_iota(jnp.int32, (blk, NKV), 1) // HEAD_DIM
    scores = []
    for b in range(q.shape[0] // blk):
        keys = kk[b * blk:(b + 2) * blk]
        for g in range(GROUP):
            qg = q[b * blk:(b + 1) * blk, g * NKV:(g + 1) * NKV]
            qm = jnp.concatenate(
                [jnp.where(head_of_lane == j, qg, jnp.zeros_like(qg))
                 for j in range(N_KV_HEADS)], axis=0)
            scores.append(lax.dot_general(keys, qm, (((1,), (1,)), ((), ())),
                                          preferred_element_type=F32))
    return scores


def _band_outputs(scores, vvt, first_tile, sinks_ref, bias_ref, o_ref):
    blk = ATTN_BLOCK
    for b in range(len(scores) // GROUP):
        table = 1 if first_tile and b == 0 else 0
        vals_t = vvt[:, b * blk:(b + 2) * blk]
        for g in range(GROUP):
            st_all = scores[b * GROUP + g]
            ps = []
            for j in range(N_KV_HEADS):
                head = j * GROUP + g
                sink = sinks_ref[head]
                st = st_all[:, j * blk:(j + 1) * blk] + bias_ref[table, head]
                m = jnp.maximum(jnp.max(st, axis=0, keepdims=True), sink)
                e = jnp.exp(st - m)
                den = jnp.sum(e, axis=0, keepdims=True) + jnp.exp(sink - m)
                ps.append((e * (1.0 / den)).astype(BF16))
            ot_all = jnp.dot(vals_t, jnp.concatenate(ps, axis=1),
                             preferred_element_type=F32)
            ot = jnp.concatenate(
                [ot_all[j * HEAD_DIM:(j + 1) * HEAD_DIM, j * blk:(j + 1) * blk]
                 for j in range(N_KV_HEADS)], axis=0)
            o_ref[b * blk:(b + 1) * blk, g * NKV:(g + 1) * NKV] = ot.T.astype(BF16)


def _mlp_chunks(h, w1_ref, w2_ref, acc_ref, chunks):
    for c in chunks:
        cols = slice(c * FF_CHUNK, (c + 1) * FF_CHUNK)
        a = jnp.dot(h, w1_ref[:, cols], preferred_element_type=F32)
        a = jnp.square(jnp.maximum(a, 0.0)).astype(BF16)
        d = jnp.dot(a, w2_ref[cols, :], preferred_element_type=F32)
        if c == 0:
            acc_ref[...] = d
        else:
            acc_ref[...] += d


def _attn_layer_kernel(sinks_ref, xc_ref, xp_ref, sh0_ref, sc0_ref, gt0_ref, sh1_ref, sc1_ref,
                       gt1_ref, win_ref, wo_ref, w1_ref, w2_ref, g0_ref, b0_ref, g1_ref, b1_ref,
                       dq_ref, dkn_ref, dvn_ref, dck_ref, dcv_ref, cw1_ref, cw2_ref, cw3_ref,
                       y_ref, kl_ref, vl_ref, do_ref, dnk_ref, dnv_ref,
                       cw1b_ref, cw2b_ref, cw3b_ref,
                       o_s, kprev_s, vtprev_s, acc_ref, bias_ref, zk_ref, zv_ref):
    i = pl.program_id(0)
    last = pl.num_programs(0) - 1
    dec = _DecAttn(sinks_ref, dq_ref, dkn_ref, dvn_ref, dck_ref, dcv_ref, do_ref, dnk_ref,
                   dnv_ref, zk_ref, zv_ref)

    def cast_weights():
        cw1b_ref[...] = cw1_ref[...].astype(BF16)
        cw2b_ref[...] = cw2_ref[...].astype(BF16)
        cw3b_ref[...] = cw3_ref[...].astype(BF16)
    slot = i % 2
    blk = ATTN_BLOCK
    tm = xc_ref.shape[0]
    dot = functools.partial(jnp.dot, preferred_element_type=F32)
    n_chunks = D_FF // FF_CHUNK

    def mlp_in():
        x1 = _res_ln(xp_ref[...], gt0_ref[0:1, :], dot(o_s[1 - slot], wo_ref[...]),
                     g0_ref[...], b0_ref[...])
        return x1, _modulate(x1, sh1_ref[0:1, :], sc1_ref[0:1, :]).astype(BF16)

    def mlp_out(x1):
        y_ref[...] = _res_ln(x1, gt1_ref[0:1, :], acc_ref[...], g1_ref[...], b1_ref[...])

    def mix_in():
        h_in = _modulate(xc_ref[...], sh0_ref[0:1, :], sc0_ref[0:1, :]).astype(BF16)
        q = (dot(h_in, win_ref[:, 0:NQ]) * (HEAD_DIM ** -0.5)).astype(BF16)
        k = dot(h_in, win_ref[:, NQ:NQ + NKV])
        v = dot(h_in, win_ref[:, NQ + NKV:NQ + 2 * NKV])
        kl_ref[...] = k[tm - blk:, :]
        vl_ref[...] = v[tm - blk:, :]
        k_bf = k.astype(BF16)
        vt = v.T.astype(BF16)
        kk = jnp.concatenate([kprev_s[...], k_bf], axis=0)
        vvt = jnp.concatenate([vtprev_s[...], vt], axis=1)
        scores = _band_scores(q, kk)
        kprev_s[...] = k_bf[tm - blk:, :]
        vtprev_s[...] = vt[:, tm - blk:]
        return scores, vvt

    @pl.when(i == 0)
    def _():
        _band_bias_init(bias_ref)
        kprev_s[...] = jnp.zeros_like(kprev_s)
        vtprev_s[...] = jnp.zeros_like(vtprev_s)
        dec.zero()
        cast_weights()
        scores, vvt = mix_in()
        dec_scored = dec.scores()
        _band_outputs(scores, vvt, True, sinks_ref, bias_ref, o_s.at[slot])
        dec.finish(dec_scored)

    @pl.when((i > 0) & (i < last))
    def _():
        x1, h_mlp = mlp_in()
        cast_weights()
        scores, vvt = mix_in()
        _mlp_chunks(h_mlp, w1_ref, w2_ref, acc_ref, range(0, 1))
        dec_scored = dec.scores()
        _mlp_chunks(h_mlp, w1_ref, w2_ref, acc_ref, range(1, n_chunks))
        _band_outputs(scores, vvt, False, sinks_ref, bias_ref, o_s.at[slot])
        dec.finish(dec_scored)
        mlp_out(x1)

    @pl.when(i == last)
    def _():
        x1, h_mlp = mlp_in()
        _mlp_chunks(h_mlp, w1_ref, w2_ref, acc_ref, range(n_chunks))
        mlp_out(x1)


def _attn_layer(x, mod_mix, mod_mlp, w_in, w_out, w1, w2, ln_g, ln_b, sinks, layer,
                dec_q, dec_k, dec_v, cache_kt, cache_vt, w1_f32, w2_f32, cast_layer, w3_f32):
    m = x.shape[0]
    tm = ATTN_TILE
    blk = ATTN_BLOCK
    n = m // tm
    assert not mod_mix.per_row and not mod_mlp.per_row
    nseq, win = cache_kt.shape[0], cache_kt.shape[2]
    sb = nseq // n
    n_new = dec_k.shape[0] // nseq
    assert sb * n == nseq and (sb * n_new) % SUBLANES == 0
    cur = pl.BlockSpec((tm, D_MODEL), lambda i: (jnp.minimum(i, n - 1), 0))
    prev = pl.BlockSpec((tm, D_MODEL), lambda i: (jnp.maximum(i - 1, 0), 0))
    last = pl.BlockSpec((blk, NKV), lambda i: (0, 0))
    dblk = lambda a: pl.BlockSpec((sb,) + a.shape[1:], lambda i: (jnp.minimum(i, n - 1), 0, 0))
    drow = lambda a: pl.BlockSpec((sb * n_new, a.shape[1]), lambda i: (jnp.minimum(i, n - 1), 0))
    c_in = lambda w: pl.BlockSpec((None, w.shape[1] // n, w.shape[2]),
                                  lambda i: (cast_layer, jnp.minimum(i, n - 1), 0))
    c_out = lambda w: pl.BlockSpec((w.shape[1] // n, w.shape[2]),
                                   lambda i: (jnp.minimum(i, n - 1), 0))
    assert all(w.shape[1] % (n * 2 * SUBLANES) == 0 for w in (w1_f32, w2_f32))
    r3 = pl.cdiv(pl.cdiv(w3_f32.shape[0], n), 2 * SUBLANES) * 2 * SUBLANES
    n3 = pl.cdiv(w3_f32.shape[0], r3)
    c3 = pl.BlockSpec((r3, w3_f32.shape[1]), lambda i: (jnp.minimum(i, n3 - 1), 0))
    return pl.pallas_call(
        _attn_layer_kernel,
        grid=(n + 1,),
        in_specs=[pl.BlockSpec(memory_space=pltpu.SMEM), cur, prev,
                  mod_mix.spec(tm, 0), mod_mix.spec(tm, 1), mod_mix.spec(tm, 2),
                  mod_mlp.spec(tm, 0), mod_mlp.spec(tm, 1), mod_mlp.spec(tm, 2),
                  _full(w_in.shape), _full(w_out.shape), _resident(w1, layer),
                  _resident(w2, layer),
                  _layer(ln_g, 2 * layer), _layer(ln_b, 2 * layer),
                  _layer(ln_g, 2 * layer + 1), _layer(ln_b, 2 * layer + 1),
                  dblk(dec_q), drow(dec_k), drow(dec_v), dblk(cache_kt), dblk(cache_vt),
                  c_in(w1_f32), c_in(w2_f32), c3],
        out_specs=[prev, last, last, dblk(dec_q), dblk(cache_kt), dblk(cache_vt),
                   c_out(w1_f32), c_out(w2_f32), c3],
        out_shape=[jax.ShapeDtypeStruct((m, D_MODEL), F32),
                   jax.ShapeDtypeStruct((blk, NKV), F32),
                   jax.ShapeDtypeStruct((blk, NKV), F32),
                   jax.ShapeDtypeStruct(dec_q.shape, BF16),
                   jax.ShapeDtypeStruct(cache_kt.shape, F32),
                   jax.ShapeDtypeStruct(cache_vt.shape, F32),
                   jax.ShapeDtypeStruct(w1_f32.shape[1:], BF16),
                   jax.ShapeDtypeStruct(w2_f32.shape[1:], BF16),
                   jax.ShapeDtypeStruct(w3_f32.shape, BF16)],
        scratch_shapes=[pltpu.VMEM((2, tm, NQ), BF16),
                        pltpu.VMEM((blk, NKV), BF16),
                        pltpu.VMEM((NKV, blk), BF16),
                        pltpu.VMEM((tm, D_MODEL), F32),
                        pltpu.VMEM((2, N_Q_HEADS, 2 * blk, blk), F32),
                        pltpu.VMEM((sb, win, NKV), F32),
                        pltpu.VMEM((sb, win, NKV), F32)],
        compiler_params=_cparams(1),
        name="attn_layer",
    )(sinks, x, x, mod_mix.arr, mod_mix.arr, mod_mix.arr, mod_mlp.arr, mod_mlp.arr,
      mod_mlp.arr, w_in, w_out, w1, w2, ln_g, ln_b, ln_g, ln_b,
      dec_q, dec_k, dec_v, cache_kt, cache_vt, w1_f32, w2_f32, w3_f32)


class _DecAttn:
    def __init__(self, sinks_ref, q_ref, kn_ref, vn_ref, ck_ref, cv_ref, o_ref, nk_ref, nv_ref,
                 zk_ref, zv_ref):
        self.refs = (sinks_ref, q_ref, kn_ref, vn_ref, ck_ref, cv_ref, o_ref, nk_ref, nv_ref,
                     zk_ref, zv_ref)
        self.n_seq = ck_ref.shape[0]
        self.n_new = kn_ref.shape[0] // self.n_seq
        self.win = ck_ref.shape[2]
        assert self.win == LANES and self.n_new < SUBLANES

    def zero(self):
        for ref in self.refs[9:]:
            ref[...] = jnp.zeros_like(ref)

    def scores(self):
        (sinks_ref, q_ref, kn_ref, vn_ref, ck_ref, cv_ref, _, nk_ref, nv_ref,
         zk_ref, zv_ref) = self.refs
        n_new, win = self.n_new, self.win
        rows = n_new * GROUP
        keep = win - n_new
        head_of_lane = lax.broadcasted_iota(jnp.int32, (rows, NKV), 1) // HEAD_DIM
        lane_w = lax.broadcasted_iota(jnp.int32, (NKV, win), 1)
        out = []
        for s in range(self.n_seq):
            tok = slice(s * n_new, (s + 1) * n_new)
            k_t = ck_ref[s]
            v_t = cv_ref[s]
            zk_ref[s, keep:win, :] = kn_ref[tok, :]
            zv_ref[s, keep:win, :] = vn_ref[tok, :]
            zk_t = zk_ref[s].T
            zv_t = zv_ref[s].T
            nk_ref[s] = jnp.where(lane_w < keep, pltpu.roll(k_t, keep, 1), zk_t)
            nv_ref[s] = jnp.where(lane_w < keep, pltpu.roll(v_t, keep, 1), zv_t)
            keys = jnp.concatenate([k_t, zk_t], axis=1).astype(BF16)
            vals = jnp.concatenate([v_t, zv_t], axis=1).astype(BF16)
            qs = q_ref[s]
            qbd = jnp.concatenate(
                [jnp.where(head_of_lane == j, qs, jnp.zeros_like(qs))
                 for j in range(N_KV_HEADS)], axis=0)
            out.append((jnp.dot(qbd, keys, preferred_element_type=F32), vals))
        return out

    def finish(self, scored):
        sinks_ref, o_ref = self.refs[0], self.refs[6]
        n_new, win = self.n_new, self.win
        rows = n_new * GROUP
        n_rows = N_KV_HEADS * rows
        keep = win - n_new
        row = lax.broadcasted_iota(jnp.int32, (n_rows, 2 * win), 0)
        col = lax.broadcasted_iota(jnp.int32, (n_rows, 2 * win), 1)
        j_r = row // rows
        t_r = (row // GROUP) % n_new
        g_r = row % GROUP
        h_r = j_r * GROUP + g_r
        slope = jnp.exp2(-8.0 * (h_r + 1).astype(F32) / N_Q_HEADS)
        sink = jnp.zeros((n_rows, 1), F32)
        h_col = h_r[:, 0:1]
        for h in range(N_Q_HEADS):
            sink = jnp.where(h_col == h, sinks_ref[h], sink)
        is_key = (col < win) | (col >= win + keep)
        frame = jnp.where(col < win, col, col - keep)
        dist = t_r + win - frame
        valid = is_key & (dist >= 0) & (dist <= WINDOW)
        bias = jnp.where(valid, -slope * dist.astype(F32), NEG_BIG)
        head_of_lane = lax.broadcasted_iota(jnp.int32, (rows, NKV), 1) // HEAD_DIM
        for s, (sc, vals) in enumerate(scored):
            p = _softmax_sink(sc + bias, sink).astype(BF16)
            pv = lax.dot_general(p, vals, (((1,), (1,)), ((), ())),
                                 preferred_element_type=F32)
            o = jnp.zeros((rows, NKV), F32)
            for j in range(N_KV_HEADS):
                o = o + jnp.where(head_of_lane == j, pv[j * rows:(j + 1) * rows], 0.0)
            o_ref[s] = o.astype(BF16)


def _split2(x):
    hi = x.astype(BF16)
    lo = (x - hi.astype(F32)).astype(BF16)
    return hi, lo


def _gla_out(o, r, norm_g):
    ms = jnp.mean(o * o, axis=-1, keepdims=True)
    o = o * lax.rsqrt(ms + LN_EPS) * norm_g
    return o * (r * jax.nn.sigmoid(r))


def _gla_layer_kernel(xc_ref, xp_ref, sh0_ref, sc0_ref, gt0_ref, sh1_ref, sc1_ref, gt1_ref,
                      wt_ref, wgu_ref, bg_ref, ng_ref, wo_ref, w1_ref, w2_ref,
                      g0_ref, b0_ref, g1_ref, b1_ref,
                      dq_ref, dk_ref, dlg_ref, dv_ref, dr_ref, ds0_ref,
                      y_ref, s_out_ref, do_ref, ds1_ref,
                      o_s, s_ref, acc_ref, z_ref, v_all_ref, q_pad_ref):
    i = pl.program_id(0)
    last = pl.num_programs(0) - 1
    dec = _GlaStep(dq_ref, dk_ref, dlg_ref, dv_ref, dr_ref, ng_ref, ds0_ref, do_ref, ds1_ref,
                   z_ref, v_all_ref, q_pad_ref)
    sub = xc_ref.shape[0]
    ch = GLA_CHUNK
    n_ch = sub // ch
    n_chunks = D_FF // FF_CHUNK
    dot = functools.partial(jnp.dot, preferred_element_type=F32)
    heads = range(GLA_HEADS)
    ks = [slice(h * GLA_DK, (h + 1) * GLA_DK) for h in heads]
    vs = [slice(h * GLA_DV, (h + 1) * GLA_DV) for h in heads]

    def mlp_in():
        x1 = _res_ln(xp_ref[...], gt0_ref[0:1, :], dot(o_s[...], wo_ref[...]),
                     g0_ref[...], b0_ref[...])
        return x1, _modulate(x1, sh1_ref[0:1, :], sc1_ref[0:1, :]).astype(BF16)

    def mlp_out(x1):
        y_ref[...] = _res_ln(x1, gt1_ref[0:1, :], acc_ref[...], g1_ref[...], b1_ref[...])

    def decays():
        h_in = _modulate(xc_ref[...], sh0_ref[0:1, :], sc0_ref[0:1, :]).astype(BF16)
        q, k, lg, v, r = _gla_project(h_in, wt_ref, wgu_ref, bg_ref, BF16)
        row = lax.broadcasted_iota(jnp.int32, (sub, sub), 0)
        col = lax.broadcasted_iota(jnp.int32, (sub, sub), 1)
        causal = ((row // ch) == (col // ch)) & (col <= row)
        tril = jnp.where(causal, 1.0, 0.0).astype(BF16)
        hi, lo = _split2(lg)
        b = dot(tril, hi) + dot(tril, lo)
        ends = [b[(c + 1) * ch - 1:(c + 1) * ch, :] for c in range(n_ch)]
        b_end = jnp.concatenate([jnp.broadcast_to(e, (ch, NK)) for e in ends], axis=0)
        qd = (q * jnp.exp(b)).astype(BF16)
        kd = (k * jnp.exp(-b)).astype(BF16)
        kdec = k * jnp.exp(b_end - b)
        dec_rows = jnp.concatenate(
            [jnp.exp(e) for e in ends] + [jnp.zeros((LANES - n_ch, NK), F32)], axis=0)
        return causal, qd, kd, kdec, dec_rows, v, r

    def chunk_products(causal, qd, kd, kdec, v):
        chunk_of_col = lax.broadcasted_iota(jnp.int32, (GLA_DK, sub), 1) // ch
        a = [lax.dot_general(qd[:, ks[h]], kd[:, ks[h]], (((1,), (1,)), ((), ())),
                             preferred_element_type=F32) for h in heads]
        u = []
        for h in heads:
            kdec_t = kdec[:, ks[h]].T.astype(BF16)
            stacked = jnp.concatenate(
                [jnp.where(chunk_of_col == c, kdec_t, jnp.zeros_like(kdec_t))
                 for c in range(n_ch)], axis=0)
            u.append(dot(stacked, v[:, vs[h]]))
        o_intra = [dot(jnp.where(causal, a[h], 0.0).astype(BF16), v[:, vs[h]]) for h in heads]
        return u, o_intra

    def recurrence(qd, dec_rows, u, o_intra, r):
        for h in heads:
            s = s_ref[h]
            dec_t = dec_rows[:, ks[h]].T
            o_inter = []
            for c in range(n_ch):
                o_inter.append(dot(qd[c * ch:(c + 1) * ch, ks[h]], s.astype(BF16)))
                s = dec_t[:, c:c + 1] * s + u[h][c * GLA_DK:(c + 1) * GLA_DK]
            s_ref[h] = s
            o = o_intra[h] + jnp.concatenate(o_inter, axis=0)
            o_s[:, vs[h]] = _gla_out(o, r[:, vs[h]].astype(F32), ng_ref[...]).astype(BF16)

    @pl.when(i == 0)
    def _():
        s_ref[...] = jnp.zeros_like(s_ref)
        dec.zero()
        dec_intra = dec.stage()
        causal, qd, kd, kdec, dec_rows, v, r = decays()
        u, o_intra = chunk_products(causal, qd, kd, kdec, v)
        recurrence(qd, dec_rows, u, o_intra, r)
        dec.finish(dec_intra)

    @pl.when((i > 0) & (i < last))
    def _():
        x1, h_mlp = mlp_in()
        dec_intra = dec.stage()
        causal, qd, kd, kdec, dec_rows, v, r = decays()
        _mlp_chunks(h_mlp, w1_ref, w2_ref, acc_ref, range(0, 1))
        u, o_intra = chunk_products(causal, qd, kd, kdec, v)
        _mlp_chunks(h_mlp, w1_ref, w2_ref, acc_ref, range(1, n_chunks))
        recurrence(qd, dec_rows, u, o_intra, r)
        dec.finish(dec_intra)
        mlp_out(x1)

    @pl.when(i == last)
    def _():
        x1, h_mlp = mlp_in()
        _mlp_chunks(h_mlp, w1_ref, w2_ref, acc_ref, range(n_chunks))
        mlp_out(x1)
        s_out_ref[...] = s_ref[...]


def _gla_layer(x, mod_mix, mod_mlp, w_t, w_gu, b_gate, norm_g, w_out, w1, w2, ln_g, ln_b, layer,
               dec_rows, dec_state):
    m = x.shape[0]
    tm = GLA_TILE
    n = m // tm
    assert not mod_mix.per_row and not mod_mlp.per_row and tm % GLA_CHUNK == 0
    nseq = dec_state.shape[0]
    sb = nseq // n
    n_new = dec_rows[0].shape[0] // nseq
    assert sb * n == nseq and (sb * n_new) % SUBLANES == 0 and n_new <= GLA_CHUNK
    cur = pl.BlockSpec((tm, D_MODEL), lambda i: (jnp.minimum(i, n - 1), 0))
    prev = pl.BlockSpec((tm, D_MODEL), lambda i: (jnp.maximum(i - 1, 0), 0))
    drow = lambda a: pl.BlockSpec((sb * n_new, a.shape[1]), lambda i: (jnp.minimum(i, n - 1), 0))
    dstate = pl.BlockSpec((sb,) + dec_state.shape[1:],
                          lambda i: (jnp.minimum(i, n - 1), 0, 0, 0))
    state = (GLA_HEADS, GLA_DK, GLA_DV)
    dv = dec_rows[3]
    return pl.pallas_call(
        _gla_layer_kernel,
        grid=(n + 1,),
        in_specs=[cur, prev,
                  mod_mix.spec(tm, 0), mod_mix.spec(tm, 1), mod_mix.spec(tm, 2),
                  mod_mlp.spec(tm, 0), mod_mlp.spec(tm, 1), mod_mlp.spec(tm, 2),
                  _full(w_t.shape), _full(w_gu.shape), _full(b_gate.shape), _full(norm_g.shape),
                  _full(w_out.shape), _resident(w1, layer), _resident(w2, layer),
                  _layer(ln_g, 2 * layer), _layer(ln_b, 2 * layer),
                  _layer(ln_g, 2 * layer + 1), _layer(ln_b, 2 * layer + 1)]
                 + [drow(a) for a in dec_rows] + [dstate],
        out_specs=[prev, _full(state), drow(dv), dstate],
        out_shape=[jax.ShapeDtypeStruct((m, D_MODEL), F32), jax.ShapeDtypeStruct(state, F32),
                   jax.ShapeDtypeStruct(dv.shape, F32),
                   jax.ShapeDtypeStruct(dec_state.shape, F32)],
        scratch_shapes=[pltpu.VMEM((tm, NV), BF16),
                        pltpu.VMEM(state, F32),
                        pltpu.VMEM((tm, D_MODEL), F32),
                        pltpu.VMEM((GLA_HEADS, LANES, GLA_DK), F32),
                        pltpu.VMEM((GLA_HEADS, LANES, GLA_DV), F32),
                        pltpu.VMEM((sb, GLA_HEADS, SUBLANES, GLA_DK), F32)],
        compiler_params=_cparams(1),
        name="gla_layer",
    )(x, x, mod_mix.arr, mod_mix.arr, mod_mix.arr, mod_mlp.arr, mod_mlp.arr, mod_mlp.arr,
      w_t, w_gu, b_gate, norm_g, w_out, w1, w2, ln_g, ln_b, ln_g, ln_b, *dec_rows, dec_state)


class _GlaStep:
    def __init__(self, q_ref, k_ref, lg_ref, v_ref, r_ref, ng_ref, s0_ref, o_ref, s1_ref,
                 z_ref, v_all_ref, q_pad_ref):
        self.refs = (q_ref, k_ref, lg_ref, v_ref, r_ref, ng_ref, s0_ref, o_ref, s1_ref,
                     z_ref, v_all_ref, q_pad_ref)
        self.n_seq = s0_ref.shape[0]
        self.n_new = q_ref.shape[0] // self.n_seq
        assert self.n_new < SUBLANES and self.n_seq * SUBLANES <= LANES

    def zero(self):
        for ref in self.refs[9:]:
            ref[...] = jnp.zeros_like(ref)

    def stage(self):
        return _gla_step_stage(self.n_seq, self.n_new, *self.refs)

    def finish(self, o_intra):
        _gla_step_finish(o_intra, self.n_seq, self.n_new, *self.refs)


def _gla_step_stage(n_seq, n_new, q_ref, k_ref, lg_ref, v_ref, r_ref, ng_ref, s0_ref,
                    o_ref, s1_ref, z_ref, v_all_ref, q_pad_ref):
    tok = [slice(s * n_new, (s + 1) * n_new) for s in range(n_seq)]
    grp = SUBLANES
    rowi = lax.broadcasted_iota(jnp.int32, (n_new, NK), 0)
    rowv = lax.broadcasted_iota(jnp.int32, (n_new, GLA_DV), 0)
    heads = range(GLA_HEADS)
    ks = [slice(h * GLA_DK, (h + 1) * GLA_DK) for h in heads]
    vs = [slice(h * GLA_DV, (h + 1) * GLA_DV) for h in heads]
    o_intra = []
    for s in range(n_seq):
        lg = lg_ref[tok[s], :]
        b = jnp.zeros_like(lg)
        for t in range(n_new):
            b = b + jnp.where(rowi >= t, jnp.broadcast_to(lg[t:t + 1, :], lg.shape), 0.0)
        b_end = b[n_new - 1:n_new, :]
        qd = q_ref[tok[s], :] * jnp.exp(b)
        k = k_ref[tok[s], :]
        kd = k * jnp.exp(-b)
        kdec = k * jnp.exp(b_end - b)
        dec = jnp.exp(b_end)
        v = v_ref[tok[s], :]
        row0 = s * grp
        o_s = []
        for h in heads:
            v_h = v[:, vs[h]]
            qd_h = qd[:, ks[h]]
            o = jnp.zeros((n_new, GLA_DV), F32)
            for t in range(n_new):
                a_t = jnp.sum(qd_h * kd[t:t + 1, ks[h]], axis=-1, keepdims=True)
                o = o + jnp.where(rowv >= t, a_t * v_h[t:t + 1, :], 0.0)
            o_s.append(o)
            q_pad_ref[s, h, 0:n_new, :] = qd_h
            z_ref[h, row0:row0 + n_new, :] = kdec[:, ks[h]]
            z_ref[h, row0 + n_new:row0 + n_new + 1, :] = dec[:, ks[h]]
            v_all_ref[h, row0:row0 + n_new, :] = v_h
        o_intra.append(o_s)
    return o_intra


def _gla_step_finish(o_intra, n_seq, n_new, q_ref, k_ref, lg_ref, v_ref, r_ref, ng_ref, s0_ref,
                     o_ref, s1_ref, z_ref, v_all_ref, q_pad_ref):
    tok = [slice(s * n_new, (s + 1) * n_new) for s in range(n_seq)]
    grp = SUBLANES
    dot = functools.partial(jnp.dot, preferred_element_type=F32)
    heads = range(GLA_HEADS)
    vs = [slice(h * GLA_DV, (h + 1) * GLA_DV) for h in heads]
    group_of_lane = lax.broadcasted_iota(jnp.int32, (GLA_DK, LANES), 1) // grp
    for h in heads:
        z_t = z_ref[h].T
        v_all = v_all_ref[h]
        for s in range(n_seq):
            s0 = s0_ref[s, h]
            mine = jnp.where(group_of_lane == s, z_t, 0.0)
            col = s * grp + n_new
            s1_ref[s, h] = z_t[:, col:col + 1] * s0 + dot(mine, v_all)
            o = o_intra[s][h] + dot(q_pad_ref[s, h], s0)[0:n_new]
            o_ref[tok[s], vs[h]] = _gla_out(o, r_ref[tok[s], vs[h]], ng_ref[...])


def kernel(x_prompt, x_sample, cache_k, cache_v, state_gla, c_prompt, c_sample, w_mod, b_mod,
           ln_g, ln_b, attn_w_in, attn_w_out, attn_sinks, gla_w_in, gla_w_gate_up, gla_b_gate,
           gla_norm_g, gla_w_out, mlp_w1, mlp_w2):
    assert x_prompt.shape[0] == 1 and w_mod.shape[0] == DEPTH == 2
    seq = x_prompt.shape[1]
    nseq, n_new = x_sample.shape[0], x_sample.shape[1]
    win = cache_k.shape[2]
    m_s = nseq * n_new

    wq = attn_w_in[0][:, :NQ].reshape(D_MODEL, N_KV_HEADS, GROUP, HEAD_DIM)
    wq = wq.transpose(0, 2, 1, 3).reshape(D_MODEL, NQ)
    w_attn_in = jnp.concatenate([wq, attn_w_in[0][:, NQ:]], axis=1).astype(BF16)
    w_attn_out = attn_w_out[0].reshape(N_KV_HEADS, GROUP, HEAD_DIM, D_MODEL)
    w_attn_out = w_attn_out.transpose(1, 0, 2, 3).reshape(NQ, D_MODEL).astype(BF16)
    w_gla_t_f32 = gla_w_in[0].T
    assert w_gla_t_f32.shape[0] == 2 * NK + 2 * NV + GLA_GATE_RANK
    w_gla_gu = jnp.pad(gla_w_gate_up[0], ((LANES - GLA_GATE_RANK, 0), (0, 0))).astype(BF16)
    w_gla_out = gla_w_out[0].astype(BF16)
    b_gate = gla_b_gate[0].reshape(1, NK)
    norm_g = gla_norm_g[0].reshape(1, GLA_DV)
    ln_g4 = ln_g.reshape(2 * DEPTH, 1, D_MODEL)
    ln_b4 = ln_b.reshape(2 * DEPTH, 1, D_MODEL)
    sinks = attn_sinks[0]

    assert nseq % SUBLANES == 0 and m_s <= ROW_TILE
    c_all = jnp.concatenate([c_sample, c_prompt, jnp.zeros((SUBLANES - 1, D_MODEL), F32)],
                            axis=0)
    mod_all = _adaln_all(c_all, w_mod.reshape(2 * DEPTH, D_MODEL, 3 * D_MODEL),
                         b_mod.reshape(2 * DEPTH, 1, 3 * D_MODEL))
    mods_p = [_Mod(mod_all, p, per_row=False, row0=nseq) for p in range(2 * DEPTH)]
    mods_s = [_Mod(mod_all, p, per_row=True, n_seq=nseq) for p in range(2 * DEPTH)]

    assert seq >= WINDOW == ATTN_BLOCK
    win_p = WINDOW
    to_slab = lambda c: c[0].transpose(0, 2, 3, 1).reshape(nseq, NKV, win)
    from_slab = lambda c: c.reshape(nseq, N_KV_HEADS, HEAD_DIM, win).transpose(0, 3, 1, 2)[None]

    x_s = x_sample.reshape(m_s, D_MODEL)
    w1_0, w2_0 = mlp_w1[0].astype(BF16), mlp_w2[0].astype(BF16)
    q, k, v = _attn_proj(x_s, mods_s[0], w_attn_in)
    x1_p, k_p, v_p, o, k_s, v_s, w1_1, w2_1, w_gla_t = _attn_layer(
        x_prompt[0], mods_p[0], mods_p[1], w_attn_in, w_attn_out, w1_0, w2_0,
        ln_g4, ln_b4, sinks, 0, q.reshape(nseq, n_new * GROUP, NKV), k, v,
        to_slab(cache_k), to_slab(cache_v), mlp_w1, mlp_w2, 1, w_gla_t_f32)
    x1_s = _mix_mlp_stream(o.reshape(m_s, NQ), x_s, mods_s[0], mods_s[1], w_attn_out,
                           w1_0, w2_0, ln_g4, ln_b4, 0)

    gla_w = (w_gla_t, w_gla_gu, b_gate)
    dec_rows = _gla_proj(x1_s, mods_s[2], *gla_w, F32)
    y_p, s_p, o, s_s = _gla_layer(x1_p, mods_p[2], mods_p[3], *gla_w, norm_g, w_gla_out,
                                  w1_1, w2_1, ln_g4, ln_b4, 1, dec_rows, state_gla[0])
    y_s = _mix_mlp_stream(o, x1_s, mods_s[2], mods_s[3], w_gla_out, w1_1, w2_1,
                          ln_g4, ln_b4, 1)
    k_s, v_s = from_slab(k_s), from_slab(v_s)

    kv_shape_p = (1, 1, win_p, N_KV_HEADS, HEAD_DIM)
    return (y_p[None], y_s.reshape(nseq, n_new, D_MODEL),
            k_p.reshape(kv_shape_p), v_p.reshape(kv_shape_p), s_p[None, None],
            k_s, v_s, s_s[None])
```

```python
import functools

import jax
import jax.numpy as jnp
from jax import lax
from jax.experimental import pallas as pl
from jax.experimental.pallas import tpu as pltpu

F32 = jnp.float32
BF16 = jnp.bfloat16

D_MODEL = 1024
DEPTH = 2
HEAD_DIM = 64
N_Q_HEADS = 16
N_KV_HEADS = 4
GROUP = 4
WINDOW = 128
ATTN_BLOCK = 128
GLA_HEADS = 4
GLA_DK = 128
GLA_DV = 256
GLA_GATE_RANK = 16
GLA_TAU = 16.0
GLA_CHUNK = 64
D_FF = 4 * D_MODEL
ALPHA = (2.0 * DEPTH) ** 0.25
LN_EPS = 1e-5

NQ = N_Q_HEADS * HEAD_DIM
NKV = N_KV_HEADS * HEAD_DIM
NK = GLA_HEADS * GLA_DK
NV = GLA_HEADS * GLA_DV
LANES = 128
SUBLANES = 8
NEG_BIG = -1e30

ROW_TILE = 512
FF_CHUNK = 1024
ATTN_TILE = 256
GLA_TILE = 256
VMEM_LIMIT = 56 * 1024 * 1024


def _cparams(n_axes):
    return pltpu.CompilerParams(
        dimension_semantics=("arbitrary",) * n_axes,
        vmem_limit_bytes=VMEM_LIMIT,
    )


def _full(shape):
    zeros = (0,) * len(shape)
    return pl.BlockSpec(shape, lambda *_: zeros)


def _layer(arr, idx):
    tail = (0,) * (arr.ndim - 1)
    return pl.BlockSpec((None,) + arr.shape[1:], lambda *_: (idx,) + tail,
                        pipeline_mode=pl.Buffered(1))


def _resident(w, idx):
    return _full(w.shape) if w.ndim == 2 else _layer(w, idx)


def _row_spec(tm, n):
    return pl.BlockSpec((tm, n), lambda i: (i, 0))


class _Mod:
    def __init__(self, arr, p, per_row, row0=0, n_seq=0):
        self.arr, self.p, self.per_row, self.row0, self.n_seq = arr, p, per_row, row0, n_seq
        assert not per_row or n_seq > SUBLANES

    def spec(self, tm, col):
        p = self.p
        if self.per_row:
            return pl.BlockSpec((None, self.n_seq, D_MODEL), lambda i: (p, 0, col))
        blk = self.row0 // SUBLANES
        return pl.BlockSpec((None, SUBLANES, D_MODEL), lambda i: (p, blk, col))


def _repeat_rows(x, k):
    r = x.shape[0]
    row = lax.broadcasted_iota(jnp.int32, (r * k, r), 0)
    col = lax.broadcasted_iota(jnp.int32, (r * k, r), 1)
    pick = jnp.where(row // k == col, 1.0, 0.0).astype(BF16)
    hi = x.astype(BF16)
    rest = x - hi.astype(F32)
    mid = rest.astype(BF16)
    lo = (rest - mid.astype(F32)).astype(BF16)
    dot = functools.partial(jnp.dot, preferred_element_type=F32)
    return (dot(pick, hi) + dot(pick, mid)) + dot(pick, lo)


def _mod_rows(ref, tm):
    rows = ref.shape[0]
    if rows == SUBLANES:
        return ref[0:1, :]
    return _repeat_rows(ref[...], tm // rows)


def _modulate(x, shift, scale):
    return x * (1.0 + scale) + shift


def _res_ln(x, gate, o, g, b):
    y = ALPHA * x + gate * o
    mu = jnp.mean(y, axis=-1, keepdims=True)
    yc = y - mu
    var = jnp.mean(yc * yc, axis=-1, keepdims=True)
    return yc * lax.rsqrt(var + LN_EPS) * g + b


def _mod_kernel(c_ref, w_ref, b_ref, o_ref):
    c = c_ref[...]
    a = (c * jax.nn.sigmoid(c)).astype(BF16)
    o_ref[...] = jnp.dot(a, w_ref[...].astype(BF16), preferred_element_type=F32) + b_ref[...]


def _mod_cast_kernel(c_ref, w_ref, b_ref, cw1_ref, cw2_ref, o_ref, cw1b_ref, cw2b_ref):
    _mod_kernel(c_ref, w_ref, b_ref, o_ref)
    cw1b_ref[...] = cw1_ref[...].astype(BF16)
    cw2b_ref[...] = cw2_ref[...].astype(BF16)


def _adaln_all(c_all, w_mod, b_mod, w1_f32, w2_f32, cast_layer):
    rows = c_all.shape[0]
    n_col = 4
    tn = 3 * D_MODEL // n_col
    steps = 4 * n_col
    assert tn % LANES == 0
    assert all(w.shape[1] % (steps * 2 * SUBLANES) == 0 for w in (w1_f32, w2_f32))
    c_in = lambda w: pl.BlockSpec((None, w.shape[1] // steps, w.shape[2]),
                                  lambda p, n: (cast_layer, p * n_col + n, 0))
    c_out = lambda w: pl.BlockSpec((w.shape[1] // steps, w.shape[2]),
                                   lambda p, n: (p * n_col + n, 0))
    return pl.pallas_call(
        _mod_cast_kernel,
        grid=(4, n_col),
        in_specs=[
            pl.BlockSpec((rows, D_MODEL), lambda p, n: (0, 0)),
            pl.BlockSpec((None, D_MODEL, tn), lambda p, n: (p, 0, n)),
            pl.BlockSpec((None, 1, tn), lambda p, n: (p, 0, n)),
            c_in(w1_f32), c_in(w2_f32),
        ],
        out_specs=[pl.BlockSpec((None, rows, tn), lambda p, n: (p, 0, n)),
                   c_out(w1_f32), c_out(w2_f32)],
        out_shape=[jax.ShapeDtypeStruct((4, rows, 3 * D_MODEL), F32),
                   jax.ShapeDtypeStruct(w1_f32.shape[1:], BF16),
                   jax.ShapeDtypeStruct(w2_f32.shape[1:], BF16)],
        compiler_params=_cparams(2),
        name="adaln_mod",
    )(c_all, w_mod, b_mod, w1_f32, w2_f32)


def _attn_proj_kernel(x_ref, sh_ref, sc_ref, w_ref, q_ref, k_ref, v_ref):
    tm = x_ref.shape[0]
    h = _modulate(x_ref[...], _mod_rows(sh_ref, tm), _mod_rows(sc_ref, tm)).astype(BF16)
    q = jnp.dot(h, w_ref[:, 0:NQ], preferred_element_type=F32)
    q_ref[...] = (q * (HEAD_DIM ** -0.5)).astype(BF16)
    k_ref[...] = jnp.dot(h, w_ref[:, NQ:NQ + NKV], preferred_element_type=F32)
    v_ref[...] = jnp.dot(h, w_ref[:, NQ + NKV:NQ + 2 * NKV], preferred_element_type=F32)


def _attn_proj(x, mod, w_in):
    m = x.shape[0]
    tm = min(ROW_TILE, m)
    row = functools.partial(_row_spec, tm)
    return pl.pallas_call(
        _attn_proj_kernel,
        grid=(m // tm,),
        in_specs=[row(D_MODEL), mod.spec(tm, 0), mod.spec(tm, 1), _full(w_in.shape)],
        out_specs=[row(NQ), row(NKV), row(NKV)],
        out_shape=[
            jax.ShapeDtypeStruct((m, NQ), BF16),
            jax.ShapeDtypeStruct((m, NKV), F32),
            jax.ShapeDtypeStruct((m, NKV), F32),
        ],
        compiler_params=_cparams(1),
        name="attn_proj",
    )(x, mod.arr, mod.arr, w_in)


def _gla_project(h, wt_ref, wgu_ref, bg_ref, vr_dtype):
    def proj(lo, hi):
        return lax.dot_general(h, wt_ref[lo:hi, :], (((1,), (1,)), ((), ())),
                               preferred_element_type=F32)

    n_all = wt_ref.shape[0]
    gdown = proj(n_all - LANES, n_all)
    q = proj(0, NK) * (GLA_DK ** -0.5)
    k = proj(NK, 2 * NK)
    pre = jnp.dot(gdown.astype(BF16), wgu_ref[...], preferred_element_type=F32) + bg_ref[...]
    v = proj(2 * NK, 2 * NK + NV).astype(vr_dtype)
    r = proj(2 * NK + NV, 2 * NK + 2 * NV).astype(vr_dtype)
    log_sig = jnp.minimum(pre, 0.0) - jnp.log1p(jnp.exp(-jnp.abs(pre)))
    return q, k, log_sig / GLA_TAU, v, r


def _gla_proj_kernel(x_ref, sh_ref, sc_ref, wt_ref, wgu_ref, bg_ref,
                     q_ref, k_ref, lg_ref, v_ref, r_ref):
    tm = x_ref.shape[0]
    h = _modulate(x_ref[...], _mod_rows(sh_ref, tm), _mod_rows(sc_ref, tm)).astype(BF16)
    q_ref[...], k_ref[...], lg_ref[...], v_ref[...], r_ref[...] = _gla_project(
        h, wt_ref, wgu_ref, bg_ref, v_ref.dtype)


def _gla_proj(x, mod, w_t, w_gu, b_gate, vr_dtype):
    m = x.shape[0]
    tm = min(ROW_TILE, m)
    row = functools.partial(_row_spec, tm)
    return pl.pallas_call(
        _gla_proj_kernel,
        grid=(m // tm,),
        in_specs=[row(D_MODEL), mod.spec(tm, 0), mod.spec(tm, 1), _full(w_t.shape),
                  _full(w_gu.shape), _full(b_gate.shape)],
        out_specs=[row(NK), row(NK), row(NK), row(NV), row(NV)],
        out_shape=[
            jax.ShapeDtypeStruct((m, NK), F32),
            jax.ShapeDtypeStruct((m, NK), F32),
            jax.ShapeDtypeStruct((m, NK), F32),
            jax.ShapeDtypeStruct((m, NV), vr_dtype),
            jax.ShapeDtypeStruct((m, NV), vr_dtype),
        ],
        compiler_params=_cparams(1),
        name="gla_proj",
    )(x, mod.arr, mod.arr, w_t, w_gu, b_gate)


def _mix_mlp_stream_kernel(a_ref, x_ref, gt0_ref, sh_ref, sc_ref, gt1_ref, wo_ref, w1_ref, w2_ref,
                           g0_ref, b0_ref, g1_ref, b1_ref, y_ref, x1_s, h_s, acc_ref):
    c = pl.program_id(0)
    tm = x_ref.shape[0]

    @pl.when(c == 0)
    def _():
        o = jnp.dot(a_ref[...].astype(BF16), wo_ref[...], preferred_element_type=F32)
        x1 = _res_ln(x_ref[...], _mod_rows(gt0_ref, tm), o, g0_ref[...], b0_ref[...])
        x1_s[...] = x1
        h_s[...] = _modulate(x1, _mod_rows(sh_ref, tm), _mod_rows(sc_ref, tm)).astype(BF16)
        acc_ref[...] = jnp.zeros_like(acc_ref)

    a = jnp.dot(h_s[...], w1_ref[...], preferred_element_type=F32)
    a = jnp.square(jnp.maximum(a, 0.0)).astype(BF16)
    acc_ref[...] += jnp.dot(a, w2_ref[...], preferred_element_type=F32)

    @pl.when(c == pl.num_programs(0) - 1)
    def _():
        y_ref[...] = _res_ln(x1_s[...], _mod_rows(gt1_ref, tm), acc_ref[...],
                             g1_ref[...], b1_ref[...])


def _mix_mlp_stream(a, x, mod_mix, mod_mlp, w_out, w1, w2, ln_g, ln_b, layer):
    m = x.shape[0]
    assert m <= ROW_TILE and mod_mix.per_row and mod_mlp.per_row
    fc = FF_CHUNK
    whole = lambda n: pl.BlockSpec((m, n), lambda c: (0, 0))
    mspec = lambda mod, col: mod.spec(m, col)
    if w1.ndim == 2:
        w1_cols = pl.BlockSpec((D_MODEL, fc), lambda c: (0, c))
        w2_rows = pl.BlockSpec((fc, D_MODEL), lambda c: (c, 0))
    else:
        w1_cols = pl.BlockSpec((None, D_MODEL, fc), lambda c: (layer, 0, c))
        w2_rows = pl.BlockSpec((None, fc, D_MODEL), lambda c: (layer, c, 0))
    return pl.pallas_call(
        _mix_mlp_stream_kernel,
        grid=(D_FF // fc,),
        in_specs=[whole(a.shape[1]), whole(D_MODEL), mspec(mod_mix, 2), mspec(mod_mlp, 0),
                  mspec(mod_mlp, 1), mspec(mod_mlp, 2), _full(w_out.shape),
                  w1_cols, w2_rows,
                  _layer(ln_g, 2 * layer), _layer(ln_b, 2 * layer),
                  _layer(ln_g, 2 * layer + 1), _layer(ln_b, 2 * layer + 1)],
        out_specs=whole(D_MODEL),
        out_shape=jax.ShapeDtypeStruct((m, D_MODEL), F32),
        scratch_shapes=[pltpu.VMEM((m, D_MODEL), F32), pltpu.VMEM((m, D_MODEL), BF16),
                        pltpu.VMEM((m, D_MODEL), F32)],
        compiler_params=_cparams(1),
        name="mix_mlp_stream",
    )(a, x, mod_mix.arr, mod_mlp.arr, mod_mlp.arr, mod_mlp.arr, w_out, w1, w2,
      ln_g, ln_b, ln_g, ln_b)


def _alibi_slope(head):
    return 2.0 ** (-8.0 * (head + 1) / N_Q_HEADS)


def _softmax_sink(s, sink):
    m = jnp.maximum(jnp.max(s, axis=-1, keepdims=True), sink)
    e = jnp.exp(s - m)
    den = jnp.sum(e, axis=-1, keepdims=True) + jnp.exp(sink - m)
    return e / den


def _band_bias_init(bias_ref):
    blk = ATTN_BLOCK
    c = lax.broadcasted_iota(jnp.int32, (2 * blk, blk), 0)
    r = lax.broadcasted_iota(jnp.int32, (2 * blk, blk), 1)
    dist = blk + r - c
    valid = (dist >= 0) & (dist <= WINDOW)
    distf = dist.astype(F32)
    for head in range(N_Q_HEADS):
        pen = -_alibi_slope(head) * distf
        bias_ref[0, head] = jnp.where(valid, pen, NEG_BIG)
        bias_ref[1, head] = jnp.where(valid & (c >= blk), pen, NEG_BIG)


def _band_scores(q, kk):
    blk = ATTN_BLOCK
    head_of_lane = lax.broadcasted_iota(jnp.int32, (blk, NKV), 1) // HEAD_DIM
    scores = []
    for b in range(q.shape[0] // blk):
        keys = kk[b * blk:(b + 2) * blk]
        for g in range(GROUP):
            qg = q[b * blk:(b + 1) * blk, g * NKV:(g + 1) * NKV]
            qm = jnp.concatenate(
                [jnp.where(head_of_lane == j, qg, jnp.zeros_like(qg))
                 for j in range(N_KV_HEADS)], axis=0)
            scores.append(lax.dot_general(keys, qm, (((1,), (1,)), ((), ())),
                                          preferred_element_type=F32))
    return scores


def _band_outputs(scores, vvt, first_tile, sinks_ref, bias_ref, o_ref):
    blk = ATTN_BLOCK
    for b in range(len(scores) // GROUP):
        table = 1 if first_tile and b == 0 else 0
        vals_t = vvt[:, b * blk:(b + 2) * blk]
        for g in range(GROUP):
            st_all = scores[b * GROUP + g]
            ps = []
            for j in range(N_KV_HEADS):
                head = j * GROUP + g
                sink = sinks_ref[head]
                st = st_all[:, j * blk:(j + 1) * blk] + bias_ref[table, head]
                m = jnp.maximum(jnp.max(st, axis=0, keepdims=True), sink)
                e = jnp.exp(st - m)
                den = jnp.sum(e, axis=0, keepdims=True) + jnp.exp(sink - m)
                ps.append((e * (1.0 / den)).astype(BF16))
            ot_all = jnp.dot(vals_t, jnp.concatenate(ps, axis=1),
                             preferred_element_type=F32)
            ot = jnp.concatenate(
                [ot_all[j * HEAD_DIM:(j + 1) * HEAD_DIM, j * blk:(j + 1) * blk]
                 for j in range(N_KV_HEADS)], axis=0)
            o_ref[b * blk:(b + 1) * blk, g * NKV:(g + 1) * NKV] = ot.T.astype(BF16)


def _mlp_chunks(h, w1_ref, w2_ref, acc_ref, chunks):
    for c in chunks:
        cols = slice(c * FF_CHUNK, (c + 1) * FF_CHUNK)
        a = jnp.dot(h, w1_ref[:, cols], preferred_element_type=F32)
        a = jnp.square(jnp.maximum(a, 0.0)).astype(BF16)
        d = jnp.dot(a, w2_ref[cols, :], preferred_element_type=F32)
        if c == 0:
            acc_ref[...] = d
        else:
            acc_ref[...] += d


def _attn_layer_kernel(sinks_ref, xc_ref, xp_ref, sh0_ref, sc0_ref, gt0_ref, sh1_ref, sc1_ref,
                       gt1_ref, win_ref, wo_ref, w1_ref, w2_ref, g0_ref, b0_ref, g1_ref, b1_ref,
                       dq_ref, dkn_ref, dvn_ref, dck_ref, dcv_ref, cw1_ref, cw2_ref, cw3_ref,
                       y_ref, kl_ref, vl_ref, do_ref, dnk_ref, dnv_ref,
                       cw1b_ref, cw2b_ref, cw3b_ref,
                       o_s, kprev_s, vtprev_s, acc_ref, bias_ref, zk_ref, zv_ref):
    i = pl.program_id(0)
    last = pl.num_programs(0) - 1
    dec = _DecAttn(sinks_ref, dq_ref, dkn_ref, dvn_ref, dck_ref, dcv_ref, do_ref, dnk_ref,
                   dnv_ref, zk_ref, zv_ref)

    def cast_weights():
        cw1b_ref[...] = cw1_ref[...].astype(BF16)
        cw2b_ref[...] = cw2_ref[...].astype(BF16)
        cw3b_ref[...] = cw3_ref[...].astype(BF16)
    slot = i % 2
    blk = ATTN_BLOCK
    tm = xc_ref.shape[0]
    dot = functools.partial(jnp.dot, preferred_element_type=F32)
    n_chunks = D_FF // FF_CHUNK

    def mlp_in():
        x1 = _res_ln(xp_ref[...], gt0_ref[0:1, :], dot(o_s[1 - slot], wo_ref[...]),
                     g0_ref[...], b0_ref[...])
        return x1, _modulate(x1, sh1_ref[0:1, :], sc1_ref[0:1, :]).astype(BF16)

    def mlp_out(x1):
        y_ref[...] = _res_ln(x1, gt1_ref[0:1, :], acc_ref[...], g1_ref[...], b1_ref[...])

    def mix_in():
        h_in = _modulate(xc_ref[...], sh0_ref[0:1, :], sc0_ref[0:1, :]).astype(BF16)
        q = (dot(h_in, win_ref[:, 0:NQ]) * (HEAD_DIM ** -0.5)).astype(BF16)
        k = dot(h_in, win_ref[:, NQ:NQ + NKV])
        v = dot(h_in, win_ref[:, NQ + NKV:NQ + 2 * NKV])
        kl_ref[...] = k[tm - blk:, :]
        vl_ref[...] = v[tm - blk:, :]
        k_bf = k.astype(BF16)
        vt = v.T.astype(BF16)
        kk = jnp.concatenate([kprev_s[...], k_bf], axis=0)
        vvt = jnp.concatenate([vtprev_s[...], vt], axis=1)
        scores = _band_scores(q, kk)
        kprev_s[...] = k_bf[tm - blk:, :]
        vtprev_s[...] = vt[:, tm - blk:]
        return scores, vvt

    @pl.when(i == 0)
    def _():
        _band_bias_init(bias_ref)
        kprev_s[...] = jnp.zeros_like(kprev_s)
        vtprev_s[...] = jnp.zeros_like(vtprev_s)
        dec.zero()
        cast_weights()
        scores, vvt = mix_in()
        dec_scored = dec.scores()
        _band_outputs(scores, vvt, True, sinks_ref, bias_ref, o_s.at[slot])
        dec.finish(dec_scored)

    @pl.when((i > 0) & (i < last))
    def _():
        x1, h_mlp = mlp_in()
        cast_weights()
        scores, vvt = mix_in()
        _mlp_chunks(h_mlp, w1_ref, w2_ref, acc_ref, range(0, 1))
        dec_scored = dec.scores()
        _mlp_chunks(h_mlp, w1_ref, w2_ref, acc_ref, range(1, n_chunks))
        _band_outputs(scores, vvt, False, sinks_ref, bias_ref, o_s.at[slot])
        dec.finish(dec_scored)
        mlp_out(x1)

    @pl.when(i == last)
    def _():
        x1, h_mlp = mlp_in()
        _mlp_chunks(h_mlp, w1_ref, w2_ref, acc_ref, range(n_chunks))
        mlp_out(x1)


def _attn_layer(x, mod_mix, mod_mlp, w_in, w_out, w1, w2, ln_g, ln_b, sinks, layer,
                dec_q, dec_k, dec_v, cache_kt, cache_vt, w1_f32, w2_f32, cast_layer, w3_f32):
    m = x.shape[0]
    tm = ATTN_TILE
    blk = ATTN_BLOCK
    n = m // tm
    assert not mod_mix.per_row and not mod_mlp.per_row
    nseq, win = cache_kt.shape[0], cache_kt.shape[2]
    sb = nseq // n
    n_new = dec_k.shape[0] // nseq
    assert sb * n == nseq and (sb * n_new) % SUBLANES == 0
    cur = pl.BlockSpec((tm, D_MODEL), lambda i: (jnp.minimum(i, n - 1), 0))
    prev = pl.BlockSpec((tm, D_MODEL), lambda i: (jnp.maximum(i - 1, 0), 0))
    last = pl.BlockSpec((blk, NKV), lambda i: (0, 0))
    dblk = lambda a: pl.BlockSpec((sb,) + a.shape[1:], lambda i: (jnp.minimum(i, n - 1), 0, 0))
    drow = lambda a: pl.BlockSpec((sb * n_new, a.shape[1]), lambda i: (jnp.minimum(i, n - 1), 0))
    c_in = lambda w: pl.BlockSpec((None, w.shape[1] // n, w.shape[2]),
                                  lambda i: (cast_layer, jnp.minimum(i, n - 1), 0))
    c_out = lambda w: pl.BlockSpec((w.shape[1] // n, w.shape[2]),
                                   lambda i: (jnp.minimum(i, n - 1), 0))
    assert all(w.shape[1] % (n * 2 * SUBLANES) == 0 for w in (w1_f32, w2_f32))
    r3 = pl.cdiv(pl.cdiv(w3_f32.shape[0], n), 2 * SUBLANES) * 2 * SUBLANES
    n3 = pl.cdiv(w3_f32.shape[0], r3)
    c3 = pl.BlockSpec((r3, w3_f32.shape[1]), lambda i: (jnp.minimum(i, n3 - 1), 0))
    return pl.pallas_call(
        _attn_layer_kernel,
        grid=(n + 1,),
        in_specs=[pl.BlockSpec(memory_space=pltpu.SMEM), cur, prev,
                  mod_mix.spec(tm, 0), mod_mix.spec(tm, 1), mod_mix.spec(tm, 2),
                  mod_mlp.spec(tm, 0), mod_mlp.spec(tm, 1), mod_mlp.spec(tm, 2),
                  _full(w_in.shape), _full(w_out.shape), _resident(w1, layer),
                  _resident(w2, layer),
                  _layer(ln_g, 2 * layer), _layer(ln_b, 2 * layer),
                  _layer(ln_g, 2 * layer + 1), _layer(ln_b, 2 * layer + 1),
                  dblk(dec_q), drow(dec_k), drow(dec_v), dblk(cache_kt), dblk(cache_vt),
                  c_in(w1_f32), c_in(w2_f32), c3],
        out_specs=[prev, last, last, dblk(dec_q), dblk(cache_kt), dblk(cache_vt),
                   c_out(w1_f32), c_out(w2_f32), c3],
        out_shape=[jax.ShapeDtypeStruct((m, D_MODEL), F32),
                   jax.ShapeDtypeStruct((blk, NKV), F32),
                   jax.ShapeDtypeStruct((blk, NKV), F32),
                   jax.ShapeDtypeStruct(dec_q.shape, BF16),
                   jax.ShapeDtypeStruct(cache_kt.shape, F32),
                   jax.ShapeDtypeStruct(cache_vt.shape, F32),
                   jax.ShapeDtypeStruct(w1_f32.shape[1:], BF16),
                   jax.ShapeDtypeStruct(w2_f32.shape[1:], BF16),
                   jax.ShapeDtypeStruct(w3_f32.shape, BF16)],
        scratch_shapes=[pltpu.VMEM((2, tm, NQ), BF16),
                        pltpu.VMEM((blk, NKV), BF16),
                        pltpu.VMEM((NKV, blk), BF16),
                        pltpu.VMEM((tm, D_MODEL), F32),
                        pltpu.VMEM((2, N_Q_HEADS, 2 * blk, blk), F32),
                        pltpu.VMEM((sb, win, NKV), F32),
                        pltpu.VMEM((sb, win, NKV), F32)],
        compiler_params=_cparams(1),
        name="attn_layer",
    )(sinks, x, x, mod_mix.arr, mod_mix.arr, mod_mix.arr, mod_mlp.arr, mod_mlp.arr,
      mod_mlp.arr, w_in, w_out, w1, w2, ln_g, ln_b, ln_g, ln_b,
      dec_q, dec_k, dec_v, cache_kt, cache_vt, w1_f32, w2_f32, w3_f32)


class _DecAttn:
    def __init__(self, sinks_ref, q_ref, kn_ref, vn_ref, ck_ref, cv_ref, o_ref, nk_ref, nv_ref,
                 zk_ref, zv_ref):
        self.refs = (sinks_ref, q_ref, kn_ref, vn_ref, ck_ref, cv_ref, o_ref, nk_ref, nv_ref,
                     zk_ref, zv_ref)
        self.n_seq = ck_ref.shape[0]
        self.n_new = kn_ref.shape[0] // self.n_seq
        self.win = ck_ref.shape[2]
        assert self.win == LANES and self.n_new < SUBLANES

    def zero(self):
        for ref in self.refs[9:]:
            ref[...] = jnp.zeros_like(ref)

    def scores(self):
        (sinks_ref, q_ref, kn_ref, vn_ref, ck_ref, cv_ref, _, nk_ref, nv_ref,
         zk_ref, zv_ref) = self.refs
        n_new, win = self.n_new, self.win
        rows = n_new * GROUP
        keep = win - n_new
        head_of_lane = lax.broadcasted_iota(jnp.int32, (rows, NKV), 1) // HEAD_DIM
        lane_w = lax.broadcasted_iota(jnp.int32, (NKV, win), 1)
        out = []
        for s in range(self.n_seq):
            tok = slice(s * n_new, (s + 1) * n_new)
            k_t = ck_ref[s]
            v_t = cv_ref[s]
            zk_ref[s, keep:win, :] = kn_ref[tok, :]
            zv_ref[s, keep:win, :] = vn_ref[tok, :]
            zk_t = zk_ref[s].T
            zv_t = zv_ref[s].T
            nk_ref[s] = jnp.where(lane_w < keep, pltpu.roll(k_t, keep, 1), zk_t)
            nv_ref[s] = jnp.where(lane_w < keep, pltpu.roll(v_t, keep, 1), zv_t)
            keys = jnp.concatenate([k_t, zk_t], axis=1).astype(BF16)
            vals = jnp.concatenate([v_t, zv_t], axis=1).astype(BF16)
            qs = q_ref[s]
            qbd = jnp.concatenate(
                [jnp.where(head_of_lane == j, qs, jnp.zeros_like(qs))
                 for j in range(N_KV_HEADS)], axis=0)
            out.append((jnp.dot(qbd, keys, preferred_element_type=F32), vals))
        return out

    def finish(self, scored):
        sinks_ref, o_ref = self.refs[0], self.refs[6]
        n_new, win = self.n_new, self.win
        rows = n_new * GROUP
        n_rows = N_KV_HEADS * rows
        keep = win - n_new
        row = lax.broadcasted_iota(jnp.int32, (n_rows, 2 * win), 0)
        col = lax.broadcasted_iota(jnp.int32, (n_rows, 2 * win), 1)
        j_r = row // rows
        t_r = (row // GROUP) % n_new
        g_r = row % GROUP
        h_r = j_r * GROUP + g_r
        slope = jnp.exp2(-8.0 * (h_r + 1).astype(F32) / N_Q_HEADS)
        sink = jnp.zeros((n_rows, 1), F32)
        h_col = h_r[:, 0:1]
        for h in range(N_Q_HEADS):
            sink = jnp.where(h_col == h, sinks_ref[h], sink)
        is_key = (col < win) | (col >= win + keep)
        frame = jnp.where(col < win, col, col - keep)
        dist = t_r + win - frame
        valid = is_key & (dist >= 0) & (dist <= WINDOW)
        bias = jnp.where(valid, -slope * dist.astype(F32), NEG_BIG)
        head_of_lane = lax.broadcasted_iota(jnp.int32, (rows, NKV), 1) // HEAD_DIM
        for s, (sc, vals) in enumerate(scored):
            p = _softmax_sink(sc + bias, sink).astype(BF16)
            pv = lax.dot_general(p, vals, (((1,), (1,)), ((), ())),
                                 preferred_element_type=F32)
            o = jnp.zeros((rows, NKV), F32)
            for j in range(N_KV_HEADS):
                o = o + jnp.where(head_of_lane == j, pv[j * rows:(j + 1) * rows], 0.0)
            o_ref[s] = o.astype(BF16)


def _split2(x):
    hi = x.astype(BF16)
    lo = (x - hi.astype(F32)).astype(BF16)
    return hi, lo


def _gla_out(o, r, norm_g):
    ms = jnp.mean(o * o, axis=-1, keepdims=True)
    o = o * lax.rsqrt(ms + LN_EPS) * norm_g
    return o * (r * jax.nn.sigmoid(r))


def _gla_layer_kernel(xc_ref, xp_ref, sh0_ref, sc0_ref, gt0_ref, sh1_ref, sc1_ref, gt1_ref,
                      wt_ref, wgu_ref, bg_ref, ng_ref, wo_ref, w1_ref, w2_ref,
                      g0_ref, b0_ref, g1_ref, b1_ref,
                      dq_ref, dk_ref, dlg_ref, dv_ref, dr_ref, ds0_ref,
                      y_ref, s_out_ref, do_ref, ds1_ref,
                      o_s, s_ref, acc_ref, z_ref, v_all_ref, q_pad_ref):
    i = pl.program_id(0)
    last = pl.num_programs(0) - 1
    dec = _GlaStep(dq_ref, dk_ref, dlg_ref, dv_ref, dr_ref, ng_ref, ds0_ref, do_ref, ds1_ref,
                   z_ref, v_all_ref, q_pad_ref)
    sub = xc_ref.shape[0]
    ch = GLA_CHUNK
    n_ch = sub // ch
    n_chunks = D_FF // FF_CHUNK
    dot = functools.partial(jnp.dot, preferred_element_type=F32)
    heads = range(GLA_HEADS)
    ks = [slice(h * GLA_DK, (h + 1) * GLA_DK) for h in heads]
    vs = [slice(h * GLA_DV, (h + 1) * GLA_DV) for h in heads]

    def mlp_in():
        x1 = _res_ln(xp_ref[...], gt0_ref[0:1, :], dot(o_s[...], wo_ref[...]),
                     g0_ref[...], b0_ref[...])
        return x1, _modulate(x1, sh1_ref[0:1, :], sc1_ref[0:1, :]).astype(BF16)

    def mlp_out(x1):
        y_ref[...] = _res_ln(x1, gt1_ref[0:1, :], acc_ref[...], g1_ref[...], b1_ref[...])

    def decays():
        h_in = _modulate(xc_ref[...], sh0_ref[0:1, :], sc0_ref[0:1, :]).astype(BF16)
        q, k, lg, v, r = _gla_project(h_in, wt_ref, wgu_ref, bg_ref, BF16)
        row = lax.broadcasted_iota(jnp.int32, (sub, sub), 0)
        col = lax.broadcasted_iota(jnp.int32, (sub, sub), 1)
        causal = ((row // ch) == (col // ch)) & (col <= row)
        tril = jnp.where(causal, 1.0, 0.0).astype(BF16)
        hi, lo = _split2(lg)
        b = dot(tril, hi) + dot(tril, lo)
        ends = [b[(c + 1) * ch - 1:(c + 1) * ch, :] for c in range(n_ch)]
        b_end = jnp.concatenate([jnp.broadcast_to(e, (ch, NK)) for e in ends], axis=0)
        qd = (q * jnp.exp(b)).astype(BF16)
        kd = (k * jnp.exp(-b)).astype(BF16)
        kdec = k * jnp.exp(b_end - b)
        dec_rows = jnp.concatenate(
            [jnp.exp(e) for e in ends] + [jnp.zeros((LANES - n_ch, NK), F32)], axis=0)
        return causal, qd, kd, kdec, dec_rows, v, r

    def chunk_products(causal, qd, kd, kdec, v):
        chunk_of_col = lax.broadcasted_iota(jnp.int32, (GLA_DK, sub), 1) // ch
        a = [lax.dot_general(qd[:, ks[h]], kd[:, ks[h]], (((1,), (1,)), ((), ())),
                             preferred_element_type=F32) for h in heads]
        u = []
        for h in heads:
            kdec_t = kdec[:, ks[h]].T.astype(BF16)
            stacked = jnp.concatenate(
                [jnp.where(chunk_of_col == c, kdec_t, jnp.zeros_like(kdec_t))
                 for c in range(n_ch)], axis=0)
            u.append(dot(stacked, v[:, vs[h]]))
        o_intra = [dot(jnp.where(causal, a[h], 0.0).astype(BF16), v[:, vs[h]]) for h in heads]
        return u, o_intra

    def recurrence(qd, dec_rows, u, o_intra, r):
        for h in heads:
            s = s_ref[h]
            dec_t = dec_rows[:, ks[h]].T
            o_inter = []
            for c in range(n_ch):
                o_inter.append(dot(qd[c * ch:(c + 1) * ch, ks[h]], s.astype(BF16)))
                s = dec_t[:, c:c + 1] * s + u[h][c * GLA_DK:(c + 1) * GLA_DK]
            s_ref[h] = s
            o = o_intra[h] + jnp.concatenate(o_inter, axis=0)
            o_s[:, vs[h]] = _gla_out(o, r[:, vs[h]].astype(F32), ng_ref[...]).astype(BF16)

    @pl.when(i == 0)
    def _():
        s_ref[...] = jnp.zeros_like(s_ref)
        dec.zero()
        dec_intra = dec.stage()
        causal, qd, kd, kdec, dec_rows, v, r = decays()
        u, o_intra = chunk_products(causal, qd, kd, kdec, v)
        recurrence(qd, dec_rows, u, o_intra, r)
        dec.finish(dec_intra)

    @pl.when((i > 0) & (i < last))
    def _():
        x1, h_mlp = mlp_in()
        dec_intra = dec.stage()
        causal, qd, kd, kdec, dec_rows, v, r = decays()
        _mlp_chunks(h_mlp, w1_ref, w2_ref, acc_ref, range(0, 1))
        u, o_intra = chunk_products(causal, qd, kd, kdec, v)
        _mlp_chunks(h_mlp, w1_ref, w2_ref, acc_ref, range(1, n_chunks))
        recurrence(qd, dec_rows, u, o_intra, r)
        dec.finish(dec_intra)
        mlp_out(x1)

    @pl.when(i == last)
    def _():
        x1, h_mlp = mlp_in()
        _mlp_chunks(h_mlp, w1_ref, w2_ref, acc_ref, range(n_chunks))
        mlp_out(x1)
        s_out_ref[...] = s_ref[...]


def _gla_layer(x, mod_mix, mod_mlp, w_t, w_gu, b_gate, norm_g, w_out, w1, w2, ln_g, ln_b, layer,
               dec_rows, dec_state):
    m = x.shape[0]
    tm = GLA_TILE
    n = m // tm
    assert not mod_mix.per_row and not mod_mlp.per_row and tm % GLA_CHUNK == 0
    nseq = dec_state.shape[0]
    sb = nseq // n
    n_new = dec_rows[0].shape[0] // nseq
    assert sb * n == nseq and (sb * n_new) % SUBLANES == 0 and n_new <= GLA_CHUNK
    cur = pl.BlockSpec((tm, D_MODEL), lambda i: (jnp.minimum(i, n - 1), 0))
    prev = pl.BlockSpec((tm, D_MODEL), lambda i: (jnp.maximum(i - 1, 0), 0))
    drow = lambda a: pl.BlockSpec((sb * n_new, a.shape[1]), lambda i: (jnp.minimum(i, n - 1), 0))
    dstate = pl.BlockSpec((sb,) + dec_state.shape[1:],
                          lambda i: (jnp.minimum(i, n - 1), 0, 0, 0))
    state = (GLA_HEADS, GLA_DK, GLA_DV)
    dv = dec_rows[3]
    return pl.pallas_call(
        _gla_layer_kernel,
        grid=(n + 1,),
        in_specs=[cur, prev,
                  mod_mix.spec(tm, 0), mod_mix.spec(tm, 1), mod_mix.spec(tm, 2),
                  mod_mlp.spec(tm, 0), mod_mlp.spec(tm, 1), mod_mlp.spec(tm, 2),
                  _full(w_t.shape), _full(w_gu.shape), _full(b_gate.shape), _full(norm_g.shape),
                  _full(w_out.shape), _resident(w1, layer), _resident(w2, layer),
                  _layer(ln_g, 2 * layer), _layer(ln_b, 2 * layer),
                  _layer(ln_g, 2 * layer + 1), _layer(ln_b, 2 * layer + 1)]
                 + [drow(a) for a in dec_rows] + [dstate],
        out_specs=[prev, _full(state), drow(dv), dstate],
        out_shape=[jax.ShapeDtypeStruct((m, D_MODEL), F32), jax.ShapeDtypeStruct(state, F32),
                   jax.ShapeDtypeStruct(dv.shape, F32),
                   jax.ShapeDtypeStruct(dec_state.shape, F32)],
        scratch_shapes=[pltpu.VMEM((tm, NV), BF16),
                        pltpu.VMEM(state, F32),
                        pltpu.VMEM((tm, D_MODEL), F32),
                        pltpu.VMEM((GLA_HEADS, LANES, GLA_DK), F32),
                        pltpu.VMEM((GLA_HEADS, LANES, GLA_DV), F32),
                        pltpu.VMEM((sb, GLA_HEADS, SUBLANES, GLA_DK), F32)],
        compiler_params=_cparams(1),
        name="gla_layer",
    )(x, x, mod_mix.arr, mod_mix.arr, mod_mix.arr, mod_mlp.arr, mod_mlp.arr, mod_mlp.arr,
      w_t, w_gu, b_gate, norm_g, w_out, w1, w2, ln_g, ln_b, ln_g, ln_b, *dec_rows, dec_state)


class _GlaStep:
    def __init__(self, q_ref, k_ref, lg_ref, v_ref, r_ref, ng_ref, s0_ref, o_ref, s1_ref,
                 z_ref, v_all_ref, q_pad_ref):
        self.refs = (q_ref, k_ref, lg_ref, v_ref, r_ref, ng_ref, s0_ref, o_ref, s1_ref,
                     z_ref, v_all_ref, q_pad_ref)
        self.n_seq = s0_ref.shape[0]
        self.n_new = q_ref.shape[0] // self.n_seq
        assert self.n_new < SUBLANES and self.n_seq * SUBLANES <= LANES

    def zero(self):
        for ref in self.refs[9:]:
            ref[...] = jnp.zeros_like(ref)

    def stage(self):
        return _gla_step_stage(self.n_seq, self.n_new, *self.refs)

    def finish(self, o_intra):
        _gla_step_finish(o_intra, self.n_seq, self.n_new, *self.refs)


def _gla_step_stage(n_seq, n_new, q_ref, k_ref, lg_ref, v_ref, r_ref, ng_ref, s0_ref,
                    o_ref, s1_ref, z_ref, v_all_ref, q_pad_ref):
    tok = [slice(s * n_new, (s + 1) * n_new) for s in range(n_seq)]
    grp = SUBLANES
    rowi = lax.broadcasted_iota(jnp.int32, (n_new, NK), 0)
    rowv = lax.broadcasted_iota(jnp.int32, (n_new, GLA_DV), 0)
    heads = range(GLA_HEADS)
    ks = [slice(h * GLA_DK, (h + 1) * GLA_DK) for h in heads]
    vs = [slice(h * GLA_DV, (h + 1) * GLA_DV) for h in heads]
    o_intra = []
    for s in range(n_seq):
        lg = lg_ref[tok[s], :]
        b = jnp.zeros_like(lg)
        for t in range(n_new):
            b = b + jnp.where(rowi >= t, jnp.broadcast_to(lg[t:t + 1, :], lg.shape), 0.0)
        b_end = b[n_new - 1:n_new, :]
        qd = q_ref[tok[s], :] * jnp.exp(b)
        k = k_ref[tok[s], :]
        kd = k * jnp.exp(-b)
        kdec = k * jnp.exp(b_end - b)
        dec = jnp.exp(b_end)
        v = v_ref[tok[s], :]
        row0 = s * grp
        o_s = []
        for h in heads:
            v_h = v[:, vs[h]]
            qd_h = qd[:, ks[h]]
            o = jnp.zeros((n_new, GLA_DV), F32)
            for t in range(n_new):
                a_t = jnp.sum(qd_h * kd[t:t + 1, ks[h]], axis=-1, keepdims=True)
                o = o + jnp.where(rowv >= t, a_t * v_h[t:t + 1, :], 0.0)
            o_s.append(o)
            q_pad_ref[s, h, 0:n_new, :] = qd_h
            z_ref[h, row0:row0 + n_new, :] = kdec[:, ks[h]]
            z_ref[h, row0 + n_new:row0 + n_new + 1, :] = dec[:, ks[h]]
            v_all_ref[h, row0:row0 + n_new, :] = v_h
        o_intra.append(o_s)
    return o_intra


def _gla_step_finish(o_intra, n_seq, n_new, q_ref, k_ref, lg_ref, v_ref, r_ref, ng_ref, s0_ref,
                     o_ref, s1_ref, z_ref, v_all_ref, q_pad_ref):
    tok = [slice(s * n_new, (s + 1) * n_new) for s in range(n_seq)]
    grp = SUBLANES
    dot = functools.partial(jnp.dot, preferred_element_type=F32)
    heads = range(GLA_HEADS)
    vs = [slice(h * GLA_DV, (h + 1) * GLA_DV) for h in heads]
    group_of_lane = lax.broadcasted_iota(jnp.int32, (GLA_DK, LANES), 1) // grp
    for h in heads:
        z_t = z_ref[h].T
        v_all = v_all_ref[h]
        for s in range(n_seq):
            s0 = s0_ref[s, h]
            mine = jnp.where(group_of_lane == s, z_t, 0.0)
            col = s * grp + n_new
            s1_ref[s, h] = z_t[:, col:col + 1] * s0 + dot(mine, v_all)
            o = o_intra[s][h] + dot(q_pad_ref[s, h], s0)[0:n_new]
            o_ref[tok[s], vs[h]] = _gla_out(o, r_ref[tok[s], vs[h]], ng_ref[...])


def kernel(x_prompt, x_sample, cache_k, cache_v, state_gla, c_prompt, c_sample, w_mod, b_mod,
           ln_g, ln_b, attn_w_in, attn_w_out, attn_sinks, gla_w_in, gla_w_gate_up, gla_b_gate,
           gla_norm_g, gla_w_out, mlp_w1, mlp_w2):
    assert x_prompt.shape[0] == 1 and w_mod.shape[0] == DEPTH == 2
    seq = x_prompt.shape[1]
    nseq, n_new = x_sample.shape[0], x_sample.shape[1]
    win = cache_k.shape[2]
    m_s = nseq * n_new

    wq = attn_w_in[0][:, :NQ].reshape(D_MODEL, N_KV_HEADS, GROUP, HEAD_DIM)
    wq = wq.transpose(0, 2, 1, 3).reshape(D_MODEL, NQ)
    w_attn_in = jnp.concatenate([wq, attn_w_in[0][:, NQ:]], axis=1).astype(BF16)
    w_attn_out = attn_w_out[0].reshape(N_KV_HEADS, GROUP, HEAD_DIM, D_MODEL)
    w_attn_out = w_attn_out.transpose(1, 0, 2, 3).reshape(NQ, D_MODEL).astype(BF16)
    w_gla_t_f32 = gla_w_in[0].T
    assert w_gla_t_f32.shape[0] == 2 * NK + 2 * NV + GLA_GATE_RANK
    w_gla_gu = jnp.pad(gla_w_gate_up[0], ((LANES - GLA_GATE_RANK, 0), (0, 0))).astype(BF16)
    w_gla_out = gla_w_out[0].astype(BF16)
    b_gate = gla_b_gate[0].reshape(1, NK)
    norm_g = gla_norm_g[0].reshape(1, GLA_DV)
    ln_g4 = ln_g.reshape(2 * DEPTH, 1, D_MODEL)
    ln_b4 = ln_b.reshape(2 * DEPTH, 1, D_MODEL)
    sinks = attn_sinks[0]

    assert nseq % SUBLANES == 0 and m_s <= ROW_TILE
    c_all = jnp.concatenate([c_sample, c_prompt, jnp.zeros((SUBLANES - 1, D_MODEL), F32)],
                            axis=0)
    mod_all, w1_0, w2_0 = _adaln_all(c_all, w_mod.reshape(2 * DEPTH, D_MODEL, 3 * D_MODEL),
                                     b_mod.reshape(2 * DEPTH, 1, 3 * D_MODEL),
                                     mlp_w1, mlp_w2, 0)
    mods_p = [_Mod(mod_all, p, per_row=False, row0=nseq) for p in range(2 * DEPTH)]
    mods_s = [_Mod(mod_all, p, per_row=True, n_seq=nseq) for p in range(2 * DEPTH)]

    assert seq >= WINDOW == ATTN_BLOCK
    win_p = WINDOW
    to_slab = lambda c: c[0].transpose(0, 2, 3, 1).reshape(nseq, NKV, win)
    from_slab = lambda c: c.reshape(nseq, N_KV_HEADS, HEAD_DIM, win).transpose(0, 3, 1, 2)[None]

    x_s = x_sample.reshape(m_s, D_MODEL)
    q, k, v = _attn_proj(x_s, mods_s[0], w_attn_in)
    x1_p, k_p, v_p, o, k_s, v_s, w1_1, w2_1, w_gla_t = _attn_layer(
        x_prompt[0], mods_p[0], mods_p[1], w_attn_in, w_attn_out, w1_0, w2_0,
        ln_g4, ln_b4, sinks, 0, q.reshape(nseq, n_new * GROUP, NKV), k, v,
        to_slab(cache_k), to_slab(cache_v), mlp_w1, mlp_w2, 1, w_gla_t_f32)
    x1_s = _mix_mlp_stream(o.reshape(m_s, NQ), x_s, mods_s[0], mods_s[1], w_attn_out,
                           w1_0, w2_0, ln_g4, ln_b4, 0)

    gla_w = (w_gla_t, w_gla_gu, b_gate)
    dec_rows = _gla_proj(x1_s, mods_s[2], *gla_w, F32)
    y_p, s_p, o, s_s = _gla_layer(x1_p, mods_p[2], mods_p[3], *gla_w, norm_g, w_gla_out,
                                  w1_1, w2_1, ln_g4, ln_b4, 1, dec_rows, state_gla[0])
    y_s = _mix_mlp_stream(o, x1_s, mods_s[2], mods_s[3], w_gla_out, w1_1, w2_1,
                          ln_g4, ln_b4, 1)
    k_s, v_s = from_slab(k_s), from_slab(v_s)

    kv_shape_p = (1, 1, win_p, N_KV_HEADS, HEAD_DIM)
    return (y_p[None], y_s.reshape(nseq, n_new, D_MODEL),
            k_p.reshape(kv_shape_p), v_p.reshape(kv_shape_p), s_p[None, None],
            k_s, v_s, s_s[None])
```

```python
import functools

import jax
import jax.numpy as jnp
from jax import lax
from jax.experimental import pallas as pl
from jax.experimental.pallas import tpu as pltpu

F32 = jnp.float32
BF16 = jnp.bfloat16

D_MODEL = 1024
DEPTH = 2
HEAD_DIM = 64
N_Q_HEADS = 16
N_KV_HEADS = 4
GROUP = 4
WINDOW = 128
ATTN_BLOCK = 128
GLA_HEADS = 4
GLA_DK = 128
GLA_DV = 256
GLA_GATE_RANK = 16
GLA_TAU = 16.0
GLA_CHUNK = 64
D_FF = 4 * D_MODEL
ALPHA = (2.0 * DEPTH) ** 0.25
LN_EPS = 1e-5

NQ = N_Q_HEADS * HEAD_DIM
NKV = N_KV_HEADS * HEAD_DIM
NK = GLA_HEADS * GLA_DK
NV = GLA_HEADS * GLA_DV
LANES = 128
SUBLANES = 8
NEG_BIG = -1e30

ROW_TILE = 512
FF_CHUNK = 1024
ATTN_TILE = 256
GLA_TILE = 256
VMEM_LIMIT = 56 * 1024 * 1024


def _cparams(n_axes):
    return pltpu.CompilerParams(
        dimension_semantics=("arbitrary",) * n_axes,
        vmem_limit_bytes=VMEM_LIMIT,
    )


def _full(shape):
    zeros = (0,) * len(shape)
    return pl.BlockSpec(shape, lambda *_: zeros)


def _layer(arr, idx):
    tail = (0,) * (arr.ndim - 1)
    return pl.BlockSpec((None,) + arr.shape[1:], lambda *_: (idx,) + tail,
                        pipeline_mode=pl.Buffered(1))


def _resident(w, idx):
    return _full(w.shape) if w.ndim == 2 else _layer(w, idx)


def _row_spec(tm, n):
    return pl.BlockSpec((tm, n), lambda i: (i, 0))


class _Mod:
    def __init__(self, arr, p, per_row, row0=0, n_seq=0):
        self.arr, self.p, self.per_row, self.row0, self.n_seq = arr, p, per_row, row0, n_seq
        assert not per_row or n_seq > SUBLANES

    def spec(self, tm, col):
        p = self.p
        if self.per_row:
            return pl.BlockSpec((None, self.n_seq, D_MODEL), lambda i: (p, 0, col))
        blk = self.row0 // SUBLANES
        return pl.BlockSpec((None, SUBLANES, D_MODEL), lambda i: (p, blk, col))


def _repeat_rows(x, k):
    r = x.shape[0]
    row = lax.broadcasted_iota(jnp.int32, (r * k, r), 0)
    col = lax.broadcasted_iota(jnp.int32, (r * k, r), 1)
    pick = jnp.where(row // k == col, 1.0, 0.0).astype(BF16)
    hi = x.astype(BF16)
    rest = x - hi.astype(F32)
    mid = rest.astype(BF16)
    lo = (rest - mid.astype(F32)).astype(BF16)
    dot = functools.partial(jnp.dot, preferred_element_type=F32)
    return (dot(pick, hi) + dot(pick, mid)) + dot(pick, lo)


def _mod_rows(ref, tm):
    rows = ref.shape[0]
    if rows == SUBLANES:
        return ref[0:1, :]
    return _repeat_rows(ref[...], tm // rows)


def _modulate(x, shift, scale):
    return x * (1.0 + scale) + shift


def _res_ln(x, gate, o, g, b):
    y = ALPHA * x + gate * o
    mu = jnp.mean(y, axis=-1, keepdims=True)
    yc = y - mu
    var = jnp.mean(yc * yc, axis=-1, keepdims=True)
    return yc * lax.rsqrt(var + LN_EPS) * g + b


def _mod_kernel(c_ref, w_ref, b_ref, o_ref):
    c = c_ref[...]
    a = (c * jax.nn.sigmoid(c)).astype(BF16)
    o_ref[...] = jnp.dot(a, w_ref[...].astype(BF16), preferred_element_type=F32) + b_ref[...]


def _mod_cast_kernel(c_ref, w_ref, b_ref, cw1_ref, cw2_ref, o_ref, cw1b_ref, cw2b_ref):
    _mod_kernel(c_ref, w_ref, b_ref, o_ref)
    cw1b_ref[...] = cw1_ref[...].astype(BF16)
    cw2b_ref[...] = cw2_ref[...].astype(BF16)


def _adaln_all(c_all, w_mod, b_mod, w1_f32, w2_f32, cast_layer):
    rows = c_all.shape[0]
    n_sub = w_mod.shape[0]
    n_col = 2
    tn = 3 * D_MODEL // n_col
    steps = n_sub * n_col
    assert tn % LANES == 0
    assert all(w.shape[1] % (steps * 2 * SUBLANES) == 0 for w in (w1_f32, w2_f32))
    c_in = lambda w: pl.BlockSpec((None, w.shape[1] // steps, w.shape[2]),
                                  lambda p, n: (cast_layer, p * n_col + n, 0))
    c_out = lambda w: pl.BlockSpec((w.shape[1] // steps, w.shape[2]),
                                   lambda p, n: (p * n_col + n, 0))
    return pl.pallas_call(
        _mod_cast_kernel,
        grid=(n_sub, n_col),
        in_specs=[
            pl.BlockSpec((rows, D_MODEL), lambda p, n: (0, 0)),
            pl.BlockSpec((None, D_MODEL, tn), lambda p, n: (p, 0, n)),
            pl.BlockSpec((None, 1, tn), lambda p, n: (p, 0, n)),
            c_in(w1_f32), c_in(w2_f32),
        ],
        out_specs=[pl.BlockSpec((None, rows, tn), lambda p, n: (p, 0, n)),
                   c_out(w1_f32), c_out(w2_f32)],
        out_shape=[jax.ShapeDtypeStruct((n_sub, rows, 3 * D_MODEL), F32),
                   jax.ShapeDtypeStruct(w1_f32.shape[1:], BF16),
                   jax.ShapeDtypeStruct(w2_f32.shape[1:], BF16)],
        compiler_params=_cparams(2),
        name="adaln_mod",
    )(c_all, w_mod, b_mod, w1_f32, w2_f32)


def _attn_proj_kernel(x_ref, sh_ref, sc_ref, w_ref, q_ref, k_ref, v_ref):
    tm = x_ref.shape[0]
    h = _modulate(x_ref[...], _mod_rows(sh_ref, tm), _mod_rows(sc_ref, tm)).astype(BF16)
    q = jnp.dot(h, w_ref[:, 0:NQ], preferred_element_type=F32)
    q_ref[...] = (q * (HEAD_DIM ** -0.5)).astype(BF16)
    k_ref[...] = jnp.dot(h, w_ref[:, NQ:NQ + NKV], preferred_element_type=F32)
    v_ref[...] = jnp.dot(h, w_ref[:, NQ + NKV:NQ + 2 * NKV], preferred_element_type=F32)


def _attn_proj(x, mod, w_in):
    m = x.shape[0]
    tm = min(ROW_TILE, m)
    row = functools.partial(_row_spec, tm)
    return pl.pallas_call(
        _attn_proj_kernel,
        grid=(m // tm,),
        in_specs=[row(D_MODEL), mod.spec(tm, 0), mod.spec(tm, 1), _full(w_in.shape)],
        out_specs=[row(NQ), row(NKV), row(NKV)],
        out_shape=[
            jax.ShapeDtypeStruct((m, NQ), BF16),
            jax.ShapeDtypeStruct((m, NKV), F32),
            jax.ShapeDtypeStruct((m, NKV), F32),
        ],
        compiler_params=_cparams(1),
        name="attn_proj",
    )(x, mod.arr, mod.arr, w_in)


def _gla_project(h, wt_ref, wgu_ref, bg_ref, vr_dtype):
    def proj(lo, hi):
        return lax.dot_general(h, wt_ref[lo:hi, :], (((1,), (1,)), ((), ())),
                               preferred_element_type=F32)

    n_all = wt_ref.shape[0]
    gdown = proj(n_all - LANES, n_all)
    q = proj(0, NK) * (GLA_DK ** -0.5)
    k = proj(NK, 2 * NK)
    pre = jnp.dot(gdown.astype(BF16), wgu_ref[...], preferred_element_type=F32) + bg_ref[...]
    v = proj(2 * NK, 2 * NK + NV).astype(vr_dtype)
    r = proj(2 * NK + NV, 2 * NK + 2 * NV).astype(vr_dtype)
    log_sig = jnp.minimum(pre, 0.0) - jnp.log1p(jnp.exp(-jnp.abs(pre)))
    return q, k, log_sig / GLA_TAU, v, r


def _gla_proj_kernel(x_ref, sh_ref, sc_ref, wt_ref, wgu_ref, bg_ref,
                     q_ref, k_ref, lg_ref, v_ref, r_ref):
    tm = x_ref.shape[0]
    h = _modulate(x_ref[...], _mod_rows(sh_ref, tm), _mod_rows(sc_ref, tm)).astype(BF16)
    q_ref[...], k_ref[...], lg_ref[...], v_ref[...], r_ref[...] = _gla_project(
        h, wt_ref, wgu_ref, bg_ref, v_ref.dtype)


def _gla_proj(x, mod, w_t, w_gu, b_gate, vr_dtype):
    m = x.shape[0]
    tm = min(ROW_TILE, m)
    row = functools.partial(_row_spec, tm)
    return pl.pallas_call(
        _gla_proj_kernel,
        grid=(m // tm,),
        in_specs=[row(D_MODEL), mod.spec(tm, 0), mod.spec(tm, 1), _full(w_t.shape),
                  _full(w_gu.shape), _full(b_gate.shape)],
        out_specs=[row(NK), row(NK), row(NK), row(NV), row(NV)],
        out_shape=[
            jax.ShapeDtypeStruct((m, NK), F32),
            jax.ShapeDtypeStruct((m, NK), F32),
            jax.ShapeDtypeStruct((m, NK), F32),
            jax.ShapeDtypeStruct((m, NV), vr_dtype),
            jax.ShapeDtypeStruct((m, NV), vr_dtype),
        ],
        compiler_params=_cparams(1),
        name="gla_proj",
    )(x, mod.arr, mod.arr, w_t, w_gu, b_gate)


def _mix_mlp_stream_kernel(a_ref, x_ref, gt0_ref, sh_ref, sc_ref, gt1_ref, wo_ref, w1_ref, w2_ref,
                           g0_ref, b0_ref, g1_ref, b1_ref, y_ref, x1_s, h_s, acc_ref):
    c = pl.program_id(0)
    tm = x_ref.shape[0]

    @pl.when(c == 0)
    def _():
        o = jnp.dot(a_ref[...].astype(BF16), wo_ref[...], preferred_element_type=F32)
        x1 = _res_ln(x_ref[...], _mod_rows(gt0_ref, tm), o, g0_ref[...], b0_ref[...])
        x1_s[...] = x1
        h_s[...] = _modulate(x1, _mod_rows(sh_ref, tm), _mod_rows(sc_ref, tm)).astype(BF16)
        acc_ref[...] = jnp.zeros_like(acc_ref)

    a = jnp.dot(h_s[...], w1_ref[...], preferred_element_type=F32)
    a = jnp.square(jnp.maximum(a, 0.0)).astype(BF16)
    acc_ref[...] += jnp.dot(a, w2_ref[...], preferred_element_type=F32)

    @pl.when(c == pl.num_programs(0) - 1)
    def _():
        y_ref[...] = _res_ln(x1_s[...], _mod_rows(gt1_ref, tm), acc_ref[...],
                             g1_ref[...], b1_ref[...])


def _mix_mlp_stream(a, x, mod_mix, mod_mlp, w_out, w1, w2, ln_g, ln_b, layer):
    m = x.shape[0]
    assert m <= ROW_TILE and mod_mix.per_row and mod_mlp.per_row
    fc = FF_CHUNK
    whole = lambda n: pl.BlockSpec((m, n), lambda c: (0, 0))
    mspec = lambda mod, col: mod.spec(m, col)
    if w1.ndim == 2:
        w1_cols = pl.BlockSpec((D_MODEL, fc), lambda c: (0, c))
        w2_rows = pl.BlockSpec((fc, D_MODEL), lambda c: (c, 0))
    else:
        w1_cols = pl.BlockSpec((None, D_MODEL, fc), lambda c: (layer, 0, c))
        w2_rows = pl.BlockSpec((None, fc, D_MODEL), lambda c: (layer, c, 0))
    return pl.pallas_call(
        _mix_mlp_stream_kernel,
        grid=(D_FF // fc,),
        in_specs=[whole(a.shape[1]), whole(D_MODEL), mspec(mod_mix, 2), mspec(mod_mlp, 0),
                  mspec(mod_mlp, 1), mspec(mod_mlp, 2), _full(w_out.shape),
                  w1_cols, w2_rows,
                  _layer(ln_g, 2 * layer), _layer(ln_b, 2 * layer),
                  _layer(ln_g, 2 * layer + 1), _layer(ln_b, 2 * layer + 1)],
        out_specs=whole(D_MODEL),
        out_shape=jax.ShapeDtypeStruct((m, D_MODEL), F32),
        scratch_shapes=[pltpu.VMEM((m, D_MODEL), F32), pltpu.VMEM((m, D_MODEL), BF16),
                        pltpu.VMEM((m, D_MODEL), F32)],
        compiler_params=_cparams(1),
        name="mix_mlp_stream",
    )(a, x, mod_mix.arr, mod_mlp.arr, mod_mlp.arr, mod_mlp.arr, w_out, w1, w2,
      ln_g, ln_b, ln_g, ln_b)


def _alibi_slope(head):
    return 2.0 ** (-8.0 * (head + 1) / N_Q_HEADS)


def _softmax_sink(s, sink):
    m = jnp.maximum(jnp.max(s, axis=-1, keepdims=True), sink)
    e = jnp.exp(s - m)
    den = jnp.sum(e, axis=-1, keepdims=True) + jnp.exp(sink - m)
    return e / den


def _band_bias_init(bias_ref):
    blk = ATTN_BLOCK
    c = lax.broadcasted_iota(jnp.int32, (2 * blk, blk), 0)
    r = lax.broadcasted_iota(jnp.int32, (2 * blk, blk), 1)
    dist = blk + r - c
    valid = (dist >= 0) & (dist <= WINDOW)
    distf = dist.astype(F32)
    for head in range(N_Q_HEADS):
        pen = -_alibi_slope(head) * distf
        bias_ref[0, head] = jnp.where(valid, pen, NEG_BIG)
        bias_ref[1, head] = jnp.where(valid & (c >= blk), pen, NEG_BIG)


def _band_scores(q, kk):
    blk = ATTN_BLOCK
    head_of_lane = lax.broadcasted_iota(jnp.int32, (blk, NKV), 1) // HEAD_DIM
    scores = []
    for b in range(q.shape[0] // blk):
        keys = kk[b * blk:(b + 2) * blk]
        for g in range(GROUP):
            qg = q[b * blk:(b + 1) * blk, g * NKV:(g + 1) * NKV]
            qm = jnp.concatenate(
                [jnp.where(head_of_lane == j, qg, jnp.zeros_like(qg))
                 for j in range(N_KV_HEADS)], axis=0)
            scores.append(lax.dot_general(keys, qm, (((1,), (1,)), ((), ())),
                                          preferred_element_type=F32))
    return scores


def _band_outputs(scores, vvt, first_tile, sinks_ref, bias_ref, o_ref):
    blk = ATTN_BLOCK
    for b in range(len(scores) // GROUP):
        table = 1 if first_tile and b == 0 else 0
        vals_t = vvt[:, b * blk:(b + 2) * blk]
        for g in range(GROUP):
            st_all = scores[b * GROUP + g]
            ps = []
            for j in range(N_KV_HEADS):
                head = j * GROUP + g
                sink = sinks_ref[head]
                st = st_all[:, j * blk:(j + 1) * blk] + bias_ref[table, head]
                m = jnp.maximum(jnp.max(st, axis=0, keepdims=True), sink)
                e = jnp.exp(st - m)
                den = jnp.sum(e, axis=0, keepdims=True) + jnp.exp(sink - m)
                ps.append((e * (1.0 / den)).astype(BF16))
            ot_all = jnp.dot(vals_t, jnp.concatenate(ps, axis=1),
                             preferred_element_type=F32)
            ot = jnp.concatenate(
                [ot_all[j * HEAD_DIM:(j + 1) * HEAD_DIM, j * blk:(j + 1) * blk]
                 for j in range(N_KV_HEADS)], axis=0)
            o_ref[b * blk:(b + 1) * blk, g * NKV:(g + 1) * NKV] = ot.T.astype(BF16)


def _mlp_chunks(h, w1_ref, w2_ref, acc_ref, chunks):
    for c in chunks:
        cols = slice(c * FF_CHUNK, (c + 1) * FF_CHUNK)
        a = jnp.dot(h, w1_ref[:, cols], preferred_element_type=F32)
        a = jnp.square(jnp.maximum(a, 0.0)).astype(BF16)
        d = jnp.dot(a, w2_ref[cols, :], preferred_element_type=F32)
        if c == 0:
            acc_ref[...] = d
        else:
            acc_ref[...] += d


def _attn_layer_kernel(sinks_ref, xc_ref, xp_ref, sh0_ref, sc0_ref, gt0_ref, sh1_ref, sc1_ref,
                       gt1_ref, win_ref, wo_ref, w1_ref, w2_ref, g0_ref, b0_ref, g1_ref, b1_ref,
                       dq_ref, dkn_ref, dvn_ref, dck_ref, dcv_ref, cw1_ref, cw2_ref, cw3_ref,
                       y_ref, kl_ref, vl_ref, do_ref, dnk_ref, dnv_ref,
                       cw1b_ref, cw2b_ref, cw3b_ref,
                       o_s, kprev_s, vtprev_s, acc_ref, bias_ref, zk_ref, zv_ref):
    i = pl.program_id(0)
    last = pl.num_programs(0) - 1
    dec = _DecAttn(sinks_ref, dq_ref, dkn_ref, dvn_ref, dck_ref, dcv_ref, do_ref, dnk_ref,
                   dnv_ref, zk_ref, zv_ref)

    def cast_weights():
        cw1b_ref[...] = cw1_ref[...].astype(BF16)
        cw2b_ref[...] = cw2_ref[...].astype(BF16)
        cw3b_ref[...] = cw3_ref[...].astype(BF16)
    slot = i % 2
    blk = ATTN_BLOCK
    tm = xc_ref.shape[0]
    dot = functools.partial(jnp.dot, preferred_element_type=F32)
    n_chunks = D_FF // FF_CHUNK

    def mlp_in():
        x1 = _res_ln(xp_ref[...], gt0_ref[0:1, :], dot(o_s[1 - slot], wo_ref[...]),
                     g0_ref[...], b0_ref[...])
        return x1, _modulate(x1, sh1_ref[0:1, :], sc1_ref[0:1, :]).astype(BF16)

    def mlp_out(x1):
        y_ref[...] = _res_ln(x1, gt1_ref[0:1, :], acc_ref[...], g1_ref[...], b1_ref[...])

    def mix_in():
        h_in = _modulate(xc_ref[...], sh0_ref[0:1, :], sc0_ref[0:1, :]).astype(BF16)
        q = (dot(h_in, win_ref[:, 0:NQ]) * (HEAD_DIM ** -0.5)).astype(BF16)
        k = dot(h_in, win_ref[:, NQ:NQ + NKV])
        v = dot(h_in, win_ref[:, NQ + NKV:NQ + 2 * NKV])
        kl_ref[...] = k[tm - blk:, :]
        vl_ref[...] = v[tm - blk:, :]
        k_bf = k.astype(BF16)
        vt = v.T.astype(BF16)
        kk = jnp.concatenate([kprev_s[...], k_bf], axis=0)
        vvt = jnp.concatenate([vtprev_s[...], vt], axis=1)
        scores = _band_scores(q, kk)
        kprev_s[...] = k_bf[tm - blk:, :]
        vtprev_s[...] = vt[:, tm - blk:]
        return scores, vvt

    @pl.when(i == 0)
    def _():
        _band_bias_init(bias_ref)
        kprev_s[...] = jnp.zeros_like(kprev_s)
        vtprev_s[...] = jnp.zeros_like(vtprev_s)
        dec.zero()
        cast_weights()
        scores, vvt = mix_in()
        dec_scored = dec.scores()
        _band_outputs(scores, vvt, True, sinks_ref, bias_ref, o_s.at[slot])
        dec.finish(dec_scored)

    @pl.when((i > 0) & (i < last))
    def _():
        x1, h_mlp = mlp_in()
        cast_weights()
        scores, vvt = mix_in()
        _mlp_chunks(h_mlp, w1_ref, w2_ref, acc_ref, range(0, 1))
        dec_scored = dec.scores()
        _mlp_chunks(h_mlp, w1_ref, w2_ref, acc_ref, range(1, n_chunks))
        _band_outputs(scores, vvt, False, sinks_ref, bias_ref, o_s.at[slot])
        dec.finish(dec_scored)
        mlp_out(x1)

    @pl.when(i == last)
    def _():
        x1, h_mlp = mlp_in()
        _mlp_chunks(h_mlp, w1_ref, w2_ref, acc_ref, range(n_chunks))
        mlp_out(x1)


def _attn_layer(x, mod_mix, mod_mlp, w_in, w_out, w1, w2, ln_g, ln_b, sinks, layer,
                dec_q, dec_k, dec_v, cache_kt, cache_vt, w1_f32, w2_f32, cast_layer, w3_f32):
    m = x.shape[0]
    tm = ATTN_TILE
    blk = ATTN_BLOCK
    n = m // tm
    assert not mod_mix.per_row and not mod_mlp.per_row
    nseq, win = cache_kt.shape[0], cache_kt.shape[2]
    sb = nseq // n
    n_new = dec_k.shape[0] // nseq
    assert sb * n == nseq and (sb * n_new) % SUBLANES == 0
    cur = pl.BlockSpec((tm, D_MODEL), lambda i: (jnp.minimum(i, n - 1), 0))
    prev = pl.BlockSpec((tm, D_MODEL), lambda i: (jnp.maximum(i - 1, 0), 0))
    last = pl.BlockSpec((blk, NKV), lambda i: (0, 0))
    dblk = lambda a: pl.BlockSpec((sb,) + a.shape[1:], lambda i: (jnp.minimum(i, n - 1), 0, 0))
    drow = lambda a: pl.BlockSpec((sb * n_new, a.shape[1]), lambda i: (jnp.minimum(i, n - 1), 0))
    c_in = lambda w: pl.BlockSpec((None, w.shape[1] // n, w.shape[2]),
                                  lambda i: (cast_layer, jnp.minimum(i, n - 1), 0))
    c_out = lambda w: pl.BlockSpec((w.shape[1] // n, w.shape[2]),
                                   lambda i: (jnp.minimum(i, n - 1), 0))
    assert all(w.shape[1] % (n * 2 * SUBLANES) == 0 for w in (w1_f32, w2_f32))
    r3 = pl.cdiv(pl.cdiv(w3_f32.shape[0], n), 2 * SUBLANES) * 2 * SUBLANES
    n3 = pl.cdiv(w3_f32.shape[0], r3)
    c3 = pl.BlockSpec((r3, w3_f32.shape[1]), lambda i: (jnp.minimum(i, n3 - 1), 0))
    return pl.pallas_call(
        _attn_layer_kernel,
        grid=(n + 1,),
        in_specs=[pl.BlockSpec(memory_space=pltpu.SMEM), cur, prev,
                  mod_mix.spec(tm, 0), mod_mix.spec(tm, 1), mod_mix.spec(tm, 2),
                  mod_mlp.spec(tm, 0), mod_mlp.spec(tm, 1), mod_mlp.spec(tm, 2),
                  _full(w_in.shape), _full(w_out.shape), _resident(w1, layer),
                  _resident(w2, layer),
                  _layer(ln_g, 2 * layer), _layer(ln_b, 2 * layer),
                  _layer(ln_g, 2 * layer + 1), _layer(ln_b, 2 * layer + 1),
                  dblk(dec_q), drow(dec_k), drow(dec_v), dblk(cache_kt), dblk(cache_vt),
                  c_in(w1_f32), c_in(w2_f32), c3],
        out_specs=[prev, last, last, dblk(dec_q), dblk(cache_kt), dblk(cache_vt),
                   c_out(w1_f32), c_out(w2_f32), c3],
        out_shape=[jax.ShapeDtypeStruct((m, D_MODEL), F32),
                   jax.ShapeDtypeStruct((blk, NKV), F32),
                   jax.ShapeDtypeStruct((blk, NKV), F32),
                   jax.ShapeDtypeStruct(dec_q.shape, BF16),
                   jax.ShapeDtypeStruct(cache_kt.shape, F32),
                   jax.ShapeDtypeStruct(cache_vt.shape, F32),
                   jax.ShapeDtypeStruct(w1_f32.shape[1:], BF16),
                   jax.ShapeDtypeStruct(w2_f32.shape[1:], BF16),
                   jax.ShapeDtypeStruct(w3_f32.shape, BF16)],
        scratch_shapes=[pltpu.VMEM((2, tm, NQ), BF16),
                        pltpu.VMEM((blk, NKV), BF16),
                        pltpu.VMEM((NKV, blk), BF16),
                        pltpu.VMEM((tm, D_MODEL), F32),
                        pltpu.VMEM((2, N_Q_HEADS, 2 * blk, blk), F32),
                        pltpu.VMEM((sb, win, NKV), F32),
                        pltpu.VMEM((sb, win, NKV), F32)],
        compiler_params=_cparams(1),
        name="attn_layer",
    )(sinks, x, x, mod_mix.arr, mod_mix.arr, mod_mix.arr, mod_mlp.arr, mod_mlp.arr,
      mod_mlp.arr, w_in, w_out, w1, w2, ln_g, ln_b, ln_g, ln_b,
      dec_q, dec_k, dec_v, cache_kt, cache_vt, w1_f32, w2_f32, w3_f32)


class _DecAttn:
    def __init__(self, sinks_ref, q_ref, kn_ref, vn_ref, ck_ref, cv_ref, o_ref, nk_ref, nv_ref,
                 zk_ref, zv_ref):
        self.refs = (sinks_ref, q_ref, kn_ref, vn_ref, ck_ref, cv_ref, o_ref, nk_ref, nv_ref,
                     zk_ref, zv_ref)
        self.n_seq = ck_ref.shape[0]
        self.n_new = kn_ref.shape[0] // self.n_seq
        self.win = ck_ref.shape[2]
        assert self.win == LANES and self.n_new < SUBLANES

    def zero(self):
        for ref in self.refs[9:]:
            ref[...] = jnp.zeros_like(ref)

    def scores(self):
        (sinks_ref, q_ref, kn_ref, vn_ref, ck_ref, cv_ref, _, nk_ref, nv_ref,
         zk_ref, zv_ref) = self.refs
        n_new, win = self.n_new, self.win
        rows = n_new * GROUP
        keep = win - n_new
        head_of_lane = lax.broadcasted_iota(jnp.int32, (rows, NKV), 1) // HEAD_DIM
        lane_w = lax.broadcasted_iota(jnp.int32, (NKV, win), 1)
        out = []
        for s in range(self.n_seq):
            tok = slice(s * n_new, (s + 1) * n_new)
            k_t = ck_ref[s]
            v_t = cv_ref[s]
            zk_ref[s, keep:win, :] = kn_ref[tok, :]
            zv_ref[s, keep:win, :] = vn_ref[tok, :]
            zk_t = zk_ref[s].T
            zv_t = zv_ref[s].T
            nk_ref[s] = jnp.where(lane_w < keep, pltpu.roll(k_t, keep, 1), zk_t)
            nv_ref[s] = jnp.where(lane_w < keep, pltpu.roll(v_t, keep, 1), zv_t)
            keys = jnp.concatenate([k_t, zk_t], axis=1).astype(BF16)
            vals = jnp.concatenate([v_t, zv_t], axis=1).astype(BF16)
            qs = q_ref[s]
            qbd = jnp.concatenate(
                [jnp.where(head_of_lane == j, qs, jnp.zeros_like(qs))
                 for j in range(N_KV_HEADS)], axis=0)
            out.append((jnp.dot(qbd, keys, preferred_element_type=F32), vals))
        return out

    def finish(self, scored):
        sinks_ref, o_ref = self.refs[0], self.refs[6]
        n_new, win = self.n_new, self.win
        rows = n_new * GROUP
        n_rows = N_KV_HEADS * rows
        keep = win - n_new
        row = lax.broadcasted_iota(jnp.int32, (n_rows, 2 * win), 0)
        col = lax.broadcasted_iota(jnp.int32, (n_rows, 2 * win), 1)
        j_r = row // rows
        t_r = (row // GROUP) % n_new
        g_r = row % GROUP
        h_r = j_r * GROUP + g_r
        slope = jnp.exp2(-8.0 * (h_r + 1).astype(F32) / N_Q_HEADS)
        sink = jnp.zeros((n_rows, 1), F32)
        h_col = h_r[:, 0:1]
        for h in range(N_Q_HEADS):
            sink = jnp.where(h_col == h, sinks_ref[h], sink)
        is_key = (col < win) | (col >= win + keep)
        frame = jnp.where(col < win, col, col - keep)
        dist = t_r + win - frame
        valid = is_key & (dist >= 0) & (dist <= WINDOW)
        bias = jnp.where(valid, -slope * dist.astype(F32), NEG_BIG)
        head_of_lane = lax.broadcasted_iota(jnp.int32, (rows, NKV), 1) // HEAD_DIM
        for s, (sc, vals) in enumerate(scored):
            p = _softmax_sink(sc + bias, sink).astype(BF16)
            pv = lax.dot_general(p, vals, (((1,), (1,)), ((), ())),
                                 preferred_element_type=F32)
            o = jnp.zeros((rows, NKV), F32)
            for j in range(N_KV_HEADS):
                o = o + jnp.where(head_of_lane == j, pv[j * rows:(j + 1) * rows], 0.0)
            o_ref[s] = o.astype(BF16)


def _split2(x):
    hi = x.astype(BF16)
    lo = (x - hi.astype(F32)).astype(BF16)
    return hi, lo


def _gla_out(o, r, norm_g):
    ms = jnp.mean(o * o, axis=-1, keepdims=True)
    o = o * lax.rsqrt(ms + LN_EPS) * norm_g
    return o * (r * jax.nn.sigmoid(r))


def _gla_layer_kernel(xc_ref, xp_ref, sh0_ref, sc0_ref, gt0_ref, sh1_ref, sc1_ref, gt1_ref,
                      wt_ref, wgu_ref, bg_ref, ng_ref, wo_ref, w1_ref, w2_ref,
                      g0_ref, b0_ref, g1_ref, b1_ref,
                      dq_ref, dk_ref, dlg_ref, dv_ref, dr_ref, ds0_ref,
                      y_ref, s_out_ref, do_ref, ds1_ref,
                      o_s, s_ref, acc_ref, z_ref, v_all_ref, q_pad_ref):
    i = pl.program_id(0)
    last = pl.num_programs(0) - 1
    dec = _GlaStep(dq_ref, dk_ref, dlg_ref, dv_ref, dr_ref, ng_ref, ds0_ref, do_ref, ds1_ref,
                   z_ref, v_all_ref, q_pad_ref)
    sub = xc_ref.shape[0]
    ch = GLA_CHUNK
    n_ch = sub // ch
    n_chunks = D_FF // FF_CHUNK
    dot = functools.partial(jnp.dot, preferred_element_type=F32)
    heads = range(GLA_HEADS)
    ks = [slice(h * GLA_DK, (h + 1) * GLA_DK) for h in heads]
    vs = [slice(h * GLA_DV, (h + 1) * GLA_DV) for h in heads]

    def mlp_in():
        x1 = _res_ln(xp_ref[...], gt0_ref[0:1, :], dot(o_s[...], wo_ref[...]),
                     g0_ref[...], b0_ref[...])
        return x1, _modulate(x1, sh1_ref[0:1, :], sc1_ref[0:1, :]).astype(BF16)

    def mlp_out(x1):
        y_ref[...] = _res_ln(x1, gt1_ref[0:1, :], acc_ref[...], g1_ref[...], b1_ref[...])

    def decays():
        h_in = _modulate(xc_ref[...], sh0_ref[0:1, :], sc0_ref[0:1, :]).astype(BF16)
        q, k, lg, v, r = _gla_project(h_in, wt_ref, wgu_ref, bg_ref, BF16)
        row = lax.broadcasted_iota(jnp.int32, (sub, sub), 0)
        col = lax.broadcasted_iota(jnp.int32, (sub, sub), 1)
        causal = ((row // ch) == (col // ch)) & (col <= row)
        tril = jnp.where(causal, 1.0, 0.0).astype(BF16)
        hi, lo = _split2(lg)
        b = dot(tril, hi) + dot(tril, lo)
        ends = [b[(c + 1) * ch - 1:(c + 1) * ch, :] for c in range(n_ch)]
        b_end = jnp.concatenate([jnp.broadcast_to(e, (ch, NK)) for e in ends], axis=0)
        qd = (q * jnp.exp(b)).astype(BF16)
        kd = (k * jnp.exp(-b)).astype(BF16)
        kdec = k * jnp.exp(b_end - b)
        dec_rows = jnp.concatenate(
            [jnp.exp(e) for e in ends] + [jnp.zeros((LANES - n_ch, NK), F32)], axis=0)
        return causal, qd, kd, kdec, dec_rows, v, r

    def chunk_products(causal, qd, kd, kdec, v):
        chunk_of_col = lax.broadcasted_iota(jnp.int32, (GLA_DK, sub), 1) // ch
        a = [lax.dot_general(qd[:, ks[h]], kd[:, ks[h]], (((1,), (1,)), ((), ())),
                             preferred_element_type=F32) for h in heads]
        u = []
        for h in heads:
            kdec_t = kdec[:, ks[h]].T.astype(BF16)
            stacked = jnp.concatenate(
                [jnp.where(chunk_of_col == c, kdec_t, jnp.zeros_like(kdec_t))
                 for c in range(n_ch)], axis=0)
            u.append(dot(stacked, v[:, vs[h]]))
        o_intra = [dot(jnp.where(causal, a[h], 0.0).astype(BF16), v[:, vs[h]]) for h in heads]
        return u, o_intra

    def recurrence(qd, dec_rows, u, o_intra, r):
        for h in heads:
            s = s_ref[h]
            dec_t = dec_rows[:, ks[h]].T
            o_inter = []
            for c in range(n_ch):
                o_inter.append(dot(qd[c * ch:(c + 1) * ch, ks[h]], s.astype(BF16)))
                s = dec_t[:, c:c + 1] * s + u[h][c * GLA_DK:(c + 1) * GLA_DK]
            s_ref[h] = s
            o = o_intra[h] + jnp.concatenate(o_inter, axis=0)
            o_s[:, vs[h]] = _gla_out(o, r[:, vs[h]].astype(F32), ng_ref[...]).astype(BF16)

    @pl.when(i == 0)
    def _():
        s_ref[...] = jnp.zeros_like(s_ref)
        dec.zero()
        dec_intra = dec.stage()
        causal, qd, kd, kdec, dec_rows, v, r = decays()
        u, o_intra = chunk_products(causal, qd, kd, kdec, v)
        recurrence(qd, dec_rows, u, o_intra, r)
        dec.finish(dec_intra)

    @pl.when((i > 0) & (i < last))
    def _():
        x1, h_mlp = mlp_in()
        dec_intra = dec.stage()
        causal, qd, kd, kdec, dec_rows, v, r = decays()
        _mlp_chunks(h_mlp, w1_ref, w2_ref, acc_ref, range(0, 1))
        u, o_intra = chunk_products(causal, qd, kd, kdec, v)
        _mlp_chunks(h_mlp, w1_ref, w2_ref, acc_ref, range(1, n_chunks))
        recurrence(qd, dec_rows, u, o_intra, r)
        dec.finish(dec_intra)
        mlp_out(x1)

    @pl.when(i == last)
    def _():
        x1, h_mlp = mlp_in()
        _mlp_chunks(h_mlp, w1_ref, w2_ref, acc_ref, range(n_chunks))
        mlp_out(x1)
        s_out_ref[...] = s_ref[...]


def _gla_layer(x, mod_mix, mod_mlp, w_t, w_gu, b_gate, norm_g, w_out, w1, w2, ln_g, ln_b, layer,
               dec_rows, dec_state):
    m = x.shape[0]
    tm = GLA_TILE
    n = m // tm
    assert not mod_mix.per_row and not mod_mlp.per_row and tm % GLA_CHUNK == 0
    nseq = dec_state.shape[0]
    sb = nseq // n
    n_new = dec_rows[0].shape[0] // nseq
    assert sb * n == nseq and (sb * n_new) % SUBLANES == 0 and n_new <= GLA_CHUNK
    cur = pl.BlockSpec((tm, D_MODEL), lambda i: (jnp.minimum(i, n - 1), 0))
    prev = pl.BlockSpec((tm, D_MODEL), lambda i: (jnp.maximum(i - 1, 0), 0))
    drow = lambda a: pl.BlockSpec((sb * n_new, a.shape[1]), lambda i: (jnp.minimum(i, n - 1), 0))
    dstate = pl.BlockSpec((sb,) + dec_state.shape[1:],
                          lambda i: (jnp.minimum(i, n - 1), 0, 0, 0))
    state = (GLA_HEADS, GLA_DK, GLA_DV)
    dv = dec_rows[3]
    return pl.pallas_call(
        _gla_layer_kernel,
        grid=(n + 1,),
        in_specs=[cur, prev,
                  mod_mix.spec(tm, 0), mod_mix.spec(tm, 1), mod_mix.spec(tm, 2),
                  mod_mlp.spec(tm, 0), mod_mlp.spec(tm, 1), mod_mlp.spec(tm, 2),
                  _full(w_t.shape), _full(w_gu.shape), _full(b_gate.shape), _full(norm_g.shape),
                  _full(w_out.shape), _resident(w1, layer), _resident(w2, layer),
                  _layer(ln_g, 2 * layer), _layer(ln_b, 2 * layer),
                  _layer(ln_g, 2 * layer + 1), _layer(ln_b, 2 * layer + 1)]
                 + [drow(a) for a in dec_rows] + [dstate],
        out_specs=[prev, _full(state), drow(dv), dstate],
        out_shape=[jax.ShapeDtypeStruct((m, D_MODEL), F32), jax.ShapeDtypeStruct(state, F32),
                   jax.ShapeDtypeStruct(dv.shape, F32),
                   jax.ShapeDtypeStruct(dec_state.shape, F32)],
        scratch_shapes=[pltpu.VMEM((tm, NV), BF16),
                        pltpu.VMEM(state, F32),
                        pltpu.VMEM((tm, D_MODEL), F32),
                        pltpu.VMEM((GLA_HEADS, LANES, GLA_DK), F32),
                        pltpu.VMEM((GLA_HEADS, LANES, GLA_DV), F32),
                        pltpu.VMEM((sb, GLA_HEADS, SUBLANES, GLA_DK), F32)],
        compiler_params=_cparams(1),
        name="gla_layer",
    )(x, x, mod_mix.arr, mod_mix.arr, mod_mix.arr, mod_mlp.arr, mod_mlp.arr, mod_mlp.arr,
      w_t, w_gu, b_gate, norm_g, w_out, w1, w2, ln_g, ln_b, ln_g, ln_b, *dec_rows, dec_state)


class _GlaStep:
    def __init__(self, q_ref, k_ref, lg_ref, v_ref, r_ref, ng_ref, s0_ref, o_ref, s1_ref,
                 z_ref, v_all_ref, q_pad_ref):
        self.refs = (q_ref, k_ref, lg_ref, v_ref, r_ref, ng_ref, s0_ref, o_ref, s1_ref,
                     z_ref, v_all_ref, q_pad_ref)
        self.n_seq = s0_ref.shape[0]
        self.n_new = q_ref.shape[0] // self.n_seq
        assert self.n_new < SUBLANES and self.n_seq * SUBLANES <= LANES

    def zero(self):
        for ref in self.refs[9:]:
            ref[...] = jnp.zeros_like(ref)

    def stage(self):
        return _gla_step_stage(self.n_seq, self.n_new, *self.refs)

    def finish(self, o_intra):
        _gla_step_finish(o_intra, self.n_seq, self.n_new, *self.refs)


def _gla_step_stage(n_seq, n_new, q_ref, k_ref, lg_ref, v_ref, r_ref, ng_ref, s0_ref,
                    o_ref, s1_ref, z_ref, v_all_ref, q_pad_ref):
    tok = [slice(s * n_new, (s + 1) * n_new) for s in range(n_seq)]
    grp = SUBLANES
    rowi = lax.broadcasted_iota(jnp.int32, (n_new, NK), 0)
    rowv = lax.broadcasted_iota(jnp.int32, (n_new, GLA_DV), 0)
    heads = range(GLA_HEADS)
    ks = [slice(h * GLA_DK, (h + 1) * GLA_DK) for h in heads]
    vs = [slice(h * GLA_DV, (h + 1) * GLA_DV) for h in heads]
    o_intra = []
    for s in range(n_seq):
        lg = lg_ref[tok[s], :]
        b = jnp.zeros_like(lg)
        for t in range(n_new):
            b = b + jnp.where(rowi >= t, jnp.broadcast_to(lg[t:t + 1, :], lg.shape), 0.0)
        b_end = b[n_new - 1:n_new, :]
        qd = q_ref[tok[s], :] * jnp.exp(b)
        k = k_ref[tok[s], :]
        kd = k * jnp.exp(-b)
        kdec = k * jnp.exp(b_end - b)
        dec = jnp.exp(b_end)
        v = v_ref[tok[s], :]
        row0 = s * grp
        o_s = []
        for h in heads:
            v_h = v[:, vs[h]]
            qd_h = qd[:, ks[h]]
            o = jnp.zeros((n_new, GLA_DV), F32)
            for t in range(n_new):
                a_t = jnp.sum(qd_h * kd[t:t + 1, ks[h]], axis=-1, keepdims=True)
                o = o + jnp.where(rowv >= t, a_t * v_h[t:t + 1, :], 0.0)
            o_s.append(o)
            q_pad_ref[s, h, 0:n_new, :] = qd_h
            z_ref[h, row0:row0 + n_new, :] = kdec[:, ks[h]]
            z_ref[h, row0 + n_new:row0 + n_new + 1, :] = dec[:, ks[h]]
            v_all_ref[h, row0:row0 + n_new, :] = v_h
        o_intra.append(o_s)
    return o_intra


def _gla_step_finish(o_intra, n_seq, n_new, q_ref, k_ref, lg_ref, v_ref, r_ref, ng_ref, s0_ref,
                     o_ref, s1_ref, z_ref, v_all_ref, q_pad_ref):
    tok = [slice(s * n_new, (s + 1) * n_new) for s in range(n_seq)]
    grp = SUBLANES
    dot = functools.partial(jnp.dot, preferred_element_type=F32)
    heads = range(GLA_HEADS)
    vs = [slice(h * GLA_DV, (h + 1) * GLA_DV) for h in heads]
    group_of_lane = lax.broadcasted_iota(jnp.int32, (GLA_DK, LANES), 1) // grp
    for h in heads:
        z_t = z_ref[h].T
        v_all = v_all_ref[h]
        for s in range(n_seq):
            s0 = s0_ref[s, h]
            mine = jnp.where(group_of_lane == s, z_t, 0.0)
            col = s * grp + n_new
            s1_ref[s, h] = z_t[:, col:col + 1] * s0 + dot(mine, v_all)
            o = o_intra[s][h] + dot(q_pad_ref[s, h], s0)[0:n_new]
            o_ref[tok[s], vs[h]] = _gla_out(o, r_ref[tok[s], vs[h]], ng_ref[...])


def kernel(x_prompt, x_sample, cache_k, cache_v, state_gla, c_prompt, c_sample, w_mod, b_mod,
           ln_g, ln_b, attn_w_in, attn_w_out, attn_sinks, gla_w_in, gla_w_gate_up, gla_b_gate,
           gla_norm_g, gla_w_out, mlp_w1, mlp_w2):
    assert x_prompt.shape[0] == 1 and w_mod.shape[0] == DEPTH == 2
    seq = x_prompt.shape[1]
    nseq, n_new = x_sample.shape[0], x_sample.shape[1]
    win = cache_k.shape[2]
    m_s = nseq * n_new

    wq = attn_w_in[0][:, :NQ].reshape(D_MODEL, N_KV_HEADS, GROUP, HEAD_DIM)
    wq = wq.transpose(0, 2, 1, 3).reshape(D_MODEL, NQ)
    w_attn_in = jnp.concatenate([wq, attn_w_in[0][:, NQ:]], axis=1).astype(BF16)
    w_attn_out = attn_w_out[0].reshape(N_KV_HEADS, GROUP, HEAD_DIM, D_MODEL)
    w_attn_out = w_attn_out.transpose(1, 0, 2, 3).reshape(NQ, D_MODEL).astype(BF16)
    w_gla_t_f32 = gla_w_in[0].T
    assert w_gla_t_f32.shape[0] == 2 * NK + 2 * NV + GLA_GATE_RANK
    w_gla_gu = jnp.pad(gla_w_gate_up[0], ((LANES - GLA_GATE_RANK, 0), (0, 0))).astype(BF16)
    w_gla_out = gla_w_out[0].astype(BF16)
    b_gate = gla_b_gate[0].reshape(1, NK)
    norm_g = gla_norm_g[0].reshape(1, GLA_DV)
    ln_g4 = ln_g.reshape(2 * DEPTH, 1, D_MODEL)
    ln_b4 = ln_b.reshape(2 * DEPTH, 1, D_MODEL)
    sinks = attn_sinks[0]

    assert nseq % SUBLANES == 0 and m_s <= ROW_TILE
    c_all = jnp.concatenate([c_sample, c_prompt, jnp.zeros((SUBLANES - 1, D_MODEL), F32)],
                            axis=0)
    mod_all, w1_0, w2_0 = _adaln_all(c_all, w_mod.reshape(2 * DEPTH, D_MODEL, 3 * D_MODEL),
                                     b_mod.reshape(2 * DEPTH, 1, 3 * D_MODEL),
                                     mlp_w1, mlp_w2, 0)
    mods_p = [_Mod(mod_all, p, per_row=False, row0=nseq) for p in range(2 * DEPTH)]
    mods_s = [_Mod(mod_all, p, per_row=True, n_seq=nseq) for p in range(2 * DEPTH)]

    assert seq >= WINDOW == ATTN_BLOCK
    win_p = WINDOW
    to_slab = lambda c: c[0].transpose(0, 2, 3, 1).reshape(nseq, NKV, win)
    from_slab = lambda c: c.reshape(nseq, N_KV_HEADS, HEAD_DIM, win).transpose(0, 3, 1, 2)[None]

    x_s = x_sample.reshape(m_s, D_MODEL)
    q, k, v = _attn_proj(x_s, mods_s[0], w_attn_in)
    x1_p, k_p, v_p, o, k_s, v_s, w1_1, w2_1, w_gla_t = _attn_layer(
        x_prompt[0], mods_p[0], mods_p[1], w_attn_in, w_attn_out, w1_0, w2_0,
        ln_g4, ln_b4, sinks, 0, q.reshape(nseq, n_new * GROUP, NKV), k, v,
        to_slab(cache_k), to_slab(cache_v), mlp_w1, mlp_w2, 1, w_gla_t_f32)
    x1_s = _mix_mlp_stream(o.reshape(m_s, NQ), x_s, mods_s[0], mods_s[1], w_attn_out,
                           w1_0, w2_0, ln_g4, ln_b4, 0)

    gla_w = (w_gla_t, w_gla_gu, b_gate)
    dec_rows = _gla_proj(x1_s, mods_s[2], *gla_w, F32)
    y_p, s_p, o, s_s = _gla_layer(x1_p, mods_p[2], mods_p[3], *gla_w, norm_g, w_gla_out,
                                  w1_1, w2_1, ln_g4, ln_b4, 1, dec_rows, state_gla[0])
    y_s = _mix_mlp_stream(o, x1_s, mods_s[2], mods_s[3], w_gla_out, w1_1, w2_1,
                          ln_g4, ln_b4, 1)
    k_s, v_s = from_slab(k_s), from_slab(v_s)

    kv_shape_p = (1, 1, win_p, N_KV_HEADS, HEAD_DIM)
    return (y_p[None], y_s.reshape(nseq, n_new, D_MODEL),
            k_p.reshape(kv_shape_p), v_p.reshape(kv_shape_p), s_p[None, None],
            k_s, v_s, s_s[None])
```

```python
import functools

import jax
import jax.numpy as jnp
from jax import lax
from jax.experimental import pallas as pl
from jax.experimental.pallas import tpu as pltpu

F32 = jnp.float32
BF16 = jnp.bfloat16

D_MODEL = 1024
DEPTH = 2
HEAD_DIM = 64
N_Q_HEADS = 16
N_KV_HEADS = 4
GROUP = 4
WINDOW = 128
ATTN_BLOCK = 128
GLA_HEADS = 4
GLA_DK = 128
GLA_DV = 256
GLA_GATE_RANK = 16
GLA_TAU = 16.0
GLA_CHUNK = 64
D_FF = 4 * D_MODEL
ALPHA = (2.0 * DEPTH) ** 0.25
LN_EPS = 1e-5

NQ = N_Q_HEADS * HEAD_DIM
NKV = N_KV_HEADS * HEAD_DIM
NK = GLA_HEADS * GLA_DK
NV = GLA_HEADS * GLA_DV
LANES = 128
SUBLANES = 8
NEG_BIG = -1e30

ROW_TILE = 512
FF_CHUNK = 2048
ATTN_TILE = 256
GLA_TILE = 256
VMEM_LIMIT = 56 * 1024 * 1024


def _cparams(n_axes):
    return pltpu.CompilerParams(
        dimension_semantics=("arbitrary",) * n_axes,
        vmem_limit_bytes=VMEM_LIMIT,
    )


def _full(shape):
    zeros = (0,) * len(shape)
    return pl.BlockSpec(shape, lambda *_: zeros)


def _layer(arr, idx):
    tail = (0,) * (arr.ndim - 1)
    return pl.BlockSpec((None,) + arr.shape[1:], lambda *_: (idx,) + tail,
                        pipeline_mode=pl.Buffered(1))


def _resident(w, idx):
    return _full(w.shape) if w.ndim == 2 else _layer(w, idx)


def _row_spec(tm, n):
    return pl.BlockSpec((tm, n), lambda i: (i, 0))


class _Mod:
    def __init__(self, arr, p, per_row, row0=0, n_seq=0):
        self.arr, self.p, self.per_row, self.row0, self.n_seq = arr, p, per_row, row0, n_seq
        assert not per_row or n_seq > SUBLANES

    def spec(self, tm, col):
        p = self.p
        if self.per_row:
            return pl.BlockSpec((None, self.n_seq, D_MODEL), lambda i: (p, 0, col))
        blk = self.row0 // SUBLANES
        return pl.BlockSpec((None, SUBLANES, D_MODEL), lambda i: (p, blk, col))


def _repeat_rows(x, k):
    r = x.shape[0]
    row = lax.broadcasted_iota(jnp.int32, (r * k, r), 0)
    col = lax.broadcasted_iota(jnp.int32, (r * k, r), 1)
    pick = jnp.where(row // k == col, 1.0, 0.0).astype(BF16)
    hi = x.astype(BF16)
    rest = x - hi.astype(F32)
    mid = rest.astype(BF16)
    lo = (rest - mid.astype(F32)).astype(BF16)
    dot = functools.partial(jnp.dot, preferred_element_type=F32)
    return (dot(pick, hi) + dot(pick, mid)) + dot(pick, lo)


def _mod_rows(ref, tm):
    rows = ref.shape[0]
    if rows == SUBLANES:
        return ref[0:1, :]
    return _repeat_rows(ref[...], tm // rows)


def _modulate(x, shift, scale):
    return x * (1.0 + scale) + shift


def _res_ln(x, gate, o, g, b):
    y = ALPHA * x + gate * o
    mu = jnp.mean(y, axis=-1, keepdims=True)
    yc = y - mu
    var = jnp.mean(yc * yc, axis=-1, keepdims=True)
    return yc * lax.rsqrt(var + LN_EPS) * g + b


def _mod_kernel(c_ref, w_ref, b_ref, o_ref):
    c = c_ref[...]
    a = (c * jax.nn.sigmoid(c)).astype(BF16)
    o_ref[...] = jnp.dot(a, w_ref[...].astype(BF16), preferred_element_type=F32) + b_ref[...]


def _mod_cast_kernel(c_ref, w_ref, b_ref, cw1_ref, cw2_ref, o_ref, cw1b_ref, cw2b_ref):
    _mod_kernel(c_ref, w_ref, b_ref, o_ref)
    cw1b_ref[...] = cw1_ref[...].astype(BF16)
    cw2b_ref[...] = cw2_ref[...].astype(BF16)


def _adaln_all(c_all, w_mod, b_mod, w1_f32, w2_f32, cast_layer):
    rows = c_all.shape[0]
    n_sub = w_mod.shape[0]
    n_col = 2
    tn = 3 * D_MODEL // n_col
    steps = n_sub * n_col
    assert tn % LANES == 0
    assert all(w.shape[1] % (steps * 2 * SUBLANES) == 0 for w in (w1_f32, w2_f32))
    c_in = lambda w: pl.BlockSpec((None, w.shape[1] // steps, w.shape[2]),
                                  lambda p, n: (cast_layer, p * n_col + n, 0))
    c_out = lambda w: pl.BlockSpec((w.shape[1] // steps, w.shape[2]),
                                   lambda p, n: (p * n_col + n, 0))
    return pl.pallas_call(
        _mod_cast_kernel,
        grid=(n_sub, n_col),
        in_specs=[
            pl.BlockSpec((rows, D_MODEL), lambda p, n: (0, 0)),
            pl.BlockSpec((None, D_MODEL, tn), lambda p, n: (p, 0, n)),
            pl.BlockSpec((None, 1, tn), lambda p, n: (p, 0, n)),
            c_in(w1_f32), c_in(w2_f32),
        ],
        out_specs=[pl.BlockSpec((None, rows, tn), lambda p, n: (p, 0, n)),
                   c_out(w1_f32), c_out(w2_f32)],
        out_shape=[jax.ShapeDtypeStruct((n_sub, rows, 3 * D_MODEL), F32),
                   jax.ShapeDtypeStruct(w1_f32.shape[1:], BF16),
                   jax.ShapeDtypeStruct(w2_f32.shape[1:], BF16)],
        compiler_params=_cparams(2),
        name="adaln_mod",
    )(c_all, w_mod, b_mod, w1_f32, w2_f32)


def _attn_proj_kernel(x_ref, sh_ref, sc_ref, w_ref, q_ref, k_ref, v_ref):
    tm = x_ref.shape[0]
    h = _modulate(x_ref[...], _mod_rows(sh_ref, tm), _mod_rows(sc_ref, tm)).astype(BF16)
    q = jnp.dot(h, w_ref[:, 0:NQ], preferred_element_type=F32)
    q_ref[...] = (q * (HEAD_DIM ** -0.5)).astype(BF16)
    k_ref[...] = jnp.dot(h, w_ref[:, NQ:NQ + NKV], preferred_element_type=F32)
    v_ref[...] = jnp.dot(h, w_ref[:, NQ + NKV:NQ + 2 * NKV], preferred_element_type=F32)


def _attn_proj(x, mod, w_in):
    m = x.shape[0]
    tm = min(ROW_TILE, m)
    row = functools.partial(_row_spec, tm)
    return pl.pallas_call(
        _attn_proj_kernel,
        grid=(m // tm,),
        in_specs=[row(D_MODEL), mod.spec(tm, 0), mod.spec(tm, 1), _full(w_in.shape)],
        out_specs=[row(NQ), row(NKV), row(NKV)],
        out_shape=[
            jax.ShapeDtypeStruct((m, NQ), BF16),
            jax.ShapeDtypeStruct((m, NKV), F32),
            jax.ShapeDtypeStruct((m, NKV), F32),
        ],
        compiler_params=_cparams(1),
        name="attn_proj",
    )(x, mod.arr, mod.arr, w_in)


def _gla_project(h, wt_ref, wgu_ref, bg_ref, vr_dtype):
    def proj(lo, hi):
        return lax.dot_general(h, wt_ref[lo:hi, :], (((1,), (1,)), ((), ())),
                               preferred_element_type=F32)

    n_all = wt_ref.shape[0]
    gdown = proj(n_all - LANES, n_all)
    q = proj(0, NK) * (GLA_DK ** -0.5)
    k = proj(NK, 2 * NK)
    pre = jnp.dot(gdown.astype(BF16), wgu_ref[...], preferred_element_type=F32) + bg_ref[...]
    v = proj(2 * NK, 2 * NK + NV).astype(vr_dtype)
    r = proj(2 * NK + NV, 2 * NK + 2 * NV).astype(vr_dtype)
    log_sig = jnp.minimum(pre, 0.0) - jnp.log1p(jnp.exp(-jnp.abs(pre)))
    return q, k, log_sig / GLA_TAU, v, r


def _gla_proj_kernel(x_ref, sh_ref, sc_ref, wt_ref, wgu_ref, bg_ref,
                     q_ref, k_ref, lg_ref, v_ref, r_ref):
    tm = x_ref.shape[0]
    h = _modulate(x_ref[...], _mod_rows(sh_ref, tm), _mod_rows(sc_ref, tm)).astype(BF16)
    q_ref[...], k_ref[...], lg_ref[...], v_ref[...], r_ref[...] = _gla_project(
        h, wt_ref, wgu_ref, bg_ref, v_ref.dtype)


def _gla_proj(x, mod, w_t, w_gu, b_gate, vr_dtype):
    m = x.shape[0]
    tm = min(ROW_TILE, m)
    row = functools.partial(_row_spec, tm)
    return pl.pallas_call(
        _gla_proj_kernel,
        grid=(m // tm,),
        in_specs=[row(D_MODEL), mod.spec(tm, 0), mod.spec(tm, 1), _full(w_t.shape),
                  _full(w_gu.shape), _full(b_gate.shape)],
        out_specs=[row(NK), row(NK), row(NK), row(NV), row(NV)],
        out_shape=[
            jax.ShapeDtypeStruct((m, NK), F32),
            jax.ShapeDtypeStruct((m, NK), F32),
            jax.ShapeDtypeStruct((m, NK), F32),
            jax.ShapeDtypeStruct((m, NV), vr_dtype),
            jax.ShapeDtypeStruct((m, NV), vr_dtype),
        ],
        compiler_params=_cparams(1),
        name="gla_proj",
    )(x, mod.arr, mod.arr, w_t, w_gu, b_gate)


def _mix_mlp_stream_kernel(a_ref, x_ref, gt0_ref, sh_ref, sc_ref, gt1_ref, wo_ref, w1_ref, w2_ref,
                           g0_ref, b0_ref, g1_ref, b1_ref, y_ref, x1_s, h_s, acc_ref):
    c = pl.program_id(0)
    tm = x_ref.shape[0]

    @pl.when(c == 0)
    def _():
        o = jnp.dot(a_ref[...].astype(BF16), wo_ref[...], preferred_element_type=F32)
        x1 = _res_ln(x_ref[...], _mod_rows(gt0_ref, tm), o, g0_ref[...], b0_ref[...])
        x1_s[...] = x1
        h_s[...] = _modulate(x1, _mod_rows(sh_ref, tm), _mod_rows(sc_ref, tm)).astype(BF16)
        acc_ref[...] = jnp.zeros_like(acc_ref)

    a = jnp.dot(h_s[...], w1_ref[...], preferred_element_type=F32)
    a = jnp.square(jnp.maximum(a, 0.0)).astype(BF16)
    acc_ref[...] += jnp.dot(a, w2_ref[...], preferred_element_type=F32)

    @pl.when(c == pl.num_programs(0) - 1)
    def _():
        y_ref[...] = _res_ln(x1_s[...], _mod_rows(gt1_ref, tm), acc_ref[...],
                             g1_ref[...], b1_ref[...])


def _mix_mlp_stream(a, x, mod_mix, mod_mlp, w_out, w1, w2, ln_g, ln_b, layer):
    m = x.shape[0]
    assert m <= ROW_TILE and mod_mix.per_row and mod_mlp.per_row
    fc = FF_CHUNK
    whole = lambda n: pl.BlockSpec((m, n), lambda c: (0, 0))
    mspec = lambda mod, col: mod.spec(m, col)
    if w1.ndim == 2:
        w1_cols = pl.BlockSpec((D_MODEL, fc), lambda c: (0, c))
        w2_rows = pl.BlockSpec((fc, D_MODEL), lambda c: (c, 0))
    else:
        w1_cols = pl.BlockSpec((None, D_MODEL, fc), lambda c: (layer, 0, c))
        w2_rows = pl.BlockSpec((None, fc, D_MODEL), lambda c: (layer, c, 0))
    return pl.pallas_call(
        _mix_mlp_stream_kernel,
        grid=(D_FF // fc,),
        in_specs=[whole(a.shape[1]), whole(D_MODEL), mspec(mod_mix, 2), mspec(mod_mlp, 0),
                  mspec(mod_mlp, 1), mspec(mod_mlp, 2), _full(w_out.shape),
                  w1_cols, w2_rows,
                  _layer(ln_g, 2 * layer), _layer(ln_b, 2 * layer),
                  _layer(ln_g, 2 * layer + 1), _layer(ln_b, 2 * layer + 1)],
        out_specs=whole(D_MODEL),
        out_shape=jax.ShapeDtypeStruct((m, D_MODEL), F32),
        scratch_shapes=[pltpu.VMEM((m, D_MODEL), F32), pltpu.VMEM((m, D_MODEL), BF16),
                        pltpu.VMEM((m, D_MODEL), F32)],
        compiler_params=_cparams(1),
        name="mix_mlp_stream",
    )(a, x, mod_mix.arr, mod_mlp.arr, mod_mlp.arr, mod_mlp.arr, w_out, w1, w2,
      ln_g, ln_b, ln_g, ln_b)


def _alibi_slope(head):
    return 2.0 ** (-8.0 * (head + 1) / N_Q_HEADS)


def _softmax_sink(s, sink):
    m = jnp.maximum(jnp.max(s, axis=-1, keepdims=True), sink)
    e = jnp.exp(s - m)
    den = jnp.sum(e, axis=-1, keepdims=True) + jnp.exp(sink - m)
    return e / den


def _band_bias_init(bias_ref):
    blk = ATTN_BLOCK
    c = lax.broadcasted_iota(jnp.int32, (2 * blk, blk), 0)
    r = lax.broadcasted_iota(jnp.int32, (2 * blk, blk), 1)
    dist = blk + r - c
    valid = (dist >= 0) & (dist <= WINDOW)
    distf = dist.astype(F32)
    for head in range(N_Q_HEADS):
        pen = -_alibi_slope(head) * distf
        bias_ref[0, head] = jnp.where(valid, pen, NEG_BIG)
        bias_ref[1, head] = jnp.where(valid & (c >= blk), pen, NEG_BIG)


def _band_scores(q, kk):
    blk = ATTN_BLOCK
    head_of_lane = lax.broadcasted_iota(jnp.int32, (blk, NKV), 1) // HEAD_DIM
    scores = []
    for b in range(q.shape[0] // blk):
        keys = kk[b * blk:(b + 2) * blk]
        for g in range(GROUP):
            qg = q[b * blk:(b + 1) * blk, g * NKV:(g + 1) * NKV]
            qm = jnp.concatenate(
                [jnp.where(head_of_lane == j, qg, jnp.zeros_like(qg))
                 for j in range(N_KV_HEADS)], axis=0)
            scores.append(lax.dot_general(keys, qm, (((1,), (1,)), ((), ())),
                                          preferred_element_type=F32))
    return scores


def _band_outputs(scores, vvt, first_tile, sinks_ref, bias_ref, o_ref):
    blk = ATTN_BLOCK
    for b in range(len(scores) // GROUP):
        table = 1 if first_tile and b == 0 else 0
        vals_t = vvt[:, b * blk:(b + 2) * blk]
        for g in range(GROUP):
            st_all = scores[b * GROUP + g]
            ps = []
            for j in range(N_KV_HEADS):
                head = j * GROUP + g
                sink = sinks_ref[head]
                st = st_all[:, j * blk:(j + 1) * blk] + bias_ref[table, head]
                m = jnp.maximum(jnp.max(st, axis=0, keepdims=True), sink)
                e = jnp.exp(st - m)
                den = jnp.sum(e, axis=0, keepdims=True) + jnp.exp(sink - m)
                ps.append((e * (1.0 / den)).astype(BF16))
            ot_all = jnp.dot(vals_t, jnp.concatenate(ps, axis=1),
                             preferred_element_type=F32)
            ot = jnp.concatenate(
                [ot_all[j * HEAD_DIM:(j + 1) * HEAD_DIM, j * blk:(j + 1) * blk]
                 for j in range(N_KV_HEADS)], axis=0)
            o_ref[b * blk:(b + 1) * blk, g * NKV:(g + 1) * NKV] = ot.T.astype(BF16)


def _mlp_chunks(h, w1_ref, w2_ref, acc_ref, chunks):
    for c in chunks:
        cols = slice(c * FF_CHUNK, (c + 1) * FF_CHUNK)
        a = jnp.dot(h, w1_ref[:, cols], preferred_element_type=F32)
        a = jnp.square(jnp.maximum(a, 0.0)).astype(BF16)
        d = jnp.dot(a, w2_ref[cols, :], preferred_element_type=F32)
        if c == 0:
            acc_ref[...] = d
        else:
            acc_ref[...] += d


def _attn_layer_kernel(sinks_ref, xc_ref, xp_ref, sh0_ref, sc0_ref, gt0_ref, sh1_ref, sc1_ref,
                       gt1_ref, win_ref, wo_ref, w1_ref, w2_ref, g0_ref, b0_ref, g1_ref, b1_ref,
                       dq_ref, dkn_ref, dvn_ref, dck_ref, dcv_ref, cw1_ref, cw2_ref, cw3_ref,
                       y_ref, kl_ref, vl_ref, do_ref, dnk_ref, dnv_ref,
                       cw1b_ref, cw2b_ref, cw3b_ref,
                       o_s, kprev_s, vtprev_s, acc_ref, bias_ref, zk_ref, zv_ref):
    i = pl.program_id(0)
    last = pl.num_programs(0) - 1
    dec = _DecAttn(sinks_ref, dq_ref, dkn_ref, dvn_ref, dck_ref, dcv_ref, do_ref, dnk_ref,
                   dnv_ref, zk_ref, zv_ref)

    def cast_weights():
        cw1b_ref[...] = cw1_ref[...].astype(BF16)
        cw2b_ref[...] = cw2_ref[...].astype(BF16)
        cw3b_ref[...] = cw3_ref[...].astype(BF16)
    slot = i % 2
    blk = ATTN_BLOCK
    tm = xc_ref.shape[0]
    dot = functools.partial(jnp.dot, preferred_element_type=F32)
    n_chunks = D_FF // FF_CHUNK

    def mlp_in():
        x1 = _res_ln(xp_ref[...], gt0_ref[0:1, :], dot(o_s[1 - slot], wo_ref[...]),
                     g0_ref[...], b0_ref[...])
        return x1, _modulate(x1, sh1_ref[0:1, :], sc1_ref[0:1, :]).astype(BF16)

    def mlp_out(x1):
        y_ref[...] = _res_ln(x1, gt1_ref[0:1, :], acc_ref[...], g1_ref[...], b1_ref[...])

    def mix_in():
        h_in = _modulate(xc_ref[...], sh0_ref[0:1, :], sc0_ref[0:1, :]).astype(BF16)
        q = (dot(h_in, win_ref[:, 0:NQ]) * (HEAD_DIM ** -0.5)).astype(BF16)
        k = dot(h_in, win_ref[:, NQ:NQ + NKV])
        v = dot(h_in, win_ref[:, NQ + NKV:NQ + 2 * NKV])
        kl_ref[...] = k[tm - blk:, :]
        vl_ref[...] = v[tm - blk:, :]
        k_bf = k.astype(BF16)
        vt = v.T.astype(BF16)
        kk = jnp.concatenate([kprev_s[...], k_bf], axis=0)
        vvt = jnp.concatenate([vtprev_s[...], vt], axis=1)
        scores = _band_scores(q, kk)
        kprev_s[...] = k_bf[tm - blk:, :]
        vtprev_s[...] = vt[:, tm - blk:]
        return scores, vvt

    @pl.when(i == 0)
    def _():
        _band_bias_init(bias_ref)
        kprev_s[...] = jnp.zeros_like(kprev_s)
        vtprev_s[...] = jnp.zeros_like(vtprev_s)
        dec.zero()
        cast_weights()
        scores, vvt = mix_in()
        dec_scored = dec.scores()
        _band_outputs(scores, vvt, True, sinks_ref, bias_ref, o_s.at[slot])
        dec.finish(dec_scored)

    @pl.when((i > 0) & (i < last))
    def _():
        x1, h_mlp = mlp_in()
        cast_weights()
        scores, vvt = mix_in()
        _mlp_chunks(h_mlp, w1_ref, w2_ref, acc_ref, range(0, 1))
        dec_scored = dec.scores()
        _mlp_chunks(h_mlp, w1_ref, w2_ref, acc_ref, range(1, n_chunks))
        _band_outputs(scores, vvt, False, sinks_ref, bias_ref, o_s.at[slot])
        dec.finish(dec_scored)
        mlp_out(x1)

    @pl.when(i == last)
    def _():
        x1, h_mlp = mlp_in()
        _mlp_chunks(h_mlp, w1_ref, w2_ref, acc_ref, range(n_chunks))
        mlp_out(x1)


def _attn_layer(x, mod_mix, mod_mlp, w_in, w_out, w1, w2, ln_g, ln_b, sinks, layer,
                dec_q, dec_k, dec_v, cache_kt, cache_vt, w1_f32, w2_f32, cast_layer, w3_f32):
    m = x.shape[0]
    tm = ATTN_TILE
    blk = ATTN_BLOCK
    n = m // tm
    assert not mod_mix.per_row and not mod_mlp.per_row
    nseq, win = cache_kt.shape[0], cache_kt.shape[2]
    sb = nseq // n
    n_new = dec_k.shape[0] // nseq
    assert sb * n == nseq and (sb * n_new) % SUBLANES == 0
    cur = pl.BlockSpec((tm, D_MODEL), lambda i: (jnp.minimum(i, n - 1), 0))
    prev = pl.BlockSpec((tm, D_MODEL), lambda i: (jnp.maximum(i - 1, 0), 0))
    last = pl.BlockSpec((blk, NKV), lambda i: (0, 0))
    dblk = lambda a: pl.BlockSpec((sb,) + a.shape[1:], lambda i: (jnp.minimum(i, n - 1), 0, 0))
    drow = lambda a: pl.BlockSpec((sb * n_new, a.shape[1]), lambda i: (jnp.minimum(i, n - 1), 0))
    c_in = lambda w: pl.BlockSpec((None, w.shape[1] // n, w.shape[2]),
                                  lambda i: (cast_layer, jnp.minimum(i, n - 1), 0))
    c_out = lambda w: pl.BlockSpec((w.shape[1] // n, w.shape[2]),
                                   lambda i: (jnp.minimum(i, n - 1), 0))
    assert all(w.shape[1] % (n * 2 * SUBLANES) == 0 for w in (w1_f32, w2_f32))
    r3 = pl.cdiv(pl.cdiv(w3_f32.shape[0], n), 2 * SUBLANES) * 2 * SUBLANES
    n3 = pl.cdiv(w3_f32.shape[0], r3)
    c3 = pl.BlockSpec((r3, w3_f32.shape[1]), lambda i: (jnp.minimum(i, n3 - 1), 0))
    return pl.pallas_call(
        _attn_layer_kernel,
        grid=(n + 1,),
        in_specs=[pl.BlockSpec(memory_space=pltpu.SMEM), cur, prev,
                  mod_mix.spec(tm, 0), mod_mix.spec(tm, 1), mod_mix.spec(tm, 2),
                  mod_mlp.spec(tm, 0), mod_mlp.spec(tm, 1), mod_mlp.spec(tm, 2),
                  _full(w_in.shape), _full(w_out.shape), _resident(w1, layer),
                  _resident(w2, layer),
                  _layer(ln_g, 2 * layer), _layer(ln_b, 2 * layer),
                  _layer(ln_g, 2 * layer + 1), _layer(ln_b, 2 * layer + 1),
                  dblk(dec_q), drow(dec_k), drow(dec_v), dblk(cache_kt), dblk(cache_vt),
                  c_in(w1_f32), c_in(w2_f32), c3],
        out_specs=[prev, last, last, dblk(dec_q), dblk(cache_kt), dblk(cache_vt),
                   c_out(w1_f32), c_out(w2_f32), c3],
        out_shape=[jax.ShapeDtypeStruct((m, D_MODEL), F32),
                   jax.ShapeDtypeStruct((blk, NKV), F32),
                   jax.ShapeDtypeStruct((blk, NKV), F32),
                   jax.ShapeDtypeStruct(dec_q.shape, BF16),
                   jax.ShapeDtypeStruct(cache_kt.shape, F32),
                   jax.ShapeDtypeStruct(cache_vt.shape, F32),
                   jax.ShapeDtypeStruct(w1_f32.shape[1:], BF16),
                   jax.ShapeDtypeStruct(w2_f32.shape[1:], BF16),
                   jax.ShapeDtypeStruct(w3_f32.shape, BF16)],
        scratch_shapes=[pltpu.VMEM((2, tm, NQ), BF16),
                        pltpu.VMEM((blk, NKV), BF16),
                        pltpu.VMEM((NKV, blk), BF16),
                        pltpu.VMEM((tm, D_MODEL), F32),
                        pltpu.VMEM((2, N_Q_HEADS, 2 * blk, blk), F32),
                        pltpu.VMEM((sb, win, NKV), F32),
                        pltpu.VMEM((sb, win, NKV), F32)],
        compiler_params=_cparams(1),
        name="attn_layer",
    )(sinks, x, x, mod_mix.arr, mod_mix.arr, mod_mix.arr, mod_mlp.arr, mod_mlp.arr,
      mod_mlp.arr, w_in, w_out, w1, w2, ln_g, ln_b, ln_g, ln_b,
      dec_q, dec_k, dec_v, cache_kt, cache_vt, w1_f32, w2_f32, w3_f32)


class _DecAttn:
    def __init__(self, sinks_ref, q_ref, kn_ref, vn_ref, ck_ref, cv_ref, o_ref, nk_ref, nv_ref,
                 zk_ref, zv_ref):
        self.refs = (sinks_ref, q_ref, kn_ref, vn_ref, ck_ref, cv_ref, o_ref, nk_ref, nv_ref,
                     zk_ref, zv_ref)
        self.n_seq = ck_ref.shape[0]
        self.n_new = kn_ref.shape[0] // self.n_seq
        self.win = ck_ref.shape[2]
        assert self.win == LANES and self.n_new < SUBLANES

    def zero(self):
        for ref in self.refs[9:]:
            ref[...] = jnp.zeros_like(ref)

    def scores(self):
        (sinks_ref, q_ref, kn_ref, vn_ref, ck_ref, cv_ref, _, nk_ref, nv_ref,
         zk_ref, zv_ref) = self.refs
        n_new, win = self.n_new, self.win
        rows = n_new * GROUP
        keep = win - n_new
        head_of_lane = lax.broadcasted_iota(jnp.int32, (rows, NKV), 1) // HEAD_DIM
        lane_w = lax.broadcasted_iota(jnp.int32, (NKV, win), 1)
        out = []
        for s in range(self.n_seq):
            tok = slice(s * n_new, (s + 1) * n_new)
            k_t = ck_ref[s]
            v_t = cv_ref[s]
            zk_ref[s, keep:win, :] = kn_ref[tok, :]
            zv_ref[s, keep:win, :] = vn_ref[tok, :]
            zk_t = zk_ref[s].T
            zv_t = zv_ref[s].T
            nk_ref[s] = jnp.where(lane_w < keep, pltpu.roll(k_t, keep, 1), zk_t)
            nv_ref[s] = jnp.where(lane_w < keep, pltpu.roll(v_t, keep, 1), zv_t)
            keys = jnp.concatenate([k_t, zk_t], axis=1).astype(BF16)
            vals = jnp.concatenate([v_t, zv_t], axis=1).astype(BF16)
            qs = q_ref[s]
            qbd = jnp.concatenate(
                [jnp.where(head_of_lane == j, qs, jnp.zeros_like(qs))
                 for j in range(N_KV_HEADS)], axis=0)
            out.append((jnp.dot(qbd, keys, preferred_element_type=F32), vals))
        return out

    def finish(self, scored):
        sinks_ref, o_ref = self.refs[0], self.refs[6]
        n_new, win = self.n_new, self.win
        rows = n_new * GROUP
        n_rows = N_KV_HEADS * rows
        keep = win - n_new
        row = lax.broadcasted_iota(jnp.int32, (n_rows, 2 * win), 0)
        col = lax.broadcasted_iota(jnp.int32, (n_rows, 2 * win), 1)
        j_r = row // rows
        t_r = (row // GROUP) % n_new
        g_r = row % GROUP
        h_r = j_r * GROUP + g_r
        slope = jnp.exp2(-8.0 * (h_r + 1).astype(F32) / N_Q_HEADS)
        sink = jnp.zeros((n_rows, 1), F32)
        h_col = h_r[:, 0:1]
        for h in range(N_Q_HEADS):
            sink = jnp.where(h_col == h, sinks_ref[h], sink)
        is_key = (col < win) | (col >= win + keep)
        frame = jnp.where(col < win, col, col - keep)
        dist = t_r + win - frame
        valid = is_key & (dist >= 0) & (dist <= WINDOW)
        bias = jnp.where(valid, -slope * dist.astype(F32), NEG_BIG)
        head_of_lane = lax.broadcasted_iota(jnp.int32, (rows, NKV), 1) // HEAD_DIM
        for s, (sc, vals) in enumerate(scored):
            p = _softmax_sink(sc + bias, sink).astype(BF16)
            pv = lax.dot_general(p, vals, (((1,), (1,)), ((), ())),
                                 preferred_element_type=F32)
            o = jnp.zeros((rows, NKV), F32)
            for j in range(N_KV_HEADS):
                o = o + jnp.where(head_of_lane == j, pv[j * rows:(j + 1) * rows], 0.0)
            o_ref[s] = o.astype(BF16)


def _split2(x):
    hi = x.astype(BF16)
    lo = (x - hi.astype(F32)).astype(BF16)
    return hi, lo


def _gla_out(o, r, norm_g):
    ms = jnp.mean(o * o, axis=-1, keepdims=True)
    o = o * lax.rsqrt(ms + LN_EPS) * norm_g
    return o * (r * jax.nn.sigmoid(r))


def _gla_layer_kernel(xc_ref, xp_ref, sh0_ref, sc0_ref, gt0_ref, sh1_ref, sc1_ref, gt1_ref,
                      wt_ref, wgu_ref, bg_ref, ng_ref, wo_ref, w1_ref, w2_ref,
                      g0_ref, b0_ref, g1_ref, b1_ref,
                      dq_ref, dk_ref, dlg_ref, dv_ref, dr_ref, ds0_ref,
                      y_ref, s_out_ref, do_ref, ds1_ref,
                      o_s, s_ref, acc_ref, z_ref, v_all_ref, q_pad_ref):
    i = pl.program_id(0)
    last = pl.num_programs(0) - 1
    dec = _GlaStep(dq_ref, dk_ref, dlg_ref, dv_ref, dr_ref, ng_ref, ds0_ref, do_ref, ds1_ref,
                   z_ref, v_all_ref, q_pad_ref)
    sub = xc_ref.shape[0]
    ch = GLA_CHUNK
    n_ch = sub // ch
    n_chunks = D_FF // FF_CHUNK
    dot = functools.partial(jnp.dot, preferred_element_type=F32)
    heads = range(GLA_HEADS)
    ks = [slice(h * GLA_DK, (h + 1) * GLA_DK) for h in heads]
    vs = [slice(h * GLA_DV, (h + 1) * GLA_DV) for h in heads]

    def mlp_in():
        x1 = _res_ln(xp_ref[...], gt0_ref[0:1, :], dot(o_s[...], wo_ref[...]),
                     g0_ref[...], b0_ref[...])
        return x1, _modulate(x1, sh1_ref[0:1, :], sc1_ref[0:1, :]).astype(BF16)

    def mlp_out(x1):
        y_ref[...] = _res_ln(x1, gt1_ref[0:1, :], acc_ref[...], g1_ref[...], b1_ref[...])

    def decays():
        h_in = _modulate(xc_ref[...], sh0_ref[0:1, :], sc0_ref[0:1, :]).astype(BF16)
        q, k, lg, v, r = _gla_project(h_in, wt_ref, wgu_ref, bg_ref, BF16)
        row = lax.broadcasted_iota(jnp.int32, (sub, sub), 0)
        col = lax.broadcasted_iota(jnp.int32, (sub, sub), 1)
        causal = ((row // ch) == (col // ch)) & (col <= row)
        tril = jnp.where(causal, 1.0, 0.0).astype(BF16)
        hi, lo = _split2(lg)
        b = dot(tril, hi) + dot(tril, lo)
        ends = [b[(c + 1) * ch - 1:(c + 1) * ch, :] for c in range(n_ch)]
        b_end = jnp.concatenate([jnp.broadcast_to(e, (ch, NK)) for e in ends], axis=0)
        qd = (q * jnp.exp(b)).astype(BF16)
        kd = (k * jnp.exp(-b)).astype(BF16)
        kdec = k * jnp.exp(b_end - b)
        dec_rows = jnp.concatenate(
            [jnp.exp(e) for e in ends] + [jnp.zeros((LANES - n_ch, NK), F32)], axis=0)
        return causal, qd, kd, kdec, dec_rows, v, r

    def chunk_products(causal, qd, kd, kdec, v):
        chunk_of_col = lax.broadcasted_iota(jnp.int32, (GLA_DK, sub), 1) // ch
        a = [lax.dot_general(qd[:, ks[h]], kd[:, ks[h]], (((1,), (1,)), ((), ())),
                             preferred_element_type=F32) for h in heads]
        u = []
        for h in heads:
            kdec_t = kdec[:, ks[h]].T.astype(BF16)
            stacked = jnp.concatenate(
                [jnp.where(chunk_of_col == c, kdec_t, jnp.zeros_like(kdec_t))
                 for c in range(n_ch)], axis=0)
            u.append(dot(stacked, v[:, vs[h]]))
        o_intra = [dot(jnp.where(causal, a[h], 0.0).astype(BF16), v[:, vs[h]]) for h in heads]
        return u, o_intra

    def recurrence(qd, dec_rows, u, o_intra, r):
        for h in heads:
            s = s_ref[h]
            dec_t = dec_rows[:, ks[h]].T
            o_inter = []
            for c in range(n_ch):
                o_inter.append(dot(qd[c * ch:(c + 1) * ch, ks[h]], s.astype(BF16)))
                s = dec_t[:, c:c + 1] * s + u[h][c * GLA_DK:(c + 1) * GLA_DK]
            s_ref[h] = s
            o = o_intra[h] + jnp.concatenate(o_inter, axis=0)
            o_s[:, vs[h]] = _gla_out(o, r[:, vs[h]].astype(F32), ng_ref[...]).astype(BF16)

    @pl.when(i == 0)
    def _():
        s_ref[...] = jnp.zeros_like(s_ref)
        dec.zero()
        dec_intra = dec.stage()
        causal, qd, kd, kdec, dec_rows, v, r = decays()
        u, o_intra = chunk_products(causal, qd, kd, kdec, v)
        recurrence(qd, dec_rows, u, o_intra, r)
        dec.finish(dec_intra)

    @pl.when((i > 0) & (i < last))
    def _():
        x1, h_mlp = mlp_in()
        dec_intra = dec.stage()
        causal, qd, kd, kdec, dec_rows, v, r = decays()
        _mlp_chunks(h_mlp, w1_ref, w2_ref, acc_ref, range(0, 1))
        u, o_intra = chunk_products(causal, qd, kd, kdec, v)
        _mlp_chunks(h_mlp, w1_ref, w2_ref, acc_ref, range(1, n_chunks))
        recurrence(qd, dec_rows, u, o_intra, r)
        dec.finish(dec_intra)
        mlp_out(x1)

    @pl.when(i == last)
    def _():
        x1, h_mlp = mlp_in()
        _mlp_chunks(h_mlp, w1_ref, w2_ref, acc_ref, range(n_chunks))
        mlp_out(x1)
        s_out_ref[...] = s_ref[...]


def _gla_layer(x, mod_mix, mod_mlp, w_t, w_gu, b_gate, norm_g, w_out, w1, w2, ln_g, ln_b, layer,
               dec_rows, dec_state):
    m = x.shape[0]
    tm = GLA_TILE
    n = m // tm
    assert not mod_mix.per_row and not mod_mlp.per_row and tm % GLA_CHUNK == 0
    nseq = dec_state.shape[0]
    sb = nseq // n
    n_new = dec_rows[0].shape[0] // nseq
    assert sb * n == nseq and (sb * n_new) % SUBLANES == 0 and n_new <= GLA_CHUNK
    cur = pl.BlockSpec((tm, D_MODEL), lambda i: (jnp.minimum(i, n - 1), 0))
    prev = pl.BlockSpec((tm, D_MODEL), lambda i: (jnp.maximum(i - 1, 0), 0))
    drow = lambda a: pl.BlockSpec((sb * n_new, a.shape[1]), lambda i: (jnp.minimum(i, n - 1), 0))
    dstate = pl.BlockSpec((sb,) + dec_state.shape[1:],
                          lambda i: (jnp.minimum(i, n - 1), 0, 0, 0))
    state = (GLA_HEADS, GLA_DK, GLA_DV)
    dv = dec_rows[3]
    return pl.pallas_call(
        _gla_layer_kernel,
        grid=(n + 1,),
        in_specs=[cur, prev,
                  mod_mix.spec(tm, 0), mod_mix.spec(tm, 1), mod_mix.spec(tm, 2),
                  mod_mlp.spec(tm, 0), mod_mlp.spec(tm, 1), mod_mlp.spec(tm, 2),
                  _full(w_t.shape), _full(w_gu.shape), _full(b_gate.shape), _full(norm_g.shape),
                  _full(w_out.shape), _resident(w1, layer), _resident(w2, layer),
                  _layer(ln_g, 2 * layer), _layer(ln_b, 2 * layer),
                  _layer(ln_g, 2 * layer + 1), _layer(ln_b, 2 * layer + 1)]
                 + [drow(a) for a in dec_rows] + [dstate],
        out_specs=[prev, _full(state), drow(dv), dstate],
        out_shape=[jax.ShapeDtypeStruct((m, D_MODEL), F32), jax.ShapeDtypeStruct(state, F32),
                   jax.ShapeDtypeStruct(dv.shape, F32),
                   jax.ShapeDtypeStruct(dec_state.shape, F32)],
        scratch_shapes=[pltpu.VMEM((tm, NV), BF16),
                        pltpu.VMEM(state, F32),
                        pltpu.VMEM((tm, D_MODEL), F32),
                        pltpu.VMEM((GLA_HEADS, LANES, GLA_DK), F32),
                        pltpu.VMEM((GLA_HEADS, LANES, GLA_DV), F32),
                        pltpu.VMEM((sb, GLA_HEADS, SUBLANES, GLA_DK), F32)],
        compiler_params=_cparams(1),
        name="gla_layer",
    )(x, x, mod_mix.arr, mod_mix.arr, mod_mix.arr, mod_mlp.arr, mod_mlp.arr, mod_mlp.arr,
      w_t, w_gu, b_gate, norm_g, w_out, w1, w2, ln_g, ln_b, ln_g, ln_b, *dec_rows, dec_state)


class _GlaStep:
    def __init__(self, q_ref, k_ref, lg_ref, v_ref, r_ref, ng_ref, s0_ref, o_ref, s1_ref,
                 z_ref, v_all_ref, q_pad_ref):
        self.refs = (q_ref, k_ref, lg_ref, v_ref, r_ref, ng_ref, s0_ref, o_ref, s1_ref,
                     z_ref, v_all_ref, q_pad_ref)
        self.n_seq = s0_ref.shape[0]
        self.n_new = q_ref.shape[0] // self.n_seq
        assert self.n_new < SUBLANES and self.n_seq * SUBLANES <= LANES

    def zero(self):
        for ref in self.refs[9:]:
            ref[...] = jnp.zeros_like(ref)

    def stage(self):
        return _gla_step_stage(self.n_seq, self.n_new, *self.refs)

    def finish(self, o_intra):
        _gla_step_finish(o_intra, self.n_seq, self.n_new, *self.refs)


def _gla_step_stage(n_seq, n_new, q_ref, k_ref, lg_ref, v_ref, r_ref, ng_ref, s0_ref,
                    o_ref, s1_ref, z_ref, v_all_ref, q_pad_ref):
    tok = [slice(s * n_new, (s + 1) * n_new) for s in range(n_seq)]
    grp = SUBLANES
    rowi = lax.broadcasted_iota(jnp.int32, (n_new, NK), 0)
    rowv = lax.broadcasted_iota(jnp.int32, (n_new, GLA_DV), 0)
    heads = range(GLA_HEADS)
    ks = [slice(h * GLA_DK, (h + 1) * GLA_DK) for h in heads]
    vs = [slice(h * GLA_DV, (h + 1) * GLA_DV) for h in heads]
    o_intra = []
    for s in range(n_seq):
        lg = lg_ref[tok[s], :]
        b = jnp.zeros_like(lg)
        for t in range(n_new):
            b = b + jnp.where(rowi >= t, jnp.broadcast_to(lg[t:t + 1, :], lg.shape), 0.0)
        b_end = b[n_new - 1:n_new, :]
        qd = q_ref[tok[s], :] * jnp.exp(b)
        k = k_ref[tok[s], :]
        kd = k * jnp.exp(-b)
        kdec = k * jnp.exp(b_end - b)
        dec = jnp.exp(b_end)
        v = v_ref[tok[s], :]
        row0 = s * grp
        o_s = []
        for h in heads:
            v_h = v[:, vs[h]]
            qd_h = qd[:, ks[h]]
            o = jnp.zeros((n_new, GLA_DV), F32)
            for t in range(n_new):
                a_t = jnp.sum(qd_h * kd[t:t + 1, ks[h]], axis=-1, keepdims=True)
                o = o + jnp.where(rowv >= t, a_t * v_h[t:t + 1, :], 0.0)
            o_s.append(o)
            q_pad_ref[s, h, 0:n_new, :] = qd_h
            z_ref[h, row0:row0 + n_new, :] = kdec[:, ks[h]]
            z_ref[h, row0 + n_new:row0 + n_new + 1, :] = dec[:, ks[h]]
            v_all_ref[h, row0:row0 + n_new, :] = v_h
        o_intra.append(o_s)
    return o_intra


def _gla_step_finish(o_intra, n_seq, n_new, q_ref, k_ref, lg_ref, v_ref, r_ref, ng_ref, s0_ref,
                     o_ref, s1_ref, z_ref, v_all_ref, q_pad_ref):
    tok = [slice(s * n_new, (s + 1) * n_new) for s in range(n_seq)]
    grp = SUBLANES
    dot = functools.partial(jnp.dot, preferred_element_type=F32)
    heads = range(GLA_HEADS)
    vs = [slice(h * GLA_DV, (h + 1) * GLA_DV) for h in heads]
    group_of_lane = lax.broadcasted_iota(jnp.int32, (GLA_DK, LANES), 1) // grp
    for h in heads:
        z_t = z_ref[h].T
        v_all = v_all_ref[h]
        for s in range(n_seq):
            s0 = s0_ref[s, h]
            mine = jnp.where(group_of_lane == s, z_t, 0.0)
            col = s * grp + n_new
            s1_ref[s, h] = z_t[:, col:col + 1] * s0 + dot(mine, v_all)
            o = o_intra[s][h] + dot(q_pad_ref[s, h], s0)[0:n_new]
            o_ref[tok[s], vs[h]] = _gla_out(o, r_ref[tok[s], vs[h]], ng_ref[...])


def kernel(x_prompt, x_sample, cache_k, cache_v, state_gla, c_prompt, c_sample, w_mod, b_mod,
           ln_g, ln_b, attn_w_in, attn_w_out, attn_sinks, gla_w_in, gla_w_gate_up, gla_b_gate,
           gla_norm_g, gla_w_out, mlp_w1, mlp_w2):
    assert x_prompt.shape[0] == 1 and w_mod.shape[0] == DEPTH == 2
    seq = x_prompt.shape[1]
    nseq, n_new = x_sample.shape[0], x_sample.shape[1]
    win = cache_k.shape[2]
    m_s = nseq * n_new

    wq = attn_w_in[0][:, :NQ].reshape(D_MODEL, N_KV_HEADS, GROUP, HEAD_DIM)
    wq = wq.transpose(0, 2, 1, 3).reshape(D_MODEL, NQ)
    w_attn_in = jnp.concatenate([wq, attn_w_in[0][:, NQ:]], axis=1).astype(BF16)
    w_attn_out = attn_w_out[0].reshape(N_KV_HEADS, GROUP, HEAD_DIM, D_MODEL)
    w_attn_out = w_attn_out.transpose(1, 0, 2, 3).reshape(NQ, D_MODEL).astype(BF16)
    w_gla_t_f32 = gla_w_in[0].T
    assert w_gla_t_f32.shape[0] == 2 * NK + 2 * NV + GLA_GATE_RANK
    w_gla_gu = jnp.pad(gla_w_gate_up[0], ((LANES - GLA_GATE_RANK, 0), (0, 0))).astype(BF16)
    w_gla_out = gla_w_out[0].astype(BF16)
    b_gate = gla_b_gate[0].reshape(1, NK)
    norm_g = gla_norm_g[0].reshape(1, GLA_DV)
    ln_g4 = ln_g.reshape(2 * DEPTH, 1, D_MODEL)
    ln_b4 = ln_b.reshape(2 * DEPTH, 1, D_MODEL)
    sinks = attn_sinks[0]

    assert nseq % SUBLANES == 0 and m_s <= ROW_TILE
    c_all = jnp.concatenate([c_sample, c_prompt, jnp.zeros((SUBLANES - 1, D_MODEL), F32)],
                            axis=0)
    mod_all, w1_0, w2_0 = _adaln_all(c_all, w_mod.reshape(2 * DEPTH, D_MODEL, 3 * D_MODEL),
                                     b_mod.reshape(2 * DEPTH, 1, 3 * D_MODEL),
                                     mlp_w1, mlp_w2, 0)
    mods_p = [_Mod(mod_all, p, per_row=False, row0=nseq) for p in range(2 * DEPTH)]
    mods_s = [_Mod(mod_all, p, per_row=True, n_seq=nseq) for p in range(2 * DEPTH)]

    assert seq >= WINDOW == ATTN_BLOCK
    win_p = WINDOW
    to_slab = lambda c: c[0].transpose(0, 2, 3, 1).reshape(nseq, NKV, win)
    from_slab = lambda c: c.reshape(nseq, N_KV_HEADS, HEAD_DIM, win).transpose(0, 3, 1, 2)[None]

    x_s = x_sample.reshape(m_s, D_MODEL)
    q, k, v = _attn_proj(x_s, mods_s[0], w_attn_in)
    x1_p, k_p, v_p, o, k_s, v_s, w1_1, w2_1, w_gla_t = _attn_layer(
        x_prompt[0], mods_p[0], mods_p[1], w_attn_in, w_attn_out, w1_0, w2_0,
        ln_g4, ln_b4, sinks, 0, q.reshape(nseq, n_new * GROUP, NKV), k, v,
        to_slab(cache_k), to_slab(cache_v), mlp_w1, mlp_w2, 1, w_gla_t_f32)
    x1_s = _mix_mlp_stream(o.reshape(m_s, NQ), x_s, mods_s[0], mods_s[1], w_attn_out,
                           w1_0, w2_0, ln_g4, ln_b4, 0)

    gla_w = (w_gla_t, w_gla_gu, b_gate)
    dec_rows = _gla_proj(x1_s, mods_s[2], *gla_w, F32)
    y_p, s_p, o, s_s = _gla_layer(x1_p, mods_p[2], mods_p[3], *gla_w, norm_g, w_gla_out,
                                  w1_1, w2_1, ln_g4, ln_b4, 1, dec_rows, state_gla[0])
    y_s = _mix_mlp_stream(o, x1_s, mods_s[2], mods_s[3], w_gla_out, w1_1, w2_1,
                          ln_g4, ln_b4, 1)
    k_s, v_s = from_slab(k_s), from_slab(v_s)

    kv_shape_p = (1, 1, win_p, N_KV_HEADS, HEAD_DIM)
    return (y_p[None], y_s.reshape(nseq, n_new, D_MODEL),
            k_p.reshape(kv_shape_p), v_p.reshape(kv_shape_p), s_p[None, None],
            k_s, v_s, s_s[None])
```
